```python
import math
import jax, jax.numpy as jnp
from jax import lax
import numpy as np

D_MODEL = 1024
BATCH = 8
SEQ = 2048
DEPTH = 1

D_MIX = D_MODEL
ATT_HEADS = 4
ATT_QK_DIM = 64
ATT_V_DIM = 2 * ATT_QK_DIM
ATT_QK_WIDTH = ATT_HEADS * 2 * ATT_QK_DIM
ATT_WIDTH = ATT_HEADS * ATT_V_DIM
N_BUCKETS = 32
MAX_DISTANCE = 128
Q_BLOCK = 128
SSD_WIDTH = D_MIX - ATT_WIDTH
SSD_HEADDIM = 64
SSD_HEADS = SSD_WIDTH // SSD_HEADDIM
SSD_GROUPS = 2
SSD_STATE = 64
SSD_CONV = 5
SSD_CHUNK = 128
SSD_CONV_CH = SSD_WIDTH + 2 * SSD_GROUPS * SSD_STATE
IN_SPLITS = (ATT_QK_WIDTH, ATT_QK_WIDTH, ATT_WIDTH, SSD_WIDTH, SSD_CONV_CH, SSD_HEADS, SSD_HEADS)
IN_COLS = 2 * ATT_QK_WIDTH + ATT_WIDTH + SSD_WIDTH + SSD_CONV_CH + 2 * SSD_HEADS
N_EXPERTS = 16
CAPACITY_FACTOR = 2
EXPERT_FF = 2816
EPS = 1e-6

kernel_name = 'hybrid_diffattn_ssd_ec_moe_block'


def lambda_init(layer):
    return 0.8 - 0.6 * math.exp(-0.3 * layer)


def rmsnorm(x, g):
    xf = x.astype(jnp.float32)
    xf = xf * lax.rsqrt(jnp.mean(xf * xf, axis=-1, keepdims=True) + EPS)
    return (xf * g.astype(jnp.float32)).astype(x.dtype)


def modulate(h, shift, scale):
    return h * (1.0 + scale[:, None, :]) + shift[:, None, :]


def t5_bucket(rel):
    nb = N_BUCKETS // 2
    ret = jnp.where(rel > 0, nb, 0)
    n = jnp.abs(rel)
    max_exact = nb // 2
    nf = jnp.maximum(n, 1).astype(jnp.float32)
    large = max_exact + (jnp.log(nf / max_exact) / math.log(MAX_DISTANCE / max_exact)
                         * (nb - max_exact)).astype(jnp.int32)
    large = jnp.minimum(large, nb - 1)
    return ret + jnp.where(n < max_exact, n, large)


def diff_attention(q, k, v, lam, rel_table):
    b, s = q.shape[:2]
    nblk = s // Q_BLOCK
    scale = ATT_QK_DIM ** -0.5
    qb = q.reshape(b, nblk, Q_BLOCK, ATT_HEADS, 2, ATT_QK_DIM).transpose(1, 0, 2, 3, 4, 5)
    k_pos = jnp.arange(s)

    def block(args):
        i, qi = args
        q_pos = i * Q_BLOCK + jnp.arange(Q_BLOCK)
        bucket = t5_bucket(k_pos[None, :] - q_pos[:, None])
        bias = rel_table[bucket].transpose(2, 0, 1).astype(jnp.float32)
        logits = jnp.einsum('bqhmd,bkhmd->bhmqk', qi, k).astype(jnp.float32) * scale
        probs = jax.nn.softmax(logits + bias[None, :, None], axis=-1)
        w = probs[:, :, 0] - lam * probs[:, :, 1]
        return jnp.einsum('bhqk,bkhd->bqhd', w.astype(v.dtype), v)

    out = lax.map(block, (jnp.arange(nblk), qb))
    return out.transpose(1, 0, 2, 3, 4).reshape(b, s, ATT_HEADS, ATT_V_DIM)


def segsum(x):
    t = x.shape[-1]
    cs = jnp.cumsum(x, axis=-1)
    seg = cs[..., :, None] - cs[..., None, :]
    mask = jnp.tril(jnp.ones((t, t), dtype=bool))
    return jnp.where(mask, seg, -jnp.inf)


def ssd_chunked(xs, dt, A, Bm, Cm):
    b, s = xs.shape[:2]
    nc = s // SSD_CHUNK
    r = SSD_HEADS // SSD_GROUPS
    X = (xs * dt[..., None]).reshape(b, nc, SSD_CHUNK, SSD_GROUPS, r, SSD_HEADDIM)
    dA = (dt * A).reshape(b, nc, SSD_CHUNK, SSD_GROUPS, r).transpose(0, 3, 4, 1, 2)
    Bc = Bm.reshape(b, nc, SSD_CHUNK, SSD_GROUPS, SSD_STATE)
    Cc = Cm.reshape(b, nc, SSD_CHUNK, SSD_GROUPS, SSD_STATE)
    A_cs = jnp.cumsum(dA, axis=-1)
    L = jnp.exp(segsum(dA))
    y_diag = jnp.einsum('bclgn,bcsgn,bgrcls,bcsgrp->bclgrp', Cc, Bc, L, X)
    decay_states = jnp.exp(A_cs[..., -1:] - A_cs)
    states = jnp.einsum('bclgn,bgrcl,bclgrp->bcgrpn', Bc, decay_states, X)
    states = jnp.concatenate([jnp.zeros_like(states[:, :1]), states], axis=1)
    decay_chunk = jnp.exp(segsum(jnp.pad(A_cs[..., -1], ((0, 0), (0, 0), (0, 0), (1, 0)))))
    new_states = jnp.einsum('bgrzc,bcgrpn->bzgrpn', decay_chunk, states)[:, :-1]
    y_off = jnp.einsum('bclgn,bcgrpn,bgrcl->bclgrp', Cc, new_states, jnp.exp(A_cs))
    return (y_diag + y_off).reshape(b, s, SSD_HEADS, SSD_HEADDIM)


def ssd_mixer(z, xbc, dt_f_raw, dt_b_raw, conv_w, conv_b, dt_bias_f, dt_bias_b,
              A_log_f, A_log_b, D_skip, norm_g):
    b, s = z.shape[:2]
    f32 = jnp.float32
    xbc = lax.conv_general_dilated(xbc, conv_w, window_strides=(1,),
                                   padding=((SSD_CONV // 2, SSD_CONV // 2),),
                                   dimension_numbers=('NWC', 'WIO', 'NWC'),
                                   feature_group_count=SSD_CONV_CH) + conv_b
    xbc = jax.nn.silu(xbc.astype(f32))
    xs, Bm, Cm = jnp.split(xbc, [SSD_WIDTH, SSD_WIDTH + SSD_GROUPS * SSD_STATE], axis=-1)
    xs = xs.reshape(b, s, SSD_HEADS, SSD_HEADDIM)
    Bm = Bm.reshape(b, s, SSD_GROUPS, SSD_STATE)
    Cm = Cm.reshape(b, s, SSD_GROUPS, SSD_STATE)
    dt_f = jax.nn.softplus(dt_f_raw.astype(f32) + dt_bias_f.astype(f32))
    dt_b = jax.nn.softplus(dt_b_raw.astype(f32) + dt_bias_b.astype(f32))
    A_f = -jnp.exp(A_log_f.astype(f32))
    A_b = -jnp.exp(A_log_b.astype(f32))
    rev = lambda t: jnp.flip(t, axis=1)
    y_f = ssd_chunked(xs, dt_f, A_f, Bm, Cm)
    y_b = rev(ssd_chunked(rev(xs), rev(dt_b), A_b, rev(Bm), rev(Cm)))
    y = y_f + y_b + D_skip.astype(f32)[:, None] * xs
    y = y.reshape(b, s, SSD_WIDTH) * jax.nn.silu(z.astype(f32))
    yg = y.reshape(b, s, SSD_GROUPS, SSD_WIDTH // SSD_GROUPS)
    yg = yg * lax.rsqrt(jnp.mean(yg * yg, axis=-1, keepdims=True) + EPS)
    return (yg.reshape(b, s, SSD_WIDTH) * norm_g.astype(f32)).astype(z.dtype)


def expert_choice_ffn(h, router_w, w_gate, w_up, w_down):
    b, s, d = h.shape
    cap = CAPACITY_FACTOR * s // N_EXPERTS
    aff = jax.nn.softmax(jnp.einsum('bsd,de->bse', h, router_w).astype(jnp.float32), axis=-1)
    g, idx = lax.top_k(aff.transpose(0, 2, 1), cap)
    bidx = jnp.arange(b)[:, None, None]
    xg = h[bidx, idx]
    hid = jax.nn.silu(jnp.einsum('becd,edf->becf', xg, w_gate)) * jnp.einsum('becd,edf->becf', xg, w_up)
    y = jnp.einsum('becf,efd->becd', hid, w_down) * g[..., None].astype(h.dtype)
    return jnp.zeros_like(h).at[bidx, idx].add(y)


def setup_inputs(seed: int = 0) -> dict:
    key = jax.random.key(seed)
    ks = jax.random.split(key, 32)
    f32 = jnp.float32

    def nrm(k, shape, s):
        return jax.random.normal(k, shape, f32) * s

    def gain(k, shape):
        return 1.0 + 0.02 * jax.random.normal(k, shape, f32)

    def dt_bias(k):
        dt = jnp.exp(jax.random.uniform(k, (DEPTH, SSD_HEADS), f32, math.log(1e-3), math.log(1e-1)))
        return dt + jnp.log(-jnp.expm1(-dt))

    return {
        'x': nrm(ks[0], (BATCH, SEQ, D_MODEL), 1.0),
        'c': nrm(ks[1], (BATCH, D_MODEL), 1.0),
        'ada_w': nrm(ks[2], (DEPTH, D_MODEL, 6 * D_MODEL), D_MODEL ** -0.5),
        'ada_b': nrm(ks[3], (DEPTH, 6 * D_MODEL), 0.02),
        'norm_mix_g': gain(ks[4], (DEPTH, D_MODEL)),
        'norm_ffn_g': gain(ks[5], (DEPTH, D_MODEL)),
        'norm_final_g': gain(ks[6], (D_MODEL,)),
        'w_in': nrm(ks[7], (DEPTH, D_MODEL, IN_COLS), D_MODEL ** -0.5),
        'lambda_q1': nrm(ks[8], (DEPTH, ATT_QK_DIM), 0.1),
        'lambda_k1': nrm(ks[9], (DEPTH, ATT_QK_DIM), 0.1),
        'lambda_q2': nrm(ks[10], (DEPTH, ATT_QK_DIM), 0.1),
        'lambda_k2': nrm(ks[11], (DEPTH, ATT_QK_DIM), 0.1),
        'attn_subln_g': gain(ks[12], (DEPTH, ATT_V_DIM)),
        'rel_bias_table': nrm(ks[13], (N_BUCKETS, ATT_HEADS), 0.5),
        'conv_w': nrm(ks[14], (DEPTH, SSD_CONV, 1, SSD_CONV_CH), SSD_CONV ** -0.5),
        'conv_b': nrm(ks[15], (DEPTH, SSD_CONV_CH), 0.02),
        'dt_bias_f': dt_bias(ks[16]),
        'dt_bias_b': dt_bias(ks[17]),
        'A_log_f': jnp.log(jax.random.uniform(ks[18], (DEPTH, SSD_HEADS), f32, 1.0, 16.0)),
        'A_log_b': jnp.log(jax.random.uniform(ks[19], (DEPTH, SSD_HEADS), f32, 1.0, 16.0)),
        'D_skip': gain(ks[20], (DEPTH, SSD_HEADS)),
        'ssm_norm_g': gain(ks[21], (DEPTH, SSD_WIDTH)),
        'w_out': nrm(ks[22], (DEPTH, D_MIX, D_MODEL), D_MIX ** -0.5),
        'router_w': nrm(ks[23], (DEPTH, D_MODEL, N_EXPERTS), D_MODEL ** -0.5),
        'w_gate': nrm(ks[24], (DEPTH, N_EXPERTS, D_MODEL, EXPERT_FF), D_MODEL ** -0.5),
        'w_up': nrm(ks[25], (DEPTH, N_EXPERTS, D_MODEL, EXPERT_FF), D_MODEL ** -0.5),
        'w_down': nrm(ks[26], (DEPTH, N_EXPERTS, EXPERT_FF, D_MODEL), EXPERT_FF ** -0.5),
    }


def reference(x, c, ada_w, ada_b, norm_mix_g, norm_ffn_g, norm_final_g, w_in,
              lambda_q1, lambda_k1, lambda_q2, lambda_k2, attn_subln_g, rel_bias_table,
              conv_w, conv_b, dt_bias_f, dt_bias_b, A_log_f, A_log_b, D_skip, ssm_norm_g,
              w_out, router_w, w_gate, w_up, w_down):
    b, s, _ = x.shape
    offs = np.cumsum(IN_SPLITS)[:-1].tolist()
    for l in range(DEPTH):
        lam_init = lambda_init(l)
        mod = jnp.einsum('bd,de->be', jax.nn.silu(c), ada_w[l]) + ada_b[l]
        sh1, sc1, g1, sh2, sc2, g2 = jnp.split(mod, 6, axis=-1)
        h = modulate(rmsnorm(x, norm_mix_g[l]), sh1, sc1)
        p = jnp.einsum('bsd,de->bse', h, w_in[l])
        q, k, v, z, xbc, dt_f, dt_b = jnp.split(p, offs, axis=-1)
        lam = (jnp.exp(jnp.sum(lambda_q1[l] * lambda_k1[l]).astype(jnp.float32))
               - jnp.exp(jnp.sum(lambda_q2[l] * lambda_k2[l]).astype(jnp.float32)) + lam_init)
        att = diff_attention(q.reshape(b, s, ATT_HEADS, 2, ATT_QK_DIM),
                             k.reshape(b, s, ATT_HEADS, 2, ATT_QK_DIM),
                             v.reshape(b, s, ATT_HEADS, ATT_V_DIM), lam, rel_bias_table)
        att = (rmsnorm(att, attn_subln_g[l]) * (1.0 - lam_init)).reshape(b, s, ATT_WIDTH)
        ssd = ssd_mixer(z, xbc, dt_f, dt_b, conv_w[l], conv_b[l], dt_bias_f[l], dt_bias_b[l],
                        A_log_f[l], A_log_b[l], D_skip[l], ssm_norm_g[l])
        mix = jnp.einsum('bsm,md->bsd', jnp.concatenate([att, ssd], axis=-1), w_out[l])
        x = x + g1[:, None, :] * mix
        h2 = modulate(rmsnorm(x, norm_ffn_g[l]), sh2, sc2)
        x = x + g2[:, None, :] * expert_choice_ffn(h2, router_w[l], w_gate[l], w_up[l], w_down[l])
    return rmsnorm(x, norm_final_g)
```

```python
import functools
import math

import jax
import jax.numpy as jnp
from jax import lax
from jax.experimental import pallas as pl
from jax.experimental.pallas import tpu as pltpu

F32 = jnp.float32
BF16 = jnp.bfloat16
I32 = jnp.int32
HIGHEST = lax.Precision.HIGHEST

ATT_HEADS = 4
ATT_QK_DIM = 64
ATT_V_DIM = 128
N_BUCKETS = 32
SSD_HEADS = 8
SSD_HEADDIM = 64
SSD_GROUPS = 2
SSD_STATE = 64
SSD_CONV = 5
SSD_CHUNK = 128
SSD_WIDTH = SSD_HEADS * SSD_HEADDIM
SSD_BC = SSD_GROUPS * SSD_STATE
N_EXPERTS = 16
CAPACITY_FACTOR = 2
EPS = 1e-6
LANES = 128
NEG_BIG = -1e30

VMEM_LIMIT = 56 * 1024 * 1024


def _params(*semantics):
    return pltpu.CompilerParams(dimension_semantics=semantics, vmem_limit_bytes=VMEM_LIMIT)


def _silu(v):
    return v * jax.nn.sigmoid(v)


def _rms(v, eps=EPS):
    return v * lax.rsqrt(jnp.mean(v * v, axis=-1, keepdims=True) + eps)


def _mod_body(c_ref, w_ref, b_ref, o_ref):
    o_ref[...] = jnp.dot(_silu(c_ref[...]), w_ref[...], precision=HIGHEST,
                         preferred_element_type=F32) + b_ref[...]


def _ada_mod(c, w, b):
    bsz, d = c.shape
    n = w.shape[1]
    tn = n // 4
    return pl.pallas_call(
        _mod_body,
        grid=(n // tn,),
        in_specs=[pl.BlockSpec((bsz, d), lambda j: (0, 0)),
                  pl.BlockSpec((d, tn), lambda j: (0, j)),
                  pl.BlockSpec((1, tn), lambda j: (0, j))],
        out_specs=pl.BlockSpec((bsz, tn), lambda j: (0, j)),
        out_shape=jax.ShapeDtypeStruct((bsz, n), F32),
        compiler_params=_params("arbitrary"),
        name="ada_mod",
    )(c, w, b.reshape(1, n))


def _inproj_body(x_ref, sh_ref, sc_ref, g_ref, w_ref, q_ref, k_ref, v_ref, z_ref, xbc_ref, dt_ref,
                 *, offs, qscale):
    h = _rms(x_ref[0]) * g_ref[...]
    h = (h * (1.0 + sc_ref[0]) + sh_ref[0]).astype(BF16)

    def proj(i):
        return jnp.dot(h, w_ref[:, offs[i]:offs[i + 1]], preferred_element_type=F32)

    q_ref[0] = (proj(0) * qscale).astype(BF16)
    k_ref[0] = proj(1).astype(BF16)
    v_ref[0] = proj(2).astype(BF16)
    z_ref[0] = proj(3)
    xbc_ref[0] = proj(4)
    dt_ref[0] = proj(5)


def _in_proj(x, sh, sc, g, w_pad, offs, ts):
    bsz, s, d = x.shape
    widths = [offs[i + 1] - offs[i] for i in range(6)]
    dts = [BF16, BF16, BF16, F32, F32, F32]
    tok = lambda wd: pl.BlockSpec((1, ts, wd), lambda b, t: (b, t, 0))
    vec = pl.BlockSpec((1, 1, d), lambda b, t: (b, 0, 0))
    return pl.pallas_call(
        functools.partial(_inproj_body, offs=tuple(offs), qscale=ATT_QK_DIM ** -0.5),
        grid=(bsz, s // ts),
        in_specs=[tok(d), vec, vec,
                  pl.BlockSpec((1, d), lambda b, t: (0, 0)),
                  pl.BlockSpec(w_pad.shape, lambda b, t: (0, 0))],
        out_specs=[tok(wd) for wd in widths],
        out_shape=[jax.ShapeDtypeStruct((bsz, s, wd), dt) for wd, dt in zip(widths, dts)],
        compiler_params=_params("parallel", "arbitrary"),
        name="in_proj",
    )(x, sh, sc, g, w_pad)


def _t5_bucket(rel):
    n = jnp.abs(rel)
    large = jnp.full(rel.shape, 8, I32)
    for t in (12, 16, 23, 32, 46, 64, 91):
        large = large + jnp.where(n >= t, 1, 0)
    return jnp.where(rel > 0, 16, 0) + jnp.where(n < 8, n, large)


def _attn_body(tbl_ref, lq1_ref, lk1_ref, lq2_ref, lk2_ref, g_ref, q_ref, k_ref, v_ref, o_ref, bias_ref,
               *, lam_init):
    h = pl.program_id(0)
    qi = pl.program_id(1)
    tq, s = bias_ref.shape

    @pl.when(pl.program_id(2) == 0)
    def _():
        j = lax.broadcasted_iota(I32, (8, 2 * s), 1)
        bucket = _t5_bucket(j - (s - 1))
        vec = jnp.zeros((8, 2 * s), F32)
        for bb in range(N_BUCKETS):
            vec = jnp.where(bucket == bb, tbl_ref[bb, h], vec)
        big = jnp.broadcast_to(vec[0:1], (tq, 2 * s))
        rolled = pltpu.roll(big, qi * tq + (s + 1), axis=1, stride=1, stride_axis=0)
        bias_ref[...] = rolled[:, :s]

    lam = (jnp.exp(jnp.sum(lq1_ref[...] * lk1_ref[...], axis=-1, keepdims=True))
           - jnp.exp(jnp.sum(lq2_ref[...] * lk2_ref[...], axis=-1, keepdims=True)) + lam_init)
    q = q_ref[0]
    k = k_ref[0]
    v = v_ref[0]
    lane = lax.broadcasted_iota(I32, q.shape, 1)
    outs = []
    for m in range(2):
        qm = jnp.where((lane >= m * ATT_QK_DIM) & (lane < (m + 1) * ATT_QK_DIM), q, jnp.zeros_like(q))
        logits = lax.dot_general(qm, k, (((1,), (1,)), ((), ())), preferred_element_type=F32) + bias_ref[...]
        e = jnp.exp(logits - jnp.max(logits, axis=-1, keepdims=True))
        den = jnp.sum(e, axis=-1, keepdims=True)
        outs.append(jnp.dot(e.astype(BF16), v, preferred_element_type=F32) / den)
    att = outs[0] - lam * outs[1]
    o_ref[0] = (_rms(att) * g_ref[...] * (1.0 - lam_init)).astype(BF16)


def _attention(q, k, v, tbl, lq1, lk1, lq2, lk2, subln_g, lam_init, tq):
    bsz, s, _ = q.shape
    small = lambda n: pl.BlockSpec((1, n), lambda h, i, b: (0, 0))
    return pl.pallas_call(
        functools.partial(_attn_body, lam_init=lam_init),
        grid=(ATT_HEADS, s // tq, bsz),
        in_specs=[pl.BlockSpec(memory_space=pltpu.SMEM),
                  small(ATT_QK_DIM), small(ATT_QK_DIM), small(ATT_QK_DIM), small(ATT_QK_DIM),
                  small(ATT_V_DIM),
                  pl.BlockSpec((1, tq, ATT_V_DIM), lambda h, i, b: (b, i, h)),
                  pl.BlockSpec((1, s, ATT_V_DIM), lambda h, i, b: (b, 0, h)),
                  pl.BlockSpec((1, s, ATT_V_DIM), lambda h, i, b: (b, 0, h))],
        out_specs=pl.BlockSpec((1, tq, ATT_V_DIM), lambda h, i, b: (b, i, h)),
        out_shape=jax.ShapeDtypeStruct((bsz, s, ATT_HEADS * ATT_V_DIM), BF16),
        scratch_shapes=[pltpu.VMEM((tq, s), F32)],
        compiler_params=_params("parallel", "parallel", "arbitrary"),
        name="diff_attention",
    )(tbl, lq1, lk1, lq2, lk2, subln_g, q, k, v)


def _expand_heads(cols, off):
    r = cols.shape[0]
    lane_head = lax.broadcasted_iota(I32, (r, SSD_WIDTH), 1) // SSD_HEADDIM
    out = jnp.zeros((r, SSD_WIDTH), F32)
    for hh in range(SSD_HEADS):
        out = jnp.where(lane_head == hh, cols[:, off + hh:off + hh + 1], out)
    return out


def _ssd_body(z_ref, xbc_ref, dtr_ref, cw_ref, cb_ref, dtb_ref, alog_ref, dskip_ref, ng_ref, o_ref,
              xp_ref, xc_ref, dt_ref, da_ref, cs_ref, y_ref, hf_ref, hb_ref):
    s = xbc_ref.shape[1]
    ch = SSD_CHUNK
    nc = s // ch
    pad = 8
    nconv = xbc_ref.shape[2]

    xp_ref[0:pad, :] = jnp.zeros((pad, nconv), F32)
    xp_ref[pad + s:pad + s + pad, :] = jnp.zeros((pad, nconv), F32)
    xp_ref[pad:pad + s, :] = xbc_ref[0]

    def conv_chunk(c, carry):
        r0 = pl.multiple_of(c * ch, ch)
        win = xp_ref[pl.ds(r0, ch + 2 * pad), :]
        acc = jnp.broadcast_to(cb_ref[...], (ch, nconv))
        for j in range(SSD_CONV):
            lo = pad + j - SSD_CONV // 2
            acc = acc + cw_ref[j:j + 1, :] * win[lo:lo + ch, :]
        xc_ref[pl.ds(r0, ch), :] = _silu(acc)
        return carry

    lax.fori_loop(0, nc, conv_chunk, 0)

    lane = lax.broadcasted_iota(I32, (s, LANES), 1)
    pre = dtr_ref[0] + dtb_ref[...]
    dt = jnp.maximum(pre, 0.0) + jnp.log1p(jnp.exp(-jnp.abs(pre)))
    dt = jnp.where(lane < 2 * SSD_HEADS, dt, 0.0)
    dt_ref[...] = dt
    da_ref[...] = dt * (-jnp.exp(alog_ref[...]))

    row = lax.broadcasted_iota(I32, (ch, ch), 0)
    col = lax.broadcasted_iota(I32, (ch, ch), 1)
    tri = jnp.where(col <= row, 1.0, 0.0).astype(F32)
    lower = col <= row
    upper = col >= row
    lane_c = lax.broadcasted_iota(I32, (ch, LANES), 1)
    st_rows = lax.broadcasted_iota(I32, (SSD_BC, SSD_WIDTH), 0) // SSD_STATE
    st_cols = lax.broadcasted_iota(I32, (SSD_BC, SSD_WIDTH), 1) // (SSD_WIDTH // SSD_GROUPS)
    same_group = st_rows == st_cols

    hf_ref[...] = jnp.zeros_like(hf_ref)
    hb_ref[...] = jnp.zeros_like(hb_ref)

    def fwd_chunk(c, carry):
        r0 = pl.multiple_of(c * ch, ch)
        da = da_ref[pl.ds(r0, ch), :]
        dtc = dt_ref[pl.ds(r0, ch), :]
        cs = jnp.dot(tri, da, precision=HIGHEST, preferred_element_type=F32)
        cs_ref[pl.ds(r0, ch), :] = cs
        tot = cs[ch - 1:ch, :]
        ecs = cs - da
        cs_t = cs.T
        ecs_t = ecs.T
        dt_t = dtc.T
        xs = xc_ref[pl.ds(r0, ch), 0:SSD_WIDTH]
        bm = xc_ref[pl.ds(r0, ch), SSD_WIDTH:SSD_WIDTH + SSD_BC]
        cm = xc_ref[pl.ds(r0, ch), SSD_WIDTH + SSD_BC:SSD_WIDTH + 2 * SSD_BC]
        bt = bm.T.astype(BF16)
        cmb = cm.astype(BF16)
        xsb = xs.astype(BF16)

        g_mats = []
        for g in range(SSD_GROUPS):
            cg = jnp.where(lane_c // SSD_STATE == g, cmb, jnp.zeros_like(cmb))
            g_mats.append(jnp.dot(cg, bt, preferred_element_type=F32))
        for pair in range(SSD_HEADS // 2):
            xpair = xsb[:, pair * LANES:(pair + 1) * LANES]
            ypair = jnp.zeros((ch, LANES), F32)
            for sub in range(2):
                hh = 2 * pair + sub
                g = hh // (SSD_HEADS // SSD_GROUPS)
                d_f = cs[:, hh:hh + 1] - cs_t[hh:hh + 1, :]
                d_b = ecs_t[SSD_HEADS + hh:SSD_HEADS + hh + 1, :] - ecs[:, SSD_HEADS + hh:SSD_HEADS + hh + 1]
                decay = jnp.exp(jnp.where(lower, d_f, d_b))
                wgt = (jnp.where(lower, dt_t[hh:hh + 1, :], 0.0)
                       + jnp.where(upper, dt_t[SSD_HEADS + hh:SSD_HEADS + hh + 1, :], 0.0))
                m = (g_mats[g] * decay * wgt).astype(BF16)
                xh = jnp.where(lane_c // SSD_HEADDIM == sub, xpair, jnp.zeros_like(xpair))
                ypair = ypair + jnp.dot(m, xh, preferred_element_type=F32)
            y_ref[pl.ds(r0, ch), pair * LANES:(pair + 1) * LANES] = ypair

        hf = hf_ref[...]
        y_off = jnp.dot(cmb, hf.astype(BF16), preferred_element_type=F32) * _expand_heads(jnp.exp(cs), 0)
        y_ref[pl.ds(r0, ch), :] = y_ref[pl.ds(r0, ch), :] + y_off
        xd = (xs * _expand_heads(dtc * jnp.exp(tot - cs), 0)).astype(BF16)
        upd = jnp.dot(bt, xd, preferred_element_type=F32)
        decay_tot = _expand_heads(jnp.broadcast_to(jnp.exp(tot), (8, LANES)), 0)[0:1, :]
        hf_ref[...] = hf * decay_tot + jnp.where(same_group, upd, 0.0)
        return carry

    lax.fori_loop(0, nc, fwd_chunk, 0)

    def bwd_chunk(i, carry):
        c = nc - 1 - i
        r0 = pl.multiple_of(c * ch, ch)
        da = da_ref[pl.ds(r0, ch), :]
        dtc = dt_ref[pl.ds(r0, ch), :]
        cs = cs_ref[pl.ds(r0, ch), :]
        tot = cs[ch - 1:ch, :]
        ecs = cs - da
        xs = xc_ref[pl.ds(r0, ch), 0:SSD_WIDTH]
        bm = xc_ref[pl.ds(r0, ch), SSD_WIDTH:SSD_WIDTH + SSD_BC]
        cm = xc_ref[pl.ds(r0, ch), SSD_WIDTH + SSD_BC:SSD_WIDTH + 2 * SSD_BC]
        bt = bm.T.astype(BF16)

        hb = hb_ref[...]
        y_off = (jnp.dot(cm.astype(BF16), hb.astype(BF16), preferred_element_type=F32)
                 * _expand_heads(jnp.exp(tot - ecs), SSD_HEADS))
        xd = (xs * _expand_heads(dtc * jnp.exp(ecs), SSD_HEADS)).astype(BF16)
        upd = jnp.dot(bt, xd, preferred_element_type=F32)
        decay_tot = _expand_heads(jnp.broadcast_to(jnp.exp(tot), (8, LANES)), SSD_HEADS)[0:1, :]
        hb_ref[...] = hb * decay_tot + jnp.where(same_group, upd, 0.0)

        y = y_ref[pl.ds(r0, ch), :] + y_off + dskip_ref[...] * xs
        y = y * _silu(z_ref[0, pl.ds(r0, ch), :])
        gw = SSD_WIDTH // SSD_GROUPS
        yn = jnp.concatenate([_rms(y[:, g * gw:(g + 1) * gw]) for g in range(SSD_GROUPS)], axis=-1)
        o_ref[0, pl.ds(r0, ch), :] = (yn * ng_ref[...]).astype(BF16)
        return carry

    lax.fori_loop(0, nc, bwd_chunk, 0)


def _ssd(z, xbc, dt_raw, conv_w, conv_b, dt_bias, a_log, d_skip, norm_g):
    bsz, s, nconv = xbc.shape
    tok = lambda wd: pl.BlockSpec((1, s, wd), lambda b: (b, 0, 0))
    full = lambda a: pl.BlockSpec(a.shape, lambda b: (0, 0))
    return pl.pallas_call(
        _ssd_body,
        grid=(bsz,),
        in_specs=[tok(SSD_WIDTH), tok(nconv), tok(LANES), full(conv_w), full(conv_b), full(dt_bias),
                  full(a_log), full(d_skip), full(norm_g)],
        out_specs=tok(SSD_WIDTH),
        out_shape=jax.ShapeDtypeStruct((bsz, s, SSD_WIDTH), BF16),
        scratch_shapes=[pltpu.VMEM((s + 16, nconv), F32),
                        pltpu.VMEM((s, nconv), F32),
                        pltpu.VMEM((s, LANES), F32),
                        pltpu.VMEM((s, LANES), F32),
                        pltpu.VMEM((s, LANES), F32),
                        pltpu.VMEM((s, SSD_WIDTH), F32),
                        pltpu.VMEM((SSD_BC, SSD_WIDTH), F32),
                        pltpu.VMEM((SSD_BC, SSD_WIDTH), F32)],
        compiler_params=_params("parallel"),
        name="ssd_mixer",
    )(z, xbc, dt_raw, conv_w, conv_b, dt_bias, a_log, d_skip, norm_g)


def _outproj_body(att_ref, ssd_ref, x_ref, g1_ref, sh_ref, sc_ref, ng_ref, w_ref, rw_ref,
                  x1_ref, h2_ref, aff_ref):
    na = att_ref.shape[2]
    mix = (jnp.dot(att_ref[0], w_ref[0:na, :], preferred_element_type=F32)
           + jnp.dot(ssd_ref[0], w_ref[na:, :], preferred_element_type=F32))
    x1 = x_ref[0] + g1_ref[0] * mix
    x1_ref[0] = x1
    h2 = _rms(x1) * ng_ref[...]
    h2 = h2 * (1.0 + sc_ref[0]) + sh_ref[0]
    h2_ref[0] = h2.astype(BF16)
    logits = lax.dot_general(rw_ref[...], h2, (((1,), (1,)), ((), ())), precision=HIGHEST,
                             preferred_element_type=F32)
    e = jnp.exp(logits - jnp.max(logits, axis=0, keepdims=True))
    aff_ref[0] = e / jnp.sum(e, axis=0, keepdims=True)


def _out_proj(att, ssd, x, g1, sh2, sc2, ng, w_out, rw_t, ts):
    bsz, s, d = x.shape
    tok = lambda wd: pl.BlockSpec((1, ts, wd), lambda b, t: (b, t, 0))
    vec = pl.BlockSpec((1, 1, d), lambda b, t: (b, 0, 0))
    full = lambda a: pl.BlockSpec(a.shape, lambda b, t: (0, 0))
    ne = rw_t.shape[0]
    return pl.pallas_call(
        _outproj_body,
        grid=(bsz, s // ts),
        in_specs=[tok(att.shape[2]), tok(ssd.shape[2]), tok(d), vec, vec, vec, full(ng), full(w_out), full(rw_t)],
        out_specs=[tok(d), tok(d), pl.BlockSpec((1, ne, ts), lambda b, t: (b, 0, t))],
        out_shape=[jax.ShapeDtypeStruct((bsz, s, d), F32),
                   jax.ShapeDtypeStruct((bsz, s, d), BF16),
                   jax.ShapeDtypeStruct((bsz, ne, s), F32)],
        compiler_params=_params("parallel", "arbitrary"),
        name="out_proj_router",
    )(att, ssd, x, g1, sh2, sc2, ng, w_out, rw_t)


def _topk_body(aff_ref, pos_ref, post_ref, ut_ref, *, cap):
    ne, s = aff_ref.shape[1], aff_ref.shape[2]

    @pl.when(pl.program_id(0) == 0)
    def _():
        r = lax.broadcasted_iota(I32, (s, s), 0)
        c = lax.broadcasted_iota(I32, (s, s), 1)
        ut_ref[...] = jnp.where(r < c, 1.0, 0.0).astype(BF16)

    bits = pltpu.bitcast(aff_ref[0], I32)

    def count(mask):
        return jnp.sum(jnp.where(mask, 1, 0), axis=-1, keepdims=True)

    thr = jnp.zeros((ne, 1), I32)
    for bit in range(30, -1, -1):
        cand = thr | (1 << bit)
        thr = jnp.where(count(bits >= cand) >= cap, cand, thr)
    gt = bits > thr
    eq = bits == thr
    need = cap - count(gt)
    eq_rank = jnp.dot(jnp.where(eq, 1.0, 0.0).astype(BF16), ut_ref[...], preferred_element_type=F32)
    sel = gt | (eq & (eq_rank < need.astype(F32)))
    slot = jnp.dot(jnp.where(sel, 1.0, 0.0).astype(BF16), ut_ref[...], preferred_element_type=F32)
    pos = jnp.where(sel, slot, -1.0)
    pos_ref[0] = pos.astype(I32)
    padded = jnp.concatenate([pos, jnp.full((LANES - ne, s), -1.0, F32)], axis=0)
    post_ref[0] = padded.T.astype(I32)


def _topk(aff_t, cap):
    bsz, ne, s = aff_t.shape
    return pl.pallas_call(
        functools.partial(_topk_body, cap=cap),
        grid=(bsz,),
        in_specs=[pl.BlockSpec((1, ne, s), lambda b: (b, 0, 0))],
        out_specs=[pl.BlockSpec((1, ne, s), lambda b: (b, 0, 0)),
                   pl.BlockSpec((1, s, LANES), lambda b: (b, 0, 0))],
        out_shape=[jax.ShapeDtypeStruct((bsz, ne, s), I32),
                   jax.ShapeDtypeStruct((bsz, s, LANES), I32)],
        scratch_shapes=[pltpu.VMEM((s, s), BF16)],
        compiler_params=_params("arbitrary"),
        name="expert_topk",
    )(aff_t)


def _gather_body(h2_ref, pos_ref, aff_ref, xg_ref, gc_ref, *, cap):
    ne, s = pos_ref.shape[1], pos_ref.shape[2]
    slot = lax.broadcasted_iota(I32, (cap, s), 0)
    h2 = h2_ref[0]
    for e in range(ne):
        hit = pos_ref[0, e:e + 1, :] == slot
        xg_ref[e] = jnp.dot(jnp.where(hit, 1.0, 0.0).astype(BF16), h2,
                            preferred_element_type=F32).astype(BF16)
        gate = jnp.sum(jnp.where(hit, aff_ref[0, e:e + 1, :], 0.0), axis=-1, keepdims=True)
        gc_ref[e] = jnp.broadcast_to(gate, (cap, LANES))


def _gather(h2, pos, aff_t, cap):
    bsz, s, d = h2.shape
    ne = pos.shape[1]
    return pl.pallas_call(
        functools.partial(_gather_body, cap=cap),
        grid=(bsz,),
        in_specs=[pl.BlockSpec((1, s, d), lambda b: (b, 0, 0)),
                  pl.BlockSpec((1, ne, s), lambda b: (b, 0, 0)),
                  pl.BlockSpec((1, ne, s), lambda b: (b, 0, 0))],
        out_specs=[pl.BlockSpec((ne, cap, d), lambda b: (0, b, 0)),
                   pl.BlockSpec((ne, cap, LANES), lambda b: (0, b, 0))],
        out_shape=[jax.ShapeDtypeStruct((ne, bsz * cap, d), BF16),
                   jax.ShapeDtypeStruct((ne, bsz * cap, LANES), F32)],
        compiler_params=_params("parallel"),
        name="moe_gather",
    )(h2, pos, aff_t)


def _ffn_body(xg_ref, gc_ref, wg_ref, wu_ref, wd_ref, y_ref, acc_ref):
    f = pl.program_id(1)

    @pl.when(f == 0)
    def _():
        acc_ref[...] = jnp.zeros_like(acc_ref)

    xg = xg_ref[0]
    gate = jnp.dot(xg, wg_ref[0].astype(BF16), preferred_element_type=F32)
    up = jnp.dot(xg, wu_ref[0].astype(BF16), preferred_element_type=F32)
    hid = (_silu(gate) * up).astype(BF16)
    acc_ref[...] += jnp.dot(hid, wd_ref[0].astype(BF16), preferred_element_type=F32)

    @pl.when(f == pl.num_programs(1) - 1)
    def _():
        m, d = acc_ref.shape
        gcol = gc_ref[0]
        y_ref[0] = (acc_ref[...] * jnp.concatenate([gcol] * (d // LANES), axis=-1)).astype(BF16)


def _ffn(xg, gc, w_gate, w_up, w_down, tf):
    ne, m, d = xg.shape
    ff = w_gate.shape[2]
    return pl.pallas_call(
        _ffn_body,
        grid=(ne, ff // tf),
        in_specs=[pl.BlockSpec((1, m, d), lambda e, f: (e, 0, 0)),
                  pl.BlockSpec((1, m, LANES), lambda e, f: (e, 0, 0)),
                  pl.BlockSpec((1, d, tf), lambda e, f: (e, 0, f)),
                  pl.BlockSpec((1, d, tf), lambda e, f: (e, 0, f)),
                  pl.BlockSpec((1, tf, d), lambda e, f: (e, f, 0))],
        out_specs=pl.BlockSpec((1, m, d), lambda e, f: (e, 0, 0)),
        out_shape=jax.ShapeDtypeStruct((ne, m, d), BF16),
        scratch_shapes=[pltpu.VMEM((m, d), F32)],
        compiler_params=_params("parallel", "arbitrary"),
        name="moe_ffn",
    )(xg, gc, w_gate, w_up, w_down)


def _scatter_body(post_ref, y_ref, x1_ref, g2_ref, ng_ref, o_ref, *, cap):
    ne = y_ref.shape[0]
    ts = post_ref.shape[1]
    d = y_ref.shape[2]
    slot = lax.broadcasted_iota(I32, (ts, cap), 1)
    post = post_ref[0]
    onehot = jnp.concatenate(
        [jnp.where(post[:, e:e + 1] == slot, 1.0, 0.0).astype(BF16) for e in range(ne)], axis=-1)
    moe = jnp.dot(onehot, y_ref[...].reshape(ne * cap, d), preferred_element_type=F32)
    x2 = x1_ref[0] + g2_ref[0] * moe
    o_ref[0] = _rms(x2) * ng_ref[...]


def _scatter(post, y, x1, g2, ng, cap, ts):
    bsz, s, d = x1.shape
    ne = y.shape[0]
    return pl.pallas_call(
        functools.partial(_scatter_body, cap=cap),
        grid=(bsz, s // ts),
        in_specs=[pl.BlockSpec((1, ts, LANES), lambda b, t: (b, t, 0)),
                  pl.BlockSpec((ne, cap, d), lambda b, t: (0, b, 0)),
                  pl.BlockSpec((1, ts, d), lambda b, t: (b, t, 0)),
                  pl.BlockSpec((1, 1, d), lambda b, t: (b, 0, 0)),
                  pl.BlockSpec((1, d), lambda b, t: (0, 0))],
        out_specs=pl.BlockSpec((1, ts, d), lambda b, t: (b, t, 0)),
        out_shape=jax.ShapeDtypeStruct((bsz, s, d), F32),
        compiler_params=_params("parallel", "arbitrary"),
        name="moe_scatter_final",
    )(post, y, x1, g2, ng)


def _pad_lanes(a, n=LANES):
    return jnp.pad(a, [(0, 0)] * (a.ndim - 1) + [(0, n - a.shape[-1])])


def kernel(x, c, ada_w, ada_b, norm_mix_g, norm_ffn_g, norm_final_g, w_in, lambda_q1, lambda_k1, lambda_q2, lambda_k2, attn_subln_g, rel_bias_table, conv_w, conv_b, dt_bias_f, dt_bias_b, A_log_f, A_log_b, D_skip, ssm_norm_g, w_out, router_w, w_gate, w_up, w_down):
    bsz, s, d = x.shape
    depth = ada_w.shape[0]
    ne = router_w.shape[2]
    cap = CAPACITY_FACTOR * s // ne
    att_w = ATT_HEADS * ATT_V_DIM
    nconv = conv_w.shape[3]
    splits = (att_w, att_w, att_w, SSD_WIDTH, nconv, LANES)
    offs = [0]
    for wd in splits:
        offs.append(offs[-1] + wd)
    row = lambda a: a.reshape(1, -1)

    assert depth == 1, "the final RMSNorm is fused into the (single) layer's scatter kernel"
    for l in range(depth):
        lam_init = 0.8 - 0.6 * math.exp(-0.3 * l)
        mod = _ada_mod(c, ada_w[l], ada_b[l])
        sh1, sc1, g1, sh2, sc2, g2 = [m.reshape(bsz, 1, d) for m in jnp.split(mod, 6, axis=-1)]

        w_pad = _pad_lanes(w_in[l], offs[-1]).astype(BF16)
        q, k, v, z, xbc, dt_raw = _in_proj(x, sh1, sc1, row(norm_mix_g[l]), w_pad, offs, ts=512)

        att = _attention(q, k, v, rel_bias_table, row(lambda_q1[l]), row(lambda_k1[l]),
                         row(lambda_q2[l]), row(lambda_k2[l]), row(attn_subln_g[l]), lam_init, tq=256)

        dt_bias = _pad_lanes(jnp.concatenate([dt_bias_f[l], dt_bias_b[l]]).reshape(1, -1))
        a_log = _pad_lanes(jnp.concatenate([A_log_f[l], A_log_b[l]]).reshape(1, -1))
        d_skip = jnp.repeat(D_skip[l], SSD_HEADDIM).reshape(1, -1)
        ssd = _ssd(z, xbc, dt_raw, conv_w[l].reshape(SSD_CONV, nconv), row(conv_b[l]), dt_bias, a_log,
                   d_skip, row(ssm_norm_g[l]))

        x1, h2, aff_t = _out_proj(att, ssd, x, g1, sh2, sc2, row(norm_ffn_g[l]), w_out[l].astype(BF16),
                                  router_w[l].T, ts=512)
        pos, pos_t = _topk(aff_t, cap)
        xg, gc = _gather(h2, pos, aff_t, cap)
        y = _ffn(xg, gc, w_gate[l], w_up[l], w_down[l], tf=256)
        x = _scatter(pos_t, y, x1, g2, row(norm_final_g), cap, ts=512)
    return x
```

```python
import functools
import math

import jax
import jax.numpy as jnp
from jax import lax
from jax.experimental import pallas as pl
from jax.experimental.pallas import tpu as pltpu

F32 = jnp.float32
BF16 = jnp.bfloat16
I32 = jnp.int32
HIGHEST = lax.Precision.HIGHEST

ATT_HEADS = 4
ATT_QK_DIM = 64
ATT_V_DIM = 128
N_BUCKETS = 32
SSD_HEADS = 8
SSD_HEADDIM = 64
SSD_GROUPS = 2
SSD_STATE = 64
SSD_CONV = 5
SSD_CHUNK = 128
SSD_WIDTH = SSD_HEADS * SSD_HEADDIM
SSD_BC = SSD_GROUPS * SSD_STATE
N_EXPERTS = 16
CAPACITY_FACTOR = 2
EPS = 1e-6
LANES = 128
NEG_BIG = -1e30
LOG2E = math.log2(math.e)

VMEM_LIMIT = 56 * 1024 * 1024


def _params(*semantics):
    return pltpu.CompilerParams(dimension_semantics=semantics, vmem_limit_bytes=VMEM_LIMIT)


def _silu(v):
    return v * jax.nn.sigmoid(v)


def _rms(v, eps=EPS):
    return v * lax.rsqrt(jnp.mean(v * v, axis=-1, keepdims=True) + eps)


def _mod_body(c_ref, w_ref, b_ref, o_ref):
    o_ref[...] = jnp.dot(_silu(c_ref[...]), w_ref[...], precision=HIGHEST,
                         preferred_element_type=F32) + b_ref[...]


def _ada_mod(c, w, b):
    bsz, d = c.shape
    n = w.shape[1]
    tn = n // 4
    return pl.pallas_call(
        _mod_body,
        grid=(n // tn,),
        in_specs=[pl.BlockSpec((bsz, d), lambda j: (0, 0)),
                  pl.BlockSpec((d, tn), lambda j: (0, j)),
                  pl.BlockSpec((1, tn), lambda j: (0, j))],
        out_specs=pl.BlockSpec((bsz, tn), lambda j: (0, j)),
        out_shape=jax.ShapeDtypeStruct((bsz, n), F32),
        compiler_params=_params("arbitrary"),
        name="ada_mod",
    )(c, w, b.reshape(1, n))


def _inproj_body(x_ref, sh_ref, sc_ref, g_ref, w_ref, wvt_ref, q_ref, k_ref, vt_ref, z_ref, xbc_ref, dt_ref,
                 *, offs, qscale):
    h = _rms(x_ref[0]) * g_ref[...]
    h = (h * (1.0 + sc_ref[0]) + sh_ref[0]).astype(BF16)

    def proj(i):
        return jnp.dot(h, w_ref[:, offs[i]:offs[i + 1]], preferred_element_type=F32)

    q_ref[0] = (proj(0) * qscale).astype(BF16)
    k_ref[0] = proj(1).astype(BF16)
    vt_ref[0, 0] = lax.dot_general(wvt_ref[...], h, (((1,), (1,)), ((), ())),
                                   preferred_element_type=F32).astype(BF16)
    z_ref[0] = proj(2)
    xbc_ref[0] = proj(3)
    dt_ref[0] = proj(4)


def _in_proj(x, sh, sc, g, w_pad, w_vt, offs, ts):
    bsz, s, d = x.shape
    widths = [offs[i + 1] - offs[i] for i in range(5)]
    dts = [BF16, BF16, F32, F32, F32]
    nv = w_vt.shape[0]
    tok = lambda wd: pl.BlockSpec((1, ts, wd), lambda b, t: (b, t, 0))
    vec = pl.BlockSpec((1, 1, d), lambda b, t: (b, 0, 0))
    return pl.pallas_call(
        functools.partial(_inproj_body, offs=tuple(offs), qscale=ATT_QK_DIM ** -0.5 * LOG2E),
        grid=(bsz, s // ts),
        in_specs=[tok(d), vec, vec,
                  pl.BlockSpec((1, d), lambda b, t: (0, 0)),
                  pl.BlockSpec(w_pad.shape, lambda b, t: (0, 0)),
                  pl.BlockSpec(w_vt.shape, lambda b, t: (0, 0))],
        out_specs=[tok(widths[0]), tok(widths[1]), pl.BlockSpec((1, 1, nv, ts), lambda b, t: (b, t, 0, 0))]
                  + [tok(wd) for wd in widths[2:]],
        out_shape=[jax.ShapeDtypeStruct((bsz, s, widths[0]), BF16), jax.ShapeDtypeStruct((bsz, s, widths[1]), BF16),
                   jax.ShapeDtypeStruct((bsz, s // ts, nv, ts), BF16)]
                  + [jax.ShapeDtypeStruct((bsz, s, wd), dt) for wd, dt in zip(widths[2:], dts[2:])],
        compiler_params=_params("parallel", "arbitrary"),
        name="in_proj",
    )(x, sh, sc, g, w_pad, w_vt)


def _t5_bucket(rel):
    n = jnp.abs(rel)
    large = jnp.full(rel.shape, 8, I32)
    for t in (12, 16, 23, 32, 46, 64, 91):
        large = large + jnp.where(n >= t, 1, 0)
    return jnp.where(rel > 0, 16, 0) + jnp.where(n < 8, n, large)


def _attn_body(tbl_ref, lq1_ref, lk1_ref, lq2_ref, lk2_ref, g_ref, q_ref, k_ref, vt_ref, o_ref,
               bias_ref, s_ref, e_ref, acc_ref, *, lam_init, tk):
    h = pl.program_id(0)
    qi = pl.program_id(1)
    s, tq = bias_ref.shape
    ring = 1024
    assert tk + tq - 1 <= ring

    @pl.when(pl.program_id(2) == 0)
    def _():
        j = lax.broadcasted_iota(I32, (8, ring), 1)
        for kt in range(s // tk):
            bucket = _t5_bucket((kt * tk + tk - 1) - qi * tq - j)
            w = jnp.zeros((8, ring), F32)
            for bb in range(N_BUCKETS):
                w = jnp.where(bucket == bb, tbl_ref[bb, h] * LOG2E, w)
            big = jnp.broadcast_to(w[0:1], (tk, ring))
            rolled = pltpu.roll(big, ring - (tk - 1), axis=1, stride=1, stride_axis=0)
            bias_ref[kt * tk:(kt + 1) * tk, :] = rolled[:, :tq]

    lam = (jnp.exp(jnp.sum(lq1_ref[...] * lk1_ref[...], axis=-1, keepdims=True))
           - jnp.exp(jnp.sum(lq2_ref[...] * lk2_ref[...], axis=-1, keepdims=True)) + lam_init)
    q = q_ref[0]
    lane = lax.broadcasted_iota(I32, q.shape, 1)
    qms = [jnp.where((lane >= m * ATT_QK_DIM) & (lane < (m + 1) * ATT_QK_DIM), q, jnp.zeros_like(q))
           for m in range(2)]
    qw = 2 * LANES
    nqc = tq // qw
    nkb, kb = vt_ref.shape[1], vt_ref.shape[3]
    ones_rows = 16

    def logits_block(m, blk, mx):
        mx = list(mx)
        for t in range(kb // tk):
            r0 = pl.multiple_of(blk * kb + t * tk, tk)
            k_tile = k_ref[0, pl.ds(r0, tk), :]
            for c in range(nqc):
                cols = slice(c * qw, (c + 1) * qw)
                logit = lax.dot_general(k_tile, qms[m][cols, :], (((1,), (1,)), ((), ())),
                                        preferred_element_type=F32) + bias_ref[pl.ds(r0, tk), cols]
                s_ref[m, pl.ds(r0, tk), cols] = logit
                mx[c] = jnp.maximum(mx[c], jnp.max(logit.reshape(tk // 8, 8, qw), axis=0))
        return tuple(mx)

    def exp_block(m, blk, row_max):
        for t in range(kb // tk):
            r0 = pl.multiple_of(blk * kb + t * tk, tk)
            for c in range(nqc):
                cols = slice(c * qw, (c + 1) * qw)
                e_ref[m, pl.ds(r0, tk), cols] = jnp.exp2(s_ref[m, pl.ds(r0, tk), cols] - row_max[c]).astype(BF16)

    def pv_block(m, blk):
        vt_ext = jnp.concatenate([vt_ref[0, blk], jnp.ones((ones_rows, kb), BF16)], axis=0)
        r0 = pl.multiple_of(blk * kb, kb)
        acc_ref[m] += jnp.dot(vt_ext, e_ref[m, pl.ds(r0, kb), :], preferred_element_type=F32)

    def finish_max(mx):
        return [jnp.max(m8, axis=0, keepdims=True) for m8 in mx]

    def unrolled(n, body, carry):
        for i in range(n):
            carry = body(i, carry)
        return carry

    mx_init = tuple(jnp.full((8, qw), NEG_BIG, F32) for _ in range(nqc))
    acc_ref[...] = jnp.zeros_like(acc_ref)
    max0 = finish_max(unrolled(nkb, lambda blk, mx: logits_block(0, blk, mx), mx_init))

    def phase2(blk, mx):
        exp_block(0, blk, max0)
        return logits_block(1, blk, mx)

    max1 = finish_max(unrolled(nkb, phase2, mx_init))

    def phase3(blk, carry):
        exp_block(1, blk, max1)
        pv_block(0, blk)
        return carry

    unrolled(nkb, phase3, 0)

    def phase4(blk, carry):
        pv_block(1, blk)
        return carry

    unrolled(nkb, phase4, 0)
    outs = [acc_ref[m, :ATT_V_DIM, :] / acc_ref[m, ATT_V_DIM:ATT_V_DIM + 1, :] for m in range(2)]
    att = outs[0] - lam * outs[1]
    att = att * lax.rsqrt(jnp.mean(att * att, axis=0, keepdims=True) + EPS)
    o_ref[0] = (att * (g_ref[...] * (1.0 - lam_init))).T.astype(BF16)


def _attention(q, k, v_t, tbl, lq1, lk1, lq2, lk2, subln_g, lam_init, tq):
    bsz, s, _ = q.shape
    nkb, kb = v_t.shape[1], v_t.shape[3]
    small = lambda n: pl.BlockSpec((1, n), lambda h, i, b: (0, 0))
    return pl.pallas_call(
        functools.partial(_attn_body, lam_init=lam_init, tk=LANES),
        grid=(ATT_HEADS, s // tq, bsz),
        in_specs=[pl.BlockSpec(memory_space=pltpu.SMEM),
                  small(ATT_QK_DIM), small(ATT_QK_DIM), small(ATT_QK_DIM), small(ATT_QK_DIM),
                  pl.BlockSpec((ATT_V_DIM, 1), lambda h, i, b: (0, 0)),
                  pl.BlockSpec((1, tq, ATT_V_DIM), lambda h, i, b: (b, i, h)),
                  pl.BlockSpec((1, s, ATT_V_DIM), lambda h, i, b: (b, 0, h)),
                  pl.BlockSpec((1, nkb, ATT_V_DIM, kb), lambda h, i, b: (b, 0, h, 0))],
        out_specs=pl.BlockSpec((1, tq, ATT_V_DIM), lambda h, i, b: (b, i, h)),
        out_shape=jax.ShapeDtypeStruct((bsz, s, ATT_HEADS * ATT_V_DIM), BF16),
        scratch_shapes=[pltpu.VMEM((s, tq), F32),
                        pltpu.VMEM((2, s, tq), F32),
                        pltpu.VMEM((2, s, tq), BF16),
                        pltpu.VMEM((2, ATT_V_DIM + 16, tq), F32)],
        compiler_params=_params("parallel", "parallel", "arbitrary"),
        name="diff_attention",
    )(tbl, lq1, lk1, lq2, lk2, subln_g.reshape(ATT_V_DIM, 1), q, k, v_t)


def _expand_heads(cols, off):
    r = cols.shape[0]
    lane_head = lax.broadcasted_iota(I32, (r, SSD_WIDTH), 1) // SSD_HEADDIM
    out = jnp.zeros((r, SSD_WIDTH), F32)
    for hh in range(SSD_HEADS):
        out = jnp.where(lane_head == hh, cols[:, off + hh:off + hh + 1], out)
    return out


def _ssd_body(z_ref, xbc_ref, dtr_ref, cw_ref, cb_ref, dtb_ref, alog_ref, dskip_ref, ng_ref, o_ref,
              xp_ref, xc_ref, dt_ref, da_ref, cs_ref, y_ref, hf_ref, hb_ref):
    s = xbc_ref.shape[1]
    ch = SSD_CHUNK
    nc = s // ch
    pad = 8
    nconv = xbc_ref.shape[2]

    xp_ref[0:pad, :] = jnp.zeros((pad, nconv), F32)
    xp_ref[pad + s:pad + s + pad, :] = jnp.zeros((pad, nconv), F32)
    xp_ref[pad:pad + s, :] = xbc_ref[0]

    def conv_chunk(c, carry):
        r0 = pl.multiple_of(c * ch, ch)
        win = xp_ref[pl.ds(r0, ch + 2 * pad), :]
        acc = jnp.broadcast_to(cb_ref[...], (ch, nconv))
        for j in range(SSD_CONV):
            lo = pad + j - SSD_CONV // 2
            acc = acc + cw_ref[j:j + 1, :] * win[lo:lo + ch, :]
        xc_ref[pl.ds(r0, ch), :] = _silu(acc)
        return carry

    lax.fori_loop(0, nc, conv_chunk, 0)

    lane = lax.broadcasted_iota(I32, (s, LANES), 1)
    pre = dtr_ref[0] + dtb_ref[...]
    dt = jnp.maximum(pre, 0.0) + jnp.log1p(jnp.exp(-jnp.abs(pre)))
    dt = jnp.where(lane < 2 * SSD_HEADS, dt, 0.0)
    dt_ref[...] = dt
    da_ref[...] = dt * (-jnp.exp(alog_ref[...]))

    row = lax.broadcasted_iota(I32, (ch, ch), 0)
    col = lax.broadcasted_iota(I32, (ch, ch), 1)
    tri = jnp.where(col <= row, 1.0, 0.0).astype(F32)
    lower = col <= row
    upper = col >= row
    lane_c = lax.broadcasted_iota(I32, (ch, LANES), 1)
    st_rows = lax.broadcasted_iota(I32, (SSD_BC, SSD_WIDTH), 0) // SSD_STATE
    st_cols = lax.broadcasted_iota(I32, (SSD_BC, SSD_WIDTH), 1) // (SSD_WIDTH // SSD_GROUPS)
    same_group = st_rows == st_cols

    hf_ref[...] = jnp.zeros_like(hf_ref)
    hb_ref[...] = jnp.zeros_like(hb_ref)

    def fwd_chunk(c, carry):
        r0 = pl.multiple_of(c * ch, ch)
        da = da_ref[pl.ds(r0, ch), :]
        dtc = dt_ref[pl.ds(r0, ch), :]
        cs = jnp.dot(tri, da, precision=HIGHEST, preferred_element_type=F32)
        cs_ref[pl.ds(r0, ch), :] = cs
        tot = cs[ch - 1:ch, :]
        ecs = cs - da
        cs_t = cs.T
        ecs_t = ecs.T
        dt_t = dtc.T
        xs = xc_ref[pl.ds(r0, ch), 0:SSD_WIDTH]
        bm = xc_ref[pl.ds(r0, ch), SSD_WIDTH:SSD_WIDTH + SSD_BC]
        cm = xc_ref[pl.ds(r0, ch), SSD_WIDTH + SSD_BC:SSD_WIDTH + 2 * SSD_BC]
        bt = bm.T.astype(BF16)
        cmb = cm.astype(BF16)
        xsb = xs.astype(BF16)

        g_mats = []
        for g in range(SSD_GROUPS):
            cg = jnp.where(lane_c // SSD_STATE == g, cmb, jnp.zeros_like(cmb))
            g_mats.append(jnp.dot(cg, bt, preferred_element_type=F32))
        for pair in range(SSD_HEADS // 2):
            xpair = xsb[:, pair * LANES:(pair + 1) * LANES]
            ypair = jnp.zeros((ch, LANES), F32)
            for sub in range(2):
                hh = 2 * pair + sub
                g = hh // (SSD_HEADS // SSD_GROUPS)
                d_f = cs[:, hh:hh + 1] - cs_t[hh:hh + 1, :]
                d_b = ecs_t[SSD_HEADS + hh:SSD_HEADS + hh + 1, :] - ecs[:, SSD_HEADS + hh:SSD_HEADS + hh + 1]
                decay = jnp.exp(jnp.where(lower, d_f, d_b))
                wgt = (jnp.where(lower, dt_t[hh:hh + 1, :], 0.0)
                       + jnp.where(upper, dt_t[SSD_HEADS + hh:SSD_HEADS + hh + 1, :], 0.0))
                m = (g_mats[g] * decay * wgt).astype(BF16)
                xh = jnp.where(lane_c // SSD_HEADDIM == sub, xpair, jnp.zeros_like(xpair))
                ypair = ypair + jnp.dot(m, xh, preferred_element_type=F32)
            y_ref[pl.ds(r0, ch), pair * LANES:(pair + 1) * LANES] = ypair

        hf = hf_ref[...]
        y_off = jnp.dot(cmb, hf.astype(BF16), preferred_element_type=F32) * _expand_heads(jnp.exp(cs), 0)
        y_ref[pl.ds(r0, ch), :] = y_ref[pl.ds(r0, ch), :] + y_off
        xd = (xs * _expand_heads(dtc * jnp.exp(tot - cs), 0)).astype(BF16)
        upd = jnp.dot(bt, xd, preferred_element_type=F32)
        decay_tot = _expand_heads(jnp.broadcast_to(jnp.exp(tot), (8, LANES)), 0)[0:1, :]
        hf_ref[...] = hf * decay_tot + jnp.where(same_group, upd, 0.0)
        return carry

    lax.fori_loop(0, nc, fwd_chunk, 0)

    def bwd_chunk(i, carry):
        c = nc - 1 - i
        r0 = pl.multiple_of(c * ch, ch)
        da = da_ref[pl.ds(r0, ch), :]
        dtc = dt_ref[pl.ds(r0, ch), :]
        cs = cs_ref[pl.ds(r0, ch), :]
        tot = cs[ch - 1:ch, :]
        ecs = cs - da
        xs = xc_ref[pl.ds(r0, ch), 0:SSD_WIDTH]
        bm = xc_ref[pl.ds(r0, ch), SSD_WIDTH:SSD_WIDTH + SSD_BC]
        cm = xc_ref[pl.ds(r0, ch), SSD_WIDTH + SSD_BC:SSD_WIDTH + 2 * SSD_BC]
        bt = bm.T.astype(BF16)

        hb = hb_ref[...]
        y_off = (jnp.dot(cm.astype(BF16), hb.astype(BF16), preferred_element_type=F32)
                 * _expand_heads(jnp.exp(tot - ecs), SSD_HEADS))
        xd = (xs * _expand_heads(dtc * jnp.exp(ecs), SSD_HEADS)).astype(BF16)
        upd = jnp.dot(bt, xd, preferred_element_type=F32)
        decay_tot = _expand_heads(jnp.broadcast_to(jnp.exp(tot), (8, LANES)), SSD_HEADS)[0:1, :]
        hb_ref[...] = hb * decay_tot + jnp.where(same_group, upd, 0.0)

        y = y_ref[pl.ds(r0, ch), :] + y_off + dskip_ref[...] * xs
        y = y * _silu(z_ref[0, pl.ds(r0, ch), :])
        gw = SSD_WIDTH // SSD_GROUPS
        yn = jnp.concatenate([_rms(y[:, g * gw:(g + 1) * gw]) for g in range(SSD_GROUPS)], axis=-1)
        o_ref[0, pl.ds(r0, ch), :] = (yn * ng_ref[...]).astype(BF16)
        return carry

    lax.fori_loop(0, nc, bwd_chunk, 0)


def _ssd(z, xbc, dt_raw, conv_w, conv_b, dt_bias, a_log, d_skip, norm_g):
    bsz, s, nconv = xbc.shape
    tok = lambda wd: pl.BlockSpec((1, s, wd), lambda b: (b, 0, 0))
    full = lambda a: pl.BlockSpec(a.shape, lambda b: (0, 0))
    return pl.pallas_call(
        _ssd_body,
        grid=(bsz,),
        in_specs=[tok(SSD_WIDTH), tok(nconv), tok(LANES), full(conv_w), full(conv_b), full(dt_bias),
                  full(a_log), full(d_skip), full(norm_g)],
        out_specs=tok(SSD_WIDTH),
        out_shape=jax.ShapeDtypeStruct((bsz, s, SSD_WIDTH), BF16),
        scratch_shapes=[pltpu.VMEM((s + 16, nconv), F32),
                        pltpu.VMEM((s, nconv), F32),
                        pltpu.VMEM((s, LANES), F32),
                        pltpu.VMEM((s, LANES), F32),
                        pltpu.VMEM((s, LANES), F32),
                        pltpu.VMEM((s, SSD_WIDTH), F32),
                        pltpu.VMEM((SSD_BC, SSD_WIDTH), F32),
                        pltpu.VMEM((SSD_BC, SSD_WIDTH), F32)],
        compiler_params=_params("parallel"),
        name="ssd_mixer",
    )(z, xbc, dt_raw, conv_w, conv_b, dt_bias, a_log, d_skip, norm_g)


def _outproj_body(att_ref, ssd_ref, x_ref, g1_ref, sh_ref, sc_ref, ng_ref, w_ref, rw_ref,
                  x1_ref, h2_ref, aff_ref):
    na = att_ref.shape[2]
    mix = (jnp.dot(att_ref[0], w_ref[0:na, :], preferred_element_type=F32)
           + jnp.dot(ssd_ref[0], w_ref[na:, :], preferred_element_type=F32))
    x1 = x_ref[0] + g1_ref[0] * mix
    x1_ref[0] = x1
    h2 = _rms(x1) * ng_ref[...]
    h2 = h2 * (1.0 + sc_ref[0]) + sh_ref[0]
    h2_hi = h2.astype(BF16)
    h2_ref[0] = h2_hi
    h2_lo = (h2 - h2_hi.astype(F32)).astype(BF16)
    nt = lambda a, b: lax.dot_general(a, b, (((1,), (1,)), ((), ())), preferred_element_type=F32)
    logits = nt(rw_ref[0], h2_hi) + (nt(rw_ref[0], h2_lo) + nt(rw_ref[1], h2_hi))
    e = jnp.exp(logits - jnp.max(logits, axis=0, keepdims=True))
    aff_ref[0] = e / jnp.sum(e, axis=0, keepdims=True)


def _out_proj(att, ssd, x, g1, sh2, sc2, ng, w_out, rw_t, ts):
    bsz, s, d = x.shape
    tok = lambda wd: pl.BlockSpec((1, ts, wd), lambda b, t: (b, t, 0))
    vec = pl.BlockSpec((1, 1, d), lambda b, t: (b, 0, 0))
    full = lambda a: pl.BlockSpec(a.shape, lambda b, t: (0, 0))
    ne = rw_t.shape[1]
    return pl.pallas_call(
        _outproj_body,
        grid=(bsz, s // ts),
        in_specs=[tok(att.shape[2]), tok(ssd.shape[2]), tok(d), vec, vec, vec, full(ng), full(w_out),
                  pl.BlockSpec(rw_t.shape, lambda b, t: (0, 0, 0))],
        out_specs=[tok(d), tok(d), pl.BlockSpec((1, ne, ts), lambda b, t: (b, 0, t))],
        out_shape=[jax.ShapeDtypeStruct((bsz, s, d), F32),
                   jax.ShapeDtypeStruct((bsz, s, d), BF16),
                   jax.ShapeDtypeStruct((bsz, ne, s), F32)],
        compiler_params=_params("parallel", "arbitrary"),
        name="out_proj_router",
    )(att, ssd, x, g1, sh2, sc2, ng, w_out, rw_t)


def _topk_body(aff_ref, pos_ref, post_ref, ut_ref, *, cap):
    ne, s = aff_ref.shape[1], aff_ref.shape[2]

    @pl.when(pl.program_id(0) == 0)
    def _():
        r = lax.broadcasted_iota(I32, (s, s), 0)
        c = lax.broadcasted_iota(I32, (s, s), 1)
        ut_ref[...] = jnp.where(r < c, 1.0, 0.0).astype(BF16)

    bits = pltpu.bitcast(aff_ref[0], I32)

    def count(mask):
        return jnp.sum(jnp.where(mask, 1, 0), axis=-1, keepdims=True)

    thr = jnp.zeros((ne, 1), I32)
    for bit in range(30, -1, -1):
        cand = thr | (1 << bit)
        thr = jnp.where(count(bits >= cand) >= cap, cand, thr)
    gt = bits > thr
    eq = bits == thr
    need = cap - count(gt)
    eq_rank = jnp.dot(jnp.where(eq, 1.0, 0.0).astype(BF16), ut_ref[...], preferred_element_type=F32)
    sel = gt | (eq & (eq_rank < need.astype(F32)))
    slot = jnp.dot(jnp.where(sel, 1.0, 0.0).astype(BF16), ut_ref[...], preferred_element_type=F32)
    pos = jnp.where(sel, slot, -1.0)
    pos_ref[0] = pos.astype(I32)
    padded = jnp.concatenate([pos, jnp.full((LANES - ne, s), -1.0, F32)], axis=0)
    post_ref[0] = padded.T.astype(I32)


def _topk(aff_t, cap):
    bsz, ne, s = aff_t.shape
    return pl.pallas_call(
        functools.partial(_topk_body, cap=cap),
        grid=(bsz,),
        in_specs=[pl.BlockSpec((1, ne, s), lambda b: (b, 0, 0))],
        out_specs=[pl.BlockSpec((1, ne, s), lambda b: (b, 0, 0)),
                   pl.BlockSpec((1, s, LANES), lambda b: (b, 0, 0))],
        out_shape=[jax.ShapeDtypeStruct((bsz, ne, s), I32),
                   jax.ShapeDtypeStruct((bsz, s, LANES), I32)],
        scratch_shapes=[pltpu.VMEM((s, s), BF16)],
        compiler_params=_params("arbitrary"),
        name="expert_topk",
    )(aff_t)


def _gather_body(h2_ref, pos_ref, aff_ref, xg_ref, gc_ref, *, cap):
    ne, s = pos_ref.shape[1], pos_ref.shape[2]
    slot = lax.broadcasted_iota(I32, (cap, s), 0)
    h2 = h2_ref[0]
    for e in range(ne):
        hit = pos_ref[0, e:e + 1, :] == slot
        xg_ref[e] = jnp.dot(jnp.where(hit, 1.0, 0.0).astype(BF16), h2,
                            preferred_element_type=F32).astype(BF16)
        gate = jnp.sum(jnp.where(hit, aff_ref[0, e:e + 1, :], 0.0), axis=-1, keepdims=True)
        gc_ref[e] = jnp.broadcast_to(gate, (cap, LANES))


def _gather(h2, pos, aff_t, cap):
    bsz, s, d = h2.shape
    ne = pos.shape[1]
    return pl.pallas_call(
        functools.partial(_gather_body, cap=cap),
        grid=(bsz,),
        in_specs=[pl.BlockSpec((1, s, d), lambda b: (b, 0, 0)),
                  pl.BlockSpec((1, ne, s), lambda b: (b, 0, 0)),
                  pl.BlockSpec((1, ne, s), lambda b: (b, 0, 0))],
        out_specs=[pl.BlockSpec((ne, cap, d), lambda b: (0, b, 0)),
                   pl.BlockSpec((ne, cap, LANES), lambda b: (0, b, 0))],
        out_shape=[jax.ShapeDtypeStruct((ne, bsz * cap, d), BF16),
                   jax.ShapeDtypeStruct((ne, bsz * cap, LANES), F32)],
        compiler_params=_params("parallel"),
        name="moe_gather",
    )(h2, pos, aff_t)


def _ffn_body(xg_ref, gc_ref, wg_ref, wu_ref, wd_ref, y_ref, acc_ref):
    f = pl.program_id(1)

    @pl.when(f == 0)
    def _():
        acc_ref[...] = jnp.zeros_like(acc_ref)

    wg = wg_ref[0].astype(BF16)
    wu = wu_ref[0].astype(BF16)
    wd = wd_ref[0].astype(BF16)
    half = xg_ref.shape[1] // 2
    for r in range(2):
        rows = slice(r * half, (r + 1) * half)
        xg = xg_ref[0, rows, :]
        gate = jnp.dot(xg, wg, preferred_element_type=F32)
        up = jnp.dot(xg, wu, preferred_element_type=F32)
        hid = (_silu(gate) * up).astype(BF16)
        acc_ref[rows, :] += jnp.dot(hid, wd, preferred_element_type=F32)

    @pl.when(f == pl.num_programs(1) - 1)
    def _():
        m, d = acc_ref.shape
        gcol = gc_ref[0]
        y_ref[0] = (acc_ref[...] * jnp.concatenate([gcol] * (d // LANES), axis=-1)).astype(BF16)


def _ffn(xg, gc, w_gate, w_up, w_down, tf):
    ne, m, d = xg.shape
    ff = w_gate.shape[2]
    return pl.pallas_call(
        _ffn_body,
        grid=(ne, ff // tf),
        in_specs=[pl.BlockSpec((1, m, d), lambda e, f: (e, 0, 0)),
                  pl.BlockSpec((1, m, LANES), lambda e, f: (e, 0, 0)),
                  pl.BlockSpec((1, d, tf), lambda e, f: (e, 0, f)),
                  pl.BlockSpec((1, d, tf), lambda e, f: (e, 0, f)),
                  pl.BlockSpec((1, tf, d), lambda e, f: (e, f, 0))],
        out_specs=pl.BlockSpec((1, m, d), lambda e, f: (e, 0, 0)),
        out_shape=jax.ShapeDtypeStruct((ne, m, d), BF16),
        scratch_shapes=[pltpu.VMEM((m, d), F32)],
        compiler_params=_params("parallel", "arbitrary"),
        name="moe_ffn",
    )(xg, gc, w_gate, w_up, w_down)


def _scatter_body(post_ref, y_ref, x1_ref, g2_ref, ng_ref, o_ref, *, cap):
    ne = y_ref.shape[0]
    ts = post_ref.shape[1]
    d = y_ref.shape[2]
    slot = lax.broadcasted_iota(I32, (ts, cap), 1)
    post = post_ref[0]
    onehot = jnp.concatenate(
        [jnp.where(post[:, e:e + 1] == slot, 1.0, 0.0).astype(BF16) for e in range(ne)], axis=-1)
    moe = jnp.dot(onehot, y_ref[...].reshape(ne * cap, d), preferred_element_type=F32)
    x2 = x1_ref[0] + g2_ref[0] * moe
    o_ref[0] = _rms(x2) * ng_ref[...]


def _scatter(post, y, x1, g2, ng, cap, ts):
    bsz, s, d = x1.shape
    ne = y.shape[0]
    return pl.pallas_call(
        functools.partial(_scatter_body, cap=cap),
        grid=(bsz, s // ts),
        in_specs=[pl.BlockSpec((1, ts, LANES), lambda b, t: (b, t, 0)),
                  pl.BlockSpec((ne, cap, d), lambda b, t: (0, b, 0)),
                  pl.BlockSpec((1, ts, d), lambda b, t: (b, t, 0)),
                  pl.BlockSpec((1, 1, d), lambda b, t: (b, 0, 0)),
                  pl.BlockSpec((1, d), lambda b, t: (0, 0))],
        out_specs=pl.BlockSpec((1, ts, d), lambda b, t: (b, t, 0)),
        out_shape=jax.ShapeDtypeStruct((bsz, s, d), F32),
        compiler_params=_params("parallel", "arbitrary"),
        name="moe_scatter_final",
    )(post, y, x1, g2, ng)


def _pad_lanes(a, n=LANES):
    return jnp.pad(a, [(0, 0)] * (a.ndim - 1) + [(0, n - a.shape[-1])])


def kernel(x, c, ada_w, ada_b, norm_mix_g, norm_ffn_g, norm_final_g, w_in, lambda_q1, lambda_k1, lambda_q2, lambda_k2, attn_subln_g, rel_bias_table, conv_w, conv_b, dt_bias_f, dt_bias_b, A_log_f, A_log_b, D_skip, ssm_norm_g, w_out, router_w, w_gate, w_up, w_down):
    bsz, s, d = x.shape
    depth = ada_w.shape[0]
    ne = router_w.shape[2]
    cap = CAPACITY_FACTOR * s // ne
    att_w = ATT_HEADS * ATT_V_DIM
    nconv = conv_w.shape[3]
    splits = (att_w, att_w, SSD_WIDTH, nconv, LANES)
    offs = [0]
    for wd in splits:
        offs.append(offs[-1] + wd)
    row = lambda a: a.reshape(1, -1)

    assert depth == 1, "the final RMSNorm is fused into the (single) layer's scatter kernel"
    for l in range(depth):
        lam_init = 0.8 - 0.6 * math.exp(-0.3 * l)
        mod = _ada_mod(c, ada_w[l], ada_b[l])
        sh1, sc1, g1, sh2, sc2, g2 = [m.reshape(bsz, 1, d) for m in jnp.split(mod, 6, axis=-1)]

        w_l = w_in[l]
        w_pad = _pad_lanes(jnp.concatenate([w_l[:, :2 * att_w], w_l[:, 3 * att_w:]], axis=1), offs[-1]).astype(BF16)
        w_vt = w_l[:, 2 * att_w:3 * att_w].T.astype(BF16)
        q, k, v_t, z, xbc, dt_raw = _in_proj(x, sh1, sc1, row(norm_mix_g[l]), w_pad, w_vt, offs, ts=512)

        att = _attention(q, k, v_t, rel_bias_table, row(lambda_q1[l]), row(lambda_k1[l]),
                         row(lambda_q2[l]), row(lambda_k2[l]), row(attn_subln_g[l]), lam_init, tq=512)

        dt_bias = _pad_lanes(jnp.concatenate([dt_bias_f[l], dt_bias_b[l]]).reshape(1, -1))
        a_log = _pad_lanes(jnp.concatenate([A_log_f[l], A_log_b[l]]).reshape(1, -1))
        d_skip = jnp.repeat(D_skip[l], SSD_HEADDIM).reshape(1, -1)
        ssd = _ssd(z, xbc, dt_raw, conv_w[l].reshape(SSD_CONV, nconv), row(conv_b[l]), dt_bias, a_log,
                   d_skip, row(ssm_norm_g[l]))

        rw_t = router_w[l].T
        rw_hi = rw_t.astype(BF16)
        rw_split = jnp.stack([rw_hi, (rw_t - rw_hi.astype(F32)).astype(BF16)])
        x1, h2, aff_t = _out_proj(att, ssd, x, g1, sh2, sc2, row(norm_ffn_g[l]), w_out[l].astype(BF16),
                                  rw_split, ts=512)
        pos, pos_t = _topk(aff_t, cap)
        xg, gc = _gather(h2, pos, aff_t, cap)
        y = _ffn(xg, gc, w_gate[l], w_up[l], w_down[l], tf=256)
        x = _scatter(pos_t, y, x1, g2, row(norm_final_g), cap, ts=512)
    return x
```

```python
import functools
import math

import jax
import jax.numpy as jnp
from jax import lax
from jax.experimental import pallas as pl
from jax.experimental.pallas import tpu as pltpu

F32 = jnp.float32
BF16 = jnp.bfloat16
I32 = jnp.int32
HIGHEST = lax.Precision.HIGHEST

ATT_HEADS = 4
ATT_QK_DIM = 64
ATT_V_DIM = 128
N_BUCKETS = 32
SSD_HEADS = 8
SSD_HEADDIM = 64
SSD_GROUPS = 2
SSD_STATE = 64
SSD_CONV = 5
SSD_CHUNK = 128
SSD_WIDTH = SSD_HEADS * SSD_HEADDIM
SSD_BC = SSD_GROUPS * SSD_STATE
N_EXPERTS = 16
CAPACITY_FACTOR = 2
EPS = 1e-6
LANES = 128
NEG_BIG = -1e30
LOG2E = math.log2(math.e)

VMEM_LIMIT = 56 * 1024 * 1024


def _params(*semantics):
    return pltpu.CompilerParams(dimension_semantics=semantics, vmem_limit_bytes=VMEM_LIMIT)


def _silu(v):
    return v * jax.nn.sigmoid(v)


def _rms(v, eps=EPS):
    return v * lax.rsqrt(jnp.mean(v * v, axis=-1, keepdims=True) + eps)


def _mod_body(c_ref, w_ref, b_ref, o_ref):
    o_ref[...] = jnp.dot(_silu(c_ref[...]), w_ref[...], precision=HIGHEST,
                         preferred_element_type=F32) + b_ref[...]


def _ada_mod(c, w, b):
    bsz, d = c.shape
    n = w.shape[1]
    tn = n // 4
    return pl.pallas_call(
        _mod_body,
        grid=(n // tn,),
        in_specs=[pl.BlockSpec((bsz, d), lambda j: (0, 0)),
                  pl.BlockSpec((d, tn), lambda j: (0, j)),
                  pl.BlockSpec((1, tn), lambda j: (0, j))],
        out_specs=pl.BlockSpec((bsz, tn), lambda j: (0, j)),
        out_shape=jax.ShapeDtypeStruct((bsz, n), F32),
        compiler_params=_params("arbitrary"),
        name="ada_mod",
    )(c, w, b.reshape(1, n))


def _inproj_body(x_ref, sh_ref, sc_ref, g_ref, w_ref, wvt_ref, q_ref, k_ref, vt_ref, z_ref, xbc_ref, dt_ref,
                 *, offs, qscale):
    h = _rms(x_ref[0]) * g_ref[...]
    h = (h * (1.0 + sc_ref[0]) + sh_ref[0]).astype(BF16)

    def proj(i):
        return jnp.dot(h, w_ref[:, offs[i]:offs[i + 1]], preferred_element_type=F32)

    q_ref[0] = (proj(0) * qscale).astype(BF16)
    k_ref[0] = proj(1).astype(BF16)
    vt_ref[0, 0] = lax.dot_general(wvt_ref[...], h, (((1,), (1,)), ((), ())),
                                   preferred_element_type=F32).astype(BF16)
    z_ref[0] = proj(2)
    xbc_ref[0] = proj(3)
    dt_ref[0] = proj(4)


def _in_proj(x, sh, sc, g, w_pad, w_vt, offs, ts):
    bsz, s, d = x.shape
    widths = [offs[i + 1] - offs[i] for i in range(5)]
    dts = [BF16, BF16, F32, F32, F32]
    nv = w_vt.shape[0]
    tok = lambda wd: pl.BlockSpec((1, ts, wd), lambda b, t: (b, t, 0))
    vec = pl.BlockSpec((1, 1, d), lambda b, t: (b, 0, 0))
    return pl.pallas_call(
        functools.partial(_inproj_body, offs=tuple(offs), qscale=ATT_QK_DIM ** -0.5 * LOG2E),
        grid=(bsz, s // ts),
        in_specs=[tok(d), vec, vec,
                  pl.BlockSpec((1, d), lambda b, t: (0, 0)),
                  pl.BlockSpec(w_pad.shape, lambda b, t: (0, 0)),
                  pl.BlockSpec(w_vt.shape, lambda b, t: (0, 0))],
        out_specs=[tok(widths[0]), tok(widths[1]), pl.BlockSpec((1, 1, nv, ts), lambda b, t: (b, t, 0, 0))]
                  + [tok(wd) for wd in widths[2:]],
        out_shape=[jax.ShapeDtypeStruct((bsz, s, widths[0]), BF16), jax.ShapeDtypeStruct((bsz, s, widths[1]), BF16),
                   jax.ShapeDtypeStruct((bsz, s // ts, nv, ts), BF16)]
                  + [jax.ShapeDtypeStruct((bsz, s, wd), dt) for wd, dt in zip(widths[2:], dts[2:])],
        compiler_params=_params("parallel", "arbitrary"),
        name="in_proj",
    )(x, sh, sc, g, w_pad, w_vt)


def _t5_bucket(rel):
    n = jnp.abs(rel)
    large = jnp.full(rel.shape, 8, I32)
    for t in (12, 16, 23, 32, 46, 64, 91):
        large = large + jnp.where(n >= t, 1, 0)
    return jnp.where(rel > 0, 16, 0) + jnp.where(n < 8, n, large)


def _attn_body(tbl_ref, lq1_ref, lk1_ref, lq2_ref, lk2_ref, g_ref, q_ref, k_ref, vt_ref, o_ref,
               bias_ref, s_ref, e_ref, acc_ref, *, lam_init, tk):
    h = pl.program_id(0)
    qi = pl.program_id(1)
    s, tq = bias_ref.shape
    ring = 1024
    assert tk + tq - 1 <= ring

    @pl.when(pl.program_id(2) == 0)
    def _():
        j = lax.broadcasted_iota(I32, (8, ring), 1)
        for kt in range(s // tk):
            bucket = _t5_bucket((kt * tk + tk - 1) - qi * tq - j)
            w = jnp.zeros((8, ring), F32)
            for bb in range(N_BUCKETS):
                w = jnp.where(bucket == bb, tbl_ref[bb, h] * LOG2E, w)
            big = jnp.broadcast_to(w[0:1], (tk, ring))
            rolled = pltpu.roll(big, ring - (tk - 1), axis=1, stride=1, stride_axis=0)
            bias_ref[kt * tk:(kt + 1) * tk, :] = rolled[:, :tq]

    lam = (jnp.exp(jnp.sum(lq1_ref[...] * lk1_ref[...], axis=-1, keepdims=True))
           - jnp.exp(jnp.sum(lq2_ref[...] * lk2_ref[...], axis=-1, keepdims=True)) + lam_init)
    q = q_ref[0]
    lane = lax.broadcasted_iota(I32, q.shape, 1)
    qms = [jnp.where((lane >= m * ATT_QK_DIM) & (lane < (m + 1) * ATT_QK_DIM), q, jnp.zeros_like(q))
           for m in range(2)]
    qw = 2 * LANES
    nqc = tq // qw
    nkb, kb = vt_ref.shape[1], vt_ref.shape[3]
    ones_rows = 16

    def logits_block(m, blk, mx):
        mx = list(mx)
        for t in range(kb // tk):
            r0 = pl.multiple_of(blk * kb + t * tk, tk)
            k_tile = k_ref[0, pl.ds(r0, tk), :]
            for c in range(nqc):
                cols = slice(c * qw, (c + 1) * qw)
                logit = lax.dot_general(k_tile, qms[m][cols, :], (((1,), (1,)), ((), ())),
                                        preferred_element_type=F32) + bias_ref[pl.ds(r0, tk), cols]
                s_ref[m, pl.ds(r0, tk), cols] = logit
                mx[c] = jnp.maximum(mx[c], jnp.max(logit.reshape(tk // 8, 8, qw), axis=0))
        return tuple(mx)

    def exp_block(m, blk, row_max):
        for t in range(kb // tk):
            r0 = pl.multiple_of(blk * kb + t * tk, tk)
            for c in range(nqc):
                cols = slice(c * qw, (c + 1) * qw)
                e_ref[m, pl.ds(r0, tk), cols] = jnp.exp2(s_ref[m, pl.ds(r0, tk), cols] - row_max[c]).astype(BF16)

    def pv_block(m, blk):
        vt_ext = jnp.concatenate([vt_ref[0, blk], jnp.ones((ones_rows, kb), BF16)], axis=0)
        r0 = pl.multiple_of(blk * kb, kb)
        acc_ref[m] += jnp.dot(vt_ext, e_ref[m, pl.ds(r0, kb), :], preferred_element_type=F32)

    def finish_max(mx):
        return [jnp.max(m8, axis=0, keepdims=True) for m8 in mx]

    def unrolled(n, body, carry):
        for i in range(n):
            carry = body(i, carry)
        return carry

    mx_init = tuple(jnp.full((8, qw), NEG_BIG, F32) for _ in range(nqc))
    acc_ref[...] = jnp.zeros_like(acc_ref)
    max0 = finish_max(unrolled(nkb, lambda blk, mx: logits_block(0, blk, mx), mx_init))

    def phase2(blk, mx):
        exp_block(0, blk, max0)
        return logits_block(1, blk, mx)

    max1 = finish_max(unrolled(nkb, phase2, mx_init))

    def phase3(blk, carry):
        exp_block(1, blk, max1)
        pv_block(0, blk)
        return carry

    unrolled(nkb, phase3, 0)

    def phase4(blk, carry):
        pv_block(1, blk)
        return carry

    unrolled(nkb, phase4, 0)
    outs = [acc_ref[m, :ATT_V_DIM, :] / acc_ref[m, ATT_V_DIM:ATT_V_DIM + 1, :] for m in range(2)]
    att = outs[0] - lam * outs[1]
    att = att * lax.rsqrt(jnp.mean(att * att, axis=0, keepdims=True) + EPS)
    o_ref[0] = (att * (g_ref[...] * (1.0 - lam_init))).T.astype(BF16)


def _attention(q, k, v_t, tbl, lq1, lk1, lq2, lk2, subln_g, lam_init, tq):
    bsz, s, _ = q.shape
    nkb, kb = v_t.shape[1], v_t.shape[3]
    small = lambda n: pl.BlockSpec((1, n), lambda h, i, b: (0, 0))
    return pl.pallas_call(
        functools.partial(_attn_body, lam_init=lam_init, tk=LANES),
        grid=(ATT_HEADS, s // tq, bsz),
        in_specs=[pl.BlockSpec(memory_space=pltpu.SMEM),
                  small(ATT_QK_DIM), small(ATT_QK_DIM), small(ATT_QK_DIM), small(ATT_QK_DIM),
                  pl.BlockSpec((ATT_V_DIM, 1), lambda h, i, b: (0, 0)),
                  pl.BlockSpec((1, tq, ATT_V_DIM), lambda h, i, b: (b, i, h)),
                  pl.BlockSpec((1, s, ATT_V_DIM), lambda h, i, b: (b, 0, h)),
                  pl.BlockSpec((1, nkb, ATT_V_DIM, kb), lambda h, i, b: (b, 0, h, 0))],
        out_specs=pl.BlockSpec((1, tq, ATT_V_DIM), lambda h, i, b: (b, i, h)),
        out_shape=jax.ShapeDtypeStruct((bsz, s, ATT_HEADS * ATT_V_DIM), BF16),
        scratch_shapes=[pltpu.VMEM((s, tq), F32),
                        pltpu.VMEM((2, s, tq), F32),
                        pltpu.VMEM((2, s, tq), BF16),
                        pltpu.VMEM((2, ATT_V_DIM + 16, tq), F32)],
        compiler_params=_params("parallel", "parallel", "arbitrary"),
        name="diff_attention",
    )(tbl, lq1, lk1, lq2, lk2, subln_g.reshape(ATT_V_DIM, 1), q, k, v_t)


def _head_selector(off):
    r = lax.broadcasted_iota(I32, (LANES, SSD_WIDTH), 0)
    c = lax.broadcasted_iota(I32, (LANES, SSD_WIDTH), 1)
    return jnp.where(r - off == c // SSD_HEADDIM, 1.0, 0.0).astype(BF16)


def _expand_heads(cols, selector):
    hi = cols.astype(BF16)
    lo = (cols - hi.astype(F32)).astype(BF16)
    return (jnp.dot(hi, selector, preferred_element_type=F32)
            + jnp.dot(lo, selector, preferred_element_type=F32))


def _ssd_body(z_ref, xbc_ref, dtr_ref, cw_ref, cb_ref, dtb_ref, alog_ref, dskip_ref, ng_ref, o_ref,
              xp_ref, xc_ref, dt_ref, da_ref, cs_ref, cst_ref, ecst_ref, dtt_ref, bt_ref, y_ref, hf_ref, hb_ref):
    s = xbc_ref.shape[1]
    ch = SSD_CHUNK
    assert ch == LANES
    nc = s // ch
    pad = 8
    nconv = xbc_ref.shape[2]

    xp_ref[0:pad, :] = jnp.zeros((pad, nconv), F32)
    xp_ref[pad + s:pad + s + pad, :] = jnp.zeros((pad, nconv), F32)
    xp_ref[pad:pad + s, :] = xbc_ref[0]

    def conv_chunk(c, carry):
        r0 = pl.multiple_of(c * ch, ch)
        win = xp_ref[pl.ds(r0, ch + 2 * pad), :]
        acc = jnp.broadcast_to(cb_ref[...], (ch, nconv))
        for j in range(SSD_CONV):
            lo = pad + j - SSD_CONV // 2
            acc = acc + cw_ref[j:j + 1, :] * win[lo:lo + ch, :]
        xc_ref[pl.ds(r0, ch), :] = _silu(acc)
        return carry

    lax.fori_loop(0, nc, conv_chunk, 0)

    lane = lax.broadcasted_iota(I32, (s, LANES), 1)
    pre = dtr_ref[0] + dtb_ref[...]
    dt = jnp.maximum(pre, 0.0) + jnp.log1p(jnp.exp(-jnp.abs(pre)))
    dt = jnp.where(lane < 2 * SSD_HEADS, dt, 0.0)
    dt_ref[...] = dt
    da_ref[...] = dt * (-jnp.exp(alog_ref[...]))

    row = lax.broadcasted_iota(I32, (ch, ch), 0)
    col = lax.broadcasted_iota(I32, (ch, ch), 1)
    tri = jnp.where(col <= row, 1.0, 0.0).astype(F32)
    lower = col <= row
    upper = col >= row
    lane_c = lax.broadcasted_iota(I32, (ch, LANES), 1)
    st_rows = lax.broadcasted_iota(I32, (SSD_BC, SSD_WIDTH), 0) // SSD_STATE
    st_cols = lax.broadcasted_iota(I32, (SSD_BC, SSD_WIDTH), 1) // (SSD_WIDTH // SSD_GROUPS)
    same_group = st_rows == st_cols
    sel_f = _head_selector(0)
    sel_b = _head_selector(SSD_HEADS)

    hf_ref[...] = jnp.zeros_like(hf_ref)
    hb_ref[...] = jnp.zeros_like(hb_ref)

    def wide(x):
        return jnp.concatenate([x[c * ch:(c + 1) * ch, :] for c in range(nc)], axis=1)

    da_wide = wide(da_ref[...])
    cs_wide = jnp.dot(tri, da_wide, precision=HIGHEST, preferred_element_type=F32)
    for c in range(nc):
        cs_ref[c * ch:(c + 1) * ch, :] = cs_wide[:, c * LANES:(c + 1) * LANES]
    cst_ref[...] = cs_wide.T
    ecst_ref[...] = (cs_wide - da_wide).T
    dtt_ref[...] = wide(dt_ref[...]).T
    bt_ref[...] = wide(xc_ref[:, SSD_WIDTH:SSD_WIDTH + SSD_BC]).T.astype(BF16)

    def fwd_chunk(c, carry):
        r0 = pl.multiple_of(c * ch, ch)
        da = da_ref[pl.ds(r0, ch), :]
        dtc = dt_ref[pl.ds(r0, ch), :]
        cs = cs_ref[pl.ds(r0, ch), :]
        tot = cs[ch - 1:ch, :]
        ecs = cs - da
        nh2 = 2 * SSD_HEADS
        cs_t = cst_ref[pl.ds(r0, nh2), :]
        ecs_t = ecst_ref[pl.ds(r0, nh2), :]
        dt_t = dtt_ref[pl.ds(r0, nh2), :]
        xs = xc_ref[pl.ds(r0, ch), 0:SSD_WIDTH]
        cm = xc_ref[pl.ds(r0, ch), SSD_WIDTH + SSD_BC:SSD_WIDTH + 2 * SSD_BC]
        bt = bt_ref[pl.ds(r0, ch), :]
        cmb = cm.astype(BF16)
        xsb = xs.astype(BF16)

        g_mats = []
        for g in range(SSD_GROUPS):
            cg = jnp.where(lane_c // SSD_STATE == g, cmb, jnp.zeros_like(cmb))
            g_mats.append(jnp.dot(cg, bt, preferred_element_type=F32))
        for pair in range(SSD_HEADS // 2):
            xpair = xsb[:, pair * LANES:(pair + 1) * LANES]
            ypair = jnp.zeros((ch, LANES), F32)
            for sub in range(2):
                hh = 2 * pair + sub
                g = hh // (SSD_HEADS // SSD_GROUPS)
                d_f = cs[:, hh:hh + 1] - cs_t[hh:hh + 1, :]
                d_b = ecs_t[SSD_HEADS + hh:SSD_HEADS + hh + 1, :] - ecs[:, SSD_HEADS + hh:SSD_HEADS + hh + 1]
                decay = jnp.exp(jnp.where(lower, d_f, d_b))
                wgt = (jnp.where(lower, dt_t[hh:hh + 1, :], 0.0)
                       + jnp.where(upper, dt_t[SSD_HEADS + hh:SSD_HEADS + hh + 1, :], 0.0))
                m = (g_mats[g] * decay * wgt).astype(BF16)
                xh = jnp.where(lane_c // SSD_HEADDIM == sub, xpair, jnp.zeros_like(xpair))
                ypair = ypair + jnp.dot(m, xh, preferred_element_type=F32)
            y_ref[pl.ds(r0, ch), pair * LANES:(pair + 1) * LANES] = ypair

        hf = hf_ref[...]
        factors = _expand_heads(jnp.concatenate(
            [jnp.exp(cs), dtc * jnp.exp(tot - cs), jnp.broadcast_to(jnp.exp(tot), (8, LANES))], axis=0), sel_f)
        y_off = jnp.dot(cmb, hf.astype(BF16), preferred_element_type=F32) * factors[0:ch]
        y_ref[pl.ds(r0, ch), :] = y_ref[pl.ds(r0, ch), :] + y_off
        xd = (xs * factors[ch:2 * ch]).astype(BF16)
        upd = jnp.dot(bt, xd, preferred_element_type=F32)
        hf_ref[...] = hf * factors[2 * ch:2 * ch + 1] + jnp.where(same_group, upd, 0.0)
        return carry

    lax.fori_loop(0, nc, fwd_chunk, 0, unroll=4)

    def bwd_chunk(i, carry):
        c = nc - 1 - i
        r0 = pl.multiple_of(c * ch, ch)
        da = da_ref[pl.ds(r0, ch), :]
        dtc = dt_ref[pl.ds(r0, ch), :]
        cs = cs_ref[pl.ds(r0, ch), :]
        tot = cs[ch - 1:ch, :]
        ecs = cs - da
        xs = xc_ref[pl.ds(r0, ch), 0:SSD_WIDTH]
        cm = xc_ref[pl.ds(r0, ch), SSD_WIDTH + SSD_BC:SSD_WIDTH + 2 * SSD_BC]
        bt = bt_ref[pl.ds(r0, ch), :]

        hb = hb_ref[...]
        factors = _expand_heads(jnp.concatenate(
            [jnp.exp(tot - ecs), dtc * jnp.exp(ecs), jnp.broadcast_to(jnp.exp(tot), (8, LANES))], axis=0), sel_b)
        y_off = jnp.dot(cm.astype(BF16), hb.astype(BF16), preferred_element_type=F32) * factors[0:ch]
        xd = (xs * factors[ch:2 * ch]).astype(BF16)
        upd = jnp.dot(bt, xd, preferred_element_type=F32)
        hb_ref[...] = hb * factors[2 * ch:2 * ch + 1] + jnp.where(same_group, upd, 0.0)

        y = y_ref[pl.ds(r0, ch), :] + y_off + dskip_ref[...] * xs
        y = y * _silu(z_ref[0, pl.ds(r0, ch), :])
        gw = SSD_WIDTH // SSD_GROUPS
        yn = jnp.concatenate([_rms(y[:, g * gw:(g + 1) * gw]) for g in range(SSD_GROUPS)], axis=-1)
        o_ref[0, pl.ds(r0, ch), :] = (yn * ng_ref[...]).astype(BF16)
        return carry

    lax.fori_loop(0, nc, bwd_chunk, 0, unroll=2)


def _ssd(z, xbc, dt_raw, conv_w, conv_b, dt_bias, a_log, d_skip, norm_g):
    bsz, s, nconv = xbc.shape
    tok = lambda wd: pl.BlockSpec((1, s, wd), lambda b: (b, 0, 0))
    full = lambda a: pl.BlockSpec(a.shape, lambda b: (0, 0))
    return pl.pallas_call(
        _ssd_body,
        grid=(bsz,),
        in_specs=[tok(SSD_WIDTH), tok(nconv), tok(LANES), full(conv_w), full(conv_b), full(dt_bias),
                  full(a_log), full(d_skip), full(norm_g)],
        out_specs=tok(SSD_WIDTH),
        out_shape=jax.ShapeDtypeStruct((bsz, s, SSD_WIDTH), BF16),
        scratch_shapes=[pltpu.VMEM((s + 16, nconv), F32),
                        pltpu.VMEM((s, nconv), F32),
                        pltpu.VMEM((s, LANES), F32),
                        pltpu.VMEM((s, LANES), F32),
                        pltpu.VMEM((s, LANES), F32),
                        pltpu.VMEM((s, LANES), F32),
                        pltpu.VMEM((s, LANES), F32),
                        pltpu.VMEM((s, LANES), F32),
                        pltpu.VMEM((s, SSD_BC), BF16),
                        pltpu.VMEM((s, SSD_WIDTH), F32),
                        pltpu.VMEM((SSD_BC, SSD_WIDTH), F32),
                        pltpu.VMEM((SSD_BC, SSD_WIDTH), F32)],
        compiler_params=_params("parallel"),
        name="ssd_mixer",
    )(z, xbc, dt_raw, conv_w, conv_b, dt_bias, a_log, d_skip, norm_g)


def _outproj_body(att_ref, ssd_ref, x_ref, g1_ref, sh_ref, sc_ref, ng_ref, w_ref, rw_ref,
                  x1_ref, h2_ref, aff_ref):
    na = att_ref.shape[2]
    mix = (jnp.dot(att_ref[0], w_ref[0:na, :], preferred_element_type=F32)
           + jnp.dot(ssd_ref[0], w_ref[na:, :], preferred_element_type=F32))
    x1 = x_ref[0] + g1_ref[0] * mix
    x1_ref[0] = x1
    h2 = _rms(x1) * ng_ref[...]
    h2 = h2 * (1.0 + sc_ref[0]) + sh_ref[0]
    h2_hi = h2.astype(BF16)
    h2_ref[0] = h2_hi
    h2_lo = (h2 - h2_hi.astype(F32)).astype(BF16)
    nt = lambda a, b: lax.dot_general(a, b, (((1,), (1,)), ((), ())), preferred_element_type=F32)
    logits = nt(rw_ref[0], h2_hi) + (nt(rw_ref[0], h2_lo) + nt(rw_ref[1], h2_hi))
    e = jnp.exp(logits - jnp.max(logits, axis=0, keepdims=True))
    aff_ref[0] = e / jnp.sum(e, axis=0, keepdims=True)


def _out_proj(att, ssd, x, g1, sh2, sc2, ng, w_out, rw_t, ts):
    bsz, s, d = x.shape
    tok = lambda wd: pl.BlockSpec((1, ts, wd), lambda b, t: (b, t, 0))
    vec = pl.BlockSpec((1, 1, d), lambda b, t: (b, 0, 0))
    full = lambda a: pl.BlockSpec(a.shape, lambda b, t: (0, 0))
    ne = rw_t.shape[1]
    return pl.pallas_call(
        _outproj_body,
        grid=(bsz, s // ts),
        in_specs=[tok(att.shape[2]), tok(ssd.shape[2]), tok(d), vec, vec, vec, full(ng), full(w_out),
                  pl.BlockSpec(rw_t.shape, lambda b, t: (0, 0, 0))],
        out_specs=[tok(d), tok(d), pl.BlockSpec((1, ne, ts), lambda b, t: (b, 0, t))],
        out_shape=[jax.ShapeDtypeStruct((bsz, s, d), F32),
                   jax.ShapeDtypeStruct((bsz, s, d), BF16),
                   jax.ShapeDtypeStruct((bsz, ne, s), F32)],
        compiler_params=_params("parallel", "arbitrary"),
        name="out_proj_router",
    )(att, ssd, x, g1, sh2, sc2, ng, w_out, rw_t)


def _topk_body(aff_ref, pos_ref, post_ref, ut_ref, *, cap):
    nrow, s = aff_ref.shape
    r = lax.broadcasted_iota(I32, (s, s), 0)
    c = lax.broadcasted_iota(I32, (s, s), 1)
    ut_ref[...] = jnp.where(r < c, 1.0, 0.0).astype(BF16)

    bits = pltpu.bitcast(aff_ref[...], I32)

    def count(mask):
        return jnp.sum(jnp.where(mask, 1, 0), axis=-1, keepdims=True)

    thr = jnp.zeros((nrow, 1), I32)
    for bit in range(30, -1, -1):
        cand = thr | (1 << bit)
        thr = jnp.where(count(bits >= cand) >= cap, cand, thr)
    gt = bits > thr
    eq = bits == thr
    need = cap - count(gt)
    eq_rank = jnp.dot(jnp.where(eq, 1.0, 0.0).astype(BF16), ut_ref[...], preferred_element_type=F32)
    sel = gt | (eq & (eq_rank < need.astype(F32)))
    slot = jnp.dot(jnp.where(sel, 1.0, 0.0).astype(BF16), ut_ref[...], preferred_element_type=F32)
    pos = jnp.where(sel, slot, -1.0)
    pos_ref[...] = pos.astype(I32)
    post_ref[...] = pos.T.astype(I32)


def _topk(aff_t, cap):
    bsz, ne, s = aff_t.shape
    nrow = bsz * ne
    pos, pos_t = pl.pallas_call(
        functools.partial(_topk_body, cap=cap),
        grid=(1,),
        in_specs=[pl.BlockSpec((nrow, s), lambda i: (0, 0))],
        out_specs=[pl.BlockSpec((nrow, s), lambda i: (0, 0)),
                   pl.BlockSpec((s, nrow), lambda i: (0, 0))],
        out_shape=[jax.ShapeDtypeStruct((nrow, s), I32),
                   jax.ShapeDtypeStruct((s, nrow), I32)],
        scratch_shapes=[pltpu.VMEM((s, s), BF16)],
        compiler_params=_params("arbitrary"),
        name="expert_topk",
    )(aff_t.reshape(nrow, s))
    pos_t = jnp.pad(pos_t.reshape(s, bsz, ne).transpose(1, 0, 2), ((0, 0), (0, 0), (0, LANES - ne)),
                    constant_values=-1)
    return pos.reshape(bsz, ne, s), pos_t


def _gather_body(h2_ref, pos_ref, aff_ref, xg_ref, gc_ref, *, cap):
    ne, s = pos_ref.shape[1], pos_ref.shape[2]
    slot = lax.broadcasted_iota(I32, (cap, s), 0)
    h2 = h2_ref[0]
    for e in range(ne):
        hit = pos_ref[0, e:e + 1, :] == slot
        xg_ref[e] = jnp.dot(jnp.where(hit, 1.0, 0.0).astype(BF16), h2,
                            preferred_element_type=F32).astype(BF16)
        gate = jnp.sum(jnp.where(hit, aff_ref[0, e:e + 1, :], 0.0), axis=-1, keepdims=True)
        gc_ref[e] = jnp.broadcast_to(gate, (cap, LANES))


def _gather(h2, pos, aff_t, cap):
    bsz, s, d = h2.shape
    ne = pos.shape[1]
    return pl.pallas_call(
        functools.partial(_gather_body, cap=cap),
        grid=(bsz,),
        in_specs=[pl.BlockSpec((1, s, d), lambda b: (b, 0, 0)),
                  pl.BlockSpec((1, ne, s), lambda b: (b, 0, 0)),
                  pl.BlockSpec((1, ne, s), lambda b: (b, 0, 0))],
        out_specs=[pl.BlockSpec((ne, cap, d), lambda b: (0, b, 0)),
                   pl.BlockSpec((ne, cap, LANES), lambda b: (0, b, 0))],
        out_shape=[jax.ShapeDtypeStruct((ne, bsz * cap, d), BF16),
                   jax.ShapeDtypeStruct((ne, bsz * cap, LANES), F32)],
        compiler_params=_params("parallel"),
        name="moe_gather",
    )(h2, pos, aff_t)


def _ffn_body(xg_ref, gc_ref, wg_ref, wu_ref, wd_ref, y_ref, acc_ref):
    f = pl.program_id(1)

    @pl.when(f == 0)
    def _():
        acc_ref[...] = jnp.zeros_like(acc_ref)

    wg = wg_ref[0].astype(BF16)
    wu = wu_ref[0].astype(BF16)
    wd = wd_ref[0].astype(BF16)
    half = xg_ref.shape[1] // 2
    for r in range(2):
        rows = slice(r * half, (r + 1) * half)
        xg = xg_ref[0, rows, :]
        gate = jnp.dot(xg, wg, preferred_element_type=F32)
        up = jnp.dot(xg, wu, preferred_element_type=F32)
        hid = (_silu(gate) * up).astype(BF16)
        acc_ref[rows, :] += jnp.dot(hid, wd, preferred_element_type=F32)

    @pl.when(f == pl.num_programs(1) - 1)
    def _():
        m, d = acc_ref.shape
        gcol = gc_ref[0]
        y_ref[0] = (acc_ref[...] * jnp.concatenate([gcol] * (d // LANES), axis=-1)).astype(BF16)


def _ffn(xg, gc, w_gate, w_up, w_down, tf):
    ne, m, d = xg.shape
    ff = w_gate.shape[2]
    return pl.pallas_call(
        _ffn_body,
        grid=(ne, ff // tf),
        in_specs=[pl.BlockSpec((1, m, d), lambda e, f: (e, 0, 0)),
                  pl.BlockSpec((1, m, LANES), lambda e, f: (e, 0, 0)),
                  pl.BlockSpec((1, d, tf), lambda e, f: (e, 0, f)),
                  pl.BlockSpec((1, d, tf), lambda e, f: (e, 0, f)),
                  pl.BlockSpec((1, tf, d), lambda e, f: (e, f, 0))],
        out_specs=pl.BlockSpec((1, m, d), lambda e, f: (e, 0, 0)),
        out_shape=jax.ShapeDtypeStruct((ne, m, d), BF16),
        scratch_shapes=[pltpu.VMEM((m, d), F32)],
        compiler_params=_params("parallel", "arbitrary"),
        name="moe_ffn",
    )(xg, gc, w_gate, w_up, w_down)


def _scatter_body(post_ref, y_ref, x1_ref, g2_ref, ng_ref, o_ref, *, cap):
    ne = y_ref.shape[0]
    ts = post_ref.shape[1]
    d = y_ref.shape[2]
    slot = lax.broadcasted_iota(I32, (ts, cap), 1)
    post = post_ref[0]
    onehot = jnp.concatenate(
        [jnp.where(post[:, e:e + 1] == slot, 1.0, 0.0).astype(BF16) for e in range(ne)], axis=-1)
    moe = jnp.dot(onehot, y_ref[...].reshape(ne * cap, d), preferred_element_type=F32)
    x2 = x1_ref[0] + g2_ref[0] * moe
    o_ref[0] = _rms(x2) * ng_ref[...]


def _scatter(post, y, x1, g2, ng, cap, ts):
    bsz, s, d = x1.shape
    ne = y.shape[0]
    return pl.pallas_call(
        functools.partial(_scatter_body, cap=cap),
        grid=(bsz, s // ts),
        in_specs=[pl.BlockSpec((1, ts, LANES), lambda b, t: (b, t, 0)),
                  pl.BlockSpec((ne, cap, d), lambda b, t: (0, b, 0)),
                  pl.BlockSpec((1, ts, d), lambda b, t: (b, t, 0)),
                  pl.BlockSpec((1, 1, d), lambda b, t: (b, 0, 0)),
                  pl.BlockSpec((1, d), lambda b, t: (0, 0))],
        out_specs=pl.BlockSpec((1, ts, d), lambda b, t: (b, t, 0)),
        out_shape=jax.ShapeDtypeStruct((bsz, s, d), F32),
        compiler_params=_params("parallel", "arbitrary"),
        name="moe_scatter_final",
    )(post, y, x1, g2, ng)


def _pad_lanes(a, n=LANES):
    return jnp.pad(a, [(0, 0)] * (a.ndim - 1) + [(0, n - a.shape[-1])])


def kernel(x, c, ada_w, ada_b, norm_mix_g, norm_ffn_g, norm_final_g, w_in, lambda_q1, lambda_k1, lambda_q2, lambda_k2, attn_subln_g, rel_bias_table, conv_w, conv_b, dt_bias_f, dt_bias_b, A_log_f, A_log_b, D_skip, ssm_norm_g, w_out, router_w, w_gate, w_up, w_down):
    bsz, s, d = x.shape
    depth = ada_w.shape[0]
    ne = router_w.shape[2]
    cap = CAPACITY_FACTOR * s // ne
    att_w = ATT_HEADS * ATT_V_DIM
    nconv = conv_w.shape[3]
    splits = (att_w, att_w, SSD_WIDTH, nconv, LANES)
    offs = [0]
    for wd in splits:
        offs.append(offs[-1] + wd)
    row = lambda a: a.reshape(1, -1)

    assert depth == 1, "the final RMSNorm is fused into the (single) layer's scatter kernel"
    for l in range(depth):
        lam_init = 0.8 - 0.6 * math.exp(-0.3 * l)
        mod = _ada_mod(c, ada_w[l], ada_b[l])
        sh1, sc1, g1, sh2, sc2, g2 = [m.reshape(bsz, 1, d) for m in jnp.split(mod, 6, axis=-1)]

        w_l = w_in[l]
        w_pad = _pad_lanes(jnp.concatenate([w_l[:, :2 * att_w], w_l[:, 3 * att_w:]], axis=1), offs[-1]).astype(BF16)
        w_vt = w_l[:, 2 * att_w:3 * att_w].T.astype(BF16)
        q, k, v_t, z, xbc, dt_raw = _in_proj(x, sh1, sc1, row(norm_mix_g[l]), w_pad, w_vt, offs, ts=512)

        att = _attention(q, k, v_t, rel_bias_table, row(lambda_q1[l]), row(lambda_k1[l]),
                         row(lambda_q2[l]), row(lambda_k2[l]), row(attn_subln_g[l]), lam_init, tq=512)

        dt_bias = _pad_lanes(jnp.concatenate([dt_bias_f[l], dt_bias_b[l]]).reshape(1, -1))
        a_log = _pad_lanes(jnp.concatenate([A_log_f[l], A_log_b[l]]).reshape(1, -1))
        d_skip = jnp.repeat(D_skip[l], SSD_HEADDIM).reshape(1, -1)
        ssd = _ssd(z, xbc, dt_raw, conv_w[l].reshape(SSD_CONV, nconv), row(conv_b[l]), dt_bias, a_log,
                   d_skip, row(ssm_norm_g[l]))

        rw_t = router_w[l].T
        rw_hi = rw_t.astype(BF16)
        rw_split = jnp.stack([rw_hi, (rw_t - rw_hi.astype(F32)).astype(BF16)])
        x1, h2, aff_t = _out_proj(att, ssd, x, g1, sh2, sc2, row(norm_ffn_g[l]), w_out[l].astype(BF16),
                                  rw_split, ts=512)
        pos, pos_t = _topk(aff_t, cap)
        xg, gc = _gather(h2, pos, aff_t, cap)
        y = _ffn(xg, gc, w_gate[l], w_up[l], w_down[l], tf=256)
        x = _scatter(pos_t, y, x1, g2, row(norm_final_g), cap, ts=512)
    return x
```

```python
import functools
import math

import jax
import jax.numpy as jnp
from jax import lax
from jax.experimental import pallas as pl
from jax.experimental.pallas import tpu as pltpu

F32 = jnp.float32
BF16 = jnp.bfloat16
I32 = jnp.int32
HIGHEST = lax.Precision.HIGHEST

ATT_HEADS = 4
ATT_QK_DIM = 64
ATT_V_DIM = 128
N_BUCKETS = 32
SSD_HEADS = 8
SSD_HEADDIM = 64
SSD_GROUPS = 2
SSD_STATE = 64
SSD_CONV = 5
SSD_CHUNK = 128
SSD_WIDTH = SSD_HEADS * SSD_HEADDIM
SSD_BC = SSD_GROUPS * SSD_STATE
N_EXPERTS = 16
CAPACITY_FACTOR = 2
EPS = 1e-6
LANES = 128
NEG_BIG = -1e30
LOG2E = math.log2(math.e)

VMEM_LIMIT = 56 * 1024 * 1024


def _params(*semantics):
    return pltpu.CompilerParams(dimension_semantics=semantics, vmem_limit_bytes=VMEM_LIMIT)


def _silu(v):
    return v * jax.nn.sigmoid(v)


def _rms(v, eps=EPS):
    return v * lax.rsqrt(jnp.mean(v * v, axis=-1, keepdims=True) + eps)


def _mod_body(c_ref, w_ref, b_ref, o_ref):
    o_ref[...] = jnp.dot(_silu(c_ref[...]), w_ref[...], precision=HIGHEST,
                         preferred_element_type=F32) + b_ref[...]


def _ada_mod(c, w, b):
    bsz, d = c.shape
    n = w.shape[1]
    tn = n // 4
    return pl.pallas_call(
        _mod_body,
        grid=(n // tn,),
        in_specs=[pl.BlockSpec((bsz, d), lambda j: (0, 0)),
                  pl.BlockSpec((d, tn), lambda j: (0, j)),
                  pl.BlockSpec((1, tn), lambda j: (0, j))],
        out_specs=pl.BlockSpec((bsz, tn), lambda j: (0, j)),
        out_shape=jax.ShapeDtypeStruct((bsz, n), F32),
        compiler_params=_params("arbitrary"),
        name="ada_mod",
    )(c, w, b.reshape(1, n))


def _inproj_body(x_ref, sh_ref, sc_ref, g_ref, w_ref, wvt_ref, q_ref, k_ref, vt_ref, z_ref, xbc_ref, dt_ref,
                 *, offs, qscale):
    h = _rms(x_ref[0]) * g_ref[...]
    h = (h * (1.0 + sc_ref[0]) + sh_ref[0]).astype(BF16)

    def proj(i):
        return jnp.dot(h, w_ref[:, offs[i]:offs[i + 1]], preferred_element_type=F32)

    q_ref[0] = (proj(0) * qscale).astype(BF16)
    k_ref[0] = proj(1).astype(BF16)
    vt_ref[0, 0] = lax.dot_general(wvt_ref[...], h, (((1,), (1,)), ((), ())),
                                   preferred_element_type=F32).astype(BF16)
    z_ref[0] = proj(2)
    xbc_ref[0] = proj(3)
    dt_ref[0] = proj(4)


def _in_proj(x, sh, sc, g, w_pad, w_vt, offs, ts):
    bsz, s, d = x.shape
    widths = [offs[i + 1] - offs[i] for i in range(5)]
    dts = [BF16, BF16, F32, F32, F32]
    nv = w_vt.shape[0]
    tok = lambda wd: pl.BlockSpec((1, ts, wd), lambda b, t: (b, t, 0))
    vec = pl.BlockSpec((1, 1, d), lambda b, t: (b, 0, 0))
    return pl.pallas_call(
        functools.partial(_inproj_body, offs=tuple(offs), qscale=ATT_QK_DIM ** -0.5 * LOG2E),
        grid=(bsz, s // ts),
        in_specs=[tok(d), vec, vec,
                  pl.BlockSpec((1, d), lambda b, t: (0, 0)),
                  pl.BlockSpec(w_pad.shape, lambda b, t: (0, 0)),
                  pl.BlockSpec(w_vt.shape, lambda b, t: (0, 0))],
        out_specs=[tok(widths[0]), tok(widths[1]), pl.BlockSpec((1, 1, nv, ts), lambda b, t: (b, t, 0, 0))]
                  + [tok(wd) for wd in widths[2:]],
        out_shape=[jax.ShapeDtypeStruct((bsz, s, widths[0]), BF16), jax.ShapeDtypeStruct((bsz, s, widths[1]), BF16),
                   jax.ShapeDtypeStruct((bsz, s // ts, nv, ts), BF16)]
                  + [jax.ShapeDtypeStruct((bsz, s, wd), dt) for wd, dt in zip(widths[2:], dts[2:])],
        compiler_params=_params("parallel", "arbitrary"),
        name="in_proj",
    )(x, sh, sc, g, w_pad, w_vt)


def _t5_bucket(rel):
    n = jnp.abs(rel)
    large = jnp.full(rel.shape, 8, I32)
    for t in (12, 16, 23, 32, 46, 64, 91):
        large = large + jnp.where(n >= t, 1, 0)
    return jnp.where(rel > 0, 16, 0) + jnp.where(n < 8, n, large)


def _attn_body(tbl_ref, lq1_ref, lk1_ref, lq2_ref, lk2_ref, g_ref, q_ref, k_ref, vt_ref, o_ref,
               bias_ref, s_ref, e_ref, acc_ref, *, lam_init, tk):
    h = pl.program_id(0)
    qi = pl.program_id(1)
    s, tq = bias_ref.shape
    ring = 1024
    assert tk + tq - 1 <= ring

    @pl.when(pl.program_id(2) == 0)
    def _():
        j = lax.broadcasted_iota(I32, (8, ring), 1)
        for kt in range(s // tk):
            bucket = _t5_bucket((kt * tk + tk - 1) - qi * tq - j)
            w = jnp.zeros((8, ring), F32)
            for bb in range(N_BUCKETS):
                w = jnp.where(bucket == bb, tbl_ref[bb, h] * LOG2E, w)
            big = jnp.broadcast_to(w[0:1], (tk, ring))
            rolled = pltpu.roll(big, ring - (tk - 1), axis=1, stride=1, stride_axis=0)
            bias_ref[kt * tk:(kt + 1) * tk, :] = rolled[:, :tq]

    lam = (jnp.exp(jnp.sum(lq1_ref[...] * lk1_ref[...], axis=-1, keepdims=True))
           - jnp.exp(jnp.sum(lq2_ref[...] * lk2_ref[...], axis=-1, keepdims=True)) + lam_init)
    q = q_ref[0]
    lane = lax.broadcasted_iota(I32, q.shape, 1)
    qms = [jnp.where((lane >= m * ATT_QK_DIM) & (lane < (m + 1) * ATT_QK_DIM), q, jnp.zeros_like(q))
           for m in range(2)]
    qw = 2 * LANES
    nqc = tq // qw
    nkb, kb = vt_ref.shape[1], vt_ref.shape[3]
    ones_rows = 16

    streams = [(m, c) for m in range(2) for c in range(nqc)]
    run_max = {st: jnp.full((8, qw), NEG_BIG, F32) for st in streams}
    row_max = {}

    def logits_item(st, t):
        m, c = st
        rows, cols = slice(t * tk, (t + 1) * tk), slice(c * qw, (c + 1) * qw)
        logit = lax.dot_general(k_ref[0, rows, :], qms[m][cols, :], (((1,), (1,)), ((), ())),
                                preferred_element_type=F32) + bias_ref[rows, cols]
        s_ref[m, rows, cols] = logit
        run_max[st] = jnp.maximum(run_max[st], jnp.max(logit.reshape(tk // 8, 8, qw), axis=0))
        if t == s // tk - 1:
            row_max[st] = jnp.max(run_max[st], axis=0, keepdims=True)

    def exp_item(st, t):
        m, c = st
        rows, cols = slice(t * tk, (t + 1) * tk), slice(c * qw, (c + 1) * qw)
        e_ref[m, rows, cols] = jnp.exp2(s_ref[m, rows, cols] - row_max[st]).astype(BF16)

    def pv_item(st, blk):
        m, c = st
        rows, cols = slice(blk * kb, (blk + 1) * kb), slice(c * qw, (c + 1) * qw)
        vt_ext = jnp.concatenate([vt_ref[0, blk], jnp.ones((ones_rows, kb), BF16)], axis=0)
        part = jnp.dot(vt_ext, e_ref[m, rows, cols], preferred_element_type=F32)
        if blk == 0:
            acc_ref[m, :, cols] = part
        else:
            acc_ref[m, :, cols] += part

    nt = s // tk
    for phase in range(len(streams) + 2):
        stage_items = []
        if phase < len(streams):
            stage_items.append([functools.partial(logits_item, streams[phase], t) for t in range(nt)])
        if 0 <= phase - 1 < len(streams):
            stage_items.append([functools.partial(exp_item, streams[phase - 1], t) for t in range(nt)])
        if 0 <= phase - 2 < len(streams):
            stage_items.append([functools.partial(pv_item, streams[phase - 2], blk) for blk in range(nkb)])
        for i in range(nt):
            for items in stage_items:
                per = nt // len(items)
                if i % per == 0:
                    items[i // per]()
    outs = [acc_ref[m, :ATT_V_DIM, :] / acc_ref[m, ATT_V_DIM:ATT_V_DIM + 1, :] for m in range(2)]
    att = outs[0] - lam * outs[1]
    att = att * lax.rsqrt(jnp.mean(att * att, axis=0, keepdims=True) + EPS)
    o_ref[0] = (att * (g_ref[...] * (1.0 - lam_init))).T.astype(BF16)


def _attention(q, k, v_t, tbl, lq1, lk1, lq2, lk2, subln_g, lam_init, tq):
    bsz, s, _ = q.shape
    nkb, kb = v_t.shape[1], v_t.shape[3]
    small = lambda n: pl.BlockSpec((1, n), lambda h, i, b: (0, 0))
    return pl.pallas_call(
        functools.partial(_attn_body, lam_init=lam_init, tk=2 * LANES),
        grid=(ATT_HEADS, s // tq, bsz),
        in_specs=[pl.BlockSpec(memory_space=pltpu.SMEM),
                  small(ATT_QK_DIM), small(ATT_QK_DIM), small(ATT_QK_DIM), small(ATT_QK_DIM),
                  pl.BlockSpec((ATT_V_DIM, 1), lambda h, i, b: (0, 0)),
                  pl.BlockSpec((1, tq, ATT_V_DIM), lambda h, i, b: (b, i, h)),
                  pl.BlockSpec((1, s, ATT_V_DIM), lambda h, i, b: (b, 0, h)),
                  pl.BlockSpec((1, nkb, ATT_V_DIM, kb), lambda h, i, b: (b, 0, h, 0))],
        out_specs=pl.BlockSpec((1, tq, ATT_V_DIM), lambda h, i, b: (b, i, h)),
        out_shape=jax.ShapeDtypeStruct((bsz, s, ATT_HEADS * ATT_V_DIM), BF16),
        scratch_shapes=[pltpu.VMEM((s, tq), F32),
                        pltpu.VMEM((2, s, tq), F32),
                        pltpu.VMEM((2, s, tq), BF16),
                        pltpu.VMEM((2, ATT_V_DIM + 16, tq), F32)],
        compiler_params=_params("parallel", "parallel", "arbitrary"),
        name="diff_attention",
    )(tbl, lq1, lk1, lq2, lk2, subln_g.reshape(ATT_V_DIM, 1), q, k, v_t)


def _head_selector(off):
    r = lax.broadcasted_iota(I32, (LANES, SSD_WIDTH), 0)
    c = lax.broadcasted_iota(I32, (LANES, SSD_WIDTH), 1)
    return jnp.where(r - off == c // SSD_HEADDIM, 1.0, 0.0).astype(BF16)


def _expand_heads(cols, selector):
    hi = cols.astype(BF16)
    lo = (cols - hi.astype(F32)).astype(BF16)
    return (jnp.dot(hi, selector, preferred_element_type=F32)
            + jnp.dot(lo, selector, preferred_element_type=F32))


def _ssd_body(z_ref, xbc_ref, dtr_ref, cw_ref, cb_ref, dtb_ref, alog_ref, dskip_ref, ng_ref, o_ref,
              xp_ref, xc_ref, dt_ref, da_ref, cs_ref, cst_ref, ecst_ref, dtt_ref, bt_ref, y_ref, hf_ref, hb_ref):
    s = xbc_ref.shape[1]
    ch = SSD_CHUNK
    assert ch == LANES
    nc = s // ch
    pad = 8
    nconv = xbc_ref.shape[2]

    xp_ref[0:pad, :] = jnp.zeros((pad, nconv), F32)
    xp_ref[pad + s:pad + s + pad, :] = jnp.zeros((pad, nconv), F32)
    xp_ref[pad:pad + s, :] = xbc_ref[0]

    def conv_chunk(c, carry):
        r0 = pl.multiple_of(c * ch, ch)
        win = xp_ref[pl.ds(r0, ch + 2 * pad), :]
        acc = jnp.broadcast_to(cb_ref[...], (ch, nconv))
        for j in range(SSD_CONV):
            lo = pad + j - SSD_CONV // 2
            acc = acc + cw_ref[j:j + 1, :] * win[lo:lo + ch, :]
        xc_ref[pl.ds(r0, ch), :] = _silu(acc)
        return carry

    lax.fori_loop(0, nc, conv_chunk, 0)

    lane = lax.broadcasted_iota(I32, (s, LANES), 1)
    pre = dtr_ref[0] + dtb_ref[...]
    dt = jnp.maximum(pre, 0.0) + jnp.log1p(jnp.exp(-jnp.abs(pre)))
    dt = jnp.where(lane < 2 * SSD_HEADS, dt, 0.0)
    dt_ref[...] = dt
    da_ref[...] = dt * (-jnp.exp(alog_ref[...]))

    row = lax.broadcasted_iota(I32, (ch, ch), 0)
    col = lax.broadcasted_iota(I32, (ch, ch), 1)
    tri = jnp.where(col <= row, 1.0, 0.0).astype(F32)
    lower = col <= row
    upper = col >= row
    lane_c = lax.broadcasted_iota(I32, (ch, LANES), 1)
    st_rows = lax.broadcasted_iota(I32, (SSD_BC, SSD_WIDTH), 0) // SSD_STATE
    st_cols = lax.broadcasted_iota(I32, (SSD_BC, SSD_WIDTH), 1) // (SSD_WIDTH // SSD_GROUPS)
    same_group = st_rows == st_cols
    sel_f = _head_selector(0)
    sel_b = _head_selector(SSD_HEADS)

    hf_ref[...] = jnp.zeros_like(hf_ref)
    hb_ref[...] = jnp.zeros_like(hb_ref)

    def wide(x):
        return jnp.concatenate([x[c * ch:(c + 1) * ch, :] for c in range(nc)], axis=1)

    da_wide = wide(da_ref[...])
    cs_wide = jnp.dot(tri, da_wide, precision=HIGHEST, preferred_element_type=F32)
    for c in range(nc):
        cs_ref[c * ch:(c + 1) * ch, :] = cs_wide[:, c * LANES:(c + 1) * LANES]
    cst_ref[...] = cs_wide.T
    ecst_ref[...] = (cs_wide - da_wide).T
    dtt_ref[...] = wide(dt_ref[...]).T
    bt_ref[...] = wide(xc_ref[:, SSD_WIDTH:SSD_WIDTH + SSD_BC]).T.astype(BF16)

    def fwd_chunk(c, carry):
        r0 = pl.multiple_of(c * ch, ch)
        da = da_ref[pl.ds(r0, ch), :]
        dtc = dt_ref[pl.ds(r0, ch), :]
        cs = cs_ref[pl.ds(r0, ch), :]
        tot = cs[ch - 1:ch, :]
        ecs = cs - da
        nh2 = 2 * SSD_HEADS
        cs_t = cst_ref[pl.ds(r0, nh2), :]
        ecs_t = ecst_ref[pl.ds(r0, nh2), :]
        dt_t = dtt_ref[pl.ds(r0, nh2), :]
        xs = xc_ref[pl.ds(r0, ch), 0:SSD_WIDTH]
        cm = xc_ref[pl.ds(r0, ch), SSD_WIDTH + SSD_BC:SSD_WIDTH + 2 * SSD_BC]
        bt = bt_ref[pl.ds(r0, ch), :]
        cmb = cm.astype(BF16)
        xsb = xs.astype(BF16)

        g_mats = []
        for g in range(SSD_GROUPS):
            cg = jnp.where(lane_c // SSD_STATE == g, cmb, jnp.zeros_like(cmb))
            g_mats.append(jnp.dot(cg, bt, preferred_element_type=F32))
        for pair in range(SSD_HEADS // 2):
            xpair = xsb[:, pair * LANES:(pair + 1) * LANES]
            ypair = jnp.zeros((ch, LANES), F32)
            for sub in range(2):
                hh = 2 * pair + sub
                g = hh // (SSD_HEADS // SSD_GROUPS)
                d_f = cs[:, hh:hh + 1] - cs_t[hh:hh + 1, :]
                d_b = ecs_t[SSD_HEADS + hh:SSD_HEADS + hh + 1, :] - ecs[:, SSD_HEADS + hh:SSD_HEADS + hh + 1]
                decay = jnp.exp(jnp.where(lower, d_f, d_b))
                wgt = (jnp.where(lower, dt_t[hh:hh + 1, :], 0.0)
                       + jnp.where(upper, dt_t[SSD_HEADS + hh:SSD_HEADS + hh + 1, :], 0.0))
                m = (g_mats[g] * decay * wgt).astype(BF16)
                xh = jnp.where(lane_c // SSD_HEADDIM == sub, xpair, jnp.zeros_like(xpair))
                ypair = ypair + jnp.dot(m, xh, preferred_element_type=F32)
            y_ref[pl.ds(r0, ch), pair * LANES:(pair + 1) * LANES] = ypair

        hf = hf_ref[...]
        factors = _expand_heads(jnp.concatenate(
            [jnp.exp(cs), dtc * jnp.exp(tot - cs), jnp.broadcast_to(jnp.exp(tot), (8, LANES))], axis=0), sel_f)
        y_off = jnp.dot(cmb, hf.astype(BF16), preferred_element_type=F32) * factors[0:ch]
        y_ref[pl.ds(r0, ch), :] = y_ref[pl.ds(r0, ch), :] + y_off
        xd = (xs * factors[ch:2 * ch]).astype(BF16)
        upd = jnp.dot(bt, xd, preferred_element_type=F32)
        hf_ref[...] = hf * factors[2 * ch:2 * ch + 1] + jnp.where(same_group, upd, 0.0)
        return carry

    lax.fori_loop(0, nc, fwd_chunk, 0, unroll=4)

    def bwd_chunk(i, carry):
        c = nc - 1 - i
        r0 = pl.multiple_of(c * ch, ch)
        da = da_ref[pl.ds(r0, ch), :]
        dtc = dt_ref[pl.ds(r0, ch), :]
        cs = cs_ref[pl.ds(r0, ch), :]
        tot = cs[ch - 1:ch, :]
        ecs = cs - da
        xs = xc_ref[pl.ds(r0, ch), 0:SSD_WIDTH]
        cm = xc_ref[pl.ds(r0, ch), SSD_WIDTH + SSD_BC:SSD_WIDTH + 2 * SSD_BC]
        bt = bt_ref[pl.ds(r0, ch), :]

        hb = hb_ref[...]
        factors = _expand_heads(jnp.concatenate(
            [jnp.exp(tot - ecs), dtc * jnp.exp(ecs), jnp.broadcast_to(jnp.exp(tot), (8, LANES))], axis=0), sel_b)
        y_off = jnp.dot(cm.astype(BF16), hb.astype(BF16), preferred_element_type=F32) * factors[0:ch]
        xd = (xs * factors[ch:2 * ch]).astype(BF16)
        upd = jnp.dot(bt, xd, preferred_element_type=F32)
        hb_ref[...] = hb * factors[2 * ch:2 * ch + 1] + jnp.where(same_group, upd, 0.0)

        y = y_ref[pl.ds(r0, ch), :] + y_off + dskip_ref[...] * xs
        y = y * _silu(z_ref[0, pl.ds(r0, ch), :])
        gw = SSD_WIDTH // SSD_GROUPS
        yn = jnp.concatenate([_rms(y[:, g * gw:(g + 1) * gw]) for g in range(SSD_GROUPS)], axis=-1)
        o_ref[0, pl.ds(r0, ch), :] = (yn * ng_ref[...]).astype(BF16)
        return carry

    lax.fori_loop(0, nc, bwd_chunk, 0, unroll=2)


def _ssd(z, xbc, dt_raw, conv_w, conv_b, dt_bias, a_log, d_skip, norm_g):
    bsz, s, nconv = xbc.shape
    tok = lambda wd: pl.BlockSpec((1, s, wd), lambda b: (b, 0, 0))
    full = lambda a: pl.BlockSpec(a.shape, lambda b: (0, 0))
    return pl.pallas_call(
        _ssd_body,
        grid=(bsz,),
        in_specs=[tok(SSD_WIDTH), tok(nconv), tok(LANES), full(conv_w), full(conv_b), full(dt_bias),
                  full(a_log), full(d_skip), full(norm_g)],
        out_specs=tok(SSD_WIDTH),
        out_shape=jax.ShapeDtypeStruct((bsz, s, SSD_WIDTH), BF16),
        scratch_shapes=[pltpu.VMEM((s + 16, nconv), F32),
                        pltpu.VMEM((s, nconv), F32),
                        pltpu.VMEM((s, LANES), F32),
                        pltpu.VMEM((s, LANES), F32),
                        pltpu.VMEM((s, LANES), F32),
                        pltpu.VMEM((s, LANES), F32),
                        pltpu.VMEM((s, LANES), F32),
                        pltpu.VMEM((s, LANES), F32),
                        pltpu.VMEM((s, SSD_BC), BF16),
                        pltpu.VMEM((s, SSD_WIDTH), F32),
                        pltpu.VMEM((SSD_BC, SSD_WIDTH), F32),
                        pltpu.VMEM((SSD_BC, SSD_WIDTH), F32)],
        compiler_params=_params("parallel"),
        name="ssd_mixer",
    )(z, xbc, dt_raw, conv_w, conv_b, dt_bias, a_log, d_skip, norm_g)


def _outproj_body(att_ref, ssd_ref, x_ref, g1_ref, sh_ref, sc_ref, ng_ref, w_ref, rw_ref,
                  x1_ref, h2_ref, aff_ref):
    na = att_ref.shape[2]
    mix = (jnp.dot(att_ref[0], w_ref[0:na, :], preferred_element_type=F32)
           + jnp.dot(ssd_ref[0], w_ref[na:, :], preferred_element_type=F32))
    x1 = x_ref[0] + g1_ref[0] * mix
    x1_ref[0] = x1
    h2 = _rms(x1) * ng_ref[...]
    h2 = h2 * (1.0 + sc_ref[0]) + sh_ref[0]
    h2_hi = h2.astype(BF16)
    h2_ref[0] = h2_hi
    h2_lo = (h2 - h2_hi.astype(F32)).astype(BF16)
    nt = lambda a, b: lax.dot_general(a, b, (((1,), (1,)), ((), ())), preferred_element_type=F32)
    logits = nt(rw_ref[0], h2_hi) + (nt(rw_ref[0], h2_lo) + nt(rw_ref[1], h2_hi))
    e = jnp.exp(logits - jnp.max(logits, axis=0, keepdims=True))
    aff_ref[0] = e / jnp.sum(e, axis=0, keepdims=True)


def _out_proj(att, ssd, x, g1, sh2, sc2, ng, w_out, rw_t, ts):
    bsz, s, d = x.shape
    tok = lambda wd: pl.BlockSpec((1, ts, wd), lambda b, t: (b, t, 0))
    vec = pl.BlockSpec((1, 1, d), lambda b, t: (b, 0, 0))
    full = lambda a: pl.BlockSpec(a.shape, lambda b, t: (0, 0))
    ne = rw_t.shape[1]
    return pl.pallas_call(
        _outproj_body,
        grid=(bsz, s // ts),
        in_specs=[tok(att.shape[2]), tok(ssd.shape[2]), tok(d), vec, vec, vec, full(ng), full(w_out),
                  pl.BlockSpec(rw_t.shape, lambda b, t: (0, 0, 0))],
        out_specs=[tok(d), tok(d), pl.BlockSpec((1, ne, ts), lambda b, t: (b, 0, t))],
        out_shape=[jax.ShapeDtypeStruct((bsz, s, d), F32),
                   jax.ShapeDtypeStruct((bsz, s, d), BF16),
                   jax.ShapeDtypeStruct((bsz, ne, s), F32)],
        compiler_params=_params("parallel", "arbitrary"),
        name="out_proj_router",
    )(att, ssd, x, g1, sh2, sc2, ng, w_out, rw_t)


def _topk_body(aff_ref, pos_ref, post_ref, ut_ref, *, cap):
    nrow, s = aff_ref.shape
    r = lax.broadcasted_iota(I32, (s, s), 0)
    c = lax.broadcasted_iota(I32, (s, s), 1)
    ut_ref[...] = jnp.where(r < c, 1.0, 0.0).astype(BF16)

    bits = pltpu.bitcast(aff_ref[...], I32)

    def count(mask):
        return jnp.sum(jnp.where(mask, 1, 0), axis=-1, keepdims=True)

    thr = jnp.zeros((nrow, 1), I32)
    for bit in range(30, -1, -1):
        cand = thr | (1 << bit)
        thr = jnp.where(count(bits >= cand) >= cap, cand, thr)
    gt = bits > thr
    eq = bits == thr
    need = cap - count(gt)
    eq_rank = jnp.dot(jnp.where(eq, 1.0, 0.0).astype(BF16), ut_ref[...], preferred_element_type=F32)
    sel = gt | (eq & (eq_rank < need.astype(F32)))
    slot = jnp.dot(jnp.where(sel, 1.0, 0.0).astype(BF16), ut_ref[...], preferred_element_type=F32)
    pos = jnp.where(sel, slot, -1.0)
    pos_ref[...] = pos.astype(I32)
    post_ref[...] = pos.T.astype(I32)


def _topk(aff_t, cap):
    bsz, ne, s = aff_t.shape
    nrow = bsz * ne
    pos, pos_t = pl.pallas_call(
        functools.partial(_topk_body, cap=cap),
        grid=(1,),
        in_specs=[pl.BlockSpec((nrow, s), lambda i: (0, 0))],
        out_specs=[pl.BlockSpec((nrow, s), lambda i: (0, 0)),
                   pl.BlockSpec((s, nrow), lambda i: (0, 0))],
        out_shape=[jax.ShapeDtypeStruct((nrow, s), I32),
                   jax.ShapeDtypeStruct((s, nrow), I32)],
        scratch_shapes=[pltpu.VMEM((s, s), BF16)],
        compiler_params=_params("arbitrary"),
        name="expert_topk",
    )(aff_t.reshape(nrow, s))
    return pos.reshape(bsz, ne, s), pos_t.reshape(s, bsz, ne).transpose(1, 0, 2)


def _gather_body(h2_ref, pos_ref, aff_ref, xg_ref, gc_ref, *, cap):
    ne, s = pos_ref.shape[1], pos_ref.shape[2]
    slot = lax.broadcasted_iota(I32, (cap, s), 0)
    h2 = h2_ref[0]
    for e in range(ne):
        hit = pos_ref[0, e:e + 1, :] == slot
        xg_ref[e] = jnp.dot(jnp.where(hit, 1.0, 0.0).astype(BF16), h2,
                            preferred_element_type=F32).astype(BF16)
        gate = jnp.sum(jnp.where(hit, aff_ref[0, e:e + 1, :], 0.0), axis=-1, keepdims=True)
        gc_ref[e] = jnp.broadcast_to(gate, (cap, LANES))


def _gather(h2, pos, aff_t, cap):
    bsz, s, d = h2.shape
    ne = pos.shape[1]
    return pl.pallas_call(
        functools.partial(_gather_body, cap=cap),
        grid=(bsz,),
        in_specs=[pl.BlockSpec((1, s, d), lambda b: (b, 0, 0)),
                  pl.BlockSpec((1, ne, s), lambda b: (b, 0, 0)),
                  pl.BlockSpec((1, ne, s), lambda b: (b, 0, 0))],
        out_specs=[pl.BlockSpec((ne, cap, d), lambda b: (0, b, 0)),
                   pl.BlockSpec((ne, cap, LANES), lambda b: (0, b, 0))],
        out_shape=[jax.ShapeDtypeStruct((ne, bsz * cap, d), BF16),
                   jax.ShapeDtypeStruct((ne, bsz * cap, LANES), F32)],
        compiler_params=_params("parallel"),
        name="moe_gather",
    )(h2, pos, aff_t)


def _ffn_body(xg_ref, gc_ref, wg_ref, wu_ref, wd_ref, y_ref, acc_ref):
    f = pl.program_id(1)

    @pl.when(f == 0)
    def _():
        acc_ref[...] = jnp.zeros_like(acc_ref)

    wg = wg_ref[0].astype(BF16)
    wu = wu_ref[0].astype(BF16)
    wd = wd_ref[0].astype(BF16)
    half = xg_ref.shape[1] // 2
    for r in range(2):
        rows = slice(r * half, (r + 1) * half)
        xg = xg_ref[0, rows, :]
        gate = jnp.dot(xg, wg, preferred_element_type=F32)
        up = jnp.dot(xg, wu, preferred_element_type=F32)
        hid = (_silu(gate) * up).astype(BF16)
        acc_ref[rows, :] += jnp.dot(hid, wd, preferred_element_type=F32)

    @pl.when(f == pl.num_programs(1) - 1)
    def _():
        m, d = acc_ref.shape
        gcol = gc_ref[0]
        y_ref[0] = (acc_ref[...] * jnp.concatenate([gcol] * (d // LANES), axis=-1)).astype(BF16)


def _ffn(xg, gc, w_gate, w_up, w_down, tf):
    ne, m, d = xg.shape
    ff = w_gate.shape[2]
    return pl.pallas_call(
        _ffn_body,
        grid=(ne, ff // tf),
        in_specs=[pl.BlockSpec((1, m, d), lambda e, f: (e, 0, 0)),
                  pl.BlockSpec((1, m, LANES), lambda e, f: (e, 0, 0)),
                  pl.BlockSpec((1, d, tf), lambda e, f: (e, 0, f)),
                  pl.BlockSpec((1, d, tf), lambda e, f: (e, 0, f)),
                  pl.BlockSpec((1, tf, d), lambda e, f: (e, f, 0))],
        out_specs=pl.BlockSpec((1, m, d), lambda e, f: (e, 0, 0)),
        out_shape=jax.ShapeDtypeStruct((ne, m, d), BF16),
        scratch_shapes=[pltpu.VMEM((m, d), F32)],
        compiler_params=_params("parallel", "arbitrary"),
        name="moe_ffn",
    )(xg, gc, w_gate, w_up, w_down)


def _scatter_body(post_ref, y_ref, x1_ref, g2_ref, ng_ref, o_ref, *, cap):
    ne = y_ref.shape[0]
    ts = post_ref.shape[1]
    d = y_ref.shape[2]
    slot = lax.broadcasted_iota(I32, (ts, cap), 1)
    post = post_ref[0]
    onehot = jnp.concatenate(
        [jnp.where(post[:, e:e + 1] == slot, 1.0, 0.0).astype(BF16) for e in range(ne)], axis=-1)
    moe = jnp.dot(onehot, y_ref[...].reshape(ne * cap, d), preferred_element_type=F32)
    x2 = x1_ref[0] + g2_ref[0] * moe
    o_ref[0] = _rms(x2) * ng_ref[...]


def _scatter(post, y, x1, g2, ng, cap, ts):
    bsz, s, d = x1.shape
    ne = y.shape[0]
    return pl.pallas_call(
        functools.partial(_scatter_body, cap=cap),
        grid=(bsz, s // ts),
        in_specs=[pl.BlockSpec((1, ts, ne), lambda b, t: (b, t, 0)),
                  pl.BlockSpec((ne, cap, d), lambda b, t: (0, b, 0)),
                  pl.BlockSpec((1, ts, d), lambda b, t: (b, t, 0)),
                  pl.BlockSpec((1, 1, d), lambda b, t: (b, 0, 0)),
                  pl.BlockSpec((1, d), lambda b, t: (0, 0))],
        out_specs=pl.BlockSpec((1, ts, d), lambda b, t: (b, t, 0)),
        out_shape=jax.ShapeDtypeStruct((bsz, s, d), F32),
        compiler_params=_params("parallel", "arbitrary"),
        name="moe_scatter_final",
    )(post, y, x1, g2, ng)


def _pad_lanes(a, n=LANES):
    return jnp.pad(a, [(0, 0)] * (a.ndim - 1) + [(0, n - a.shape[-1])])


def kernel(x, c, ada_w, ada_b, norm_mix_g, norm_ffn_g, norm_final_g, w_in, lambda_q1, lambda_k1, lambda_q2, lambda_k2, attn_subln_g, rel_bias_table, conv_w, conv_b, dt_bias_f, dt_bias_b, A_log_f, A_log_b, D_skip, ssm_norm_g, w_out, router_w, w_gate, w_up, w_down):
    bsz, s, d = x.shape
    depth = ada_w.shape[0]
    ne = router_w.shape[2]
    cap = CAPACITY_FACTOR * s // ne
    att_w = ATT_HEADS * ATT_V_DIM
    nconv = conv_w.shape[3]
    splits = (att_w, att_w, SSD_WIDTH, nconv, LANES)
    offs = [0]
    for wd in splits:
        offs.append(offs[-1] + wd)
    row = lambda a: a.reshape(1, -1)

    assert depth == 1, "the final RMSNorm is fused into the (single) layer's scatter kernel"
    for l in range(depth):
        lam_init = 0.8 - 0.6 * math.exp(-0.3 * l)
        mod = _ada_mod(c, ada_w[l], ada_b[l])
        sh1, sc1, g1, sh2, sc2, g2 = [m.reshape(bsz, 1, d) for m in jnp.split(mod, 6, axis=-1)]

        w_l = w_in[l]
        w_pad = _pad_lanes(jnp.concatenate([w_l[:, :2 * att_w], w_l[:, 3 * att_w:]], axis=1), offs[-1]).astype(BF16)
        w_vt = w_l[:, 2 * att_w:3 * att_w].T.astype(BF16)
        q, k, v_t, z, xbc, dt_raw = _in_proj(x, sh1, sc1, row(norm_mix_g[l]), w_pad, w_vt, offs, ts=512)

        att = _attention(q, k, v_t, rel_bias_table, row(lambda_q1[l]), row(lambda_k1[l]),
                         row(lambda_q2[l]), row(lambda_k2[l]), row(attn_subln_g[l]), lam_init, tq=512)

        dt_bias = _pad_lanes(jnp.concatenate([dt_bias_f[l], dt_bias_b[l]]).reshape(1, -1))
        a_log = _pad_lanes(jnp.concatenate([A_log_f[l], A_log_b[l]]).reshape(1, -1))
        d_skip = jnp.repeat(D_skip[l], SSD_HEADDIM).reshape(1, -1)
        ssd = _ssd(z, xbc, dt_raw, conv_w[l].reshape(SSD_CONV, nconv), row(conv_b[l]), dt_bias, a_log,
                   d_skip, row(ssm_norm_g[l]))

        rw_t = router_w[l].T
        rw_hi = rw_t.astype(BF16)
        rw_split = jnp.stack([rw_hi, (rw_t - rw_hi.astype(F32)).astype(BF16)])
        x1, h2, aff_t = _out_proj(att, ssd, x, g1, sh2, sc2, row(norm_ffn_g[l]), w_out[l].astype(BF16),
                                  rw_split, ts=512)
        pos, pos_t = _topk(aff_t, cap)
        xg, gc = _gather(h2, pos, aff_t, cap)
        y = _ffn(xg, gc, w_gate[l], w_up[l], w_down[l], tf=256)
        x = _scatter(pos_t, y, x1, g2, row(norm_final_g), cap, ts=512)
    return x
```

```python
import functools
import math

import jax
import jax.numpy as jnp
from jax import lax
from jax.experimental import pallas as pl
from jax.experimental.pallas import tpu as pltpu

F32 = jnp.float32
BF16 = jnp.bfloat16
I32 = jnp.int32
HIGHEST = lax.Precision.HIGHEST

ATT_HEADS = 4
ATT_QK_DIM = 64
ATT_V_DIM = 128
N_BUCKETS = 32
SSD_HEADS = 8
SSD_HEADDIM = 64
SSD_GROUPS = 2
SSD_STATE = 64
SSD_CONV = 5
SSD_CHUNK = 128
SSD_WIDTH = SSD_HEADS * SSD_HEADDIM
SSD_BC = SSD_GROUPS * SSD_STATE
N_EXPERTS = 16
CAPACITY_FACTOR = 2
EPS = 1e-6
LANES = 128
NEG_BIG = -1e30
LOG2E = math.log2(math.e)

VMEM_LIMIT = 56 * 1024 * 1024


def _params(*semantics):
    return pltpu.CompilerParams(dimension_semantics=semantics, vmem_limit_bytes=VMEM_LIMIT)


def _silu(v):
    return v * jax.nn.sigmoid(v)


def _rms(v, eps=EPS):
    return v * lax.rsqrt(jnp.mean(v * v, axis=-1, keepdims=True) + eps)


def _split_bf16(v):
    hi = v.astype(BF16)
    return hi, (v - hi.astype(F32)).astype(BF16)


def _mod_body(c_ref, w_ref, b_ref, o_ref):
    a_hi, a_lo = _split_bf16(_silu(c_ref[...]))
    w_hi, w_lo = _split_bf16(w_ref[...])
    dot = functools.partial(jnp.dot, preferred_element_type=F32)
    o_ref[...] = dot(a_hi, w_hi) + (dot(a_lo, w_hi) + dot(a_hi, w_lo)) + b_ref[...]


def _ada_mod(c, w, b):
    bsz, d = c.shape
    n = w.shape[1]
    tn = n // 4
    return pl.pallas_call(
        _mod_body,
        grid=(n // tn,),
        in_specs=[pl.BlockSpec((bsz, d), lambda j: (0, 0)),
                  pl.BlockSpec((d, tn), lambda j: (0, j)),
                  pl.BlockSpec((1, tn), lambda j: (0, j))],
        out_specs=pl.BlockSpec((bsz, tn), lambda j: (0, j)),
        out_shape=jax.ShapeDtypeStruct((bsz, n), F32),
        compiler_params=_params("arbitrary"),
        name="ada_mod",
    )(c, w, b.reshape(1, n))


def _inproj_body(x_ref, sh_ref, sc_ref, g_ref, w_ref, wvt_ref, q_ref, k_ref, vt_ref, z_ref, xbc_ref, dt_ref,
                 *, offs, qscale):
    h = _rms(x_ref[0]) * g_ref[...]
    h = (h * (1.0 + sc_ref[0]) + sh_ref[0]).astype(BF16)

    def proj(i):
        return jnp.dot(h, w_ref[:, offs[i]:offs[i + 1]], preferred_element_type=F32)

    q_ref[0] = (proj(0) * qscale).astype(BF16)
    k_ref[0] = proj(1).astype(BF16)
    vt_ref[0, 0] = lax.dot_general(wvt_ref[...], h, (((1,), (1,)), ((), ())),
                                   preferred_element_type=F32).astype(BF16)
    z_ref[0] = proj(2)
    xbc_ref[0] = proj(3)
    dt_ref[0] = proj(4)


def _in_proj(x, sh, sc, g, w_pad, w_vt, offs, ts):
    bsz, s, d = x.shape
    widths = [offs[i + 1] - offs[i] for i in range(5)]
    dts = [BF16, BF16, F32, F32, F32]
    nv = w_vt.shape[0]
    tok = lambda wd: pl.BlockSpec((1, ts, wd), lambda b, t: (b, t, 0))
    vec = pl.BlockSpec((1, 1, d), lambda b, t: (b, 0, 0))
    return pl.pallas_call(
        functools.partial(_inproj_body, offs=tuple(offs), qscale=ATT_QK_DIM ** -0.5 * LOG2E),
        grid=(bsz, s // ts),
        in_specs=[tok(d), vec, vec,
                  pl.BlockSpec((1, d), lambda b, t: (0, 0)),
                  pl.BlockSpec(w_pad.shape, lambda b, t: (0, 0)),
                  pl.BlockSpec(w_vt.shape, lambda b, t: (0, 0))],
        out_specs=[tok(widths[0]), tok(widths[1]), pl.BlockSpec((1, 1, nv, ts), lambda b, t: (b, t, 0, 0))]
                  + [tok(wd) for wd in widths[2:]],
        out_shape=[jax.ShapeDtypeStruct((bsz, s, widths[0]), BF16), jax.ShapeDtypeStruct((bsz, s, widths[1]), BF16),
                   jax.ShapeDtypeStruct((bsz, s // ts, nv, ts), BF16)]
                  + [jax.ShapeDtypeStruct((bsz, s, wd), dt) for wd, dt in zip(widths[2:], dts[2:])],
        compiler_params=_params("parallel", "arbitrary"),
        name="in_proj",
    )(x, sh, sc, g, w_pad, w_vt)


def _t5_bucket(rel):
    n = jnp.abs(rel)
    large = jnp.full(rel.shape, 8, I32)
    for t in (12, 16, 23, 32, 46, 64, 91):
        large = large + jnp.where(n >= t, 1, 0)
    return jnp.where(rel > 0, 16, 0) + jnp.where(n < 8, n, large)


def _attn_body(tbl_ref, lq1_ref, lk1_ref, lq2_ref, lk2_ref, g_ref, q_ref, k_ref, vt_ref, o_ref,
               bias_ref, s_ref, e_ref, acc_ref, *, lam_init, tk):
    h = pl.program_id(0)
    qi = pl.program_id(1)
    s, tq = bias_ref.shape
    ring = 1024
    assert tk + tq - 1 <= ring

    @pl.when(pl.program_id(2) == 0)
    def _():
        j = lax.broadcasted_iota(I32, (8, ring), 1)
        for kt in range(s // tk):
            bucket = _t5_bucket((kt * tk + tk - 1) - qi * tq - j)
            w = jnp.zeros((8, ring), F32)
            for bb in range(N_BUCKETS):
                w = jnp.where(bucket == bb, tbl_ref[bb, h] * LOG2E, w)
            big = jnp.broadcast_to(w[0:1], (tk, ring))
            rolled = pltpu.roll(big, ring - (tk - 1), axis=1, stride=1, stride_axis=0)
            bias_ref[kt * tk:(kt + 1) * tk, :] = rolled[:, :tq]

    lam = (jnp.exp(jnp.sum(lq1_ref[...] * lk1_ref[...], axis=-1, keepdims=True))
           - jnp.exp(jnp.sum(lq2_ref[...] * lk2_ref[...], axis=-1, keepdims=True)) + lam_init)
    q = q_ref[0]
    lane = lax.broadcasted_iota(I32, q.shape, 1)
    qms = [jnp.where((lane >= m * ATT_QK_DIM) & (lane < (m + 1) * ATT_QK_DIM), q, jnp.zeros_like(q))
           for m in range(2)]
    qw = 2 * LANES
    nqc = tq // qw
    nkb, kb = vt_ref.shape[1], vt_ref.shape[3]
    ones_rows = 16

    streams = [(m, c) for m in range(2) for c in range(nqc)]
    run_max = {st: jnp.full((8, qw), NEG_BIG, F32) for st in streams}
    row_max = {}

    def logits_item(st, t):
        m, c = st
        rows, cols = slice(t * tk, (t + 1) * tk), slice(c * qw, (c + 1) * qw)
        logit = lax.dot_general(k_ref[0, rows, :], qms[m][cols, :], (((1,), (1,)), ((), ())),
                                preferred_element_type=F32) + bias_ref[rows, cols]
        s_ref[m, rows, cols] = logit
        run_max[st] = jnp.maximum(run_max[st], jnp.max(logit.reshape(tk // 8, 8, qw), axis=0))
        if t == s // tk - 1:
            row_max[st] = jnp.max(run_max[st], axis=0, keepdims=True)

    def exp_item(st, t):
        m, c = st
        rows, cols = slice(t * tk, (t + 1) * tk), slice(c * qw, (c + 1) * qw)
        e_ref[m, rows, cols] = jnp.exp2(s_ref[m, rows, cols] - row_max[st]).astype(BF16)

    def pv_item(st, blk):
        m, c = st
        rows, cols = slice(blk * kb, (blk + 1) * kb), slice(c * qw, (c + 1) * qw)
        vt_ext = jnp.concatenate([vt_ref[0, blk], jnp.ones((ones_rows, kb), BF16)], axis=0)
        part = jnp.dot(vt_ext, e_ref[m, rows, cols], preferred_element_type=F32)
        if blk == 0:
            acc_ref[m, :, cols] = part
        else:
            acc_ref[m, :, cols] += part

    nt = s // tk
    for phase in range(len(streams) + 2):
        stage_items = []
        if phase < len(streams):
            stage_items.append([functools.partial(logits_item, streams[phase], t) for t in range(nt)])
        if 0 <= phase - 1 < len(streams):
            stage_items.append([functools.partial(exp_item, streams[phase - 1], t) for t in range(nt)])
        if 0 <= phase - 2 < len(streams):
            stage_items.append([functools.partial(pv_item, streams[phase - 2], blk) for blk in range(nkb)])
        for i in range(nt):
            for items in stage_items:
                per = nt // len(items)
                if i % per == 0:
                    items[i // per]()
    outs = [acc_ref[m, :ATT_V_DIM, :] / acc_ref[m, ATT_V_DIM:ATT_V_DIM + 1, :] for m in range(2)]
    att = outs[0] - lam * outs[1]
    att = att * lax.rsqrt(jnp.mean(att * att, axis=0, keepdims=True) + EPS)
    o_ref[0] = (att * (g_ref[...] * (1.0 - lam_init))).T.astype(BF16)


def _attention(q, k, v_t, tbl, lq1, lk1, lq2, lk2, subln_g, lam_init, tq):
    bsz, s, _ = q.shape
    nkb, kb = v_t.shape[1], v_t.shape[3]
    small = lambda n: pl.BlockSpec((1, n), lambda h, i, b: (0, 0))
    return pl.pallas_call(
        functools.partial(_attn_body, lam_init=lam_init, tk=2 * LANES),
        grid=(ATT_HEADS, s // tq, bsz),
        in_specs=[pl.BlockSpec(memory_space=pltpu.SMEM),
                  small(ATT_QK_DIM), small(ATT_QK_DIM), small(ATT_QK_DIM), small(ATT_QK_DIM),
                  pl.BlockSpec((ATT_V_DIM, 1), lambda h, i, b: (0, 0)),
                  pl.BlockSpec((1, tq, ATT_V_DIM), lambda h, i, b: (b, i, h)),
                  pl.BlockSpec((1, s, ATT_V_DIM), lambda h, i, b: (b, 0, h)),
                  pl.BlockSpec((1, nkb, ATT_V_DIM, kb), lambda h, i, b: (b, 0, h, 0))],
        out_specs=pl.BlockSpec((1, tq, ATT_V_DIM), lambda h, i, b: (b, i, h)),
        out_shape=jax.ShapeDtypeStruct((bsz, s, ATT_HEADS * ATT_V_DIM), BF16),
        scratch_shapes=[pltpu.VMEM((s, tq), F32),
                        pltpu.VMEM((2, s, tq), F32),
                        pltpu.VMEM((2, s, tq), BF16),
                        pltpu.VMEM((2, ATT_V_DIM + 16, tq), F32)],
        compiler_params=_params("parallel", "parallel", "arbitrary"),
        name="diff_attention",
    )(tbl, lq1, lk1, lq2, lk2, subln_g.reshape(ATT_V_DIM, 1), q, k, v_t)


def _head_selector(off):
    r = lax.broadcasted_iota(I32, (LANES, SSD_WIDTH), 0)
    c = lax.broadcasted_iota(I32, (LANES, SSD_WIDTH), 1)
    return jnp.where(r - off == c // SSD_HEADDIM, 1.0, 0.0).astype(BF16)


def _expand_heads(cols, selector):
    hi = cols.astype(BF16)
    lo = (cols - hi.astype(F32)).astype(BF16)
    return (jnp.dot(hi, selector, preferred_element_type=F32)
            + jnp.dot(lo, selector, preferred_element_type=F32))


def _ssd_body(z_ref, xbc_ref, dtr_ref, cw_ref, cb_ref, dtb_ref, alog_ref, dskip_ref, ng_ref, o_ref,
              xp_ref, xc_ref, dt_ref, da_ref, cs_ref, cst_ref, ecst_ref, dtt_ref, bt_ref, y_ref, hf_ref, hb_ref):
    s = xbc_ref.shape[1]
    ch = SSD_CHUNK
    assert ch == LANES
    nc = s // ch
    pad = 8
    nconv = xbc_ref.shape[2]

    xp_ref[0:pad, :] = jnp.zeros((pad, nconv), F32)
    xp_ref[pad + s:pad + s + pad, :] = jnp.zeros((pad, nconv), F32)
    xp_ref[pad:pad + s, :] = xbc_ref[0]

    def conv_chunk(c, carry):
        r0 = pl.multiple_of(c * ch, ch)
        win = xp_ref[pl.ds(r0, ch + 2 * pad), :]
        acc = jnp.broadcast_to(cb_ref[...], (ch, nconv))
        for j in range(SSD_CONV):
            lo = pad + j - SSD_CONV // 2
            acc = acc + cw_ref[j:j + 1, :] * win[lo:lo + ch, :]
        xc_ref[pl.ds(r0, ch), :] = _silu(acc)
        return carry

    lax.fori_loop(0, nc, conv_chunk, 0)

    lane = lax.broadcasted_iota(I32, (s, LANES), 1)
    pre = dtr_ref[0] + dtb_ref[...]
    dt = jnp.maximum(pre, 0.0) + jnp.log1p(jnp.exp(-jnp.abs(pre)))
    dt = jnp.where(lane < 2 * SSD_HEADS, dt, 0.0)
    dt_ref[...] = dt
    da_ref[...] = dt * (-jnp.exp(alog_ref[...]))

    row = lax.broadcasted_iota(I32, (ch, ch), 0)
    col = lax.broadcasted_iota(I32, (ch, ch), 1)
    tri = jnp.where(col <= row, 1.0, 0.0).astype(F32)
    lower = col <= row
    upper = col >= row
    lane_c = lax.broadcasted_iota(I32, (ch, LANES), 1)
    st_rows = lax.broadcasted_iota(I32, (SSD_BC, SSD_WIDTH), 0) // SSD_STATE
    st_cols = lax.broadcasted_iota(I32, (SSD_BC, SSD_WIDTH), 1) // (SSD_WIDTH // SSD_GROUPS)
    same_group = st_rows == st_cols
    sel_f = _head_selector(0)
    sel_b = _head_selector(SSD_HEADS)

    hf_ref[...] = jnp.zeros_like(hf_ref)
    hb_ref[...] = jnp.zeros_like(hb_ref)

    def wide(x):
        return jnp.concatenate([x[c * ch:(c + 1) * ch, :] for c in range(nc)], axis=1)

    da_wide = wide(da_ref[...])
    cs_wide = jnp.dot(tri, da_wide, precision=HIGHEST, preferred_element_type=F32)
    for c in range(nc):
        cs_ref[c * ch:(c + 1) * ch, :] = cs_wide[:, c * LANES:(c + 1) * LANES]
    cst_ref[...] = cs_wide.T
    ecst_ref[...] = (cs_wide - da_wide).T
    dtt_ref[...] = wide(dt_ref[...]).T
    bt_ref[...] = wide(xc_ref[:, SSD_WIDTH:SSD_WIDTH + SSD_BC]).T.astype(BF16)

    def fwd_chunk(c, carry):
        r0 = pl.multiple_of(c * ch, ch)
        da = da_ref[pl.ds(r0, ch), :]
        dtc = dt_ref[pl.ds(r0, ch), :]
        cs = cs_ref[pl.ds(r0, ch), :]
        tot = cs[ch - 1:ch, :]
        ecs = cs - da
        nh2 = 2 * SSD_HEADS
        cs_t = cst_ref[pl.ds(r0, nh2), :]
        ecs_t = ecst_ref[pl.ds(r0, nh2), :]
        dt_t = dtt_ref[pl.ds(r0, nh2), :]
        xs = xc_ref[pl.ds(r0, ch), 0:SSD_WIDTH]
        cm = xc_ref[pl.ds(r0, ch), SSD_WIDTH + SSD_BC:SSD_WIDTH + 2 * SSD_BC]
        bt = bt_ref[pl.ds(r0, ch), :]
        cmb = cm.astype(BF16)
        xsb = xs.astype(BF16)

        g_mats = []
        for g in range(SSD_GROUPS):
            cg = jnp.where(lane_c // SSD_STATE == g, cmb, jnp.zeros_like(cmb))
            g_mats.append(jnp.dot(cg, bt, preferred_element_type=F32))
        for pair in range(SSD_HEADS // 2):
            xpair = xsb[:, pair * LANES:(pair + 1) * LANES]
            ypair = jnp.zeros((ch, LANES), F32)
            for sub in range(2):
                hh = 2 * pair + sub
                g = hh // (SSD_HEADS // SSD_GROUPS)
                d_f = cs[:, hh:hh + 1] - cs_t[hh:hh + 1, :]
                d_b = ecs_t[SSD_HEADS + hh:SSD_HEADS + hh + 1, :] - ecs[:, SSD_HEADS + hh:SSD_HEADS + hh + 1]
                decay = jnp.exp(jnp.where(lower, d_f, d_b))
                wgt = (jnp.where(lower, dt_t[hh:hh + 1, :], 0.0)
                       + jnp.where(upper, dt_t[SSD_HEADS + hh:SSD_HEADS + hh + 1, :], 0.0))
                m = (g_mats[g] * decay * wgt).astype(BF16)
                xh = jnp.where(lane_c // SSD_HEADDIM == sub, xpair, jnp.zeros_like(xpair))
                ypair = ypair + jnp.dot(m, xh, preferred_element_type=F32)
            y_ref[pl.ds(r0, ch), pair * LANES:(pair + 1) * LANES] = ypair

        hf = hf_ref[...]
        factors = _expand_heads(jnp.concatenate(
            [jnp.exp(cs), dtc * jnp.exp(tot - cs), jnp.broadcast_to(jnp.exp(tot), (8, LANES))], axis=0), sel_f)
        y_off = jnp.dot(cmb, hf.astype(BF16), preferred_element_type=F32) * factors[0:ch]
        y_ref[pl.ds(r0, ch), :] = y_ref[pl.ds(r0, ch), :] + y_off
        xd = (xs * factors[ch:2 * ch]).astype(BF16)
        upd = jnp.dot(bt, xd, preferred_element_type=F32)
        hf_ref[...] = hf * factors[2 * ch:2 * ch + 1] + jnp.where(same_group, upd, 0.0)
        return carry

    lax.fori_loop(0, nc, fwd_chunk, 0, unroll=4)

    def bwd_chunk(i, carry):
        c = nc - 1 - i
        r0 = pl.multiple_of(c * ch, ch)
        da = da_ref[pl.ds(r0, ch), :]
        dtc = dt_ref[pl.ds(r0, ch), :]
        cs = cs_ref[pl.ds(r0, ch), :]
        tot = cs[ch - 1:ch, :]
        ecs = cs - da
        xs = xc_ref[pl.ds(r0, ch), 0:SSD_WIDTH]
        cm = xc_ref[pl.ds(r0, ch), SSD_WIDTH + SSD_BC:SSD_WIDTH + 2 * SSD_BC]
        bt = bt_ref[pl.ds(r0, ch), :]

        hb = hb_ref[...]
        factors = _expand_heads(jnp.concatenate(
            [jnp.exp(tot - ecs), dtc * jnp.exp(ecs), jnp.broadcast_to(jnp.exp(tot), (8, LANES))], axis=0), sel_b)
        y_off = jnp.dot(cm.astype(BF16), hb.astype(BF16), preferred_element_type=F32) * factors[0:ch]
        xd = (xs * factors[ch:2 * ch]).astype(BF16)
        upd = jnp.dot(bt, xd, preferred_element_type=F32)
        hb_ref[...] = hb * factors[2 * ch:2 * ch + 1] + jnp.where(same_group, upd, 0.0)

        y = y_ref[pl.ds(r0, ch), :] + y_off + dskip_ref[...] * xs
        y = y * _silu(z_ref[0, pl.ds(r0, ch), :])
        gw = SSD_WIDTH // SSD_GROUPS
        yn = jnp.concatenate([_rms(y[:, g * gw:(g + 1) * gw]) for g in range(SSD_GROUPS)], axis=-1)
        o_ref[0, pl.ds(r0, ch), :] = (yn * ng_ref[...]).astype(BF16)
        return carry

    lax.fori_loop(0, nc, bwd_chunk, 0, unroll=4)


def _ssd(z, xbc, dt_raw, conv_w, conv_b, dt_bias, a_log, d_skip, norm_g):
    bsz, s, nconv = xbc.shape
    tok = lambda wd: pl.BlockSpec((1, s, wd), lambda b: (b, 0, 0))
    full = lambda a: pl.BlockSpec(a.shape, lambda b: (0, 0))
    return pl.pallas_call(
        _ssd_body,
        grid=(bsz,),
        in_specs=[tok(SSD_WIDTH), tok(nconv), tok(LANES), full(conv_w), full(conv_b), full(dt_bias),
                  full(a_log), full(d_skip), full(norm_g)],
        out_specs=tok(SSD_WIDTH),
        out_shape=jax.ShapeDtypeStruct((bsz, s, SSD_WIDTH), BF16),
        scratch_shapes=[pltpu.VMEM((s + 16, nconv), F32),
                        pltpu.VMEM((s, nconv), F32),
                        pltpu.VMEM((s, LANES), F32),
                        pltpu.VMEM((s, LANES), F32),
                        pltpu.VMEM((s, LANES), F32),
                        pltpu.VMEM((s, LANES), F32),
                        pltpu.VMEM((s, LANES), F32),
                        pltpu.VMEM((s, LANES), F32),
                        pltpu.VMEM((s, SSD_BC), BF16),
                        pltpu.VMEM((s, SSD_WIDTH), F32),
                        pltpu.VMEM((SSD_BC, SSD_WIDTH), F32),
                        pltpu.VMEM((SSD_BC, SSD_WIDTH), F32)],
        compiler_params=_params("parallel"),
        name="ssd_mixer",
    )(z, xbc, dt_raw, conv_w, conv_b, dt_bias, a_log, d_skip, norm_g)


def _outproj_body(att_ref, ssd_ref, x_ref, g1_ref, sh_ref, sc_ref, ng_ref, w_ref, rw_ref,
                  x1_ref, h2_ref, aff_ref):
    na = att_ref.shape[2]
    mix = (jnp.dot(att_ref[0], w_ref[0:na, :], preferred_element_type=F32)
           + jnp.dot(ssd_ref[0], w_ref[na:, :], preferred_element_type=F32))
    x1 = x_ref[0] + g1_ref[0] * mix
    x1_ref[0] = x1
    h2 = _rms(x1) * ng_ref[...]
    h2 = h2 * (1.0 + sc_ref[0]) + sh_ref[0]
    h2_hi = h2.astype(BF16)
    h2_ref[0] = h2_hi
    h2_lo = (h2 - h2_hi.astype(F32)).astype(BF16)
    nt = lambda a, b: lax.dot_general(a, b, (((1,), (1,)), ((), ())), preferred_element_type=F32)
    logits = nt(rw_ref[0], h2_hi) + (nt(rw_ref[0], h2_lo) + nt(rw_ref[1], h2_hi))
    e = jnp.exp(logits - jnp.max(logits, axis=0, keepdims=True))
    aff_ref[0] = e / jnp.sum(e, axis=0, keepdims=True)


def _out_proj(att, ssd, x, g1, sh2, sc2, ng, w_out, rw_t, ts):
    bsz, s, d = x.shape
    tok = lambda wd: pl.BlockSpec((1, ts, wd), lambda b, t: (b, t, 0))
    vec = pl.BlockSpec((1, 1, d), lambda b, t: (b, 0, 0))
    full = lambda a: pl.BlockSpec(a.shape, lambda b, t: (0, 0))
    ne = rw_t.shape[1]
    return pl.pallas_call(
        _outproj_body,
        grid=(bsz, s // ts),
        in_specs=[tok(att.shape[2]), tok(ssd.shape[2]), tok(d), vec, vec, vec, full(ng), full(w_out),
                  pl.BlockSpec(rw_t.shape, lambda b, t: (0, 0, 0))],
        out_specs=[tok(d), tok(d), pl.BlockSpec((1, ne, ts), lambda b, t: (b, 0, t))],
        out_shape=[jax.ShapeDtypeStruct((bsz, s, d), F32),
                   jax.ShapeDtypeStruct((bsz, s, d), BF16),
                   jax.ShapeDtypeStruct((bsz, ne, s), F32)],
        compiler_params=_params("parallel", "arbitrary"),
        name="out_proj_router",
    )(att, ssd, x, g1, sh2, sc2, ng, w_out, rw_t)


def _topk_body(aff_ref, pos_ref, post_ref, ut_ref, *, cap):
    nrow, s = aff_ref.shape
    r = lax.broadcasted_iota(I32, (s, s), 0)
    c = lax.broadcasted_iota(I32, (s, s), 1)
    ut_ref[...] = jnp.where(r < c, 1.0, 0.0).astype(BF16)

    bits = pltpu.bitcast(aff_ref[...], I32)

    def count(mask):
        return jnp.sum(jnp.where(mask, 1, 0), axis=-1, keepdims=True)

    thr = jnp.zeros((nrow, 1), I32)
    for bit in range(30, -1, -1):
        cand = thr | (1 << bit)
        thr = jnp.where(count(bits >= cand) >= cap, cand, thr)
    gt = bits > thr
    eq = bits == thr
    need = cap - count(gt)
    eq_rank = jnp.dot(jnp.where(eq, 1.0, 0.0).astype(BF16), ut_ref[...], preferred_element_type=F32)
    sel = gt | (eq & (eq_rank < need.astype(F32)))
    slot = jnp.dot(jnp.where(sel, 1.0, 0.0).astype(BF16), ut_ref[...], preferred_element_type=F32)
    pos = jnp.where(sel, slot, -1.0)
    pos_ref[...] = pos.astype(I32)
    post_ref[...] = pos.T.astype(I32)


def _topk(aff_t, cap):
    bsz, ne, s = aff_t.shape
    nrow = bsz * ne
    pos, pos_t = pl.pallas_call(
        functools.partial(_topk_body, cap=cap),
        grid=(1,),
        in_specs=[pl.BlockSpec((nrow, s), lambda i: (0, 0))],
        out_specs=[pl.BlockSpec((nrow, s), lambda i: (0, 0)),
                   pl.BlockSpec((s, nrow), lambda i: (0, 0))],
        out_shape=[jax.ShapeDtypeStruct((nrow, s), I32),
                   jax.ShapeDtypeStruct((s, nrow), I32)],
        scratch_shapes=[pltpu.VMEM((s, s), BF16)],
        compiler_params=_params("arbitrary"),
        name="expert_topk",
    )(aff_t.reshape(nrow, s))
    return pos.reshape(bsz, ne, s), pos_t.reshape(s, bsz, ne).transpose(1, 0, 2)


def _gather_body(h2_ref, pos_ref, aff_ref, xg_ref, gc_ref, *, cap):
    ne, s = pos_ref.shape[1], pos_ref.shape[2]
    slot = lax.broadcasted_iota(I32, (cap, s), 0)
    h2 = h2_ref[0]
    for e in range(ne):
        hit = pos_ref[0, e:e + 1, :] == slot
        xg_ref[e] = jnp.dot(jnp.where(hit, 1.0, 0.0).astype(BF16), h2,
                            preferred_element_type=F32).astype(BF16)
        gate = jnp.sum(jnp.where(hit, aff_ref[0, e:e + 1, :], 0.0), axis=-1, keepdims=True)
        gc_ref[e] = jnp.broadcast_to(gate, (cap, LANES))


def _gather(h2, pos, aff_t, cap):
    bsz, s, d = h2.shape
    ne = pos.shape[1]
    return pl.pallas_call(
        functools.partial(_gather_body, cap=cap),
        grid=(bsz,),
        in_specs=[pl.BlockSpec((1, s, d), lambda b: (b, 0, 0)),
                  pl.BlockSpec((1, ne, s), lambda b: (b, 0, 0)),
                  pl.BlockSpec((1, ne, s), lambda b: (b, 0, 0))],
        out_specs=[pl.BlockSpec((ne, cap, d), lambda b: (0, b, 0)),
                   pl.BlockSpec((ne, cap, LANES), lambda b: (0, b, 0))],
        out_shape=[jax.ShapeDtypeStruct((ne, bsz * cap, d), BF16),
                   jax.ShapeDtypeStruct((ne, bsz * cap, LANES), F32)],
        compiler_params=_params("parallel"),
        name="moe_gather",
    )(h2, pos, aff_t)


def _ffn_body(xg_ref, gc_ref, wg_ref, wu_ref, wd_ref, y_ref, acc_ref):
    f = pl.program_id(1)

    @pl.when(f == 0)
    def _():
        acc_ref[...] = jnp.zeros_like(acc_ref)

    wg = wg_ref[0].astype(BF16)
    wu = wu_ref[0].astype(BF16)
    wd = wd_ref[0].astype(BF16)
    half = xg_ref.shape[1] // 2
    for r in range(2):
        rows = slice(r * half, (r + 1) * half)
        xg = xg_ref[0, rows, :]
        gate = jnp.dot(xg, wg, preferred_element_type=F32)
        up = jnp.dot(xg, wu, preferred_element_type=F32)
        hid = (_silu(gate) * up).astype(BF16)
        acc_ref[rows, :] += jnp.dot(hid, wd, preferred_element_type=F32)

    @pl.when(f == pl.num_programs(1) - 1)
    def _():
        m, d = acc_ref.shape
        gcol = gc_ref[0]
        y_ref[0] = (acc_ref[...] * jnp.concatenate([gcol] * (d // LANES), axis=-1)).astype(BF16)


def _ffn(xg, gc, w_gate, w_up, w_down, tf):
    ne, m, d = xg.shape
    ff = w_gate.shape[2]
    return pl.pallas_call(
        _ffn_body,
        grid=(ne, ff // tf),
        in_specs=[pl.BlockSpec((1, m, d), lambda e, f: (e, 0, 0)),
                  pl.BlockSpec((1, m, LANES), lambda e, f: (e, 0, 0)),
                  pl.BlockSpec((1, d, tf), lambda e, f: (e, 0, f)),
                  pl.BlockSpec((1, d, tf), lambda e, f: (e, 0, f)),
                  pl.BlockSpec((1, tf, d), lambda e, f: (e, f, 0))],
        out_specs=pl.BlockSpec((1, m, d), lambda e, f: (e, 0, 0)),
        out_shape=jax.ShapeDtypeStruct((ne, m, d), BF16),
        scratch_shapes=[pltpu.VMEM((m, d), F32)],
        compiler_params=_params("parallel", "arbitrary"),
        name="moe_ffn",
    )(xg, gc, w_gate, w_up, w_down)


def _scatter_body(post_ref, y_ref, x1_ref, g2_ref, ng_ref, o_ref, *, cap):
    ne = y_ref.shape[0]
    ts = post_ref.shape[1]
    d = y_ref.shape[2]
    slot = lax.broadcasted_iota(I32, (ts, cap), 1)
    post = post_ref[0]
    onehot = jnp.concatenate(
        [jnp.where(post[:, e:e + 1] == slot, 1.0, 0.0).astype(BF16) for e in range(ne)], axis=-1)
    moe = jnp.dot(onehot, y_ref[...].reshape(ne * cap, d), preferred_element_type=F32)
    x2 = x1_ref[0] + g2_ref[0] * moe
    o_ref[0] = _rms(x2) * ng_ref[...]


def _scatter(post, y, x1, g2, ng, cap, ts):
    bsz, s, d = x1.shape
    ne = y.shape[0]
    return pl.pallas_call(
        functools.partial(_scatter_body, cap=cap),
        grid=(bsz, s // ts),
        in_specs=[pl.BlockSpec((1, ts, ne), lambda b, t: (b, t, 0)),
                  pl.BlockSpec((ne, cap, d), lambda b, t: (0, b, 0)),
                  pl.BlockSpec((1, ts, d), lambda b, t: (b, t, 0)),
                  pl.BlockSpec((1, 1, d), lambda b, t: (b, 0, 0)),
                  pl.BlockSpec((1, d), lambda b, t: (0, 0))],
        out_specs=pl.BlockSpec((1, ts, d), lambda b, t: (b, t, 0)),
        out_shape=jax.ShapeDtypeStruct((bsz, s, d), F32),
        compiler_params=_params("parallel", "arbitrary"),
        name="moe_scatter_final",
    )(post, y, x1, g2, ng)


def _pad_lanes(a, n=LANES):
    return jnp.pad(a, [(0, 0)] * (a.ndim - 1) + [(0, n - a.shape[-1])])


def kernel(x, c, ada_w, ada_b, norm_mix_g, norm_ffn_g, norm_final_g, w_in, lambda_q1, lambda_k1, lambda_q2, lambda_k2, attn_subln_g, rel_bias_table, conv_w, conv_b, dt_bias_f, dt_bias_b, A_log_f, A_log_b, D_skip, ssm_norm_g, w_out, router_w, w_gate, w_up, w_down):
    bsz, s, d = x.shape
    depth = ada_w.shape[0]
    ne = router_w.shape[2]
    cap = CAPACITY_FACTOR * s // ne
    att_w = ATT_HEADS * ATT_V_DIM
    nconv = conv_w.shape[3]
    splits = (att_w, att_w, SSD_WIDTH, nconv, LANES)
    offs = [0]
    for wd in splits:
        offs.append(offs[-1] + wd)
    row = lambda a: a.reshape(1, -1)

    assert depth == 1, "the final RMSNorm is fused into the (single) layer's scatter kernel"
    for l in range(depth):
        lam_init = 0.8 - 0.6 * math.exp(-0.3 * l)
        mod = _ada_mod(c, ada_w[l], ada_b[l])
        sh1, sc1, g1, sh2, sc2, g2 = [m.reshape(bsz, 1, d) for m in jnp.split(mod, 6, axis=-1)]

        w_l = w_in[l]
        w_pad = _pad_lanes(jnp.concatenate([w_l[:, :2 * att_w], w_l[:, 3 * att_w:]], axis=1), offs[-1]).astype(BF16)
        w_vt = w_l[:, 2 * att_w:3 * att_w].T.astype(BF16)
        q, k, v_t, z, xbc, dt_raw = _in_proj(x, sh1, sc1, row(norm_mix_g[l]), w_pad, w_vt, offs, ts=512)

        att = _attention(q, k, v_t, rel_bias_table, row(lambda_q1[l]), row(lambda_k1[l]),
                         row(lambda_q2[l]), row(lambda_k2[l]), row(attn_subln_g[l]), lam_init, tq=512)

        dt_bias = _pad_lanes(jnp.concatenate([dt_bias_f[l], dt_bias_b[l]]).reshape(1, -1))
        a_log = _pad_lanes(jnp.concatenate([A_log_f[l], A_log_b[l]]).reshape(1, -1))
        d_skip = jnp.repeat(D_skip[l], SSD_HEADDIM).reshape(1, -1)
        ssd = _ssd(z, xbc, dt_raw, conv_w[l].reshape(SSD_CONV, nconv), row(conv_b[l]), dt_bias, a_log,
                   d_skip, row(ssm_norm_g[l]))

        rw_t = router_w[l].T
        rw_hi = rw_t.astype(BF16)
        rw_split = jnp.stack([rw_hi, (rw_t - rw_hi.astype(F32)).astype(BF16)])
        x1, h2, aff_t = _out_proj(att, ssd, x, g1, sh2, sc2, row(norm_ffn_g[l]), w_out[l].astype(BF16),
                                  rw_split, ts=512)
        pos, pos_t = _topk(aff_t, cap)
        xg, gc = _gather(h2, pos, aff_t, cap)
        y = _ffn(xg, gc, w_gate[l], w_up[l], w_down[l], tf=256)
        x = _scatter(pos_t, y, x1, g2, row(norm_final_g), cap, ts=512)
    return x
```

```python
import functools
import math

import jax
import jax.numpy as jnp
from jax import lax
from jax.experimental import pallas as pl
from jax.experimental.pallas import tpu as pltpu

F32 = jnp.float32
BF16 = jnp.bfloat16
I32 = jnp.int32
HIGHEST = lax.Precision.HIGHEST

ATT_HEADS = 4
ATT_QK_DIM = 64
ATT_V_DIM = 128
N_BUCKETS = 32
SSD_HEADS = 8
SSD_HEADDIM = 64
SSD_GROUPS = 2
SSD_STATE = 64
SSD_CONV = 5
SSD_CHUNK = 128
SSD_WIDTH = SSD_HEADS * SSD_HEADDIM
SSD_BC = SSD_GROUPS * SSD_STATE
N_EXPERTS = 16
CAPACITY_FACTOR = 2
EPS = 1e-6
LANES = 128
NEG_BIG = -1e30
LOG2E = math.log2(math.e)

VMEM_LIMIT = 56 * 1024 * 1024


def _params(*semantics):
    return pltpu.CompilerParams(dimension_semantics=semantics, vmem_limit_bytes=VMEM_LIMIT)


def _silu(v):
    return v * jax.nn.sigmoid(v)


def _rms(v, eps=EPS):
    return v * lax.rsqrt(jnp.mean(v * v, axis=-1, keepdims=True) + eps)


def _split_bf16(v):
    hi = v.astype(BF16)
    return hi, (v - hi.astype(F32)).astype(BF16)


def _mod_body(c_ref, w_ref, b_ref, o_ref):
    a_hi, a_lo = _split_bf16(_silu(c_ref[...]))
    w_hi, w_lo = _split_bf16(w_ref[...])
    dot = functools.partial(jnp.dot, preferred_element_type=F32)
    o_ref[...] = dot(a_hi, w_hi) + (dot(a_lo, w_hi) + dot(a_hi, w_lo)) + b_ref[...]


def _ada_mod(c, w, b):
    bsz, d = c.shape
    n = w.shape[1]
    tn = n // 4
    return pl.pallas_call(
        _mod_body,
        grid=(n // tn,),
        in_specs=[pl.BlockSpec((bsz, d), lambda j: (0, 0)),
                  pl.BlockSpec((d, tn), lambda j: (0, j)),
                  pl.BlockSpec((1, tn), lambda j: (0, j))],
        out_specs=pl.BlockSpec((bsz, tn), lambda j: (0, j)),
        out_shape=jax.ShapeDtypeStruct((bsz, n), F32),
        compiler_params=_params("arbitrary"),
        name="ada_mod",
    )(c, w, b.reshape(1, n))


def _inproj_body(x_ref, sh_ref, sc_ref, g_ref, w_ref, wvt_ref, q_ref, k_ref, vt_ref, z_ref, xbc_ref, dt_ref,
                 *, offs, qscale):
    h = _rms(x_ref[0]) * g_ref[...]
    h = (h * (1.0 + sc_ref[0]) + sh_ref[0]).astype(BF16)

    def proj(i):
        return jnp.dot(h, w_ref[:, offs[i]:offs[i + 1]], preferred_element_type=F32)

    q_ref[0] = (proj(0) * qscale).astype(BF16)
    k_ref[0] = proj(1).astype(BF16)
    vt_ref[0, 0] = lax.dot_general(wvt_ref[...], h, (((1,), (1,)), ((), ())),
                                   preferred_element_type=F32).astype(BF16)
    z_ref[0] = proj(2)
    xbc_ref[0] = proj(3)
    dt_ref[0] = proj(4)


def _in_proj(x, sh, sc, g, w_pad, w_vt, offs, ts):
    bsz, s, d = x.shape
    widths = [offs[i + 1] - offs[i] for i in range(5)]
    dts = [BF16, BF16, F32, F32, F32]
    nv = w_vt.shape[0]
    tok = lambda wd: pl.BlockSpec((1, ts, wd), lambda b, t: (b, t, 0))
    vec = pl.BlockSpec((1, 1, d), lambda b, t: (b, 0, 0))
    return pl.pallas_call(
        functools.partial(_inproj_body, offs=tuple(offs), qscale=ATT_QK_DIM ** -0.5 * LOG2E),
        grid=(bsz, s // ts),
        in_specs=[tok(d), vec, vec,
                  pl.BlockSpec((1, d), lambda b, t: (0, 0)),
                  pl.BlockSpec(w_pad.shape, lambda b, t: (0, 0)),
                  pl.BlockSpec(w_vt.shape, lambda b, t: (0, 0))],
        out_specs=[tok(widths[0]), tok(widths[1]), pl.BlockSpec((1, 1, nv, ts), lambda b, t: (b, t, 0, 0))]
                  + [tok(wd) for wd in widths[2:]],
        out_shape=[jax.ShapeDtypeStruct((bsz, s, widths[0]), BF16), jax.ShapeDtypeStruct((bsz, s, widths[1]), BF16),
                   jax.ShapeDtypeStruct((bsz, s // ts, nv, ts), BF16)]
                  + [jax.ShapeDtypeStruct((bsz, s, wd), dt) for wd, dt in zip(widths[2:], dts[2:])],
        compiler_params=_params("parallel", "arbitrary"),
        name="in_proj",
    )(x, sh, sc, g, w_pad, w_vt)


def _t5_bucket(rel):
    n = jnp.abs(rel)
    large = jnp.full(rel.shape, 8, I32)
    for t in (12, 16, 23, 32, 46, 64, 91):
        large = large + jnp.where(n >= t, 1, 0)
    return jnp.where(rel > 0, 16, 0) + jnp.where(n < 8, n, large)


def _attn_body(tbl_ref, lq1_ref, lk1_ref, lq2_ref, lk2_ref, g_ref, q_ref, k_ref, vt_ref, o_ref,
               bias_ref, s_ref, e_ref, acc_ref, *, lam_init, tk):
    h = pl.program_id(0)
    qi = pl.program_id(1)
    s, tq = bias_ref.shape
    ring = pl.next_power_of_2(tk + tq - 1)

    @pl.when(pl.program_id(2) == 0)
    def _():
        j = lax.broadcasted_iota(I32, (8, ring), 1)
        for kt in range(s // tk):
            bucket = _t5_bucket((kt * tk + tk - 1) - qi * tq - j)
            w = jnp.zeros((8, ring), F32)
            for bb in range(N_BUCKETS):
                w = jnp.where(bucket == bb, tbl_ref[bb, h] * LOG2E, w)
            big = jnp.broadcast_to(w[0:1], (tk, ring))
            rolled = pltpu.roll(big, ring - (tk - 1), axis=1, stride=1, stride_axis=0)
            bias_ref[kt * tk:(kt + 1) * tk, :] = rolled[:, :tq]

    lam = (jnp.exp(jnp.sum(lq1_ref[...] * lk1_ref[...], axis=-1, keepdims=True))
           - jnp.exp(jnp.sum(lq2_ref[...] * lk2_ref[...], axis=-1, keepdims=True)) + lam_init)
    q = q_ref[0]
    lane = lax.broadcasted_iota(I32, q.shape, 1)
    qms = [jnp.where((lane >= m * ATT_QK_DIM) & (lane < (m + 1) * ATT_QK_DIM), q, jnp.zeros_like(q))
           for m in range(2)]
    qw = 2 * LANES
    nqc = tq // qw
    nkb, kb = vt_ref.shape[1], vt_ref.shape[3]
    ones_rows = 16

    streams = [(m, c) for m in range(2) for c in range(nqc)]
    run_max = {st: jnp.full((8, qw), NEG_BIG, F32) for st in streams}
    row_max = {}

    def logits_item(st, t):
        m, c = st
        rows, cols = slice(t * tk, (t + 1) * tk), slice(c * qw, (c + 1) * qw)
        logit = lax.dot_general(k_ref[0, rows, :], qms[m][cols, :], (((1,), (1,)), ((), ())),
                                preferred_element_type=F32) + bias_ref[rows, cols]
        s_ref[m, rows, cols] = logit
        run_max[st] = jnp.maximum(run_max[st], jnp.max(logit.reshape(tk // 8, 8, qw), axis=0))
        if t == s // tk - 1:
            row_max[st] = jnp.max(run_max[st], axis=0, keepdims=True)

    def exp_item(st, t):
        m, c = st
        rows, cols = slice(t * tk, (t + 1) * tk), slice(c * qw, (c + 1) * qw)
        e_ref[m, rows, cols] = jnp.exp2(s_ref[m, rows, cols] - row_max[st]).astype(BF16)

    def pv_item(st, blk):
        m, c = st
        rows, cols = slice(blk * kb, (blk + 1) * kb), slice(c * qw, (c + 1) * qw)
        vt_ext = jnp.concatenate([vt_ref[0, blk], jnp.ones((ones_rows, kb), BF16)], axis=0)
        part = jnp.dot(vt_ext, e_ref[m, rows, cols], preferred_element_type=F32)
        if blk == 0:
            acc_ref[m, :, cols] = part
        else:
            acc_ref[m, :, cols] += part

    nt = s // tk
    for phase in range(len(streams) + 2):
        stage_items = []
        if phase < len(streams):
            stage_items.append([functools.partial(logits_item, streams[phase], t) for t in range(nt)])
        if 0 <= phase - 1 < len(streams):
            stage_items.append([functools.partial(exp_item, streams[phase - 1], t) for t in range(nt)])
        if 0 <= phase - 2 < len(streams):
            stage_items.append([functools.partial(pv_item, streams[phase - 2], blk) for blk in range(nkb)])
        for i in range(nt):
            for items in stage_items:
                per = nt // len(items)
                if i % per == 0:
                    items[i // per]()
    outs = [acc_ref[m, :ATT_V_DIM, :] / acc_ref[m, ATT_V_DIM:ATT_V_DIM + 1, :] for m in range(2)]
    att = outs[0] - lam * outs[1]
    att = att * lax.rsqrt(jnp.mean(att * att, axis=0, keepdims=True) + EPS)
    o_ref[0] = (att * (g_ref[...] * (1.0 - lam_init))).T.astype(BF16)


def _attention(q, k, v_t, tbl, lq1, lk1, lq2, lk2, subln_g, lam_init, tq):
    bsz, s, _ = q.shape
    nkb, kb = v_t.shape[1], v_t.shape[3]
    small = lambda n: pl.BlockSpec((1, n), lambda h, i, b: (0, 0))
    return pl.pallas_call(
        functools.partial(_attn_body, lam_init=lam_init, tk=2 * LANES),
        grid=(ATT_HEADS, s // tq, bsz),
        in_specs=[pl.BlockSpec(memory_space=pltpu.SMEM),
                  small(ATT_QK_DIM), small(ATT_QK_DIM), small(ATT_QK_DIM), small(ATT_QK_DIM),
                  pl.BlockSpec((ATT_V_DIM, 1), lambda h, i, b: (0, 0)),
                  pl.BlockSpec((1, tq, ATT_V_DIM), lambda h, i, b: (b, i, h)),
                  pl.BlockSpec((1, s, ATT_V_DIM), lambda h, i, b: (b, 0, h)),
                  pl.BlockSpec((1, nkb, ATT_V_DIM, kb), lambda h, i, b: (b, 0, h, 0))],
        out_specs=pl.BlockSpec((1, tq, ATT_V_DIM), lambda h, i, b: (b, i, h)),
        out_shape=jax.ShapeDtypeStruct((bsz, s, ATT_HEADS * ATT_V_DIM), BF16),
        scratch_shapes=[pltpu.VMEM((s, tq), F32),
                        pltpu.VMEM((2, s, tq), F32),
                        pltpu.VMEM((2, s, tq), BF16),
                        pltpu.VMEM((2, ATT_V_DIM + 16, tq), F32)],
        compiler_params=_params("parallel", "parallel", "arbitrary"),
        name="diff_attention",
    )(tbl, lq1, lk1, lq2, lk2, subln_g.reshape(ATT_V_DIM, 1), q, k, v_t)


def _head_selector(off):
    r = lax.broadcasted_iota(I32, (LANES, SSD_WIDTH), 0)
    c = lax.broadcasted_iota(I32, (LANES, SSD_WIDTH), 1)
    return jnp.where(r - off == c // SSD_HEADDIM, 1.0, 0.0).astype(BF16)


def _expand_heads(cols, selector):
    hi = cols.astype(BF16)
    lo = (cols - hi.astype(F32)).astype(BF16)
    return (jnp.dot(hi, selector, preferred_element_type=F32)
            + jnp.dot(lo, selector, preferred_element_type=F32))


def _ssd_body(z_ref, xbc_ref, dtr_ref, cw_ref, cb_ref, dtb_ref, alog_ref, dskip_ref, ng_ref, o_ref,
              xp_ref, xc_ref, dt_ref, da_ref, cs_ref, cst_ref, ecst_ref, dtt_ref, bt_ref, y_ref, hf_ref, hb_ref):
    s = xbc_ref.shape[1]
    ch = SSD_CHUNK
    assert ch == LANES
    nc = s // ch
    pad = 8
    nconv = xbc_ref.shape[2]

    xp_ref[0:pad, :] = jnp.zeros((pad, nconv), F32)
    xp_ref[pad + s:pad + s + pad, :] = jnp.zeros((pad, nconv), F32)
    xp_ref[pad:pad + s, :] = xbc_ref[0]

    def conv_chunk(c, carry):
        r0 = pl.multiple_of(c * ch, ch)
        win = xp_ref[pl.ds(r0, ch + 2 * pad), :]
        acc = jnp.broadcast_to(cb_ref[...], (ch, nconv))
        for j in range(SSD_CONV):
            lo = pad + j - SSD_CONV // 2
            acc = acc + cw_ref[j:j + 1, :] * win[lo:lo + ch, :]
        xc_ref[pl.ds(r0, ch), :] = _silu(acc)
        return carry

    lax.fori_loop(0, nc, conv_chunk, 0)

    lane = lax.broadcasted_iota(I32, (s, LANES), 1)
    pre = dtr_ref[0] + dtb_ref[...]
    dt = jnp.maximum(pre, 0.0) + jnp.log1p(jnp.exp(-jnp.abs(pre)))
    dt = jnp.where(lane < 2 * SSD_HEADS, dt, 0.0)
    dt_ref[...] = dt
    da_ref[...] = dt * (-jnp.exp(alog_ref[...]))

    row = lax.broadcasted_iota(I32, (ch, ch), 0)
    col = lax.broadcasted_iota(I32, (ch, ch), 1)
    tri = jnp.where(col <= row, 1.0, 0.0).astype(F32)
    lower = col <= row
    upper = col >= row
    lane_c = lax.broadcasted_iota(I32, (ch, LANES), 1)
    st_rows = lax.broadcasted_iota(I32, (SSD_BC, SSD_WIDTH), 0) // SSD_STATE
    st_cols = lax.broadcasted_iota(I32, (SSD_BC, SSD_WIDTH), 1) // (SSD_WIDTH // SSD_GROUPS)
    same_group = st_rows == st_cols
    sel_f = _head_selector(0)
    sel_b = _head_selector(SSD_HEADS)

    hf_ref[...] = jnp.zeros_like(hf_ref)
    hb_ref[...] = jnp.zeros_like(hb_ref)

    def wide(x):
        return jnp.concatenate([x[c * ch:(c + 1) * ch, :] for c in range(nc)], axis=1)

    da_wide = wide(da_ref[...])
    cs_wide = jnp.dot(tri, da_wide, precision=HIGHEST, preferred_element_type=F32)
    for c in range(nc):
        cs_ref[c * ch:(c + 1) * ch, :] = cs_wide[:, c * LANES:(c + 1) * LANES]
    cst_ref[...] = cs_wide.T
    ecst_ref[...] = (cs_wide - da_wide).T
    dtt_ref[...] = wide(dt_ref[...]).T
    bt_ref[...] = wide(xc_ref[:, SSD_WIDTH:SSD_WIDTH + SSD_BC]).T.astype(BF16)

    def fwd_chunk(c, carry):
        r0 = pl.multiple_of(c * ch, ch)
        da = da_ref[pl.ds(r0, ch), :]
        dtc = dt_ref[pl.ds(r0, ch), :]
        cs = cs_ref[pl.ds(r0, ch), :]
        tot = cs[ch - 1:ch, :]
        ecs = cs - da
        nh2 = 2 * SSD_HEADS
        cs_t = cst_ref[pl.ds(r0, nh2), :]
        ecs_t = ecst_ref[pl.ds(r0, nh2), :]
        dt_t = dtt_ref[pl.ds(r0, nh2), :]
        xs = xc_ref[pl.ds(r0, ch), 0:SSD_WIDTH]
        cm = xc_ref[pl.ds(r0, ch), SSD_WIDTH + SSD_BC:SSD_WIDTH + 2 * SSD_BC]
        bt = bt_ref[pl.ds(r0, ch), :]
        cmb = cm.astype(BF16)
        xsb = xs.astype(BF16)

        g_mats = []
        for g in range(SSD_GROUPS):
            cg = jnp.where(lane_c // SSD_STATE == g, cmb, jnp.zeros_like(cmb))
            g_mats.append(jnp.dot(cg, bt, preferred_element_type=F32))
        for pair in range(SSD_HEADS // 2):
            xpair = xsb[:, pair * LANES:(pair + 1) * LANES]
            ypair = jnp.zeros((ch, LANES), F32)
            for sub in range(2):
                hh = 2 * pair + sub
                g = hh // (SSD_HEADS // SSD_GROUPS)
                d_f = cs[:, hh:hh + 1] - cs_t[hh:hh + 1, :]
                d_b = ecs_t[SSD_HEADS + hh:SSD_HEADS + hh + 1, :] - ecs[:, SSD_HEADS + hh:SSD_HEADS + hh + 1]
                decay = jnp.exp(jnp.where(lower, d_f, d_b))
                wgt = (jnp.where(lower, dt_t[hh:hh + 1, :], 0.0)
                       + jnp.where(upper, dt_t[SSD_HEADS + hh:SSD_HEADS + hh + 1, :], 0.0))
                m = (g_mats[g] * decay * wgt).astype(BF16)
                xh = jnp.where(lane_c // SSD_HEADDIM == sub, xpair, jnp.zeros_like(xpair))
                ypair = ypair + jnp.dot(m, xh, preferred_element_type=F32)
            y_ref[pl.ds(r0, ch), pair * LANES:(pair + 1) * LANES] = ypair

        hf = hf_ref[...]
        factors = _expand_heads(jnp.concatenate(
            [jnp.exp(cs), dtc * jnp.exp(tot - cs), jnp.broadcast_to(jnp.exp(tot), (8, LANES))], axis=0), sel_f)
        y_off = jnp.dot(cmb, hf.astype(BF16), preferred_element_type=F32) * factors[0:ch]
        y_ref[pl.ds(r0, ch), :] = y_ref[pl.ds(r0, ch), :] + y_off
        xd = (xs * factors[ch:2 * ch]).astype(BF16)
        upd = jnp.dot(bt, xd, preferred_element_type=F32)
        hf_ref[...] = hf * factors[2 * ch:2 * ch + 1] + jnp.where(same_group, upd, 0.0)
        return carry

    lax.fori_loop(0, nc, fwd_chunk, 0, unroll=4)

    def bwd_chunk(i, carry):
        c = nc - 1 - i
        r0 = pl.multiple_of(c * ch, ch)
        da = da_ref[pl.ds(r0, ch), :]
        dtc = dt_ref[pl.ds(r0, ch), :]
        cs = cs_ref[pl.ds(r0, ch), :]
        tot = cs[ch - 1:ch, :]
        ecs = cs - da
        xs = xc_ref[pl.ds(r0, ch), 0:SSD_WIDTH]
        cm = xc_ref[pl.ds(r0, ch), SSD_WIDTH + SSD_BC:SSD_WIDTH + 2 * SSD_BC]
        bt = bt_ref[pl.ds(r0, ch), :]

        hb = hb_ref[...]
        factors = _expand_heads(jnp.concatenate(
            [jnp.exp(tot - ecs), dtc * jnp.exp(ecs), jnp.broadcast_to(jnp.exp(tot), (8, LANES))], axis=0), sel_b)
        y_off = jnp.dot(cm.astype(BF16), hb.astype(BF16), preferred_element_type=F32) * factors[0:ch]
        xd = (xs * factors[ch:2 * ch]).astype(BF16)
        upd = jnp.dot(bt, xd, preferred_element_type=F32)
        hb_ref[...] = hb * factors[2 * ch:2 * ch + 1] + jnp.where(same_group, upd, 0.0)

        y = y_ref[pl.ds(r0, ch), :] + y_off + dskip_ref[...] * xs
        y = y * _silu(z_ref[0, pl.ds(r0, ch), :])
        gw = SSD_WIDTH // SSD_GROUPS
        yn = jnp.concatenate([_rms(y[:, g * gw:(g + 1) * gw]) for g in range(SSD_GROUPS)], axis=-1)
        o_ref[0, pl.ds(r0, ch), :] = (yn * ng_ref[...]).astype(BF16)
        return carry

    lax.fori_loop(0, nc, bwd_chunk, 0, unroll=4)


def _ssd(z, xbc, dt_raw, conv_w, conv_b, dt_bias, a_log, d_skip, norm_g):
    bsz, s, nconv = xbc.shape
    tok = lambda wd: pl.BlockSpec((1, s, wd), lambda b: (b, 0, 0))
    full = lambda a: pl.BlockSpec(a.shape, lambda b: (0, 0))
    return pl.pallas_call(
        _ssd_body,
        grid=(bsz,),
        in_specs=[tok(SSD_WIDTH), tok(nconv), tok(LANES), full(conv_w), full(conv_b), full(dt_bias),
                  full(a_log), full(d_skip), full(norm_g)],
        out_specs=tok(SSD_WIDTH),
        out_shape=jax.ShapeDtypeStruct((bsz, s, SSD_WIDTH), BF16),
        scratch_shapes=[pltpu.VMEM((s + 16, nconv), F32),
                        pltpu.VMEM((s, nconv), F32),
                        pltpu.VMEM((s, LANES), F32),
                        pltpu.VMEM((s, LANES), F32),
                        pltpu.VMEM((s, LANES), F32),
                        pltpu.VMEM((s, LANES), F32),
                        pltpu.VMEM((s, LANES), F32),
                        pltpu.VMEM((s, LANES), F32),
                        pltpu.VMEM((s, SSD_BC), BF16),
                        pltpu.VMEM((s, SSD_WIDTH), F32),
                        pltpu.VMEM((SSD_BC, SSD_WIDTH), F32),
                        pltpu.VMEM((SSD_BC, SSD_WIDTH), F32)],
        compiler_params=_params("parallel"),
        name="ssd_mixer",
    )(z, xbc, dt_raw, conv_w, conv_b, dt_bias, a_log, d_skip, norm_g)


def _outproj_body(att_ref, ssd_ref, x_ref, g1_ref, sh_ref, sc_ref, ng_ref, w_ref, rw_ref,
                  x1_ref, h2_ref, aff_ref):
    na = att_ref.shape[2]
    mix = (jnp.dot(att_ref[0], w_ref[0:na, :], preferred_element_type=F32)
           + jnp.dot(ssd_ref[0], w_ref[na:, :], preferred_element_type=F32))
    x1 = x_ref[0] + g1_ref[0] * mix
    x1_ref[0] = x1
    h2 = _rms(x1) * ng_ref[...]
    h2 = h2 * (1.0 + sc_ref[0]) + sh_ref[0]
    h2_hi = h2.astype(BF16)
    h2_ref[0] = h2_hi
    h2_lo = (h2 - h2_hi.astype(F32)).astype(BF16)
    nt = lambda a, b: lax.dot_general(a, b, (((1,), (1,)), ((), ())), preferred_element_type=F32)
    logits = nt(rw_ref[0], h2_hi) + (nt(rw_ref[0], h2_lo) + nt(rw_ref[1], h2_hi))
    e = jnp.exp(logits - jnp.max(logits, axis=0, keepdims=True))
    aff_ref[0] = e / jnp.sum(e, axis=0, keepdims=True)


def _out_proj(att, ssd, x, g1, sh2, sc2, ng, w_out, rw_t, ts):
    bsz, s, d = x.shape
    tok = lambda wd: pl.BlockSpec((1, ts, wd), lambda b, t: (b, t, 0))
    vec = pl.BlockSpec((1, 1, d), lambda b, t: (b, 0, 0))
    full = lambda a: pl.BlockSpec(a.shape, lambda b, t: (0, 0))
    ne = rw_t.shape[1]
    return pl.pallas_call(
        _outproj_body,
        grid=(bsz, s // ts),
        in_specs=[tok(att.shape[2]), tok(ssd.shape[2]), tok(d), vec, vec, vec, full(ng), full(w_out),
                  pl.BlockSpec(rw_t.shape, lambda b, t: (0, 0, 0))],
        out_specs=[tok(d), tok(d), pl.BlockSpec((1, ne, ts), lambda b, t: (b, 0, t))],
        out_shape=[jax.ShapeDtypeStruct((bsz, s, d), F32),
                   jax.ShapeDtypeStruct((bsz, s, d), BF16),
                   jax.ShapeDtypeStruct((bsz, ne, s), F32)],
        compiler_params=_params("parallel", "arbitrary"),
        name="out_proj_router",
    )(att, ssd, x, g1, sh2, sc2, ng, w_out, rw_t)


def _topk_body(aff_ref, pos_ref, post_ref, ut_ref, *, cap):
    nrow, s = aff_ref.shape
    r = lax.broadcasted_iota(I32, (s, s), 0)
    c = lax.broadcasted_iota(I32, (s, s), 1)
    ut_ref[...] = jnp.where(r < c, 1.0, 0.0).astype(BF16)

    bits = pltpu.bitcast(aff_ref[...], I32)

    def count(mask):
        return jnp.sum(jnp.where(mask, 1, 0), axis=-1, keepdims=True)

    thr = jnp.zeros((nrow, 1), I32)
    for bit in range(30, -1, -1):
        cand = thr | (1 << bit)
        thr = jnp.where(count(bits >= cand) >= cap, cand, thr)
    gt = bits > thr
    eq = bits == thr
    need = cap - count(gt)
    eq_rank = jnp.dot(jnp.where(eq, 1.0, 0.0).astype(BF16), ut_ref[...], preferred_element_type=F32)
    sel = gt | (eq & (eq_rank < need.astype(F32)))
    slot = jnp.dot(jnp.where(sel, 1.0, 0.0).astype(BF16), ut_ref[...], preferred_element_type=F32)
    pos = jnp.where(sel, slot, -1.0)
    pos_ref[...] = pos.astype(I32)
    post_ref[...] = pos.T.astype(I32)


def _topk(aff_t, cap):
    bsz, ne, s = aff_t.shape
    nrow = bsz * ne
    pos, pos_t = pl.pallas_call(
        functools.partial(_topk_body, cap=cap),
        grid=(1,),
        in_specs=[pl.BlockSpec((nrow, s), lambda i: (0, 0))],
        out_specs=[pl.BlockSpec((nrow, s), lambda i: (0, 0)),
                   pl.BlockSpec((s, nrow), lambda i: (0, 0))],
        out_shape=[jax.ShapeDtypeStruct((nrow, s), I32),
                   jax.ShapeDtypeStruct((s, nrow), I32)],
        scratch_shapes=[pltpu.VMEM((s, s), BF16)],
        compiler_params=_params("arbitrary"),
        name="expert_topk",
    )(aff_t.reshape(nrow, s))
    return pos.reshape(bsz, ne, s), pos_t.reshape(s, bsz, ne).transpose(1, 0, 2)


def _gather_body(h2_ref, pos_ref, aff_ref, xg_ref, gc_ref, *, cap):
    ne, s = pos_ref.shape[1], pos_ref.shape[2]
    slot = lax.broadcasted_iota(I32, (cap, s), 0)
    h2 = h2_ref[0]
    for e in range(ne):
        hit = pos_ref[0, e:e + 1, :] == slot
        xg_ref[e] = jnp.dot(jnp.where(hit, 1.0, 0.0).astype(BF16), h2,
                            preferred_element_type=F32).astype(BF16)
        gate = jnp.sum(jnp.where(hit, aff_ref[0, e:e + 1, :], 0.0), axis=-1, keepdims=True)
        gc_ref[e] = jnp.broadcast_to(gate, (cap, LANES))


def _gather(h2, pos, aff_t, cap):
    bsz, s, d = h2.shape
    ne = pos.shape[1]
    return pl.pallas_call(
        functools.partial(_gather_body, cap=cap),
        grid=(bsz,),
        in_specs=[pl.BlockSpec((1, s, d), lambda b: (b, 0, 0)),
                  pl.BlockSpec((1, ne, s), lambda b: (b, 0, 0)),
                  pl.BlockSpec((1, ne, s), lambda b: (b, 0, 0))],
        out_specs=[pl.BlockSpec((ne, cap, d), lambda b: (0, b, 0)),
                   pl.BlockSpec((ne, cap, LANES), lambda b: (0, b, 0))],
        out_shape=[jax.ShapeDtypeStruct((ne, bsz * cap, d), BF16),
                   jax.ShapeDtypeStruct((ne, bsz * cap, LANES), F32)],
        compiler_params=_params("parallel"),
        name="moe_gather",
    )(h2, pos, aff_t)


def _ffn_body(xg_ref, gc_ref, wg_ref, wu_ref, wd_ref, y_ref, acc_ref):
    f = pl.program_id(1)

    @pl.when(f == 0)
    def _():
        acc_ref[...] = jnp.zeros_like(acc_ref)

    wg = wg_ref[0].astype(BF16)
    wu = wu_ref[0].astype(BF16)
    wd = wd_ref[0].astype(BF16)
    half = xg_ref.shape[1] // 2
    for r in range(2):
        rows = slice(r * half, (r + 1) * half)
        xg = xg_ref[0, rows, :]
        gate = jnp.dot(xg, wg, preferred_element_type=F32)
        up = jnp.dot(xg, wu, preferred_element_type=F32)
        hid = (_silu(gate) * up).astype(BF16)
        acc_ref[rows, :] += jnp.dot(hid, wd, preferred_element_type=F32)

    @pl.when(f == pl.num_programs(1) - 1)
    def _():
        m, d = acc_ref.shape
        gcol = gc_ref[0]
        y_ref[0] = (acc_ref[...] * jnp.concatenate([gcol] * (d // LANES), axis=-1)).astype(BF16)


def _ffn(xg, gc, w_gate, w_up, w_down, tf):
    ne, m, d = xg.shape
    ff = w_gate.shape[2]
    return pl.pallas_call(
        _ffn_body,
        grid=(ne, ff // tf),
        in_specs=[pl.BlockSpec((1, m, d), lambda e, f: (e, 0, 0)),
                  pl.BlockSpec((1, m, LANES), lambda e, f: (e, 0, 0)),
                  pl.BlockSpec((1, d, tf), lambda e, f: (e, 0, f)),
                  pl.BlockSpec((1, d, tf), lambda e, f: (e, 0, f)),
                  pl.BlockSpec((1, tf, d), lambda e, f: (e, f, 0))],
        out_specs=pl.BlockSpec((1, m, d), lambda e, f: (e, 0, 0)),
        out_shape=jax.ShapeDtypeStruct((ne, m, d), BF16),
        scratch_shapes=[pltpu.VMEM((m, d), F32)],
        compiler_params=_params("parallel", "arbitrary"),
        name="moe_ffn",
    )(xg, gc, w_gate, w_up, w_down)


def _scatter_body(post_ref, y_ref, x1_ref, g2_ref, ng_ref, o_ref, *, cap):
    ne = y_ref.shape[0]
    ts = post_ref.shape[1]
    d = y_ref.shape[2]
    slot = lax.broadcasted_iota(I32, (ts, cap), 1)
    post = post_ref[0]
    onehot = jnp.concatenate(
        [jnp.where(post[:, e:e + 1] == slot, 1.0, 0.0).astype(BF16) for e in range(ne)], axis=-1)
    moe = jnp.dot(onehot, y_ref[...].reshape(ne * cap, d), preferred_element_type=F32)
    x2 = x1_ref[0] + g2_ref[0] * moe
    o_ref[0] = _rms(x2) * ng_ref[...]


def _scatter(post, y, x1, g2, ng, cap, ts):
    bsz, s, d = x1.shape
    ne = y.shape[0]
    return pl.pallas_call(
        functools.partial(_scatter_body, cap=cap),
        grid=(bsz, s // ts),
        in_specs=[pl.BlockSpec((1, ts, ne), lambda b, t: (b, t, 0)),
                  pl.BlockSpec((ne, cap, d), lambda b, t: (0, b, 0)),
                  pl.BlockSpec((1, ts, d), lambda b, t: (b, t, 0)),
                  pl.BlockSpec((1, 1, d), lambda b, t: (b, 0, 0)),
                  pl.BlockSpec((1, d), lambda b, t: (0, 0))],
        out_specs=pl.BlockSpec((1, ts, d), lambda b, t: (b, t, 0)),
        out_shape=jax.ShapeDtypeStruct((bsz, s, d), F32),
        compiler_params=_params("parallel", "arbitrary"),
        name="moe_scatter_final",
    )(post, y, x1, g2, ng)


def _pad_lanes(a, n=LANES):
    return jnp.pad(a, [(0, 0)] * (a.ndim - 1) + [(0, n - a.shape[-1])])


def kernel(x, c, ada_w, ada_b, norm_mix_g, norm_ffn_g, norm_final_g, w_in, lambda_q1, lambda_k1, lambda_q2, lambda_k2, attn_subln_g, rel_bias_table, conv_w, conv_b, dt_bias_f, dt_bias_b, A_log_f, A_log_b, D_skip, ssm_norm_g, w_out, router_w, w_gate, w_up, w_down):
    bsz, s, d = x.shape
    depth = ada_w.shape[0]
    ne = router_w.shape[2]
    cap = CAPACITY_FACTOR * s // ne
    att_w = ATT_HEADS * ATT_V_DIM
    nconv = conv_w.shape[3]
    splits = (att_w, att_w, SSD_WIDTH, nconv, LANES)
    offs = [0]
    for wd in splits:
        offs.append(offs[-1] + wd)
    row = lambda a: a.reshape(1, -1)

    assert depth == 1, "the final RMSNorm is fused into the (single) layer's scatter kernel"
    for l in range(depth):
        lam_init = 0.8 - 0.6 * math.exp(-0.3 * l)
        mod = _ada_mod(c, ada_w[l], ada_b[l])
        sh1, sc1, g1, sh2, sc2, g2 = [m.reshape(bsz, 1, d) for m in jnp.split(mod, 6, axis=-1)]

        w_l = w_in[l]
        w_pad = _pad_lanes(jnp.concatenate([w_l[:, :2 * att_w], w_l[:, 3 * att_w:]], axis=1), offs[-1]).astype(BF16)
        w_vt = w_l[:, 2 * att_w:3 * att_w].T.astype(BF16)
        q, k, v_t, z, xbc, dt_raw = _in_proj(x, sh1, sc1, row(norm_mix_g[l]), w_pad, w_vt, offs, ts=512)

        att = _attention(q, k, v_t, rel_bias_table, row(lambda_q1[l]), row(lambda_k1[l]),
                         row(lambda_q2[l]), row(lambda_k2[l]), row(attn_subln_g[l]), lam_init, tq=1024)

        dt_bias = _pad_lanes(jnp.concatenate([dt_bias_f[l], dt_bias_b[l]]).reshape(1, -1))
        a_log = _pad_lanes(jnp.concatenate([A_log_f[l], A_log_b[l]]).reshape(1, -1))
        d_skip = jnp.repeat(D_skip[l], SSD_HEADDIM).reshape(1, -1)
        ssd = _ssd(z, xbc, dt_raw, conv_w[l].reshape(SSD_CONV, nconv), row(conv_b[l]), dt_bias, a_log,
                   d_skip, row(ssm_norm_g[l]))

        rw_t = router_w[l].T
        rw_hi = rw_t.astype(BF16)
        rw_split = jnp.stack([rw_hi, (rw_t - rw_hi.astype(F32)).astype(BF16)])
        x1, h2, aff_t = _out_proj(att, ssd, x, g1, sh2, sc2, row(norm_ffn_g[l]), w_out[l].astype(BF16),
                                  rw_split, ts=512)
        pos, pos_t = _topk(aff_t, cap)
        xg, gc = _gather(h2, pos, aff_t, cap)
        y = _ffn(xg, gc, w_gate[l], w_up[l], w_down[l], tf=256)
        x = _scatter(pos_t, y, x1, g2, row(norm_final_g), cap, ts=512)
    return x
```

```python
import functools
import math

import jax
import jax.numpy as jnp
from jax import lax
from jax.experimental import pallas as pl
from jax.experimental.pallas import tpu as pltpu

F32 = jnp.float32
BF16 = jnp.bfloat16
I32 = jnp.int32
HIGHEST = lax.Precision.HIGHEST

ATT_HEADS = 4
ATT_QK_DIM = 64
ATT_V_DIM = 128
N_BUCKETS = 32
SSD_HEADS = 8
SSD_HEADDIM = 64
SSD_GROUPS = 2
SSD_STATE = 64
SSD_CONV = 5
SSD_CHUNK = 128
SSD_WIDTH = SSD_HEADS * SSD_HEADDIM
SSD_BC = SSD_GROUPS * SSD_STATE
N_EXPERTS = 16
CAPACITY_FACTOR = 2
EPS = 1e-6
LANES = 128
NEG_BIG = -1e30
LOG2E = math.log2(math.e)

VMEM_LIMIT = 56 * 1024 * 1024


def _params(*semantics):
    return pltpu.CompilerParams(dimension_semantics=semantics, vmem_limit_bytes=VMEM_LIMIT)


def _silu(v):
    return v * jax.nn.sigmoid(v)


def _rms(v, eps=EPS):
    return v * lax.rsqrt(jnp.mean(v * v, axis=-1, keepdims=True) + eps)


def _split_bf16(v):
    hi = v.astype(BF16)
    return hi, (v - hi.astype(F32)).astype(BF16)


def _mod_body(c_ref, w_ref, b_ref, o_ref):
    a_hi, a_lo = _split_bf16(_silu(c_ref[...]))
    w_hi, w_lo = _split_bf16(w_ref[...])
    dot = functools.partial(jnp.dot, preferred_element_type=F32)
    o_ref[...] = dot(a_hi, w_hi) + (dot(a_lo, w_hi) + dot(a_hi, w_lo)) + b_ref[...]


def _ada_mod(c, w, b):
    bsz, d = c.shape
    n = w.shape[1]
    tn = n // 4
    return pl.pallas_call(
        _mod_body,
        grid=(n // tn,),
        in_specs=[pl.BlockSpec((bsz, d), lambda j: (0, 0)),
                  pl.BlockSpec((d, tn), lambda j: (0, j)),
                  pl.BlockSpec((1, tn), lambda j: (0, j))],
        out_specs=pl.BlockSpec((bsz, tn), lambda j: (0, j)),
        out_shape=jax.ShapeDtypeStruct((bsz, n), F32),
        compiler_params=_params("arbitrary"),
        name="ada_mod",
    )(c, w, b.reshape(1, n))


def _inproj_body(x_ref, sh_ref, sc_ref, g_ref, w_ref, wvt_ref, q_ref, k_ref, vt_ref, z_ref, xbc_ref, dt_ref,
                 *, offs, qscale):
    h = _rms(x_ref[0]) * g_ref[...]
    h = (h * (1.0 + sc_ref[0]) + sh_ref[0]).astype(BF16)

    def proj(i):
        return jnp.dot(h, w_ref[:, offs[i]:offs[i + 1]], preferred_element_type=F32)

    q_ref[0] = (proj(0) * qscale).astype(BF16)
    k_ref[0] = proj(1).astype(BF16)
    vt_ref[0, 0] = lax.dot_general(wvt_ref[...], h, (((1,), (1,)), ((), ())),
                                   preferred_element_type=F32).astype(BF16)
    z_ref[0] = proj(2)
    xbc_ref[0] = proj(3)
    dt_ref[0] = proj(4)


def _in_proj(x, sh, sc, g, w_pad, w_vt, offs, ts):
    bsz, s, d = x.shape
    widths = [offs[i + 1] - offs[i] for i in range(5)]
    dts = [BF16, BF16, F32, F32, F32]
    nv = w_vt.shape[0]
    tok = lambda wd: pl.BlockSpec((1, ts, wd), lambda b, t: (b, t, 0))
    vec = pl.BlockSpec((1, 1, d), lambda b, t: (b, 0, 0))
    return pl.pallas_call(
        functools.partial(_inproj_body, offs=tuple(offs), qscale=ATT_QK_DIM ** -0.5 * LOG2E),
        grid=(bsz, s // ts),
        in_specs=[tok(d), vec, vec,
                  pl.BlockSpec((1, d), lambda b, t: (0, 0)),
                  pl.BlockSpec(w_pad.shape, lambda b, t: (0, 0)),
                  pl.BlockSpec(w_vt.shape, lambda b, t: (0, 0))],
        out_specs=[tok(widths[0]), tok(widths[1]), pl.BlockSpec((1, 1, nv, ts), lambda b, t: (b, t, 0, 0))]
                  + [tok(wd) for wd in widths[2:]],
        out_shape=[jax.ShapeDtypeStruct((bsz, s, widths[0]), BF16), jax.ShapeDtypeStruct((bsz, s, widths[1]), BF16),
                   jax.ShapeDtypeStruct((bsz, s // ts, nv, ts), BF16)]
                  + [jax.ShapeDtypeStruct((bsz, s, wd), dt) for wd, dt in zip(widths[2:], dts[2:])],
        compiler_params=_params("parallel", "arbitrary"),
        name="in_proj",
    )(x, sh, sc, g, w_pad, w_vt)


def _t5_bucket(rel):
    n = jnp.abs(rel)
    large = jnp.full(rel.shape, 8, I32)
    for t in (12, 16, 23, 32, 46, 64, 91):
        large = large + jnp.where(n >= t, 1, 0)
    return jnp.where(rel > 0, 16, 0) + jnp.where(n < 8, n, large)


def _attn_body(tbl_ref, lq1_ref, lk1_ref, lq2_ref, lk2_ref, g_ref, q_ref, k_ref, vt_ref, o_ref,
               bias_ref, s_ref, e_ref, acc_ref, *, lam_init, tk):
    h = pl.program_id(0)
    qi = pl.program_id(1)
    s, tq = bias_ref.shape
    ring = pl.next_power_of_2(tk + tq - 1)

    @pl.when(pl.program_id(2) == 0)
    def _():
        j = lax.broadcasted_iota(I32, (8, ring), 1)
        for kt in range(s // tk):
            bucket = _t5_bucket((kt * tk + tk - 1) - qi * tq - j)
            w = jnp.zeros((8, ring), F32)
            for bb in range(N_BUCKETS):
                w = jnp.where(bucket == bb, tbl_ref[bb, h] * LOG2E, w)
            big = jnp.broadcast_to(w[0:1], (tk, ring))
            rolled = pltpu.roll(big, ring - (tk - 1), axis=1, stride=1, stride_axis=0)
            bias_ref[kt * tk:(kt + 1) * tk, :] = rolled[:, :tq]

    lam = (jnp.exp(jnp.sum(lq1_ref[...] * lk1_ref[...], axis=-1, keepdims=True))
           - jnp.exp(jnp.sum(lq2_ref[...] * lk2_ref[...], axis=-1, keepdims=True)) + lam_init)
    q = q_ref[0]
    lane = lax.broadcasted_iota(I32, q.shape, 1)
    qms = [jnp.where((lane >= m * ATT_QK_DIM) & (lane < (m + 1) * ATT_QK_DIM), q, jnp.zeros_like(q))
           for m in range(2)]
    qw = 2 * LANES
    nqc = tq // qw
    nkb, kb = vt_ref.shape[1], vt_ref.shape[3]
    ones_rows = 16

    streams = [(m, c, (m * nqc + c) % 2) for m in range(2) for c in range(nqc)]
    run_max = {st: jnp.full((8, qw), NEG_BIG, F32) for st in streams}
    row_max = {}

    def logits_item(st, t):
        m, c, slot = st
        rows, cols = slice(t * tk, (t + 1) * tk), slice(c * qw, (c + 1) * qw)
        logit = lax.dot_general(k_ref[0, rows, :], qms[m][cols, :], (((1,), (1,)), ((), ())),
                                preferred_element_type=F32) + bias_ref[rows, cols]
        s_ref[slot, rows, :] = logit
        run_max[st] = jnp.maximum(run_max[st], jnp.max(logit.reshape(tk // 8, 8, qw), axis=0))
        if t == s // tk - 1:
            row_max[st] = jnp.max(run_max[st], axis=0, keepdims=True)

    def exp_item(st, t):
        slot = st[2]
        rows = slice(t * tk, (t + 1) * tk)
        e_ref[slot, rows, :] = jnp.exp2(s_ref[slot, rows, :] - row_max[st]).astype(BF16)

    def pv_item(st, blk):
        m, c, slot = st
        rows, cols = slice(blk * kb, (blk + 1) * kb), slice(c * qw, (c + 1) * qw)
        vt_ext = jnp.concatenate([vt_ref[0, blk], jnp.ones((ones_rows, kb), BF16)], axis=0)
        part = jnp.dot(vt_ext, e_ref[slot, rows, :], preferred_element_type=F32)
        if blk == 0:
            acc_ref[m, :, cols] = part
        else:
            acc_ref[m, :, cols] += part

    nt = s // tk
    for phase in range(len(streams) + 2):
        stage_items = []
        if phase < len(streams):
            stage_items.append([functools.partial(logits_item, streams[phase], t) for t in range(nt)])
        if 0 <= phase - 1 < len(streams):
            stage_items.append([functools.partial(exp_item, streams[phase - 1], t) for t in range(nt)])
        if 0 <= phase - 2 < len(streams):
            stage_items.append([functools.partial(pv_item, streams[phase - 2], blk) for blk in range(nkb)])
        for i in range(nt):
            for items in stage_items:
                per = nt // len(items)
                if i % per == 0:
                    items[i // per]()
    outs = [acc_ref[m, :ATT_V_DIM, :] / acc_ref[m, ATT_V_DIM:ATT_V_DIM + 1, :] for m in range(2)]
    att = outs[0] - lam * outs[1]
    att = att * lax.rsqrt(jnp.mean(att * att, axis=0, keepdims=True) + EPS)
    o_ref[0] = (att * (g_ref[...] * (1.0 - lam_init))).T.astype(BF16)


def _attention(q, k, v_t, tbl, lq1, lk1, lq2, lk2, subln_g, lam_init, tq):
    bsz, s, _ = q.shape
    nkb, kb = v_t.shape[1], v_t.shape[3]
    small = lambda n: pl.BlockSpec((1, n), lambda h, i, b: (0, 0))
    return pl.pallas_call(
        functools.partial(_attn_body, lam_init=lam_init, tk=2 * LANES),
        grid=(ATT_HEADS, s // tq, bsz),
        in_specs=[pl.BlockSpec(memory_space=pltpu.SMEM),
                  small(ATT_QK_DIM), small(ATT_QK_DIM), small(ATT_QK_DIM), small(ATT_QK_DIM),
                  pl.BlockSpec((ATT_V_DIM, 1), lambda h, i, b: (0, 0)),
                  pl.BlockSpec((1, tq, ATT_V_DIM), lambda h, i, b: (b, i, h)),
                  pl.BlockSpec((1, s, ATT_V_DIM), lambda h, i, b: (b, 0, h)),
                  pl.BlockSpec((1, nkb, ATT_V_DIM, kb), lambda h, i, b: (b, 0, h, 0))],
        out_specs=pl.BlockSpec((1, tq, ATT_V_DIM), lambda h, i, b: (b, i, h)),
        out_shape=jax.ShapeDtypeStruct((bsz, s, ATT_HEADS * ATT_V_DIM), BF16),
        scratch_shapes=[pltpu.VMEM((s, tq), F32),
                        pltpu.VMEM((2, s, 2 * LANES), F32),
                        pltpu.VMEM((2, s, 2 * LANES), BF16),
                        pltpu.VMEM((2, ATT_V_DIM + 16, tq), F32)],
        compiler_params=_params("parallel", "parallel", "arbitrary"),
        name="diff_attention",
    )(tbl, lq1, lk1, lq2, lk2, subln_g.reshape(ATT_V_DIM, 1), q, k, v_t)


def _head_selector(off):
    r = lax.broadcasted_iota(I32, (LANES, SSD_WIDTH), 0)
    c = lax.broadcasted_iota(I32, (LANES, SSD_WIDTH), 1)
    return jnp.where(r - off == c // SSD_HEADDIM, 1.0, 0.0).astype(BF16)


def _expand_heads(cols, selector):
    hi = cols.astype(BF16)
    lo = (cols - hi.astype(F32)).astype(BF16)
    return (jnp.dot(hi, selector, preferred_element_type=F32)
            + jnp.dot(lo, selector, preferred_element_type=F32))


def _ssd_body(z_ref, xbc_ref, dtr_ref, cw_ref, cb_ref, dtb_ref, alog_ref, dskip_ref, ng_ref, o_ref,
              xp_ref, xc_ref, dt_ref, da_ref, cs_ref, cst_ref, ecst_ref, dtt_ref, bt_ref, y_ref, hf_ref, hb_ref):
    s = xbc_ref.shape[1]
    ch = SSD_CHUNK
    assert ch == LANES
    nc = s // ch
    pad = 8
    nconv = xbc_ref.shape[2]

    xp_ref[0:pad, :] = jnp.zeros((pad, nconv), F32)
    xp_ref[pad + s:pad + s + pad, :] = jnp.zeros((pad, nconv), F32)
    xp_ref[pad:pad + s, :] = xbc_ref[0]

    def conv_chunk(c, carry):
        r0 = pl.multiple_of(c * ch, ch)
        win = xp_ref[pl.ds(r0, ch + 2 * pad), :]
        acc = jnp.broadcast_to(cb_ref[...], (ch, nconv))
        for j in range(SSD_CONV):
            lo = pad + j - SSD_CONV // 2
            acc = acc + cw_ref[j:j + 1, :] * win[lo:lo + ch, :]
        xc_ref[pl.ds(r0, ch), :] = _silu(acc)
        return carry

    lax.fori_loop(0, nc, conv_chunk, 0)

    lane = lax.broadcasted_iota(I32, (s, LANES), 1)
    pre = dtr_ref[0] + dtb_ref[...]
    dt = jnp.maximum(pre, 0.0) + jnp.log1p(jnp.exp(-jnp.abs(pre)))
    dt = jnp.where(lane < 2 * SSD_HEADS, dt, 0.0)
    dt_ref[...] = dt
    da_ref[...] = dt * (-jnp.exp(alog_ref[...]))

    row = lax.broadcasted_iota(I32, (ch, ch), 0)
    col = lax.broadcasted_iota(I32, (ch, ch), 1)
    tri = jnp.where(col <= row, 1.0, 0.0).astype(F32)
    lower = col <= row
    upper = col >= row
    lane_c = lax.broadcasted_iota(I32, (ch, LANES), 1)
    st_rows = lax.broadcasted_iota(I32, (SSD_BC, SSD_WIDTH), 0) // SSD_STATE
    st_cols = lax.broadcasted_iota(I32, (SSD_BC, SSD_WIDTH), 1) // (SSD_WIDTH // SSD_GROUPS)
    same_group = st_rows == st_cols
    sel_f = _head_selector(0)
    sel_b = _head_selector(SSD_HEADS)

    hf_ref[...] = jnp.zeros_like(hf_ref)
    hb_ref[...] = jnp.zeros_like(hb_ref)

    def wide(x):
        return jnp.concatenate([x[c * ch:(c + 1) * ch, :] for c in range(nc)], axis=1)

    da_wide = wide(da_ref[...])
    cs_wide = jnp.dot(tri, da_wide, precision=HIGHEST, preferred_element_type=F32)
    for c in range(nc):
        cs_ref[c * ch:(c + 1) * ch, :] = cs_wide[:, c * LANES:(c + 1) * LANES]
    cst_ref[...] = cs_wide.T
    ecst_ref[...] = (cs_wide - da_wide).T
    dtt_ref[...] = wide(dt_ref[...]).T
    bt_ref[...] = wide(xc_ref[:, SSD_WIDTH:SSD_WIDTH + SSD_BC]).T.astype(BF16)

    def fwd_chunk(c, carry):
        r0 = pl.multiple_of(c * ch, ch)
        da = da_ref[pl.ds(r0, ch), :]
        dtc = dt_ref[pl.ds(r0, ch), :]
        cs = cs_ref[pl.ds(r0, ch), :]
        tot = cs[ch - 1:ch, :]
        ecs = cs - da
        nh2 = 2 * SSD_HEADS
        cs_t = cst_ref[pl.ds(r0, nh2), :]
        ecs_t = ecst_ref[pl.ds(r0, nh2), :]
        dt_t = dtt_ref[pl.ds(r0, nh2), :]
        xs = xc_ref[pl.ds(r0, ch), 0:SSD_WIDTH]
        cm = xc_ref[pl.ds(r0, ch), SSD_WIDTH + SSD_BC:SSD_WIDTH + 2 * SSD_BC]
        bt = bt_ref[pl.ds(r0, ch), :]
        cmb = cm.astype(BF16)
        xsb = xs.astype(BF16)

        g_mats = []
        for g in range(SSD_GROUPS):
            cg = jnp.where(lane_c // SSD_STATE == g, cmb, jnp.zeros_like(cmb))
            g_mats.append(jnp.dot(cg, bt, preferred_element_type=F32))
        for pair in range(SSD_HEADS // 2):
            xpair = xsb[:, pair * LANES:(pair + 1) * LANES]
            ypair = jnp.zeros((ch, LANES), F32)
            for sub in range(2):
                hh = 2 * pair + sub
                g = hh // (SSD_HEADS // SSD_GROUPS)
                d_f = cs[:, hh:hh + 1] - cs_t[hh:hh + 1, :]
                d_b = ecs_t[SSD_HEADS + hh:SSD_HEADS + hh + 1, :] - ecs[:, SSD_HEADS + hh:SSD_HEADS + hh + 1]
                decay = jnp.exp(jnp.where(lower, d_f, d_b))
                wgt = (jnp.where(lower, dt_t[hh:hh + 1, :], 0.0)
                       + jnp.where(upper, dt_t[SSD_HEADS + hh:SSD_HEADS + hh + 1, :], 0.0))
                m = (g_mats[g] * decay * wgt).astype(BF16)
                xh = jnp.where(lane_c // SSD_HEADDIM == sub, xpair, jnp.zeros_like(xpair))
                ypair = ypair + jnp.dot(m, xh, preferred_element_type=F32)
            y_ref[pl.ds(r0, ch), pair * LANES:(pair + 1) * LANES] = ypair

        hf = hf_ref[...]
        factors = _expand_heads(jnp.concatenate(
            [jnp.exp(cs), dtc * jnp.exp(tot - cs), jnp.broadcast_to(jnp.exp(tot), (8, LANES))], axis=0), sel_f)
        y_off = jnp.dot(cmb, hf.astype(BF16), preferred_element_type=F32) * factors[0:ch]
        y_ref[pl.ds(r0, ch), :] = y_ref[pl.ds(r0, ch), :] + y_off
        xd = (xs * factors[ch:2 * ch]).astype(BF16)
        upd = jnp.dot(bt, xd, preferred_element_type=F32)
        hf_ref[...] = hf * factors[2 * ch:2 * ch + 1] + jnp.where(same_group, upd, 0.0)
        return carry

    lax.fori_loop(0, nc, fwd_chunk, 0, unroll=4)

    def bwd_chunk(i, carry):
        c = nc - 1 - i
        r0 = pl.multiple_of(c * ch, ch)
        da = da_ref[pl.ds(r0, ch), :]
        dtc = dt_ref[pl.ds(r0, ch), :]
        cs = cs_ref[pl.ds(r0, ch), :]
        tot = cs[ch - 1:ch, :]
        ecs = cs - da
        xs = xc_ref[pl.ds(r0, ch), 0:SSD_WIDTH]
        cm = xc_ref[pl.ds(r0, ch), SSD_WIDTH + SSD_BC:SSD_WIDTH + 2 * SSD_BC]
        bt = bt_ref[pl.ds(r0, ch), :]

        hb = hb_ref[...]
        factors = _expand_heads(jnp.concatenate(
            [jnp.exp(tot - ecs), dtc * jnp.exp(ecs), jnp.broadcast_to(jnp.exp(tot), (8, LANES))], axis=0), sel_b)
        y_off = jnp.dot(cm.astype(BF16), hb.astype(BF16), preferred_element_type=F32) * factors[0:ch]
        xd = (xs * factors[ch:2 * ch]).astype(BF16)
        upd = jnp.dot(bt, xd, preferred_element_type=F32)
        hb_ref[...] = hb * factors[2 * ch:2 * ch + 1] + jnp.where(same_group, upd, 0.0)

        y = y_ref[pl.ds(r0, ch), :] + y_off + dskip_ref[...] * xs
        y = y * _silu(z_ref[0, pl.ds(r0, ch), :])
        gw = SSD_WIDTH // SSD_GROUPS
        yn = jnp.concatenate([_rms(y[:, g * gw:(g + 1) * gw]) for g in range(SSD_GROUPS)], axis=-1)
        o_ref[0, pl.ds(r0, ch), :] = (yn * ng_ref[...]).astype(BF16)
        return carry

    lax.fori_loop(0, nc, bwd_chunk, 0, unroll=4)


def _ssd(z, xbc, dt_raw, conv_w, conv_b, dt_bias, a_log, d_skip, norm_g):
    bsz, s, nconv = xbc.shape
    tok = lambda wd: pl.BlockSpec((1, s, wd), lambda b: (b, 0, 0))
    full = lambda a: pl.BlockSpec(a.shape, lambda b: (0, 0))
    return pl.pallas_call(
        _ssd_body,
        grid=(bsz,),
        in_specs=[tok(SSD_WIDTH), tok(nconv), tok(LANES), full(conv_w), full(conv_b), full(dt_bias),
                  full(a_log), full(d_skip), full(norm_g)],
        out_specs=tok(SSD_WIDTH),
        out_shape=jax.ShapeDtypeStruct((bsz, s, SSD_WIDTH), BF16),
        scratch_shapes=[pltpu.VMEM((s + 16, nconv), F32),
                        pltpu.VMEM((s, nconv), F32),
                        pltpu.VMEM((s, LANES), F32),
                        pltpu.VMEM((s, LANES), F32),
                        pltpu.VMEM((s, LANES), F32),
                        pltpu.VMEM((s, LANES), F32),
                        pltpu.VMEM((s, LANES), F32),
                        pltpu.VMEM((s, LANES), F32),
                        pltpu.VMEM((s, SSD_BC), BF16),
                        pltpu.VMEM((s, SSD_WIDTH), F32),
                        pltpu.VMEM((SSD_BC, SSD_WIDTH), F32),
                        pltpu.VMEM((SSD_BC, SSD_WIDTH), F32)],
        compiler_params=_params("parallel"),
        name="ssd_mixer",
    )(z, xbc, dt_raw, conv_w, conv_b, dt_bias, a_log, d_skip, norm_g)


def _outproj_body(att_ref, ssd_ref, x_ref, g1_ref, sh_ref, sc_ref, ng_ref, w_ref, rw_ref,
                  x1_ref, h2_ref, aff_ref):
    na = att_ref.shape[2]
    mix = (jnp.dot(att_ref[0], w_ref[0:na, :], preferred_element_type=F32)
           + jnp.dot(ssd_ref[0], w_ref[na:, :], preferred_element_type=F32))
    x1 = x_ref[0] + g1_ref[0] * mix
    x1_ref[0] = x1
    h2 = _rms(x1) * ng_ref[...]
    h2 = h2 * (1.0 + sc_ref[0]) + sh_ref[0]
    h2_hi = h2.astype(BF16)
    h2_ref[0] = h2_hi
    h2_lo = (h2 - h2_hi.astype(F32)).astype(BF16)
    nt = lambda a, b: lax.dot_general(a, b, (((1,), (1,)), ((), ())), preferred_element_type=F32)
    logits = nt(rw_ref[0], h2_hi) + (nt(rw_ref[0], h2_lo) + nt(rw_ref[1], h2_hi))
    e = jnp.exp(logits - jnp.max(logits, axis=0, keepdims=True))
    aff_ref[0] = e / jnp.sum(e, axis=0, keepdims=True)


def _out_proj(att, ssd, x, g1, sh2, sc2, ng, w_out, rw_t, ts):
    bsz, s, d = x.shape
    tok = lambda wd: pl.BlockSpec((1, ts, wd), lambda b, t: (b, t, 0))
    vec = pl.BlockSpec((1, 1, d), lambda b, t: (b, 0, 0))
    full = lambda a: pl.BlockSpec(a.shape, lambda b, t: (0, 0))
    ne = rw_t.shape[1]
    return pl.pallas_call(
        _outproj_body,
        grid=(bsz, s // ts),
        in_specs=[tok(att.shape[2]), tok(ssd.shape[2]), tok(d), vec, vec, vec, full(ng), full(w_out),
                  pl.BlockSpec(rw_t.shape, lambda b, t: (0, 0, 0))],
        out_specs=[tok(d), tok(d), pl.BlockSpec((1, ne, ts), lambda b, t: (b, 0, t))],
        out_shape=[jax.ShapeDtypeStruct((bsz, s, d), F32),
                   jax.ShapeDtypeStruct((bsz, s, d), BF16),
                   jax.ShapeDtypeStruct((bsz, ne, s), F32)],
        compiler_params=_params("parallel", "arbitrary"),
        name="out_proj_router",
    )(att, ssd, x, g1, sh2, sc2, ng, w_out, rw_t)


def _topk_body(aff_ref, pos_ref, post_ref, ut_ref, *, cap):
    nrow, s = aff_ref.shape
    r = lax.broadcasted_iota(I32, (s, s), 0)
    c = lax.broadcasted_iota(I32, (s, s), 1)
    ut_ref[...] = jnp.where(r < c, 1.0, 0.0).astype(BF16)

    bits = pltpu.bitcast(aff_ref[...], I32)

    def count(mask):
        return jnp.sum(jnp.where(mask, 1, 0), axis=-1, keepdims=True)

    thr = jnp.zeros((nrow, 1), I32)
    for bit in range(30, -1, -1):
        cand = thr | (1 << bit)
        thr = jnp.where(count(bits >= cand) >= cap, cand, thr)
    gt = bits > thr
    eq = bits == thr
    need = cap - count(gt)
    eq_rank = jnp.dot(jnp.where(eq, 1.0, 0.0).astype(BF16), ut_ref[...], preferred_element_type=F32)
    sel = gt | (eq & (eq_rank < need.astype(F32)))
    slot = jnp.dot(jnp.where(sel, 1.0, 0.0).astype(BF16), ut_ref[...], preferred_element_type=F32)
    pos = jnp.where(sel, slot, -1.0)
    pos_ref[...] = pos.astype(I32)
    post_ref[...] = pos.T.astype(I32)


def _topk(aff_t, cap):
    bsz, ne, s = aff_t.shape
    nrow = bsz * ne
    pos, pos_t = pl.pallas_call(
        functools.partial(_topk_body, cap=cap),
        grid=(1,),
        in_specs=[pl.BlockSpec((nrow, s), lambda i: (0, 0))],
        out_specs=[pl.BlockSpec((nrow, s), lambda i: (0, 0)),
                   pl.BlockSpec((s, nrow), lambda i: (0, 0))],
        out_shape=[jax.ShapeDtypeStruct((nrow, s), I32),
                   jax.ShapeDtypeStruct((s, nrow), I32)],
        scratch_shapes=[pltpu.VMEM((s, s), BF16)],
        compiler_params=_params("arbitrary"),
        name="expert_topk",
    )(aff_t.reshape(nrow, s))
    return pos.reshape(bsz, ne, s), pos_t.reshape(s, bsz, ne).transpose(1, 0, 2)


def _gather_body(h2_ref, pos_ref, aff_ref, xg_ref, gc_ref, *, cap):
    ne, s = pos_ref.shape[1], pos_ref.shape[2]
    slot = lax.broadcasted_iota(I32, (cap, s), 0)
    h2 = h2_ref[0]
    for e in range(ne):
        hit = pos_ref[0, e:e + 1, :] == slot
        xg_ref[e] = jnp.dot(jnp.where(hit, 1.0, 0.0).astype(BF16), h2,
                            preferred_element_type=F32).astype(BF16)
        gate = jnp.sum(jnp.where(hit, aff_ref[0, e:e + 1, :], 0.0), axis=-1, keepdims=True)
        gc_ref[e] = jnp.broadcast_to(gate, (cap, LANES))


def _gather(h2, pos, aff_t, cap):
    bsz, s, d = h2.shape
    ne = pos.shape[1]
    return pl.pallas_call(
        functools.partial(_gather_body, cap=cap),
        grid=(bsz,),
        in_specs=[pl.BlockSpec((1, s, d), lambda b: (b, 0, 0)),
                  pl.BlockSpec((1, ne, s), lambda b: (b, 0, 0)),
                  pl.BlockSpec((1, ne, s), lambda b: (b, 0, 0))],
        out_specs=[pl.BlockSpec((ne, cap, d), lambda b: (0, b, 0)),
                   pl.BlockSpec((ne, cap, LANES), lambda b: (0, b, 0))],
        out_shape=[jax.ShapeDtypeStruct((ne, bsz * cap, d), BF16),
                   jax.ShapeDtypeStruct((ne, bsz * cap, LANES), F32)],
        compiler_params=_params("parallel"),
        name="moe_gather",
    )(h2, pos, aff_t)


def _ffn_body(xg_ref, gc_ref, wg_ref, wu_ref, wd_ref, y_ref, acc_ref):
    f = pl.program_id(1)

    @pl.when(f == 0)
    def _():
        acc_ref[...] = jnp.zeros_like(acc_ref)

    wg = wg_ref[0].astype(BF16)
    wu = wu_ref[0].astype(BF16)
    wd = wd_ref[0].astype(BF16)
    half = xg_ref.shape[1] // 2
    for r in range(2):
        rows = slice(r * half, (r + 1) * half)
        xg = xg_ref[0, rows, :]
        gate = jnp.dot(xg, wg, preferred_element_type=F32)
        up = jnp.dot(xg, wu, preferred_element_type=F32)
        hid = (_silu(gate) * up).astype(BF16)
        acc_ref[rows, :] += jnp.dot(hid, wd, preferred_element_type=F32)

    @pl.when(f == pl.num_programs(1) - 1)
    def _():
        m, d = acc_ref.shape
        gcol = gc_ref[0]
        y_ref[0] = (acc_ref[...] * jnp.concatenate([gcol] * (d // LANES), axis=-1)).astype(BF16)


def _ffn(xg, gc, w_gate, w_up, w_down, tf):
    ne, m, d = xg.shape
    ff = w_gate.shape[2]
    return pl.pallas_call(
        _ffn_body,
        grid=(ne, ff // tf),
        in_specs=[pl.BlockSpec((1, m, d), lambda e, f: (e, 0, 0)),
                  pl.BlockSpec((1, m, LANES), lambda e, f: (e, 0, 0)),
                  pl.BlockSpec((1, d, tf), lambda e, f: (e, 0, f)),
                  pl.BlockSpec((1, d, tf), lambda e, f: (e, 0, f)),
                  pl.BlockSpec((1, tf, d), lambda e, f: (e, f, 0))],
        out_specs=pl.BlockSpec((1, m, d), lambda e, f: (e, 0, 0)),
        out_shape=jax.ShapeDtypeStruct((ne, m, d), BF16),
        scratch_shapes=[pltpu.VMEM((m, d), F32)],
        compiler_params=_params("parallel", "arbitrary"),
        name="moe_ffn",
    )(xg, gc, w_gate, w_up, w_down)


def _scatter_body(post_ref, y_ref, x1_ref, g2_ref, ng_ref, o_ref, *, cap):
    ne = y_ref.shape[0]
    ts = post_ref.shape[1]
    d = y_ref.shape[2]
    slot = lax.broadcasted_iota(I32, (ts, cap), 1)
    post = post_ref[0]
    onehot = jnp.concatenate(
        [jnp.where(post[:, e:e + 1] == slot, 1.0, 0.0).astype(BF16) for e in range(ne)], axis=-1)
    moe = jnp.dot(onehot, y_ref[...].reshape(ne * cap, d), preferred_element_type=F32)
    x2 = x1_ref[0] + g2_ref[0] * moe
    o_ref[0] = _rms(x2) * ng_ref[...]


def _scatter(post, y, x1, g2, ng, cap, ts):
    bsz, s, d = x1.shape
    ne = y.shape[0]
    return pl.pallas_call(
        functools.partial(_scatter_body, cap=cap),
        grid=(bsz, s // ts),
        in_specs=[pl.BlockSpec((1, ts, ne), lambda b, t: (b, t, 0)),
                  pl.BlockSpec((ne, cap, d), lambda b, t: (0, b, 0)),
                  pl.BlockSpec((1, ts, d), lambda b, t: (b, t, 0)),
                  pl.BlockSpec((1, 1, d), lambda b, t: (b, 0, 0)),
                  pl.BlockSpec((1, d), lambda b, t: (0, 0))],
        out_specs=pl.BlockSpec((1, ts, d), lambda b, t: (b, t, 0)),
        out_shape=jax.ShapeDtypeStruct((bsz, s, d), F32),
        compiler_params=_params("parallel", "arbitrary"),
        name="moe_scatter_final",
    )(post, y, x1, g2, ng)


def _pad_lanes(a, n=LANES):
    return jnp.pad(a, [(0, 0)] * (a.ndim - 1) + [(0, n - a.shape[-1])])


def kernel(x, c, ada_w, ada_b, norm_mix_g, norm_ffn_g, norm_final_g, w_in, lambda_q1, lambda_k1, lambda_q2, lambda_k2, attn_subln_g, rel_bias_table, conv_w, conv_b, dt_bias_f, dt_bias_b, A_log_f, A_log_b, D_skip, ssm_norm_g, w_out, router_w, w_gate, w_up, w_down):
    bsz, s, d = x.shape
    depth = ada_w.shape[0]
    ne = router_w.shape[2]
    cap = CAPACITY_FACTOR * s // ne
    att_w = ATT_HEADS * ATT_V_DIM
    nconv = conv_w.shape[3]
    splits = (att_w, att_w, SSD_WIDTH, nconv, LANES)
    offs = [0]
    for wd in splits:
        offs.append(offs[-1] + wd)
    row = lambda a: a.reshape(1, -1)

    assert depth == 1, "the final RMSNorm is fused into the (single) layer's scatter kernel"
    for l in range(depth):
        lam_init = 0.8 - 0.6 * math.exp(-0.3 * l)
        mod = _ada_mod(c, ada_w[l], ada_b[l])
        sh1, sc1, g1, sh2, sc2, g2 = [m.reshape(bsz, 1, d) for m in jnp.split(mod, 6, axis=-1)]

        w_l = w_in[l]
        w_pad = _pad_lanes(jnp.concatenate([w_l[:, :2 * att_w], w_l[:, 3 * att_w:]], axis=1), offs[-1]).astype(BF16)
        w_vt = w_l[:, 2 * att_w:3 * att_w].T.astype(BF16)
        q, k, v_t, z, xbc, dt_raw = _in_proj(x, sh1, sc1, row(norm_mix_g[l]), w_pad, w_vt, offs, ts=512)

        att = _attention(q, k, v_t, rel_bias_table, row(lambda_q1[l]), row(lambda_k1[l]),
                         row(lambda_q2[l]), row(lambda_k2[l]), row(attn_subln_g[l]), lam_init, tq=2048)

        dt_bias = _pad_lanes(jnp.concatenate([dt_bias_f[l], dt_bias_b[l]]).reshape(1, -1))
        a_log = _pad_lanes(jnp.concatenate([A_log_f[l], A_log_b[l]]).reshape(1, -1))
        d_skip = jnp.repeat(D_skip[l], SSD_HEADDIM).reshape(1, -1)
        ssd = _ssd(z, xbc, dt_raw, conv_w[l].reshape(SSD_CONV, nconv), row(conv_b[l]), dt_bias, a_log,
                   d_skip, row(ssm_norm_g[l]))

        rw_t = router_w[l].T
        rw_hi = rw_t.astype(BF16)
        rw_split = jnp.stack([rw_hi, (rw_t - rw_hi.astype(F32)).astype(BF16)])
        x1, h2, aff_t = _out_proj(att, ssd, x, g1, sh2, sc2, row(norm_ffn_g[l]), w_out[l].astype(BF16),
                                  rw_split, ts=512)
        pos, pos_t = _topk(aff_t, cap)
        xg, gc = _gather(h2, pos, aff_t, cap)
        y = _ffn(xg, gc, w_gate[l], w_up[l], w_down[l], tf=256)
        x = _scatter(pos_t, y, x1, g2, row(norm_final_g), cap, ts=512)
    return x
```

```python
import functools
import math

import jax
import jax.numpy as jnp
from jax import lax
from jax.experimental import pallas as pl
from jax.experimental.pallas import tpu as pltpu

F32 = jnp.float32
BF16 = jnp.bfloat16
I32 = jnp.int32
HIGHEST = lax.Precision.HIGHEST

ATT_HEADS = 4
ATT_QK_DIM = 64
ATT_V_DIM = 128
N_BUCKETS = 32
SSD_HEADS = 8
SSD_HEADDIM = 64
SSD_GROUPS = 2
SSD_STATE = 64
SSD_CONV = 5
SSD_CHUNK = 128
SSD_WIDTH = SSD_HEADS * SSD_HEADDIM
SSD_BC = SSD_GROUPS * SSD_STATE
N_EXPERTS = 16
CAPACITY_FACTOR = 2
EPS = 1e-6
LANES = 128
NEG_BIG = -1e30
LOG2E = math.log2(math.e)

VMEM_LIMIT = 56 * 1024 * 1024


def _params(*semantics):
    return pltpu.CompilerParams(dimension_semantics=semantics, vmem_limit_bytes=VMEM_LIMIT)


def _silu(v):
    return v * jax.nn.sigmoid(v)


def _rms(v, eps=EPS):
    return v * lax.rsqrt(jnp.mean(v * v, axis=-1, keepdims=True) + eps)


def _split_bf16(v):
    hi = v.astype(BF16)
    return hi, (v - hi.astype(F32)).astype(BF16)


def _mod_body(c_ref, w_ref, b_ref, o_ref):
    a_hi, a_lo = _split_bf16(_silu(c_ref[...]))
    w_hi, w_lo = _split_bf16(w_ref[...])
    dot = functools.partial(jnp.dot, preferred_element_type=F32)
    o_ref[...] = dot(a_hi, w_hi) + (dot(a_lo, w_hi) + dot(a_hi, w_lo)) + b_ref[...]


def _ada_mod(c, w, b):
    bsz, d = c.shape
    n = w.shape[1]
    tn = n // 4
    return pl.pallas_call(
        _mod_body,
        grid=(n // tn,),
        in_specs=[pl.BlockSpec((bsz, d), lambda j: (0, 0)),
                  pl.BlockSpec((d, tn), lambda j: (0, j)),
                  pl.BlockSpec((1, tn), lambda j: (0, j))],
        out_specs=pl.BlockSpec((bsz, tn), lambda j: (0, j)),
        out_shape=jax.ShapeDtypeStruct((bsz, n), F32),
        compiler_params=_params("arbitrary"),
        name="ada_mod",
    )(c, w, b.reshape(1, n))


def _inproj_body(x_ref, sh_ref, sc_ref, g_ref, w_ref, wvt_ref, q_ref, k_ref, vt_ref, z_ref, xbc_ref, dt_ref,
                 *, offs, qscale):
    h = _rms(x_ref[0]) * g_ref[...]
    h = (h * (1.0 + sc_ref[0]) + sh_ref[0]).astype(BF16)

    def proj(i):
        return jnp.dot(h, w_ref[:, offs[i]:offs[i + 1]], preferred_element_type=F32)

    q_ref[0] = (proj(0) * qscale).astype(BF16)
    k_ref[0] = proj(1).astype(BF16)
    vt_ref[0, 0] = lax.dot_general(wvt_ref[...], h, (((1,), (1,)), ((), ())),
                                   preferred_element_type=F32).astype(BF16)
    z_ref[0] = proj(2)
    xbc_ref[0] = proj(3)
    dt_ref[0] = proj(4)


def _in_proj(x, sh, sc, g, w_pad, w_vt, offs, ts):
    bsz, s, d = x.shape
    widths = [offs[i + 1] - offs[i] for i in range(5)]
    dts = [BF16, BF16, F32, F32, F32]
    nv = w_vt.shape[0]
    tok = lambda wd: pl.BlockSpec((1, ts, wd), lambda b, t: (b, t, 0))
    vec = pl.BlockSpec((1, 1, d), lambda b, t: (b, 0, 0))
    return pl.pallas_call(
        functools.partial(_inproj_body, offs=tuple(offs), qscale=ATT_QK_DIM ** -0.5 * LOG2E),
        grid=(bsz, s // ts),
        in_specs=[tok(d), vec, vec,
                  pl.BlockSpec((1, d), lambda b, t: (0, 0)),
                  pl.BlockSpec(w_pad.shape, lambda b, t: (0, 0)),
                  pl.BlockSpec(w_vt.shape, lambda b, t: (0, 0))],
        out_specs=[tok(widths[0]), tok(widths[1]), pl.BlockSpec((1, 1, nv, ts), lambda b, t: (b, t, 0, 0))]
                  + [tok(wd) for wd in widths[2:]],
        out_shape=[jax.ShapeDtypeStruct((bsz, s, widths[0]), BF16), jax.ShapeDtypeStruct((bsz, s, widths[1]), BF16),
                   jax.ShapeDtypeStruct((bsz, s // ts, nv, ts), BF16)]
                  + [jax.ShapeDtypeStruct((bsz, s, wd), dt) for wd, dt in zip(widths[2:], dts[2:])],
        compiler_params=_params("parallel", "arbitrary"),
        name="in_proj",
    )(x, sh, sc, g, w_pad, w_vt)


def _t5_bucket(rel):
    n = jnp.abs(rel)
    large = jnp.full(rel.shape, 8, I32)
    for t in (12, 16, 23, 32, 46, 64, 91):
        large = large + jnp.where(n >= t, 1, 0)
    return jnp.where(rel > 0, 16, 0) + jnp.where(n < 8, n, large)


def _attn_body(tbl_ref, lq1_ref, lk1_ref, lq2_ref, lk2_ref, g_ref, q_ref, k_ref, vt_ref, o_ref,
               bias_ref, acc_ref, *, lam_init, tk):
    h = pl.program_id(0)
    qi = pl.program_id(1)
    s, tq = bias_ref.shape
    tb = 2 * LANES
    ring = pl.next_power_of_2(tb + tq - 1)

    @pl.when(pl.program_id(2) == 0)
    def _():
        j = lax.broadcasted_iota(I32, (8, ring), 1)
        for kt in range(s // tb):
            bucket = _t5_bucket((kt * tb + tb - 1) - qi * tq - j)
            w = jnp.zeros((8, ring), F32)
            for bb in range(N_BUCKETS):
                w = jnp.where(bucket == bb, tbl_ref[bb, h] * LOG2E, w)
            big = jnp.broadcast_to(w[0:1], (tb, ring))
            rolled = pltpu.roll(big, ring - (tb - 1), axis=1, stride=1, stride_axis=0)
            bias_ref[kt * tb:(kt + 1) * tb, :] = rolled[:, :tq]

    lam = (jnp.exp(jnp.sum(lq1_ref[...] * lk1_ref[...], axis=-1, keepdims=True))
           - jnp.exp(jnp.sum(lq2_ref[...] * lk2_ref[...], axis=-1, keepdims=True)) + lam_init)
    q = q_ref[0]
    lane = lax.broadcasted_iota(I32, q.shape, 1)
    qms = [jnp.where((lane >= m * ATT_QK_DIM) & (lane < (m + 1) * ATT_QK_DIM), q, jnp.zeros_like(q))
           for m in range(2)]
    qw = 2 * LANES
    nqc = tq // qw
    nkb, kb = vt_ref.shape[1], vt_ref.shape[3]
    ones_rows = 16

    streams = [(m, c) for m in range(2) for c in range(nqc)]
    run_max = {st: jnp.full((8, qw), NEG_BIG, F32) for st in streams}
    row_max, logit_tiles, prob_tiles = {}, {}, {}

    def logits_item(st, t):
        m, c = st
        rows, cols = slice(t * tk, (t + 1) * tk), slice(c * qw, (c + 1) * qw)
        logit = bias_ref[rows, cols] + lax.dot_general(k_ref[0, rows, :], qms[m][cols, :], (((1,), (1,)), ((), ())),
                                                       preferred_element_type=F32)
        logit_tiles[st, t] = logit
        mx = run_max[st]
        for r in range(tk // 8):
            mx = jnp.maximum(mx, logit[r * 8:(r + 1) * 8, :])
        run_max[st] = mx
        if t == s // tk - 1:
            row_max[st] = jnp.max(run_max[st], axis=0, keepdims=True)

    def exp_item(st, t):
        prob_tiles[st, t] = jnp.exp2(logit_tiles.pop((st, t)) - row_max[st]).astype(BF16)

    def pv_item(st, blk):
        m, c = st
        cols = slice(c * qw, (c + 1) * qw)
        probs = jnp.concatenate([prob_tiles.pop((st, t)) for t in range(blk * kb // tk, (blk + 1) * kb // tk)], axis=0)
        vt_ext = jnp.concatenate([vt_ref[0, blk], jnp.ones((ones_rows, kb), BF16)], axis=0)
        part = jnp.dot(vt_ext, probs, preferred_element_type=F32)
        if blk == 0:
            acc_ref[m, :, cols] = part
        else:
            acc_ref[m, :, cols] += part

    nt = s // tk
    for phase in range(len(streams) + 2):
        stage_items = []
        if phase < len(streams):
            stage_items.append([functools.partial(logits_item, streams[phase], t) for t in range(nt)])
        if 0 <= phase - 1 < len(streams):
            stage_items.append([functools.partial(exp_item, streams[phase - 1], t) for t in range(nt)])
        if 0 <= phase - 2 < len(streams):
            stage_items.append([functools.partial(pv_item, streams[phase - 2], blk) for blk in range(nkb)])
        for i in range(nt):
            for items in stage_items:
                per = nt // len(items)
                if i % per == 0:
                    items[i // per]()
    outs = [acc_ref[m, :ATT_V_DIM, :] / acc_ref[m, ATT_V_DIM:ATT_V_DIM + 1, :] for m in range(2)]
    att = outs[0] - lam * outs[1]
    att = att * lax.rsqrt(jnp.mean(att * att, axis=0, keepdims=True) + EPS)
    o_ref[0] = (att * (g_ref[...] * (1.0 - lam_init))).T.astype(BF16)


def _attention(q, k, v_t, tbl, lq1, lk1, lq2, lk2, subln_g, lam_init, tq):
    bsz, s, _ = q.shape
    nkb, kb = v_t.shape[1], v_t.shape[3]
    small = lambda n: pl.BlockSpec((1, n), lambda h, i, b: (0, 0))
    return pl.pallas_call(
        functools.partial(_attn_body, lam_init=lam_init, tk=LANES),
        grid=(ATT_HEADS, s // tq, bsz),
        in_specs=[pl.BlockSpec(memory_space=pltpu.SMEM),
                  small(ATT_QK_DIM), small(ATT_QK_DIM), small(ATT_QK_DIM), small(ATT_QK_DIM),
                  pl.BlockSpec((ATT_V_DIM, 1), lambda h, i, b: (0, 0)),
                  pl.BlockSpec((1, tq, ATT_V_DIM), lambda h, i, b: (b, i, h)),
                  pl.BlockSpec((1, s, ATT_V_DIM), lambda h, i, b: (b, 0, h)),
                  pl.BlockSpec((1, nkb, ATT_V_DIM, kb), lambda h, i, b: (b, 0, h, 0))],
        out_specs=pl.BlockSpec((1, tq, ATT_V_DIM), lambda h, i, b: (b, i, h)),
        out_shape=jax.ShapeDtypeStruct((bsz, s, ATT_HEADS * ATT_V_DIM), BF16),
        scratch_shapes=[pltpu.VMEM((s, tq), F32),
                        pltpu.VMEM((2, ATT_V_DIM + 16, tq), F32)],
        compiler_params=_params("parallel", "parallel", "arbitrary"),
        name="diff_attention",
    )(tbl, lq1, lk1, lq2, lk2, subln_g.reshape(ATT_V_DIM, 1), q, k, v_t)


def _head_selector(off):
    r = lax.broadcasted_iota(I32, (LANES, SSD_WIDTH), 0)
    c = lax.broadcasted_iota(I32, (LANES, SSD_WIDTH), 1)
    return jnp.where(r - off == c // SSD_HEADDIM, 1.0, 0.0).astype(BF16)


def _expand_heads(cols, selector):
    hi = cols.astype(BF16)
    lo = (cols - hi.astype(F32)).astype(BF16)
    return (jnp.dot(hi, selector, preferred_element_type=F32)
            + jnp.dot(lo, selector, preferred_element_type=F32))


def _ssd_body(z_ref, xbc_ref, dtr_ref, cw_ref, cb_ref, dtb_ref, alog_ref, dskip_ref, ng_ref, o_ref,
              xp_ref, xc_ref, dt_ref, da_ref, cs_ref, cst_ref, ecst_ref, dtt_ref, bt_ref, y_ref, hf_ref, hb_ref):
    s = xbc_ref.shape[1]
    ch = SSD_CHUNK
    assert ch == LANES
    nc = s // ch
    pad = 8
    nconv = xbc_ref.shape[2]

    xp_ref[0:pad, :] = jnp.zeros((pad, nconv), F32)
    xp_ref[pad + s:pad + s + pad, :] = jnp.zeros((pad, nconv), F32)
    xp_ref[pad:pad + s, :] = xbc_ref[0]

    def conv_chunk(c, carry):
        r0 = pl.multiple_of(c * ch, ch)
        win = xp_ref[pl.ds(r0, ch + 2 * pad), :]
        acc = jnp.broadcast_to(cb_ref[...], (ch, nconv))
        for j in range(SSD_CONV):
            lo = pad + j - SSD_CONV // 2
            acc = acc + cw_ref[j:j + 1, :] * win[lo:lo + ch, :]
        xc_ref[pl.ds(r0, ch), :] = _silu(acc)
        return carry

    lax.fori_loop(0, nc, conv_chunk, 0)

    lane = lax.broadcasted_iota(I32, (s, LANES), 1)
    pre = dtr_ref[0] + dtb_ref[...]
    dt = jnp.maximum(pre, 0.0) + jnp.log1p(jnp.exp(-jnp.abs(pre)))
    dt = jnp.where(lane < 2 * SSD_HEADS, dt, 0.0)
    dt_ref[...] = dt
    da_ref[...] = dt * (-jnp.exp(alog_ref[...]))

    row = lax.broadcasted_iota(I32, (ch, ch), 0)
    col = lax.broadcasted_iota(I32, (ch, ch), 1)
    tri = jnp.where(col <= row, 1.0, 0.0).astype(F32)
    lower = col <= row
    upper = col >= row
    lane_c = lax.broadcasted_iota(I32, (ch, LANES), 1)
    st_rows = lax.broadcasted_iota(I32, (SSD_BC, SSD_WIDTH), 0) // SSD_STATE
    st_cols = lax.broadcasted_iota(I32, (SSD_BC, SSD_WIDTH), 1) // (SSD_WIDTH // SSD_GROUPS)
    same_group = st_rows == st_cols
    sel_f = _head_selector(0)
    sel_b = _head_selector(SSD_HEADS)

    hf_ref[...] = jnp.zeros_like(hf_ref)
    hb_ref[...] = jnp.zeros_like(hb_ref)

    def wide(x):
        return jnp.concatenate([x[c * ch:(c + 1) * ch, :] for c in range(nc)], axis=1)

    da_wide = wide(da_ref[...])
    cs_wide = jnp.dot(tri, da_wide, precision=HIGHEST, preferred_element_type=F32)
    for c in range(nc):
        cs_ref[c * ch:(c + 1) * ch, :] = cs_wide[:, c * LANES:(c + 1) * LANES]
    cst_ref[...] = cs_wide.T
    ecst_ref[...] = (cs_wide - da_wide).T
    dtt_ref[...] = wide(dt_ref[...]).T
    bt_ref[...] = wide(xc_ref[:, SSD_WIDTH:SSD_WIDTH + SSD_BC]).T.astype(BF16)

    def fwd_chunk(c, carry):
        r0 = pl.multiple_of(c * ch, ch)
        da = da_ref[pl.ds(r0, ch), :]
        dtc = dt_ref[pl.ds(r0, ch), :]
        cs = cs_ref[pl.ds(r0, ch), :]
        tot = cs[ch - 1:ch, :]
        ecs = cs - da
        nh2 = 2 * SSD_HEADS
        cs_t = cst_ref[pl.ds(r0, nh2), :]
        ecs_t = ecst_ref[pl.ds(r0, nh2), :]
        dt_t = dtt_ref[pl.ds(r0, nh2), :]
        xs = xc_ref[pl.ds(r0, ch), 0:SSD_WIDTH]
        cm = xc_ref[pl.ds(r0, ch), SSD_WIDTH + SSD_BC:SSD_WIDTH + 2 * SSD_BC]
        bt = bt_ref[pl.ds(r0, ch), :]
        cmb = cm.astype(BF16)
        xsb = xs.astype(BF16)

        g_mats = []
        for g in range(SSD_GROUPS):
            cg = jnp.where(lane_c // SSD_STATE == g, cmb, jnp.zeros_like(cmb))
            g_mats.append(jnp.dot(cg, bt, preferred_element_type=F32))
        for pair in range(SSD_HEADS // 2):
            xpair = xsb[:, pair * LANES:(pair + 1) * LANES]
            ypair = jnp.zeros((ch, LANES), F32)
            for sub in range(2):
                hh = 2 * pair + sub
                g = hh // (SSD_HEADS // SSD_GROUPS)
                d_f = cs[:, hh:hh + 1] - cs_t[hh:hh + 1, :]
                d_b = ecs_t[SSD_HEADS + hh:SSD_HEADS + hh + 1, :] - ecs[:, SSD_HEADS + hh:SSD_HEADS + hh + 1]
                decay = jnp.exp(jnp.where(lower, d_f, d_b))
                wgt = (jnp.where(lower, dt_t[hh:hh + 1, :], 0.0)
                       + jnp.where(upper, dt_t[SSD_HEADS + hh:SSD_HEADS + hh + 1, :], 0.0))
                m = (g_mats[g] * decay * wgt).astype(BF16)
                xh = jnp.where(lane_c // SSD_HEADDIM == sub, xpair, jnp.zeros_like(xpair))
                ypair = ypair + jnp.dot(m, xh, preferred_element_type=F32)
            y_ref[pl.ds(r0, ch), pair * LANES:(pair + 1) * LANES] = ypair

        hf = hf_ref[...]
        factors = _expand_heads(jnp.concatenate(
            [jnp.exp(cs), dtc * jnp.exp(tot - cs), jnp.broadcast_to(jnp.exp(tot), (8, LANES))], axis=0), sel_f)
        y_off = jnp.dot(cmb, hf.astype(BF16), preferred_element_type=F32) * factors[0:ch]
        y_ref[pl.ds(r0, ch), :] = y_ref[pl.ds(r0, ch), :] + y_off
        xd = (xs * factors[ch:2 * ch]).astype(BF16)
        upd = jnp.dot(bt, xd, preferred_element_type=F32)
        hf_ref[...] = hf * factors[2 * ch:2 * ch + 1] + jnp.where(same_group, upd, 0.0)
        return carry

    lax.fori_loop(0, nc, fwd_chunk, 0, unroll=4)

    def bwd_chunk(i, carry):
        c = nc - 1 - i
        r0 = pl.multiple_of(c * ch, ch)
        da = da_ref[pl.ds(r0, ch), :]
        dtc = dt_ref[pl.ds(r0, ch), :]
        cs = cs_ref[pl.ds(r0, ch), :]
        tot = cs[ch - 1:ch, :]
        ecs = cs - da
        xs = xc_ref[pl.ds(r0, ch), 0:SSD_WIDTH]
        cm = xc_ref[pl.ds(r0, ch), SSD_WIDTH + SSD_BC:SSD_WIDTH + 2 * SSD_BC]
        bt = bt_ref[pl.ds(r0, ch), :]

        hb = hb_ref[...]
        factors = _expand_heads(jnp.concatenate(
            [jnp.exp(tot - ecs), dtc * jnp.exp(ecs), jnp.broadcast_to(jnp.exp(tot), (8, LANES))], axis=0), sel_b)
        y_off = jnp.dot(cm.astype(BF16), hb.astype(BF16), preferred_element_type=F32) * factors[0:ch]
        xd = (xs * factors[ch:2 * ch]).astype(BF16)
        upd = jnp.dot(bt, xd, preferred_element_type=F32)
        hb_ref[...] = hb * factors[2 * ch:2 * ch + 1] + jnp.where(same_group, upd, 0.0)

        y = y_ref[pl.ds(r0, ch), :] + y_off + dskip_ref[...] * xs
        y = y * _silu(z_ref[0, pl.ds(r0, ch), :])
        gw = SSD_WIDTH // SSD_GROUPS
        yn = jnp.concatenate([_rms(y[:, g * gw:(g + 1) * gw]) for g in range(SSD_GROUPS)], axis=-1)
        o_ref[0, pl.ds(r0, ch), :] = (yn * ng_ref[...]).astype(BF16)
        return carry

    lax.fori_loop(0, nc, bwd_chunk, 0, unroll=4)


def _ssd(z, xbc, dt_raw, conv_w, conv_b, dt_bias, a_log, d_skip, norm_g):
    bsz, s, nconv = xbc.shape
    tok = lambda wd: pl.BlockSpec((1, s, wd), lambda b: (b, 0, 0))
    full = lambda a: pl.BlockSpec(a.shape, lambda b: (0, 0))
    return pl.pallas_call(
        _ssd_body,
        grid=(bsz,),
        in_specs=[tok(SSD_WIDTH), tok(nconv), tok(LANES), full(conv_w), full(conv_b), full(dt_bias),
                  full(a_log), full(d_skip), full(norm_g)],
        out_specs=tok(SSD_WIDTH),
        out_shape=jax.ShapeDtypeStruct((bsz, s, SSD_WIDTH), BF16),
        scratch_shapes=[pltpu.VMEM((s + 16, nconv), F32),
                        pltpu.VMEM((s, nconv), F32),
                        pltpu.VMEM((s, LANES), F32),
                        pltpu.VMEM((s, LANES), F32),
                        pltpu.VMEM((s, LANES), F32),
                        pltpu.VMEM((s, LANES), F32),
                        pltpu.VMEM((s, LANES), F32),
                        pltpu.VMEM((s, LANES), F32),
                        pltpu.VMEM((s, SSD_BC), BF16),
                        pltpu.VMEM((s, SSD_WIDTH), F32),
                        pltpu.VMEM((SSD_BC, SSD_WIDTH), F32),
                        pltpu.VMEM((SSD_BC, SSD_WIDTH), F32)],
        compiler_params=_params("parallel"),
        name="ssd_mixer",
    )(z, xbc, dt_raw, conv_w, conv_b, dt_bias, a_log, d_skip, norm_g)


def _outproj_body(att_ref, ssd_ref, x_ref, g1_ref, sh_ref, sc_ref, ng_ref, w_ref, rw_ref,
                  x1_ref, h2_ref, aff_ref):
    na = att_ref.shape[2]
    mix = (jnp.dot(att_ref[0], w_ref[0:na, :], preferred_element_type=F32)
           + jnp.dot(ssd_ref[0], w_ref[na:, :], preferred_element_type=F32))
    x1 = x_ref[0] + g1_ref[0] * mix
    x1_ref[0] = x1
    h2 = _rms(x1) * ng_ref[...]
    h2 = h2 * (1.0 + sc_ref[0]) + sh_ref[0]
    h2_hi = h2.astype(BF16)
    h2_ref[0] = h2_hi
    h2_lo = (h2 - h2_hi.astype(F32)).astype(BF16)
    nt = lambda a, b: lax.dot_general(a, b, (((1,), (1,)), ((), ())), preferred_element_type=F32)
    logits = nt(rw_ref[0], h2_hi) + (nt(rw_ref[0], h2_lo) + nt(rw_ref[1], h2_hi))
    e = jnp.exp(logits - jnp.max(logits, axis=0, keepdims=True))
    aff_ref[0] = e / jnp.sum(e, axis=0, keepdims=True)


def _out_proj(att, ssd, x, g1, sh2, sc2, ng, w_out, rw_t, ts):
    bsz, s, d = x.shape
    tok = lambda wd: pl.BlockSpec((1, ts, wd), lambda b, t: (b, t, 0))
    vec = pl.BlockSpec((1, 1, d), lambda b, t: (b, 0, 0))
    full = lambda a: pl.BlockSpec(a.shape, lambda b, t: (0, 0))
    ne = rw_t.shape[1]
    return pl.pallas_call(
        _outproj_body,
        grid=(bsz, s // ts),
        in_specs=[tok(att.shape[2]), tok(ssd.shape[2]), tok(d), vec, vec, vec, full(ng), full(w_out),
                  pl.BlockSpec(rw_t.shape, lambda b, t: (0, 0, 0))],
        out_specs=[tok(d), tok(d), pl.BlockSpec((1, ne, ts), lambda b, t: (b, 0, t))],
        out_shape=[jax.ShapeDtypeStruct((bsz, s, d), F32),
                   jax.ShapeDtypeStruct((bsz, s, d), BF16),
                   jax.ShapeDtypeStruct((bsz, ne, s), F32)],
        compiler_params=_params("parallel", "arbitrary"),
        name="out_proj_router",
    )(att, ssd, x, g1, sh2, sc2, ng, w_out, rw_t)


def _topk_body(aff_ref, pos_ref, post_ref, ut_ref, *, cap):
    nrow, s = aff_ref.shape
    r = lax.broadcasted_iota(I32, (s, s), 0)
    c = lax.broadcasted_iota(I32, (s, s), 1)
    ut_ref[...] = jnp.where(r < c, 1.0, 0.0).astype(BF16)

    bits = pltpu.bitcast(aff_ref[...], I32)

    def count(mask):
        return jnp.sum(jnp.where(mask, 1, 0), axis=-1, keepdims=True)

    thr = jnp.zeros((nrow, 1), I32)
    for bit in range(30, -1, -1):
        cand = thr | (1 << bit)
        thr = jnp.where(count(bits >= cand) >= cap, cand, thr)
    gt = bits > thr
    eq = bits == thr
    need = cap - count(gt)
    eq_rank = jnp.dot(jnp.where(eq, 1.0, 0.0).astype(BF16), ut_ref[...], preferred_element_type=F32)
    sel = gt | (eq & (eq_rank < need.astype(F32)))
    slot = jnp.dot(jnp.where(sel, 1.0, 0.0).astype(BF16), ut_ref[...], preferred_element_type=F32)
    pos = jnp.where(sel, slot, -1.0)
    pos_ref[...] = pos.astype(I32)
    post_ref[...] = pos.T.astype(I32)


def _topk(aff_t, cap):
    bsz, ne, s = aff_t.shape
    nrow = bsz * ne
    pos, pos_t = pl.pallas_call(
        functools.partial(_topk_body, cap=cap),
        grid=(1,),
        in_specs=[pl.BlockSpec((nrow, s), lambda i: (0, 0))],
        out_specs=[pl.BlockSpec((nrow, s), lambda i: (0, 0)),
                   pl.BlockSpec((s, nrow), lambda i: (0, 0))],
        out_shape=[jax.ShapeDtypeStruct((nrow, s), I32),
                   jax.ShapeDtypeStruct((s, nrow), I32)],
        scratch_shapes=[pltpu.VMEM((s, s), BF16)],
        compiler_params=_params("arbitrary"),
        name="expert_topk",
    )(aff_t.reshape(nrow, s))
    return pos.reshape(bsz, ne, s), pos_t.reshape(s, bsz, ne).transpose(1, 0, 2)


def _gather_body(h2_ref, pos_ref, aff_ref, xg_ref, gc_ref, *, cap):
    ne, s = pos_ref.shape[1], pos_ref.shape[2]
    slot = lax.broadcasted_iota(I32, (cap, s), 0)
    h2 = h2_ref[0]
    for e in range(ne):
        hit = pos_ref[0, e:e + 1, :] == slot
        xg_ref[e] = jnp.dot(jnp.where(hit, 1.0, 0.0).astype(BF16), h2,
                            preferred_element_type=F32).astype(BF16)
        gate = jnp.sum(jnp.where(hit, aff_ref[0, e:e + 1, :], 0.0), axis=-1, keepdims=True)
        gc_ref[e] = jnp.broadcast_to(gate, (cap, LANES))


def _gather(h2, pos, aff_t, cap):
    bsz, s, d = h2.shape
    ne = pos.shape[1]
    return pl.pallas_call(
        functools.partial(_gather_body, cap=cap),
        grid=(bsz,),
        in_specs=[pl.BlockSpec((1, s, d), lambda b: (b, 0, 0)),
                  pl.BlockSpec((1, ne, s), lambda b: (b, 0, 0)),
                  pl.BlockSpec((1, ne, s), lambda b: (b, 0, 0))],
        out_specs=[pl.BlockSpec((ne, cap, d), lambda b: (0, b, 0)),
                   pl.BlockSpec((ne, cap, LANES), lambda b: (0, b, 0))],
        out_shape=[jax.ShapeDtypeStruct((ne, bsz * cap, d), BF16),
                   jax.ShapeDtypeStruct((ne, bsz * cap, LANES), F32)],
        compiler_params=_params("parallel"),
        name="moe_gather",
    )(h2, pos, aff_t)


def _ffn_body(xg_ref, gc_ref, wg_ref, wu_ref, wd_ref, y_ref, acc_ref):
    f = pl.program_id(1)

    @pl.when(f == 0)
    def _():
        acc_ref[...] = jnp.zeros_like(acc_ref)

    wg = wg_ref[0].astype(BF16)
    wu = wu_ref[0].astype(BF16)
    wd = wd_ref[0].astype(BF16)
    half = xg_ref.shape[1] // 2
    for r in range(2):
        rows = slice(r * half, (r + 1) * half)
        xg = xg_ref[0, rows, :]
        gate = jnp.dot(xg, wg, preferred_element_type=F32)
        up = jnp.dot(xg, wu, preferred_element_type=F32)
        hid = (_silu(gate) * up).astype(BF16)
        acc_ref[rows, :] += jnp.dot(hid, wd, preferred_element_type=F32)

    @pl.when(f == pl.num_programs(1) - 1)
    def _():
        m, d = acc_ref.shape
        gcol = gc_ref[0]
        y_ref[0] = (acc_ref[...] * jnp.concatenate([gcol] * (d // LANES), axis=-1)).astype(BF16)


def _ffn(xg, gc, w_gate, w_up, w_down, tf):
    ne, m, d = xg.shape
    ff = w_gate.shape[2]
    return pl.pallas_call(
        _ffn_body,
        grid=(ne, ff // tf),
        in_specs=[pl.BlockSpec((1, m, d), lambda e, f: (e, 0, 0)),
                  pl.BlockSpec((1, m, LANES), lambda e, f: (e, 0, 0)),
                  pl.BlockSpec((1, d, tf), lambda e, f: (e, 0, f)),
                  pl.BlockSpec((1, d, tf), lambda e, f: (e, 0, f)),
                  pl.BlockSpec((1, tf, d), lambda e, f: (e, f, 0))],
        out_specs=pl.BlockSpec((1, m, d), lambda e, f: (e, 0, 0)),
        out_shape=jax.ShapeDtypeStruct((ne, m, d), BF16),
        scratch_shapes=[pltpu.VMEM((m, d), F32)],
        compiler_params=_params("parallel", "arbitrary"),
        name="moe_ffn",
    )(xg, gc, w_gate, w_up, w_down)


def _scatter_body(post_ref, y_ref, x1_ref, g2_ref, ng_ref, o_ref, *, cap):
    ne = y_ref.shape[0]
    ts = post_ref.shape[1]
    d = y_ref.shape[2]
    slot = lax.broadcasted_iota(I32, (ts, cap), 1)
    post = post_ref[0]
    onehot = jnp.concatenate(
        [jnp.where(post[:, e:e + 1] == slot, 1.0, 0.0).astype(BF16) for e in range(ne)], axis=-1)
    moe = jnp.dot(onehot, y_ref[...].reshape(ne * cap, d), preferred_element_type=F32)
    x2 = x1_ref[0] + g2_ref[0] * moe
    o_ref[0] = _rms(x2) * ng_ref[...]


def _scatter(post, y, x1, g2, ng, cap, ts):
    bsz, s, d = x1.shape
    ne = y.shape[0]
    return pl.pallas_call(
        functools.partial(_scatter_body, cap=cap),
        grid=(bsz, s // ts),
        in_specs=[pl.BlockSpec((1, ts, ne), lambda b, t: (b, t, 0)),
                  pl.BlockSpec((ne, cap, d), lambda b, t: (0, b, 0)),
                  pl.BlockSpec((1, ts, d), lambda b, t: (b, t, 0)),
                  pl.BlockSpec((1, 1, d), lambda b, t: (b, 0, 0)),
                  pl.BlockSpec((1, d), lambda b, t: (0, 0))],
        out_specs=pl.BlockSpec((1, ts, d), lambda b, t: (b, t, 0)),
        out_shape=jax.ShapeDtypeStruct((bsz, s, d), F32),
        compiler_params=_params("parallel", "arbitrary"),
        name="moe_scatter_final",
    )(post, y, x1, g2, ng)


def _pad_lanes(a, n=LANES):
    return jnp.pad(a, [(0, 0)] * (a.ndim - 1) + [(0, n - a.shape[-1])])


def kernel(x, c, ada_w, ada_b, norm_mix_g, norm_ffn_g, norm_final_g, w_in, lambda_q1, lambda_k1, lambda_q2, lambda_k2, attn_subln_g, rel_bias_table, conv_w, conv_b, dt_bias_f, dt_bias_b, A_log_f, A_log_b, D_skip, ssm_norm_g, w_out, router_w, w_gate, w_up, w_down):
    bsz, s, d = x.shape
    depth = ada_w.shape[0]
    ne = router_w.shape[2]
    cap = CAPACITY_FACTOR * s // ne
    att_w = ATT_HEADS * ATT_V_DIM
    nconv = conv_w.shape[3]
    splits = (att_w, att_w, SSD_WIDTH, nconv, LANES)
    offs = [0]
    for wd in splits:
        offs.append(offs[-1] + wd)
    row = lambda a: a.reshape(1, -1)

    assert depth == 1, "the final RMSNorm is fused into the (single) layer's scatter kernel"
    for l in range(depth):
        lam_init = 0.8 - 0.6 * math.exp(-0.3 * l)
        mod = _ada_mod(c, ada_w[l], ada_b[l])
        sh1, sc1, g1, sh2, sc2, g2 = [m.reshape(bsz, 1, d) for m in jnp.split(mod, 6, axis=-1)]

        w_l = w_in[l]
        w_pad = _pad_lanes(jnp.concatenate([w_l[:, :2 * att_w], w_l[:, 3 * att_w:]], axis=1), offs[-1]).astype(BF16)
        w_vt = w_l[:, 2 * att_w:3 * att_w].T.astype(BF16)
        q, k, v_t, z, xbc, dt_raw = _in_proj(x, sh1, sc1, row(norm_mix_g[l]), w_pad, w_vt, offs, ts=512)

        att = _attention(q, k, v_t, rel_bias_table, row(lambda_q1[l]), row(lambda_k1[l]),
                         row(lambda_q2[l]), row(lambda_k2[l]), row(attn_subln_g[l]), lam_init, tq=2048)

        dt_bias = _pad_lanes(jnp.concatenate([dt_bias_f[l], dt_bias_b[l]]).reshape(1, -1))
        a_log = _pad_lanes(jnp.concatenate([A_log_f[l], A_log_b[l]]).reshape(1, -1))
        d_skip = jnp.repeat(D_skip[l], SSD_HEADDIM).reshape(1, -1)
        ssd = _ssd(z, xbc, dt_raw, conv_w[l].reshape(SSD_CONV, nconv), row(conv_b[l]), dt_bias, a_log,
                   d_skip, row(ssm_norm_g[l]))

        rw_t = router_w[l].T
        rw_hi = rw_t.astype(BF16)
        rw_split = jnp.stack([rw_hi, (rw_t - rw_hi.astype(F32)).astype(BF16)])
        x1, h2, aff_t = _out_proj(att, ssd, x, g1, sh2, sc2, row(norm_ffn_g[l]), w_out[l].astype(BF16),
                                  rw_split, ts=512)
        pos, pos_t = _topk(aff_t, cap)
        xg, gc = _gather(h2, pos, aff_t, cap)
        y = _ffn(xg, gc, w_gate[l], w_up[l], w_down[l], tf=256)
        x = _scatter(pos_t, y, x1, g2, row(norm_final_g), cap, ts=512)
    return x
```

```python
import functools
import math

import jax
import jax.numpy as jnp
from jax import lax
from jax.experimental import pallas as pl
from jax.experimental.pallas import tpu as pltpu

F32 = jnp.float32
BF16 = jnp.bfloat16
I32 = jnp.int32
HIGHEST = lax.Precision.HIGHEST

ATT_HEADS = 4
ATT_QK_DIM = 64
ATT_V_DIM = 128
N_BUCKETS = 32
SSD_HEADS = 8
SSD_HEADDIM = 64
SSD_GROUPS = 2
SSD_STATE = 64
SSD_CONV = 5
SSD_CHUNK = 128
SSD_WIDTH = SSD_HEADS * SSD_HEADDIM
SSD_BC = SSD_GROUPS * SSD_STATE
N_EXPERTS = 16
CAPACITY_FACTOR = 2
EPS = 1e-6
LANES = 128
NEG_BIG = -1e30
LOG2E = math.log2(math.e)

VMEM_LIMIT = 56 * 1024 * 1024


def _params(*semantics):
    return pltpu.CompilerParams(dimension_semantics=semantics, vmem_limit_bytes=VMEM_LIMIT)


def _silu(v):
    return v * jax.nn.sigmoid(v)


def _rms(v, eps=EPS):
    return v * lax.rsqrt(jnp.mean(v * v, axis=-1, keepdims=True) + eps)


def _split_bf16(v):
    hi = v.astype(BF16)
    return hi, (v - hi.astype(F32)).astype(BF16)


def _mod_body(c_ref, w_ref, b_ref, o_ref):
    a_hi, a_lo = _split_bf16(_silu(c_ref[...]))
    w_hi, w_lo = _split_bf16(w_ref[...])
    dot = functools.partial(jnp.dot, preferred_element_type=F32)
    o_ref[...] = dot(a_hi, w_hi) + (dot(a_lo, w_hi) + dot(a_hi, w_lo)) + b_ref[...]


def _ada_mod(c, w, b):
    bsz, d = c.shape
    n = w.shape[1]
    tn = n // 4
    return pl.pallas_call(
        _mod_body,
        grid=(n // tn,),
        in_specs=[pl.BlockSpec((bsz, d), lambda j: (0, 0)),
                  pl.BlockSpec((d, tn), lambda j: (0, j)),
                  pl.BlockSpec((1, tn), lambda j: (0, j))],
        out_specs=pl.BlockSpec((bsz, tn), lambda j: (0, j)),
        out_shape=jax.ShapeDtypeStruct((bsz, n), F32),
        compiler_params=_params("arbitrary"),
        name="ada_mod",
    )(c, w, b.reshape(1, n))


def _inproj_body(x_ref, sh_ref, sc_ref, g_ref, w_ref, wvt_ref, q_ref, k_ref, vt_ref, z_ref, xbc_ref, dt_ref,
                 *, offs, qscale):
    h = _rms(x_ref[0]) * g_ref[...]
    h = (h * (1.0 + sc_ref[0]) + sh_ref[0]).astype(BF16)

    def proj(i):
        return jnp.dot(h, w_ref[:, offs[i]:offs[i + 1]], preferred_element_type=F32)

    q_ref[0] = (proj(0) * qscale).astype(BF16)
    k_ref[0] = proj(1).astype(BF16)
    vt_ref[0, 0] = lax.dot_general(wvt_ref[...], h, (((1,), (1,)), ((), ())),
                                   preferred_element_type=F32).astype(BF16)
    z_ref[0] = proj(2)
    xbc_ref[0] = proj(3)
    dt_ref[0] = proj(4)


def _in_proj(x, sh, sc, g, w_pad, w_vt, offs, ts):
    bsz, s, d = x.shape
    widths = [offs[i + 1] - offs[i] for i in range(5)]
    dts = [BF16, BF16, F32, F32, F32]
    nv = w_vt.shape[0]
    tok = lambda wd: pl.BlockSpec((1, ts, wd), lambda b, t: (b, t, 0))
    vec = pl.BlockSpec((1, 1, d), lambda b, t: (b, 0, 0))
    return pl.pallas_call(
        functools.partial(_inproj_body, offs=tuple(offs), qscale=ATT_QK_DIM ** -0.5 * LOG2E),
        grid=(bsz, s // ts),
        in_specs=[tok(d), vec, vec,
                  pl.BlockSpec((1, d), lambda b, t: (0, 0)),
                  pl.BlockSpec(w_pad.shape, lambda b, t: (0, 0)),
                  pl.BlockSpec(w_vt.shape, lambda b, t: (0, 0))],
        out_specs=[tok(widths[0]), tok(widths[1]), pl.BlockSpec((1, 1, nv, ts), lambda b, t: (b, t, 0, 0))]
                  + [tok(wd) for wd in widths[2:]],
        out_shape=[jax.ShapeDtypeStruct((bsz, s, widths[0]), BF16), jax.ShapeDtypeStruct((bsz, s, widths[1]), BF16),
                   jax.ShapeDtypeStruct((bsz, s // ts, nv, ts), BF16)]
                  + [jax.ShapeDtypeStruct((bsz, s, wd), dt) for wd, dt in zip(widths[2:], dts[2:])],
        compiler_params=_params("parallel", "arbitrary"),
        name="in_proj",
    )(x, sh, sc, g, w_pad, w_vt)


def _t5_bucket(rel):
    n = jnp.abs(rel)
    large = jnp.full(rel.shape, 8, I32)
    for t in (12, 16, 23, 32, 46, 64, 91):
        large = large + jnp.where(n >= t, 1, 0)
    return jnp.where(rel > 0, 16, 0) + jnp.where(n < 8, n, large)


def _attn_body(tbl_ref, lq1_ref, lk1_ref, lq2_ref, lk2_ref, g_ref, q_ref, k_ref, vt_ref, o_ref,
               bias_ref, acc_ref, *, lam_init, tk):
    h = pl.program_id(0)
    qi = pl.program_id(1)
    s, tq = bias_ref.shape
    tb = 2 * LANES
    ring = pl.next_power_of_2(tb + tq - 1)

    @pl.when(pl.program_id(2) == 0)
    def _():
        j = lax.broadcasted_iota(I32, (8, ring), 1)
        for kt in range(s // tb):
            bucket = _t5_bucket((kt * tb + tb - 1) - qi * tq - j)
            w = jnp.zeros((8, ring), F32)
            for bb in range(N_BUCKETS):
                w = jnp.where(bucket == bb, tbl_ref[bb, h] * LOG2E, w)
            big = jnp.broadcast_to(w[0:1], (tb, ring))
            rolled = pltpu.roll(big, ring - (tb - 1), axis=1, stride=1, stride_axis=0)
            bias_ref[kt * tb:(kt + 1) * tb, :] = rolled[:, :tq]

    lam = (jnp.exp(jnp.sum(lq1_ref[...] * lk1_ref[...], axis=-1, keepdims=True))
           - jnp.exp(jnp.sum(lq2_ref[...] * lk2_ref[...], axis=-1, keepdims=True)) + lam_init)
    q = q_ref[0]
    lane = lax.broadcasted_iota(I32, q.shape, 1)
    qms = [jnp.where((lane >= m * ATT_QK_DIM) & (lane < (m + 1) * ATT_QK_DIM), q, jnp.zeros_like(q))
           for m in range(2)]
    qw = 2 * LANES
    nqc = tq // qw
    nkb, kb = vt_ref.shape[1], vt_ref.shape[3]
    ones_rows = 16

    streams = [(m, c) for m in range(2) for c in range(nqc)]
    run_max = {st: jnp.full((8, qw), NEG_BIG, F32) for st in streams}
    row_max, logit_tiles, prob_tiles = {}, {}, {}

    def logits_item(st, t):
        m, c = st
        rows, cols = slice(t * tk, (t + 1) * tk), slice(c * qw, (c + 1) * qw)
        logit = bias_ref[rows, cols] + lax.dot_general(k_ref[0, rows, :], qms[m][cols, :], (((1,), (1,)), ((), ())),
                                                       preferred_element_type=F32)
        logit_tiles[st, t] = logit
        mx = run_max[st]
        for r in range(tk // 8):
            mx = jnp.maximum(mx, logit[r * 8:(r + 1) * 8, :])
        run_max[st] = mx
        if t == s // tk - 1:
            row_max[st] = jnp.max(run_max[st], axis=0, keepdims=True)

    def exp_item(st, t):
        prob_tiles[st, t] = jnp.exp2(logit_tiles.pop((st, t)) - row_max[st]).astype(BF16)

    def pv_item(st, blk):
        m, c = st
        cols = slice(c * qw, (c + 1) * qw)
        probs = jnp.concatenate([prob_tiles.pop((st, t)) for t in range(blk * kb // tk, (blk + 1) * kb // tk)], axis=0)
        vt_ext = jnp.concatenate([vt_ref[0, blk], jnp.ones((ones_rows, kb), BF16)], axis=0)
        part = jnp.dot(vt_ext, probs, preferred_element_type=F32)
        if blk == 0:
            acc_ref[m, :, cols] = part
        else:
            acc_ref[m, :, cols] += part

    nt = s // tk
    for phase in range(len(streams) + 2):
        stage_items = []
        if phase < len(streams):
            stage_items.append([functools.partial(logits_item, streams[phase], t) for t in range(nt)])
        if 0 <= phase - 1 < len(streams):
            stage_items.append([functools.partial(exp_item, streams[phase - 1], t) for t in range(nt)])
        if 0 <= phase - 2 < len(streams):
            stage_items.append([functools.partial(pv_item, streams[phase - 2], blk) for blk in range(nkb)])
        for i in range(nt):
            for items in stage_items:
                per = nt // len(items)
                if i % per == 0:
                    items[i // per]()
    outs = [acc_ref[m, :ATT_V_DIM, :] / acc_ref[m, ATT_V_DIM:ATT_V_DIM + 1, :] for m in range(2)]
    att = outs[0] - lam * outs[1]
    att = att * lax.rsqrt(jnp.mean(att * att, axis=0, keepdims=True) + EPS)
    o_ref[0] = (att * (g_ref[...] * (1.0 - lam_init))).T.astype(BF16)


def _attention(q, k, v_t, tbl, lq1, lk1, lq2, lk2, subln_g, lam_init, tq):
    bsz, s, _ = q.shape
    nkb, kb = v_t.shape[1], v_t.shape[3]
    small = lambda n: pl.BlockSpec((1, n), lambda h, i, b: (0, 0))
    return pl.pallas_call(
        functools.partial(_attn_body, lam_init=lam_init, tk=LANES),
        grid=(ATT_HEADS, s // tq, bsz),
        in_specs=[pl.BlockSpec(memory_space=pltpu.SMEM),
                  small(ATT_QK_DIM), small(ATT_QK_DIM), small(ATT_QK_DIM), small(ATT_QK_DIM),
                  pl.BlockSpec((ATT_V_DIM, 1), lambda h, i, b: (0, 0)),
                  pl.BlockSpec((1, tq, ATT_V_DIM), lambda h, i, b: (b, i, h)),
                  pl.BlockSpec((1, s, ATT_V_DIM), lambda h, i, b: (b, 0, h)),
                  pl.BlockSpec((1, nkb, ATT_V_DIM, kb), lambda h, i, b: (b, 0, h, 0))],
        out_specs=pl.BlockSpec((1, tq, ATT_V_DIM), lambda h, i, b: (b, i, h)),
        out_shape=jax.ShapeDtypeStruct((bsz, s, ATT_HEADS * ATT_V_DIM), BF16),
        scratch_shapes=[pltpu.VMEM((s, tq), F32),
                        pltpu.VMEM((2, ATT_V_DIM + 16, tq), F32)],
        compiler_params=_params("parallel", "parallel", "arbitrary"),
        name="diff_attention",
    )(tbl, lq1, lk1, lq2, lk2, subln_g.reshape(ATT_V_DIM, 1), q, k, v_t)


def _head_selector(off):
    r = lax.broadcasted_iota(I32, (LANES, SSD_WIDTH), 0)
    c = lax.broadcasted_iota(I32, (LANES, SSD_WIDTH), 1)
    return jnp.where(r - off == c // SSD_HEADDIM, 1.0, 0.0).astype(BF16)


def _expand_heads(cols, selector):
    hi = cols.astype(BF16)
    lo = (cols - hi.astype(F32)).astype(BF16)
    return (jnp.dot(hi, selector, preferred_element_type=F32)
            + jnp.dot(lo, selector, preferred_element_type=F32))


def _ssd_body(z_ref, xbc_ref, dtr_ref, cw_ref, cb_ref, dtb_ref, alog_ref, dskip_ref, ng_ref, o_ref,
              xp_ref, xc_ref, dt_ref, da_ref, cs_ref, cst_ref, ecst_ref, dtt_ref, bt_ref, y_ref, hf_ref, hb_ref):
    s = xbc_ref.shape[1]
    ch = SSD_CHUNK
    assert ch == LANES
    nc = s // ch
    pad = 8
    nconv = xbc_ref.shape[2]

    xp_ref[0:pad, :] = jnp.zeros((pad, nconv), F32)
    xp_ref[pad + s:pad + s + pad, :] = jnp.zeros((pad, nconv), F32)
    xp_ref[pad:pad + s, :] = xbc_ref[0]

    def conv_chunk(c, carry):
        r0 = pl.multiple_of(c * ch, ch)
        win = xp_ref[pl.ds(r0, ch + 2 * pad), :]
        acc = jnp.broadcast_to(cb_ref[...], (ch, nconv))
        for j in range(SSD_CONV):
            lo = pad + j - SSD_CONV // 2
            acc = acc + cw_ref[j:j + 1, :] * win[lo:lo + ch, :]
        xc_ref[pl.ds(r0, ch), :] = _silu(acc)
        return carry

    lax.fori_loop(0, nc, conv_chunk, 0)

    lane = lax.broadcasted_iota(I32, (s, LANES), 1)
    pre = dtr_ref[0] + dtb_ref[...]
    dt = jnp.maximum(pre, 0.0) + jnp.log1p(jnp.exp(-jnp.abs(pre)))
    dt = jnp.where(lane < 2 * SSD_HEADS, dt, 0.0)
    dt_ref[...] = dt
    da_ref[...] = dt * (-jnp.exp(alog_ref[...]))

    row = lax.broadcasted_iota(I32, (ch, ch), 0)
    col = lax.broadcasted_iota(I32, (ch, ch), 1)
    tri = jnp.where(col <= row, 1.0, 0.0).astype(F32)
    lower = col <= row
    upper = col >= row
    lane_c = lax.broadcasted_iota(I32, (ch, LANES), 1)
    st_rows = lax.broadcasted_iota(I32, (SSD_BC, SSD_WIDTH), 0) // SSD_STATE
    st_cols = lax.broadcasted_iota(I32, (SSD_BC, SSD_WIDTH), 1) // (SSD_WIDTH // SSD_GROUPS)
    same_group = st_rows == st_cols
    sel_f = _head_selector(0)
    sel_b = _head_selector(SSD_HEADS)

    hf_ref[...] = jnp.zeros_like(hf_ref)
    hb_ref[...] = jnp.zeros_like(hb_ref)

    def wide(x):
        return jnp.concatenate([x[c * ch:(c + 1) * ch, :] for c in range(nc)], axis=1)

    da_wide = wide(da_ref[...])
    cs_wide = jnp.dot(tri, da_wide, precision=HIGHEST, preferred_element_type=F32)
    for c in range(nc):
        cs_ref[c * ch:(c + 1) * ch, :] = cs_wide[:, c * LANES:(c + 1) * LANES]
    cst_ref[...] = cs_wide.T
    ecst_ref[...] = (cs_wide - da_wide).T
    dtt_ref[...] = wide(dt_ref[...]).T
    bt_ref[...] = wide(xc_ref[:, SSD_WIDTH:SSD_WIDTH + SSD_BC]).T.astype(BF16)

    def fwd_chunk(c, carry):
        r0 = pl.multiple_of(c * ch, ch)
        da = da_ref[pl.ds(r0, ch), :]
        dtc = dt_ref[pl.ds(r0, ch), :]
        cs = cs_ref[pl.ds(r0, ch), :]
        tot = cs[ch - 1:ch, :]
        ecs = cs - da
        nh2 = 2 * SSD_HEADS
        cs_t = cst_ref[pl.ds(r0, nh2), :]
        ecs_t = ecst_ref[pl.ds(r0, nh2), :]
        dt_t = dtt_ref[pl.ds(r0, nh2), :]
        xs = xc_ref[pl.ds(r0, ch), 0:SSD_WIDTH]
        cm = xc_ref[pl.ds(r0, ch), SSD_WIDTH + SSD_BC:SSD_WIDTH + 2 * SSD_BC]
        bt = bt_ref[pl.ds(r0, ch), :]
        cmb = cm.astype(BF16)
        xsb = xs.astype(BF16)

        g_mats = []
        for g in range(SSD_GROUPS):
            cg = jnp.where(lane_c // SSD_STATE == g, cmb, jnp.zeros_like(cmb))
            g_mats.append(jnp.dot(cg, bt, preferred_element_type=F32))
        for pair in range(SSD_HEADS // 2):
            xpair = xsb[:, pair * LANES:(pair + 1) * LANES]
            ypair = jnp.zeros((ch, LANES), F32)
            for sub in range(2):
                hh = 2 * pair + sub
                g = hh // (SSD_HEADS // SSD_GROUPS)
                d_f = cs[:, hh:hh + 1] - cs_t[hh:hh + 1, :]
                d_b = ecs_t[SSD_HEADS + hh:SSD_HEADS + hh + 1, :] - ecs[:, SSD_HEADS + hh:SSD_HEADS + hh + 1]
                decay = jnp.exp(jnp.where(lower, d_f, d_b))
                wgt = (jnp.where(lower, dt_t[hh:hh + 1, :], 0.0)
                       + jnp.where(upper, dt_t[SSD_HEADS + hh:SSD_HEADS + hh + 1, :], 0.0))
                m = (g_mats[g] * decay * wgt).astype(BF16)
                xh = jnp.where(lane_c // SSD_HEADDIM == sub, xpair, jnp.zeros_like(xpair))
                ypair = ypair + jnp.dot(m, xh, preferred_element_type=F32)
            y_ref[pl.ds(r0, ch), pair * LANES:(pair + 1) * LANES] = ypair

        hf = hf_ref[...]
        factors = _expand_heads(jnp.concatenate(
            [jnp.exp(cs), dtc * jnp.exp(tot - cs), jnp.broadcast_to(jnp.exp(tot), (8, LANES))], axis=0), sel_f)
        y_off = jnp.dot(cmb, hf.astype(BF16), preferred_element_type=F32) * factors[0:ch]
        y_ref[pl.ds(r0, ch), :] = y_ref[pl.ds(r0, ch), :] + y_off
        xd = (xs * factors[ch:2 * ch]).astype(BF16)
        upd = jnp.dot(bt, xd, preferred_element_type=F32)
        hf_ref[...] = hf * factors[2 * ch:2 * ch + 1] + jnp.where(same_group, upd, 0.0)
        return carry

    lax.fori_loop(0, nc, fwd_chunk, 0, unroll=4)

    def bwd_chunk(i, carry):
        c = nc - 1 - i
        r0 = pl.multiple_of(c * ch, ch)
        da = da_ref[pl.ds(r0, ch), :]
        dtc = dt_ref[pl.ds(r0, ch), :]
        cs = cs_ref[pl.ds(r0, ch), :]
        tot = cs[ch - 1:ch, :]
        ecs = cs - da
        xs = xc_ref[pl.ds(r0, ch), 0:SSD_WIDTH]
        cm = xc_ref[pl.ds(r0, ch), SSD_WIDTH + SSD_BC:SSD_WIDTH + 2 * SSD_BC]
        bt = bt_ref[pl.ds(r0, ch), :]

        hb = hb_ref[...]
        factors = _expand_heads(jnp.concatenate(
            [jnp.exp(tot - ecs), dtc * jnp.exp(ecs), jnp.broadcast_to(jnp.exp(tot), (8, LANES))], axis=0), sel_b)
        y_off = jnp.dot(cm.astype(BF16), hb.astype(BF16), preferred_element_type=F32) * factors[0:ch]
        xd = (xs * factors[ch:2 * ch]).astype(BF16)
        upd = jnp.dot(bt, xd, preferred_element_type=F32)
        hb_ref[...] = hb * factors[2 * ch:2 * ch + 1] + jnp.where(same_group, upd, 0.0)

        y = y_ref[pl.ds(r0, ch), :] + y_off + dskip_ref[...] * xs
        y = y * _silu(z_ref[0, pl.ds(r0, ch), :])
        gw = SSD_WIDTH // SSD_GROUPS
        yn = jnp.concatenate([_rms(y[:, g * gw:(g + 1) * gw]) for g in range(SSD_GROUPS)], axis=-1)
        o_ref[0, pl.ds(r0, ch), :] = (yn * ng_ref[...]).astype(BF16)
        return carry

    lax.fori_loop(0, nc, bwd_chunk, 0, unroll=4)


def _ssd(z, xbc, dt_raw, conv_w, conv_b, dt_bias, a_log, d_skip, norm_g):
    bsz, s, nconv = xbc.shape
    tok = lambda wd: pl.BlockSpec((1, s, wd), lambda b: (b, 0, 0))
    full = lambda a: pl.BlockSpec(a.shape, lambda b: (0, 0))
    return pl.pallas_call(
        _ssd_body,
        grid=(bsz,),
        in_specs=[tok(SSD_WIDTH), tok(nconv), tok(LANES), full(conv_w), full(conv_b), full(dt_bias),
                  full(a_log), full(d_skip), full(norm_g)],
        out_specs=tok(SSD_WIDTH),
        out_shape=jax.ShapeDtypeStruct((bsz, s, SSD_WIDTH), BF16),
        scratch_shapes=[pltpu.VMEM((s + 16, nconv), F32),
                        pltpu.VMEM((s, nconv), F32),
                        pltpu.VMEM((s, LANES), F32),
                        pltpu.VMEM((s, LANES), F32),
                        pltpu.VMEM((s, LANES), F32),
                        pltpu.VMEM((s, LANES), F32),
                        pltpu.VMEM((s, LANES), F32),
                        pltpu.VMEM((s, LANES), F32),
                        pltpu.VMEM((s, SSD_BC), BF16),
                        pltpu.VMEM((s, SSD_WIDTH), F32),
                        pltpu.VMEM((SSD_BC, SSD_WIDTH), F32),
                        pltpu.VMEM((SSD_BC, SSD_WIDTH), F32)],
        compiler_params=_params("parallel"),
        name="ssd_mixer",
    )(z, xbc, dt_raw, conv_w, conv_b, dt_bias, a_log, d_skip, norm_g)


def _outproj_body(att_ref, ssd_ref, x_ref, g1_ref, sh_ref, sc_ref, ng_ref, w_ref, rw_ref,
                  x1_ref, h2_ref, aff_ref):
    na = att_ref.shape[2]
    mix = (jnp.dot(att_ref[0], w_ref[0:na, :], preferred_element_type=F32)
           + jnp.dot(ssd_ref[0], w_ref[na:, :], preferred_element_type=F32))
    x1 = x_ref[0] + g1_ref[0] * mix
    x1_ref[0] = x1
    h2 = _rms(x1) * ng_ref[...]
    h2 = h2 * (1.0 + sc_ref[0]) + sh_ref[0]
    h2_hi = h2.astype(BF16)
    h2_ref[0] = h2_hi
    h2_lo = (h2 - h2_hi.astype(F32)).astype(BF16)
    nt = lambda a, b: lax.dot_general(a, b, (((1,), (1,)), ((), ())), preferred_element_type=F32)
    logits = nt(rw_ref[0], h2_hi) + (nt(rw_ref[0], h2_lo) + nt(rw_ref[1], h2_hi))
    e = jnp.exp(logits - jnp.max(logits, axis=0, keepdims=True))
    aff_ref[0] = e / jnp.sum(e, axis=0, keepdims=True)


def _out_proj(att, ssd, x, g1, sh2, sc2, ng, w_out, rw_t, ts):
    bsz, s, d = x.shape
    tok = lambda wd: pl.BlockSpec((1, ts, wd), lambda b, t: (b, t, 0))
    vec = pl.BlockSpec((1, 1, d), lambda b, t: (b, 0, 0))
    full = lambda a: pl.BlockSpec(a.shape, lambda b, t: (0, 0))
    ne = rw_t.shape[1]
    return pl.pallas_call(
        _outproj_body,
        grid=(bsz, s // ts),
        in_specs=[tok(att.shape[2]), tok(ssd.shape[2]), tok(d), vec, vec, vec, full(ng), full(w_out),
                  pl.BlockSpec(rw_t.shape, lambda b, t: (0, 0, 0))],
        out_specs=[tok(d), tok(d), pl.BlockSpec((1, ne, ts), lambda b, t: (b, 0, t))],
        out_shape=[jax.ShapeDtypeStruct((bsz, s, d), F32),
                   jax.ShapeDtypeStruct((bsz, s, d), BF16),
                   jax.ShapeDtypeStruct((bsz, ne, s), F32)],
        compiler_params=_params("parallel", "arbitrary"),
        name="out_proj_router",
    )(att, ssd, x, g1, sh2, sc2, ng, w_out, rw_t)


def _topk_body(aff_ref, pos_ref, post_ref, ut_ref, *, cap):
    nrow, s = aff_ref.shape
    r = lax.broadcasted_iota(I32, (s, s), 0)
    c = lax.broadcasted_iota(I32, (s, s), 1)
    ut_ref[...] = jnp.where(r < c, 1.0, 0.0).astype(BF16)

    bits = pltpu.bitcast(aff_ref[...], I32)

    def count(mask):
        return jnp.sum(jnp.where(mask, 1, 0), axis=-1, keepdims=True)

    thr = jnp.zeros((nrow, 1), I32)
    for bit in range(30, -1, -1):
        cand = thr | (1 << bit)
        thr = jnp.where(count(bits >= cand) >= cap, cand, thr)
    gt = bits > thr
    eq = bits == thr
    need = cap - count(gt)
    eq_rank = jnp.dot(jnp.where(eq, 1.0, 0.0).astype(BF16), ut_ref[...], preferred_element_type=F32)
    sel = gt | (eq & (eq_rank < need.astype(F32)))
    slot = jnp.dot(jnp.where(sel, 1.0, 0.0).astype(BF16), ut_ref[...], preferred_element_type=F32)
    pos = jnp.where(sel, slot, -1.0)
    pos_ref[...] = pos.astype(I32)
    post_ref[...] = pos.T.astype(I32)


def _topk(aff_t, cap):
    bsz, ne, s = aff_t.shape
    nrow = bsz * ne
    pos, pos_t = pl.pallas_call(
        functools.partial(_topk_body, cap=cap),
        grid=(1,),
        in_specs=[pl.BlockSpec((nrow, s), lambda i: (0, 0))],
        out_specs=[pl.BlockSpec((nrow, s), lambda i: (0, 0)),
                   pl.BlockSpec((s, nrow), lambda i: (0, 0))],
        out_shape=[jax.ShapeDtypeStruct((nrow, s), I32),
                   jax.ShapeDtypeStruct((s, nrow), I32)],
        scratch_shapes=[pltpu.VMEM((s, s), BF16)],
        compiler_params=_params("arbitrary"),
        name="expert_topk",
    )(aff_t.reshape(nrow, s))
    return pos.reshape(bsz, ne, s), pos_t.reshape(s, bsz, ne).transpose(1, 0, 2)


def _gather_body(h2_ref, pos_ref, aff_ref, xg_ref, gc_ref, *, cap):
    ne, s = pos_ref.shape[1], pos_ref.shape[2]
    slot = lax.broadcasted_iota(I32, (cap, s), 0)
    h2 = h2_ref[0]
    for e in range(ne):
        hit = pos_ref[0, e:e + 1, :] == slot
        xg_ref[e] = jnp.dot(jnp.where(hit, 1.0, 0.0).astype(BF16), h2,
                            preferred_element_type=F32).astype(BF16)
        gate = jnp.sum(jnp.where(hit, aff_ref[0, e:e + 1, :], 0.0), axis=-1, keepdims=True)
        gc_ref[e] = jnp.broadcast_to(gate, (cap, LANES))


def _gather(h2, pos, aff_t, cap):
    bsz, s, d = h2.shape
    ne = pos.shape[1]
    return pl.pallas_call(
        functools.partial(_gather_body, cap=cap),
        grid=(bsz,),
        in_specs=[pl.BlockSpec((1, s, d), lambda b: (b, 0, 0)),
                  pl.BlockSpec((1, ne, s), lambda b: (b, 0, 0)),
                  pl.BlockSpec((1, ne, s), lambda b: (b, 0, 0))],
        out_specs=[pl.BlockSpec((ne, cap, d), lambda b: (0, b, 0)),
                   pl.BlockSpec((ne, cap, LANES), lambda b: (0, b, 0))],
        out_shape=[jax.ShapeDtypeStruct((ne, bsz * cap, d), BF16),
                   jax.ShapeDtypeStruct((ne, bsz * cap, LANES), F32)],
        compiler_params=_params("parallel"),
        name="moe_gather",
    )(h2, pos, aff_t)


def _ffn_body(xg_ref, gc_ref, wg_ref, wu_ref, wd_ref, y_ref, acc_ref):
    f = pl.program_id(1)

    @pl.when(f == 0)
    def _():
        acc_ref[...] = jnp.zeros_like(acc_ref)

    wg = wg_ref[0].astype(BF16)
    wu = wu_ref[0].astype(BF16)
    wd = wd_ref[0].astype(BF16)
    half = xg_ref.shape[1] // 2
    for r in range(2):
        rows = slice(r * half, (r + 1) * half)
        xg = xg_ref[0, rows, :]
        gate = jnp.dot(xg, wg, preferred_element_type=F32)
        up = jnp.dot(xg, wu, preferred_element_type=F32)
        hid = (_silu(gate) * up).astype(BF16)
        acc_ref[rows, :] += jnp.dot(hid, wd, preferred_element_type=F32)

    @pl.when(f == pl.num_programs(1) - 1)
    def _():
        m, d = acc_ref.shape
        gcol = gc_ref[0]
        y_ref[0] = (acc_ref[...] * jnp.concatenate([gcol] * (d // LANES), axis=-1)).astype(BF16)


def _ffn(xg, gc, w_gate, w_up, w_down, tf):
    ne, m, d = xg.shape
    ff = w_gate.shape[2]
    return pl.pallas_call(
        _ffn_body,
        grid=(ne, ff // tf),
        in_specs=[pl.BlockSpec((1, m, d), lambda e, f: (e, 0, 0)),
                  pl.BlockSpec((1, m, LANES), lambda e, f: (e, 0, 0)),
                  pl.BlockSpec((1, d, tf), lambda e, f: (e, 0, f)),
                  pl.BlockSpec((1, d, tf), lambda e, f: (e, 0, f)),
                  pl.BlockSpec((1, tf, d), lambda e, f: (e, f, 0))],
        out_specs=pl.BlockSpec((1, m, d), lambda e, f: (e, 0, 0)),
        out_shape=jax.ShapeDtypeStruct((ne, m, d), BF16),
        scratch_shapes=[pltpu.VMEM((m, d), F32)],
        compiler_params=_params("parallel", "arbitrary"),
        name="moe_ffn",
    )(xg, gc, w_gate, w_up, w_down)


def _scatter_body(post_ref, y_ref, x1_ref, g2_ref, ng_ref, o_ref, *, cap):
    ne = y_ref.shape[0]
    ts = post_ref.shape[1]
    d = y_ref.shape[2]
    slot = lax.broadcasted_iota(I32, (ts, cap), 1)
    post = post_ref[0]
    onehot = jnp.concatenate(
        [jnp.where(post[:, e:e + 1] == slot, 1.0, 0.0).astype(BF16) for e in range(ne)], axis=-1)
    moe = jnp.dot(onehot, y_ref[...].reshape(ne * cap, d), preferred_element_type=F32)
    x2 = x1_ref[0] + g2_ref[0] * moe
    o_ref[0] = _rms(x2) * ng_ref[...]


def _scatter(post, y, x1, g2, ng, cap, ts):
    bsz, s, d = x1.shape
    ne = y.shape[0]
    return pl.pallas_call(
        functools.partial(_scatter_body, cap=cap),
        grid=(bsz, s // ts),
        in_specs=[pl.BlockSpec((1, ts, ne), lambda b, t: (b, t, 0)),
                  pl.BlockSpec((ne, cap, d), lambda b, t: (0, b, 0)),
                  pl.BlockSpec((1, ts, d), lambda b, t: (b, t, 0)),
                  pl.BlockSpec((1, 1, d), lambda b, t: (b, 0, 0)),
                  pl.BlockSpec((1, d), lambda b, t: (0, 0))],
        out_specs=pl.BlockSpec((1, ts, d), lambda b, t: (b, t, 0)),
        out_shape=jax.ShapeDtypeStruct((bsz, s, d), F32),
        compiler_params=_params("parallel", "arbitrary"),
        name="moe_scatter_final",
    )(post, y, x1, g2, ng)


def _pad_lanes(a, n=LANES):
    return jnp.pad(a, [(0, 0)] * (a.ndim - 1) + [(0, n - a.shape[-1])])


def kernel(x, c, ada_w, ada_b, norm_mix_g, norm_ffn_g, norm_final_g, w_in, lambda_q1, lambda_k1, lambda_q2, lambda_k2, attn_subln_g, rel_bias_table, conv_w, conv_b, dt_bias_f, dt_bias_b, A_log_f, A_log_b, D_skip, ssm_norm_g, w_out, router_w, w_gate, w_up, w_down):
    bsz, s, d = x.shape
    depth = ada_w.shape[0]
    ne = router_w.shape[2]
    cap = CAPACITY_FACTOR * s // ne
    att_w = ATT_HEADS * ATT_V_DIM
    nconv = conv_w.shape[3]
    splits = (att_w, att_w, SSD_WIDTH, nconv, LANES)
    offs = [0]
    for wd in splits:
        offs.append(offs[-1] + wd)
    row = lambda a: a.reshape(1, -1)

    assert depth == 1, "the final RMSNorm is fused into the (single) layer's scatter kernel"
    for l in range(depth):
        lam_init = 0.8 - 0.6 * math.exp(-0.3 * l)
        mod = _ada_mod(c, ada_w[l], ada_b[l])
        sh1, sc1, g1, sh2, sc2, g2 = [m.reshape(bsz, 1, d) for m in jnp.split(mod, 6, axis=-1)]

        w_l = w_in[l]
        w_pad = _pad_lanes(jnp.concatenate([w_l[:, :2 * att_w], w_l[:, 3 * att_w:]], axis=1), offs[-1]).astype(BF16)
        w_vt = w_l[:, 2 * att_w:3 * att_w].T.astype(BF16)
        q, k, v_t, z, xbc, dt_raw = _in_proj(x, sh1, sc1, row(norm_mix_g[l]), w_pad, w_vt, offs, ts=1024)

        att = _attention(q, k, v_t, rel_bias_table, row(lambda_q1[l]), row(lambda_k1[l]),
                         row(lambda_q2[l]), row(lambda_k2[l]), row(attn_subln_g[l]), lam_init, tq=2048)

        dt_bias = _pad_lanes(jnp.concatenate([dt_bias_f[l], dt_bias_b[l]]).reshape(1, -1))
        a_log = _pad_lanes(jnp.concatenate([A_log_f[l], A_log_b[l]]).reshape(1, -1))
        d_skip = jnp.repeat(D_skip[l], SSD_HEADDIM).reshape(1, -1)
        ssd = _ssd(z, xbc, dt_raw, conv_w[l].reshape(SSD_CONV, nconv), row(conv_b[l]), dt_bias, a_log,
                   d_skip, row(ssm_norm_g[l]))

        rw_t = router_w[l].T
        rw_hi = rw_t.astype(BF16)
        rw_split = jnp.stack([rw_hi, (rw_t - rw_hi.astype(F32)).astype(BF16)])
        x1, h2, aff_t = _out_proj(att, ssd, x, g1, sh2, sc2, row(norm_ffn_g[l]), w_out[l].astype(BF16),
                                  rw_split, ts=1024)
        pos, pos_t = _topk(aff_t, cap)
        xg, gc = _gather(h2, pos, aff_t, cap)
        y = _ffn(xg, gc, w_gate[l], w_up[l], w_down[l], tf=256)
        x = _scatter(pos_t, y, x1, g2, row(norm_final_g), cap, ts=1024)
    return x
```

```python
import functools
import math

import jax
import jax.numpy as jnp
from jax import lax
from jax.experimental import pallas as pl
from jax.experimental.pallas import tpu as pltpu

F32 = jnp.float32
BF16 = jnp.bfloat16
I32 = jnp.int32
HIGHEST = lax.Precision.HIGHEST

ATT_HEADS = 4
ATT_QK_DIM = 64
ATT_V_DIM = 128
N_BUCKETS = 32
SSD_HEADS = 8
SSD_HEADDIM = 64
SSD_GROUPS = 2
SSD_STATE = 64
SSD_CONV = 5
SSD_CHUNK = 128
SSD_WIDTH = SSD_HEADS * SSD_HEADDIM
SSD_BC = SSD_GROUPS * SSD_STATE
N_EXPERTS = 16
CAPACITY_FACTOR = 2
EPS = 1e-6
LANES = 128
NEG_BIG = -1e30
LOG2E = math.log2(math.e)

VMEM_LIMIT = 56 * 1024 * 1024


def _params(*semantics):
    return pltpu.CompilerParams(dimension_semantics=semantics, vmem_limit_bytes=VMEM_LIMIT)


def _silu(v):
    return v * jax.nn.sigmoid(v)


def _rms(v, eps=EPS):
    return v * lax.rsqrt(jnp.mean(v * v, axis=-1, keepdims=True) + eps)


def _split_bf16(v):
    hi = v.astype(BF16)
    return hi, (v - hi.astype(F32)).astype(BF16)


def _mod_body(c_ref, w_ref, b_ref, o_ref):
    a_hi, a_lo = _split_bf16(_silu(c_ref[...]))
    w_hi, w_lo = _split_bf16(w_ref[...])
    dot = functools.partial(jnp.dot, preferred_element_type=F32)
    o_ref[...] = dot(a_hi, w_hi) + (dot(a_lo, w_hi) + dot(a_hi, w_lo)) + b_ref[...]


def _ada_mod(c, w, b):
    bsz, d = c.shape
    n = w.shape[1]
    tn = n // 4
    return pl.pallas_call(
        _mod_body,
        grid=(n // tn,),
        in_specs=[pl.BlockSpec((bsz, d), lambda j: (0, 0)),
                  pl.BlockSpec((d, tn), lambda j: (0, j)),
                  pl.BlockSpec((1, tn), lambda j: (0, j))],
        out_specs=pl.BlockSpec((bsz, tn), lambda j: (0, j)),
        out_shape=jax.ShapeDtypeStruct((bsz, n), F32),
        compiler_params=_params("arbitrary"),
        name="ada_mod",
    )(c, w, b.reshape(1, n))


def _inproj_body(x_ref, sh_ref, sc_ref, g_ref, w_ref, q_ref, k_ref, vt_ref, z_ref, xbc_ref, dt_ref, wvt_ref,
                 *, offs, qscale):
    @pl.when((pl.program_id(0) == 0) & (pl.program_id(1) == 0))
    def _():
        wvt_ref[...] = w_ref[:, offs[2]:offs[3]].astype(F32).T.astype(BF16)

    h = _rms(x_ref[0]) * g_ref[...]
    h = (h * (1.0 + sc_ref[0]) + sh_ref[0]).astype(BF16)

    def proj(i):
        return jnp.dot(h, w_ref[:, offs[i]:offs[i + 1]], preferred_element_type=F32)

    q_ref[0] = (proj(0) * qscale).astype(BF16)
    k_ref[0] = proj(1).astype(BF16)
    vt_ref[0, 0] = lax.dot_general(wvt_ref[...], h, (((1,), (1,)), ((), ())),
                                   preferred_element_type=F32).astype(BF16)
    z_ref[0] = proj(3)
    xbc_ref[0] = proj(4)
    dt = proj(5)
    dt_ref[0] = jnp.concatenate([dt, jnp.zeros((dt.shape[0], LANES - dt.shape[1]), F32)], axis=1)


def _in_proj(x, sh, sc, g, w, offs, ts):
    bsz, s, d = x.shape
    wq, wk, wv, wz, wxbc = [offs[i + 1] - offs[i] for i in range(5)]
    tok = lambda wd: pl.BlockSpec((1, ts, wd), lambda b, t: (b, t, 0))
    vec = pl.BlockSpec((1, 1, d), lambda b, t: (b, 0, 0))
    tok_out = lambda wd, dt: (tok(wd), jax.ShapeDtypeStruct((bsz, s, wd), dt))
    outs = [tok_out(wq, BF16), tok_out(wk, BF16),
            (pl.BlockSpec((1, 1, wv, ts), lambda b, t: (b, t, 0, 0)), jax.ShapeDtypeStruct((bsz, s // ts, wv, ts), BF16)),
            tok_out(wz, F32), tok_out(wxbc, F32), tok_out(LANES, F32)]
    return pl.pallas_call(
        functools.partial(_inproj_body, offs=tuple(offs), qscale=ATT_QK_DIM ** -0.5 * LOG2E),
        grid=(bsz, s // ts),
        in_specs=[tok(d), vec, vec,
                  pl.BlockSpec((1, d), lambda b, t: (0, 0)),
                  pl.BlockSpec(w.shape, lambda b, t: (0, 0))],
        out_specs=[o[0] for o in outs],
        out_shape=[o[1] for o in outs],
        scratch_shapes=[pltpu.VMEM((wv, d), BF16)],
        compiler_params=_params("arbitrary", "arbitrary"),
        name="in_proj",
    )(x, sh, sc, g, w)


def _t5_bucket(rel):
    n = jnp.abs(rel)
    large = jnp.full(rel.shape, 8, I32)
    for t in (12, 16, 23, 32, 46, 64, 91):
        large = large + jnp.where(n >= t, 1, 0)
    return jnp.where(rel > 0, 16, 0) + jnp.where(n < 8, n, large)


def _attn_body(tbl_ref, lq1_ref, lk1_ref, lq2_ref, lk2_ref, g_ref, q_ref, k_ref, vt_ref, o_ref,
               bias_ref, acc_ref, *, lam_init, tk):
    h = pl.program_id(0)
    qi = pl.program_id(1)
    s, tq = bias_ref.shape
    tb = 2 * LANES
    ring = pl.next_power_of_2(tb + tq - 1)

    @pl.when(pl.program_id(2) == 0)
    def _():
        j = lax.broadcasted_iota(I32, (8, ring), 1)
        for kt in range(s // tb):
            bucket = _t5_bucket((kt * tb + tb - 1) - qi * tq - j)
            w = jnp.zeros((8, ring), F32)
            for bb in range(N_BUCKETS):
                w = jnp.where(bucket == bb, tbl_ref[bb, h] * LOG2E, w)
            big = jnp.broadcast_to(w[0:1], (tb, ring))
            rolled = pltpu.roll(big, ring - (tb - 1), axis=1, stride=1, stride_axis=0)
            bias_ref[kt * tb:(kt + 1) * tb, :] = rolled[:, :tq]

    lam = (jnp.exp(jnp.sum(lq1_ref[...] * lk1_ref[...], axis=-1, keepdims=True))
           - jnp.exp(jnp.sum(lq2_ref[...] * lk2_ref[...], axis=-1, keepdims=True)) + lam_init)
    q = q_ref[0]
    lane = lax.broadcasted_iota(I32, q.shape, 1)
    qms = [jnp.where((lane >= m * ATT_QK_DIM) & (lane < (m + 1) * ATT_QK_DIM), q, jnp.zeros_like(q))
           for m in range(2)]
    qw = 2 * LANES
    nqc = tq // qw
    nkb, kb = vt_ref.shape[1], vt_ref.shape[3]
    ones_rows = 16

    streams = [(m, c) for m in range(2) for c in range(nqc)]
    run_max = {st: jnp.full((8, qw), NEG_BIG, F32) for st in streams}
    row_max, logit_tiles, prob_tiles = {}, {}, {}

    def logits_item(st, t):
        m, c = st
        rows, cols = slice(t * tk, (t + 1) * tk), slice(c * qw, (c + 1) * qw)
        logit = bias_ref[rows, cols] + lax.dot_general(k_ref[0, rows, :], qms[m][cols, :], (((1,), (1,)), ((), ())),
                                                       preferred_element_type=F32)
        logit_tiles[st, t] = logit
        mx = run_max[st]
        for r in range(tk // 8):
            mx = jnp.maximum(mx, logit[r * 8:(r + 1) * 8, :])
        run_max[st] = mx
        if t == s // tk - 1:
            row_max[st] = jnp.max(run_max[st], axis=0, keepdims=True)

    def exp_item(st, t):
        prob_tiles[st, t] = jnp.exp2(logit_tiles.pop((st, t)) - row_max[st]).astype(BF16)

    def pv_item(st, blk):
        m, c = st
        cols = slice(c * qw, (c + 1) * qw)
        probs = jnp.concatenate([prob_tiles.pop((st, t)) for t in range(blk * kb // tk, (blk + 1) * kb // tk)], axis=0)
        vt_ext = jnp.concatenate([vt_ref[0, blk], jnp.ones((ones_rows, kb), BF16)], axis=0)
        part = jnp.dot(vt_ext, probs, preferred_element_type=F32)
        if blk == 0:
            acc_ref[m, :, cols] = part
        else:
            acc_ref[m, :, cols] += part

    nt = s // tk
    for phase in range(len(streams) + 2):
        stage_items = []
        if phase < len(streams):
            stage_items.append([functools.partial(logits_item, streams[phase], t) for t in range(nt)])
        if 0 <= phase - 1 < len(streams):
            stage_items.append([functools.partial(exp_item, streams[phase - 1], t) for t in range(nt)])
        if 0 <= phase - 2 < len(streams):
            stage_items.append([functools.partial(pv_item, streams[phase - 2], blk) for blk in range(nkb)])
        for i in range(nt):
            for items in stage_items:
                per = nt // len(items)
                if i % per == 0:
                    items[i // per]()
    outs = [acc_ref[m, :ATT_V_DIM, :] / acc_ref[m, ATT_V_DIM:ATT_V_DIM + 1, :] for m in range(2)]
    att = outs[0] - lam * outs[1]
    att = att * lax.rsqrt(jnp.mean(att * att, axis=0, keepdims=True) + EPS)
    o_ref[0] = (att * (g_ref[...] * (1.0 - lam_init))).T.astype(BF16)


def _attention(q, k, v_t, tbl, lq1, lk1, lq2, lk2, subln_g, lam_init, tq):
    bsz, s, _ = q.shape
    nkb, kb = v_t.shape[1], v_t.shape[3]
    small = lambda n: pl.BlockSpec((1, n), lambda h, i, b: (0, 0))
    return pl.pallas_call(
        functools.partial(_attn_body, lam_init=lam_init, tk=LANES),
        grid=(ATT_HEADS, s // tq, bsz),
        in_specs=[pl.BlockSpec(memory_space=pltpu.SMEM),
                  small(ATT_QK_DIM), small(ATT_QK_DIM), small(ATT_QK_DIM), small(ATT_QK_DIM),
                  pl.BlockSpec((ATT_V_DIM, 1), lambda h, i, b: (0, 0)),
                  pl.BlockSpec((1, tq, ATT_V_DIM), lambda h, i, b: (b, i, h)),
                  pl.BlockSpec((1, s, ATT_V_DIM), lambda h, i, b: (b, 0, h)),
                  pl.BlockSpec((1, nkb, ATT_V_DIM, kb), lambda h, i, b: (b, 0, h, 0))],
        out_specs=pl.BlockSpec((1, tq, ATT_V_DIM), lambda h, i, b: (b, i, h)),
        out_shape=jax.ShapeDtypeStruct((bsz, s, ATT_HEADS * ATT_V_DIM), BF16),
        scratch_shapes=[pltpu.VMEM((s, tq), F32),
                        pltpu.VMEM((2, ATT_V_DIM + 16, tq), F32)],
        compiler_params=_params("parallel", "parallel", "arbitrary"),
        name="diff_attention",
    )(tbl, lq1, lk1, lq2, lk2, subln_g.reshape(ATT_V_DIM, 1), q, k, v_t)


def _head_selector(off):
    r = lax.broadcasted_iota(I32, (LANES, SSD_WIDTH), 0)
    c = lax.broadcasted_iota(I32, (LANES, SSD_WIDTH), 1)
    return jnp.where(r - off == c // SSD_HEADDIM, 1.0, 0.0).astype(BF16)


def _expand_heads(cols, selector):
    hi = cols.astype(BF16)
    lo = (cols - hi.astype(F32)).astype(BF16)
    return (jnp.dot(hi, selector, preferred_element_type=F32)
            + jnp.dot(lo, selector, preferred_element_type=F32))


def _ssd_body(z_ref, xbc_ref, dtr_ref, cw_ref, cb_ref, dtb_ref, alog_ref, dskip_ref, ng_ref, o_ref,
              xp_ref, xc_ref, dt_ref, da_ref, cs_ref, cst_ref, ecst_ref, dtt_ref, bt_ref, y_ref, hf_ref, hb_ref):
    s = xbc_ref.shape[1]
    ch = SSD_CHUNK
    assert ch == LANES
    nc = s // ch
    pad = 8
    nconv = xbc_ref.shape[2]

    xp_ref[0:pad, :] = jnp.zeros((pad, nconv), F32)
    xp_ref[pad + s:pad + s + pad, :] = jnp.zeros((pad, nconv), F32)
    xp_ref[pad:pad + s, :] = xbc_ref[0]

    def conv_chunk(c, carry):
        r0 = pl.multiple_of(c * ch, ch)
        win = xp_ref[pl.ds(r0, ch + 2 * pad), :]
        acc = jnp.broadcast_to(cb_ref[...], (ch, nconv))
        for j in range(SSD_CONV):
            lo = pad + j - SSD_CONV // 2
            acc = acc + cw_ref[j:j + 1, :] * win[lo:lo + ch, :]
        xc_ref[pl.ds(r0, ch), :] = _silu(acc)
        return carry

    lax.fori_loop(0, nc, conv_chunk, 0)

    lane = lax.broadcasted_iota(I32, (s, LANES), 1)
    pre = dtr_ref[0] + dtb_ref[...]
    dt = jnp.maximum(pre, 0.0) + jnp.log1p(jnp.exp(-jnp.abs(pre)))
    dt = jnp.where(lane < 2 * SSD_HEADS, dt, 0.0)
    dt_ref[...] = dt
    da_ref[...] = dt * (-jnp.exp(alog_ref[...]))

    row = lax.broadcasted_iota(I32, (ch, ch), 0)
    col = lax.broadcasted_iota(I32, (ch, ch), 1)
    tri = jnp.where(col <= row, 1.0, 0.0).astype(F32)
    lower = col <= row
    upper = col >= row
    lane_c = lax.broadcasted_iota(I32, (ch, LANES), 1)
    st_rows = lax.broadcasted_iota(I32, (SSD_BC, SSD_WIDTH), 0) // SSD_STATE
    st_cols = lax.broadcasted_iota(I32, (SSD_BC, SSD_WIDTH), 1) // (SSD_WIDTH // SSD_GROUPS)
    same_group = st_rows == st_cols
    sel_f = _head_selector(0)
    sel_b = _head_selector(SSD_HEADS)

    hf_ref[...] = jnp.zeros_like(hf_ref)
    hb_ref[...] = jnp.zeros_like(hb_ref)

    def wide(x):
        return jnp.concatenate([x[c * ch:(c + 1) * ch, :] for c in range(nc)], axis=1)

    da_wide = wide(da_ref[...])
    cs_wide = jnp.dot(tri, da_wide, precision=HIGHEST, preferred_element_type=F32)
    for c in range(nc):
        cs_ref[c * ch:(c + 1) * ch, :] = cs_wide[:, c * LANES:(c + 1) * LANES]
    cst_ref[...] = cs_wide.T
    ecst_ref[...] = (cs_wide - da_wide).T
    dtt_ref[...] = wide(dt_ref[...]).T
    bt_ref[...] = wide(xc_ref[:, SSD_WIDTH:SSD_WIDTH + SSD_BC]).T.astype(BF16)

    def fwd_chunk(c, carry):
        r0 = pl.multiple_of(c * ch, ch)
        da = da_ref[pl.ds(r0, ch), :]
        dtc = dt_ref[pl.ds(r0, ch), :]
        cs = cs_ref[pl.ds(r0, ch), :]
        tot = cs[ch - 1:ch, :]
        ecs = cs - da
        nh2 = 2 * SSD_HEADS
        cs_t = cst_ref[pl.ds(r0, nh2), :]
        ecs_t = ecst_ref[pl.ds(r0, nh2), :]
        dt_t = dtt_ref[pl.ds(r0, nh2), :]
        xs = xc_ref[pl.ds(r0, ch), 0:SSD_WIDTH]
        cm = xc_ref[pl.ds(r0, ch), SSD_WIDTH + SSD_BC:SSD_WIDTH + 2 * SSD_BC]
        bt = bt_ref[pl.ds(r0, ch), :]
        cmb = cm.astype(BF16)
        xsb = xs.astype(BF16)

        g_mats = []
        for g in range(SSD_GROUPS):
            cg = jnp.where(lane_c // SSD_STATE == g, cmb, jnp.zeros_like(cmb))
            g_mats.append(jnp.dot(cg, bt, preferred_element_type=F32))
        for pair in range(SSD_HEADS // 2):
            xpair = xsb[:, pair * LANES:(pair + 1) * LANES]
            ypair = jnp.zeros((ch, LANES), F32)
            for sub in range(2):
                hh = 2 * pair + sub
                g = hh // (SSD_HEADS // SSD_GROUPS)
                d_f = cs[:, hh:hh + 1] - cs_t[hh:hh + 1, :]
                d_b = ecs_t[SSD_HEADS + hh:SSD_HEADS + hh + 1, :] - ecs[:, SSD_HEADS + hh:SSD_HEADS + hh + 1]
                decay = jnp.exp(jnp.where(lower, d_f, d_b))
                wgt = (jnp.where(lower, dt_t[hh:hh + 1, :], 0.0)
                       + jnp.where(upper, dt_t[SSD_HEADS + hh:SSD_HEADS + hh + 1, :], 0.0))
                m = (g_mats[g] * decay * wgt).astype(BF16)
                xh = jnp.where(lane_c // SSD_HEADDIM == sub, xpair, jnp.zeros_like(xpair))
                ypair = ypair + jnp.dot(m, xh, preferred_element_type=F32)
            y_ref[pl.ds(r0, ch), pair * LANES:(pair + 1) * LANES] = ypair

        hf = hf_ref[...]
        factors = _expand_heads(jnp.concatenate(
            [jnp.exp(cs), dtc * jnp.exp(tot - cs), jnp.broadcast_to(jnp.exp(tot), (8, LANES))], axis=0), sel_f)
        y_off = jnp.dot(cmb, hf.astype(BF16), preferred_element_type=F32) * factors[0:ch]
        y_ref[pl.ds(r0, ch), :] = y_ref[pl.ds(r0, ch), :] + y_off
        xd = (xs * factors[ch:2 * ch]).astype(BF16)
        upd = jnp.dot(bt, xd, preferred_element_type=F32)
        hf_ref[...] = hf * factors[2 * ch:2 * ch + 1] + jnp.where(same_group, upd, 0.0)
        return carry

    lax.fori_loop(0, nc, fwd_chunk, 0, unroll=4)

    def bwd_chunk(i, carry):
        c = nc - 1 - i
        r0 = pl.multiple_of(c * ch, ch)
        da = da_ref[pl.ds(r0, ch), :]
        dtc = dt_ref[pl.ds(r0, ch), :]
        cs = cs_ref[pl.ds(r0, ch), :]
        tot = cs[ch - 1:ch, :]
        ecs = cs - da
        xs = xc_ref[pl.ds(r0, ch), 0:SSD_WIDTH]
        cm = xc_ref[pl.ds(r0, ch), SSD_WIDTH + SSD_BC:SSD_WIDTH + 2 * SSD_BC]
        bt = bt_ref[pl.ds(r0, ch), :]

        hb = hb_ref[...]
        factors = _expand_heads(jnp.concatenate(
            [jnp.exp(tot - ecs), dtc * jnp.exp(ecs), jnp.broadcast_to(jnp.exp(tot), (8, LANES))], axis=0), sel_b)
        y_off = jnp.dot(cm.astype(BF16), hb.astype(BF16), preferred_element_type=F32) * factors[0:ch]
        xd = (xs * factors[ch:2 * ch]).astype(BF16)
        upd = jnp.dot(bt, xd, preferred_element_type=F32)
        hb_ref[...] = hb * factors[2 * ch:2 * ch + 1] + jnp.where(same_group, upd, 0.0)

        y = y_ref[pl.ds(r0, ch), :] + y_off + dskip_ref[...] * xs
        y = y * _silu(z_ref[0, pl.ds(r0, ch), :])
        gw = SSD_WIDTH // SSD_GROUPS
        yn = jnp.concatenate([_rms(y[:, g * gw:(g + 1) * gw]) for g in range(SSD_GROUPS)], axis=-1)
        o_ref[0, pl.ds(r0, ch), :] = (yn * ng_ref[...]).astype(BF16)
        return carry

    lax.fori_loop(0, nc, bwd_chunk, 0, unroll=4)


def _ssd(z, xbc, dt_raw, conv_w, conv_b, dt_bias, a_log, d_skip, norm_g):
    bsz, s, nconv = xbc.shape
    tok = lambda wd: pl.BlockSpec((1, s, wd), lambda b: (b, 0, 0))
    full = lambda a: pl.BlockSpec(a.shape, lambda b: (0, 0))
    return pl.pallas_call(
        _ssd_body,
        grid=(bsz,),
        in_specs=[tok(SSD_WIDTH), tok(nconv), tok(LANES), full(conv_w), full(conv_b), full(dt_bias),
                  full(a_log), full(d_skip), full(norm_g)],
        out_specs=tok(SSD_WIDTH),
        out_shape=jax.ShapeDtypeStruct((bsz, s, SSD_WIDTH), BF16),
        scratch_shapes=[pltpu.VMEM((s + 16, nconv), F32),
                        pltpu.VMEM((s, nconv), F32),
                        pltpu.VMEM((s, LANES), F32),
                        pltpu.VMEM((s, LANES), F32),
                        pltpu.VMEM((s, LANES), F32),
                        pltpu.VMEM((s, LANES), F32),
                        pltpu.VMEM((s, LANES), F32),
                        pltpu.VMEM((s, LANES), F32),
                        pltpu.VMEM((s, SSD_BC), BF16),
                        pltpu.VMEM((s, SSD_WIDTH), F32),
                        pltpu.VMEM((SSD_BC, SSD_WIDTH), F32),
                        pltpu.VMEM((SSD_BC, SSD_WIDTH), F32)],
        compiler_params=_params("parallel"),
        name="ssd_mixer",
    )(z, xbc, dt_raw, conv_w, conv_b, dt_bias, a_log, d_skip, norm_g)


def _outproj_body(att_ref, ssd_ref, x_ref, g1_ref, sh_ref, sc_ref, ng_ref, w_ref, rw_ref,
                  x1_ref, h2_ref, aff_ref):
    na = att_ref.shape[2]
    mix = (jnp.dot(att_ref[0], w_ref[0:na, :], preferred_element_type=F32)
           + jnp.dot(ssd_ref[0], w_ref[na:, :], preferred_element_type=F32))
    x1 = x_ref[0] + g1_ref[0] * mix
    x1_ref[0] = x1
    h2 = _rms(x1) * ng_ref[...]
    h2 = h2 * (1.0 + sc_ref[0]) + sh_ref[0]
    h2_hi = h2.astype(BF16)
    h2_ref[0] = h2_hi
    h2_lo = (h2 - h2_hi.astype(F32)).astype(BF16)
    nt = lambda a, b: lax.dot_general(a, b, (((1,), (1,)), ((), ())), preferred_element_type=F32)
    logits = nt(rw_ref[0], h2_hi) + (nt(rw_ref[0], h2_lo) + nt(rw_ref[1], h2_hi))
    e = jnp.exp(logits - jnp.max(logits, axis=0, keepdims=True))
    aff_ref[0] = e / jnp.sum(e, axis=0, keepdims=True)


def _out_proj(att, ssd, x, g1, sh2, sc2, ng, w_out, rw_t, ts):
    bsz, s, d = x.shape
    tok = lambda wd: pl.BlockSpec((1, ts, wd), lambda b, t: (b, t, 0))
    vec = pl.BlockSpec((1, 1, d), lambda b, t: (b, 0, 0))
    full = lambda a: pl.BlockSpec(a.shape, lambda b, t: (0, 0))
    ne = rw_t.shape[1]
    return pl.pallas_call(
        _outproj_body,
        grid=(bsz, s // ts),
        in_specs=[tok(att.shape[2]), tok(ssd.shape[2]), tok(d), vec, vec, vec, full(ng), full(w_out),
                  pl.BlockSpec(rw_t.shape, lambda b, t: (0, 0, 0))],
        out_specs=[tok(d), tok(d), pl.BlockSpec((1, ne, ts), lambda b, t: (b, 0, t))],
        out_shape=[jax.ShapeDtypeStruct((bsz, s, d), F32),
                   jax.ShapeDtypeStruct((bsz, s, d), BF16),
                   jax.ShapeDtypeStruct((bsz, ne, s), F32)],
        compiler_params=_params("parallel", "arbitrary"),
        name="out_proj_router",
    )(att, ssd, x, g1, sh2, sc2, ng, w_out, rw_t)


def _topk_body(aff_ref, pos_ref, post_ref, ut_ref, *, cap):
    nrow, s = aff_ref.shape
    r = lax.broadcasted_iota(I32, (s, s), 0)
    c = lax.broadcasted_iota(I32, (s, s), 1)
    ut_ref[...] = jnp.where(r < c, 1.0, 0.0).astype(BF16)

    bits = pltpu.bitcast(aff_ref[...], I32)

    def count(mask):
        return jnp.sum(jnp.where(mask, 1, 0), axis=-1, keepdims=True)

    thr = jnp.zeros((nrow, 1), I32)
    for bit in range(30, -1, -1):
        cand = thr | (1 << bit)
        thr = jnp.where(count(bits >= cand) >= cap, cand, thr)
    gt = bits > thr
    eq = bits == thr
    need = cap - count(gt)
    eq_rank = jnp.dot(jnp.where(eq, 1.0, 0.0).astype(BF16), ut_ref[...], preferred_element_type=F32)
    sel = gt | (eq & (eq_rank < need.astype(F32)))
    slot = jnp.dot(jnp.where(sel, 1.0, 0.0).astype(BF16), ut_ref[...], preferred_element_type=F32)
    pos = jnp.where(sel, slot, -1.0)
    pos_ref[...] = pos.astype(I32)
    post_ref[...] = pos.T.astype(I32)


def _topk(aff_t, cap):
    bsz, ne, s = aff_t.shape
    nrow = bsz * ne
    pos, pos_t = pl.pallas_call(
        functools.partial(_topk_body, cap=cap),
        grid=(1,),
        in_specs=[pl.BlockSpec((nrow, s), lambda i: (0, 0))],
        out_specs=[pl.BlockSpec((nrow, s), lambda i: (0, 0)),
                   pl.BlockSpec((s, nrow), lambda i: (0, 0))],
        out_shape=[jax.ShapeDtypeStruct((nrow, s), I32),
                   jax.ShapeDtypeStruct((s, nrow), I32)],
        scratch_shapes=[pltpu.VMEM((s, s), BF16)],
        compiler_params=_params("arbitrary"),
        name="expert_topk",
    )(aff_t.reshape(nrow, s))
    return pos.reshape(bsz, ne, s), pos_t


def _gather_body(h2_ref, pos_ref, aff_ref, xg_ref, gc_ref, *, cap):
    ne, s = pos_ref.shape[1], pos_ref.shape[2]
    slot = lax.broadcasted_iota(I32, (cap, s), 0)
    h2 = h2_ref[0]
    for e in range(ne):
        hit = pos_ref[0, e:e + 1, :] == slot
        xg_ref[e] = jnp.dot(jnp.where(hit, 1.0, 0.0).astype(BF16), h2,
                            preferred_element_type=F32).astype(BF16)
        gate = jnp.sum(jnp.where(hit, aff_ref[0, e:e + 1, :], 0.0), axis=-1, keepdims=True)
        gc_ref[e] = jnp.broadcast_to(gate, (cap, LANES))


def _gather(h2, pos, aff_t, cap):
    bsz, s, d = h2.shape
    ne = pos.shape[1]
    return pl.pallas_call(
        functools.partial(_gather_body, cap=cap),
        grid=(bsz,),
        in_specs=[pl.BlockSpec((1, s, d), lambda b: (b, 0, 0)),
                  pl.BlockSpec((1, ne, s), lambda b: (b, 0, 0)),
                  pl.BlockSpec((1, ne, s), lambda b: (b, 0, 0))],
        out_specs=[pl.BlockSpec((ne, cap, d), lambda b: (0, b, 0)),
                   pl.BlockSpec((ne, cap, LANES), lambda b: (0, b, 0))],
        out_shape=[jax.ShapeDtypeStruct((ne, bsz * cap, d), BF16),
                   jax.ShapeDtypeStruct((ne, bsz * cap, LANES), F32)],
        compiler_params=_params("parallel"),
        name="moe_gather",
    )(h2, pos, aff_t)


def _ffn_body(xg_ref, gc_ref, wg_ref, wu_ref, wd_ref, y_ref, acc_ref):
    f = pl.program_id(1)

    @pl.when(f == 0)
    def _():
        acc_ref[...] = jnp.zeros_like(acc_ref)

    wg = wg_ref[0].astype(BF16)
    wu = wu_ref[0].astype(BF16)
    wd = wd_ref[0].astype(BF16)
    half = xg_ref.shape[1] // 2
    for r in range(2):
        rows = slice(r * half, (r + 1) * half)
        xg = xg_ref[0, rows, :]
        gate = jnp.dot(xg, wg, preferred_element_type=F32)
        up = jnp.dot(xg, wu, preferred_element_type=F32)
        hid = (_silu(gate) * up).astype(BF16)
        acc_ref[rows, :] += jnp.dot(hid, wd, preferred_element_type=F32)

    @pl.when(f == pl.num_programs(1) - 1)
    def _():
        m, d = acc_ref.shape
        gcol = gc_ref[0]
        y_ref[0] = (acc_ref[...] * jnp.concatenate([gcol] * (d // LANES), axis=-1)).astype(BF16)


def _ffn(xg, gc, w_gate, w_up, w_down, tf):
    ne, m, d = xg.shape
    ff = w_gate.shape[2]
    return pl.pallas_call(
        _ffn_body,
        grid=(ne, ff // tf),
        in_specs=[pl.BlockSpec((1, m, d), lambda e, f: (e, 0, 0)),
                  pl.BlockSpec((1, m, LANES), lambda e, f: (e, 0, 0)),
                  pl.BlockSpec((1, d, tf), lambda e, f: (e, 0, f)),
                  pl.BlockSpec((1, d, tf), lambda e, f: (e, 0, f)),
                  pl.BlockSpec((1, tf, d), lambda e, f: (e, f, 0))],
        out_specs=pl.BlockSpec((1, m, d), lambda e, f: (e, 0, 0)),
        out_shape=jax.ShapeDtypeStruct((ne, m, d), BF16),
        scratch_shapes=[pltpu.VMEM((m, d), F32)],
        compiler_params=_params("parallel", "arbitrary"),
        name="moe_ffn",
    )(xg, gc, w_gate, w_up, w_down)


def _scatter_body(post_ref, y_ref, x1_ref, g2_ref, ng_ref, o_ref, *, cap):
    ne = y_ref.shape[0]
    ts, nrow = post_ref.shape
    d = y_ref.shape[2]
    r = lax.broadcasted_iota(I32, (nrow, LANES), 0)
    c = lax.broadcasted_iota(I32, (nrow, LANES), 1)
    pick = jnp.where(r == pl.program_id(0) * ne + c, 1.0, 0.0).astype(BF16)
    post = jnp.dot(post_ref[...].astype(F32).astype(BF16), pick, preferred_element_type=F32)
    slot = lax.broadcasted_iota(I32, (ts, cap), 1).astype(F32)
    onehot = jnp.concatenate(
        [jnp.where(post[:, e:e + 1] == slot, 1.0, 0.0).astype(BF16) for e in range(ne)], axis=-1)
    moe = jnp.dot(onehot, y_ref[...].reshape(ne * cap, d), preferred_element_type=F32)
    x2 = x1_ref[0] + g2_ref[0] * moe
    o_ref[0] = _rms(x2) * ng_ref[...]


def _scatter(post, y, x1, g2, ng, cap, ts):
    bsz, s, d = x1.shape
    ne = y.shape[0]
    return pl.pallas_call(
        functools.partial(_scatter_body, cap=cap),
        grid=(bsz, s // ts),
        in_specs=[pl.BlockSpec((ts, post.shape[1]), lambda b, t: (t, 0)),
                  pl.BlockSpec((ne, cap, d), lambda b, t: (0, b, 0)),
                  pl.BlockSpec((1, ts, d), lambda b, t: (b, t, 0)),
                  pl.BlockSpec((1, 1, d), lambda b, t: (b, 0, 0)),
                  pl.BlockSpec((1, d), lambda b, t: (0, 0))],
        out_specs=pl.BlockSpec((1, ts, d), lambda b, t: (b, t, 0)),
        out_shape=jax.ShapeDtypeStruct((bsz, s, d), F32),
        compiler_params=_params("parallel", "arbitrary"),
        name="moe_scatter_final",
    )(post, y, x1, g2, ng)


def _pad_lanes(a, n=LANES):
    return jnp.pad(a, [(0, 0)] * (a.ndim - 1) + [(0, n - a.shape[-1])])


def kernel(x, c, ada_w, ada_b, norm_mix_g, norm_ffn_g, norm_final_g, w_in, lambda_q1, lambda_k1, lambda_q2, lambda_k2, attn_subln_g, rel_bias_table, conv_w, conv_b, dt_bias_f, dt_bias_b, A_log_f, A_log_b, D_skip, ssm_norm_g, w_out, router_w, w_gate, w_up, w_down):
    bsz, s, d = x.shape
    depth = ada_w.shape[0]
    ne = router_w.shape[2]
    cap = CAPACITY_FACTOR * s // ne
    att_w = ATT_HEADS * ATT_V_DIM
    nconv = conv_w.shape[3]
    splits = (att_w, att_w, att_w, SSD_WIDTH, nconv, 2 * SSD_HEADS)
    offs = [0]
    for wd in splits:
        offs.append(offs[-1] + wd)
    assert offs[-1] == w_in.shape[2]
    row = lambda a: a.reshape(1, -1)

    assert depth == 1, "the final RMSNorm is fused into the (single) layer's scatter kernel"
    for l in range(depth):
        lam_init = 0.8 - 0.6 * math.exp(-0.3 * l)
        mod = _ada_mod(c, ada_w[l], ada_b[l])
        sh1, sc1, g1, sh2, sc2, g2 = [m.reshape(bsz, 1, d) for m in jnp.split(mod, 6, axis=-1)]

        q, k, v_t, z, xbc, dt_raw = _in_proj(x, sh1, sc1, row(norm_mix_g[l]), w_in[l].astype(BF16), offs, ts=1024)

        att = _attention(q, k, v_t, rel_bias_table, row(lambda_q1[l]), row(lambda_k1[l]),
                         row(lambda_q2[l]), row(lambda_k2[l]), row(attn_subln_g[l]), lam_init, tq=2048)

        dt_bias = _pad_lanes(jnp.concatenate([dt_bias_f[l], dt_bias_b[l]]).reshape(1, -1))
        a_log = _pad_lanes(jnp.concatenate([A_log_f[l], A_log_b[l]]).reshape(1, -1))
        d_skip = jnp.repeat(D_skip[l], SSD_HEADDIM).reshape(1, -1)
        ssd = _ssd(z, xbc, dt_raw, conv_w[l].reshape(SSD_CONV, nconv), row(conv_b[l]), dt_bias, a_log,
                   d_skip, row(ssm_norm_g[l]))

        rw_t = router_w[l].T
        rw_hi = rw_t.astype(BF16)
        rw_split = jnp.stack([rw_hi, (rw_t - rw_hi.astype(F32)).astype(BF16)])
        x1, h2, aff_t = _out_proj(att, ssd, x, g1, sh2, sc2, row(norm_ffn_g[l]), w_out[l].astype(BF16),
                                  rw_split, ts=1024)
        pos, pos_t = _topk(aff_t, cap)
        xg, gc = _gather(h2, pos, aff_t, cap)
        y = _ffn(xg, gc, w_gate[l], w_up[l], w_down[l], tf=256)
        x = _scatter(pos_t, y, x1, g2, row(norm_final_g), cap, ts=1024)
    return x
```

```python
import functools
import math

import jax
import jax.numpy as jnp
from jax import lax
from jax.experimental import pallas as pl
from jax.experimental.pallas import tpu as pltpu

F32 = jnp.float32
BF16 = jnp.bfloat16
I32 = jnp.int32
HIGHEST = lax.Precision.HIGHEST

ATT_HEADS = 4
ATT_QK_DIM = 64
ATT_V_DIM = 128
N_BUCKETS = 32
SSD_HEADS = 8
SSD_HEADDIM = 64
SSD_GROUPS = 2
SSD_STATE = 64
SSD_CONV = 5
SSD_CHUNK = 128
SSD_WIDTH = SSD_HEADS * SSD_HEADDIM
SSD_BC = SSD_GROUPS * SSD_STATE
N_EXPERTS = 16
CAPACITY_FACTOR = 2
EPS = 1e-6
LANES = 128
NEG_BIG = -1e30
LOG2E = math.log2(math.e)

VMEM_LIMIT = 56 * 1024 * 1024


def _params(*semantics):
    return pltpu.CompilerParams(dimension_semantics=semantics, vmem_limit_bytes=VMEM_LIMIT)


def _silu(v):
    return v * jax.nn.sigmoid(v)


def _rms(v, eps=EPS):
    return v * lax.rsqrt(jnp.mean(v * v, axis=-1, keepdims=True) + eps)


def _split_bf16(v):
    hi = v.astype(BF16)
    return hi, (v - hi.astype(F32)).astype(BF16)


def _mod_body(c_ref, w_ref, b_ref, o_ref):
    a_hi, a_lo = _split_bf16(_silu(c_ref[...]))
    w_hi, w_lo = _split_bf16(w_ref[...])
    dot = functools.partial(jnp.dot, preferred_element_type=F32)
    o_ref[...] = dot(a_hi, w_hi) + (dot(a_lo, w_hi) + dot(a_hi, w_lo)) + b_ref[...]


def _ada_mod(c, w, b):
    bsz, d = c.shape
    n = w.shape[1]
    tn = n // 4
    return pl.pallas_call(
        _mod_body,
        grid=(n // tn,),
        in_specs=[pl.BlockSpec((bsz, d), lambda j: (0, 0)),
                  pl.BlockSpec((d, tn), lambda j: (0, j)),
                  pl.BlockSpec((1, tn), lambda j: (0, j))],
        out_specs=pl.BlockSpec((bsz, tn), lambda j: (0, j)),
        out_shape=jax.ShapeDtypeStruct((bsz, n), F32),
        compiler_params=_params("arbitrary"),
        name="ada_mod",
    )(c, w, b.reshape(1, n))


def _inproj_body(x_ref, sh_ref, sc_ref, g_ref, w_ref, q_ref, k_ref, vt_ref, z_ref, xbc_ref, dt_ref, wvt_ref,
                 *, offs, qscale):
    @pl.when((pl.program_id(0) == 0) & (pl.program_id(1) == 0))
    def _():
        wvt_ref[...] = w_ref[:, offs[2]:offs[3]].astype(F32).T.astype(BF16)

    h = _rms(x_ref[0]) * g_ref[...]
    h = (h * (1.0 + sc_ref[0]) + sh_ref[0]).astype(BF16)

    def proj(i):
        return jnp.dot(h, w_ref[:, offs[i]:offs[i + 1]], preferred_element_type=F32)

    q_ref[0] = (proj(0) * qscale).astype(BF16)
    k_ref[0] = proj(1).astype(BF16)
    vt_ref[0, 0] = lax.dot_general(wvt_ref[...], h, (((1,), (1,)), ((), ())),
                                   preferred_element_type=F32).astype(BF16)
    z_ref[0] = proj(3)
    xbc_ref[0] = proj(4)
    dt = proj(5)
    dt_ref[0] = jnp.concatenate([dt, jnp.zeros((dt.shape[0], LANES - dt.shape[1]), F32)], axis=1)


def _in_proj(x, sh, sc, g, w, offs, ts):
    bsz, s, d = x.shape
    wq, wk, wv, wz, wxbc = [offs[i + 1] - offs[i] for i in range(5)]
    tok = lambda wd: pl.BlockSpec((1, ts, wd), lambda b, t: (b, t, 0))
    vec = pl.BlockSpec((1, 1, d), lambda b, t: (b, 0, 0))
    tok_out = lambda wd, dt: (tok(wd), jax.ShapeDtypeStruct((bsz, s, wd), dt))
    outs = [tok_out(wq, BF16), tok_out(wk, BF16),
            (pl.BlockSpec((1, 1, wv, ts), lambda b, t: (b, t, 0, 0)), jax.ShapeDtypeStruct((bsz, s // ts, wv, ts), BF16)),
            tok_out(wz, F32), tok_out(wxbc, F32), tok_out(LANES, F32)]
    return pl.pallas_call(
        functools.partial(_inproj_body, offs=tuple(offs), qscale=ATT_QK_DIM ** -0.5 * LOG2E),
        grid=(bsz, s // ts),
        in_specs=[tok(d), vec, vec,
                  pl.BlockSpec((1, d), lambda b, t: (0, 0)),
                  pl.BlockSpec(w.shape, lambda b, t: (0, 0))],
        out_specs=[o[0] for o in outs],
        out_shape=[o[1] for o in outs],
        scratch_shapes=[pltpu.VMEM((wv, d), BF16)],
        compiler_params=_params("arbitrary", "arbitrary"),
        name="in_proj",
    )(x, sh, sc, g, w)


def _t5_bucket(rel):
    n = jnp.abs(rel)
    large = jnp.full(rel.shape, 8, I32)
    for t in (12, 16, 23, 32, 46, 64, 91):
        large = large + jnp.where(n >= t, 1, 0)
    return jnp.where(rel > 0, 16, 0) + jnp.where(n < 8, n, large)


def _attn_body(tbl_ref, lq1_ref, lk1_ref, lq2_ref, lk2_ref, g_ref, q_ref, k_ref, vt_ref, o_ref,
               bias_ref, acc_ref, *, lam_init, tk):
    h = pl.program_id(0)
    qi = pl.program_id(1)
    s, tq = bias_ref.shape
    tb = 2 * LANES
    ring = pl.next_power_of_2(tb + tq - 1)

    @pl.when(pl.program_id(2) == 0)
    def _():
        j = lax.broadcasted_iota(I32, (8, ring), 1)
        for kt in range(s // tb):
            bucket = _t5_bucket((kt * tb + tb - 1) - qi * tq - j)
            w = jnp.zeros((8, ring), F32)
            for bb in range(N_BUCKETS):
                w = jnp.where(bucket == bb, tbl_ref[bb, h] * LOG2E, w)
            big = jnp.broadcast_to(w[0:1], (tb, ring))
            rolled = pltpu.roll(big, ring - (tb - 1), axis=1, stride=1, stride_axis=0)
            bias_ref[kt * tb:(kt + 1) * tb, :] = rolled[:, :tq]

    lam = (jnp.exp(jnp.sum(lq1_ref[...] * lk1_ref[...], axis=-1, keepdims=True))
           - jnp.exp(jnp.sum(lq2_ref[...] * lk2_ref[...], axis=-1, keepdims=True)) + lam_init)
    q = q_ref[0]
    lane = lax.broadcasted_iota(I32, q.shape, 1)
    qms = [jnp.where((lane >= m * ATT_QK_DIM) & (lane < (m + 1) * ATT_QK_DIM), q, jnp.zeros_like(q))
           for m in range(2)]
    qw = 2 * LANES
    nqc = tq // qw
    nkb, kb = vt_ref.shape[1], vt_ref.shape[3]
    ones_rows = 16

    streams = [(m, c) for m in range(2) for c in range(nqc)]
    run_max = {st: jnp.full((1, qw), NEG_BIG, F32) for st in streams}
    nt = s // tk
    items = [(t, st) for t in range(nt) for st in streams]
    pending = {}
    depth = min(7, len(streams) - 1)

    def logits_part(item):
        t, (m, c) = item
        rows, cols = slice(t * tk, (t + 1) * tk), slice(c * qw, (c + 1) * qw)
        logit = bias_ref[rows, cols] + lax.dot_general(k_ref[0, rows, :], qms[m][cols, :], (((1,), (1,)), ((), ())),
                                                       preferred_element_type=F32)
        mx = logit[0:8, :]
        for r in range(1, tk // 8):
            mx = jnp.maximum(mx, logit[r * 8:(r + 1) * 8, :])
        pending[item] = (logit, jnp.max(mx, axis=0, keepdims=True))

    def pv_part(item):
        t, st = item
        m, c = st
        cols = slice(c * qw, (c + 1) * qw)
        logit, tile_max = pending.pop(item)
        m_old = run_max[st]
        m_new = jnp.maximum(m_old, tile_max)
        run_max[st] = m_new
        probs = jnp.exp2(logit - m_new).astype(BF16)
        blk, off = (t * tk) // kb, (t * tk) % kb
        vt_ext = jnp.concatenate([vt_ref[0, blk][:, off:off + tk], jnp.ones((ones_rows, tk), BF16)], axis=0)
        part = jnp.dot(vt_ext, probs, preferred_element_type=F32)
        if t == 0:
            acc_ref[m, :, cols] = part
        else:
            acc_ref[m, :, cols] = acc_ref[m, :, cols] * jnp.exp2(m_old - m_new) + part

    for i in range(len(items) + depth):
        if i < len(items):
            logits_part(items[i])
        if i >= depth:
            pv_part(items[i - depth])
    outs = [acc_ref[m, :ATT_V_DIM, :] / acc_ref[m, ATT_V_DIM:ATT_V_DIM + 1, :] for m in range(2)]
    att = outs[0] - lam * outs[1]
    att = att * lax.rsqrt(jnp.mean(att * att, axis=0, keepdims=True) + EPS)
    o_ref[0] = (att * (g_ref[...] * (1.0 - lam_init))).T.astype(BF16)


def _attention(q, k, v_t, tbl, lq1, lk1, lq2, lk2, subln_g, lam_init, tq):
    bsz, s, _ = q.shape
    nkb, kb = v_t.shape[1], v_t.shape[3]
    small = lambda n: pl.BlockSpec((1, n), lambda h, i, b: (0, 0))
    return pl.pallas_call(
        functools.partial(_attn_body, lam_init=lam_init, tk=2 * LANES),
        grid=(ATT_HEADS, s // tq, bsz),
        in_specs=[pl.BlockSpec(memory_space=pltpu.SMEM),
                  small(ATT_QK_DIM), small(ATT_QK_DIM), small(ATT_QK_DIM), small(ATT_QK_DIM),
                  pl.BlockSpec((ATT_V_DIM, 1), lambda h, i, b: (0, 0)),
                  pl.BlockSpec((1, tq, ATT_V_DIM), lambda h, i, b: (b, i, h)),
                  pl.BlockSpec((1, s, ATT_V_DIM), lambda h, i, b: (b, 0, h)),
                  pl.BlockSpec((1, nkb, ATT_V_DIM, kb), lambda h, i, b: (b, 0, h, 0))],
        out_specs=pl.BlockSpec((1, tq, ATT_V_DIM), lambda h, i, b: (b, i, h)),
        out_shape=jax.ShapeDtypeStruct((bsz, s, ATT_HEADS * ATT_V_DIM), BF16),
        scratch_shapes=[pltpu.VMEM((s, tq), F32),
                        pltpu.VMEM((2, ATT_V_DIM + 16, tq), F32)],
        compiler_params=_params("parallel", "parallel", "arbitrary"),
        name="diff_attention",
    )(tbl, lq1, lk1, lq2, lk2, subln_g.reshape(ATT_V_DIM, 1), q, k, v_t)


def _head_selector(off):
    r = lax.broadcasted_iota(I32, (LANES, SSD_WIDTH), 0)
    c = lax.broadcasted_iota(I32, (LANES, SSD_WIDTH), 1)
    return jnp.where(r - off == c // SSD_HEADDIM, 1.0, 0.0).astype(BF16)


def _expand_heads(cols, selector):
    hi = cols.astype(BF16)
    lo = (cols - hi.astype(F32)).astype(BF16)
    return (jnp.dot(hi, selector, preferred_element_type=F32)
            + jnp.dot(lo, selector, preferred_element_type=F32))


def _ssd_body(z_ref, xbc_ref, dtr_ref, cw_ref, cb_ref, dtb_ref, alog_ref, dskip_ref, ng_ref, o_ref,
              xp_ref, xc_ref, dt_ref, da_ref, cs_ref, cst_ref, ecst_ref, dtt_ref, bt_ref, y_ref, hf_ref, hb_ref):
    s = xbc_ref.shape[1]
    ch = SSD_CHUNK
    assert ch == LANES
    nc = s // ch
    pad = 8
    nconv = xbc_ref.shape[2]

    xp_ref[0:pad, :] = jnp.zeros((pad, nconv), F32)
    xp_ref[pad + s:pad + s + pad, :] = jnp.zeros((pad, nconv), F32)
    xp_ref[pad:pad + s, :] = xbc_ref[0]

    def conv_chunk(c, carry):
        r0 = pl.multiple_of(c * ch, ch)
        nwin = ch + 2 * pad
        win = xp_ref[pl.ds(r0, nwin), :]
        acc = jnp.broadcast_to(cb_ref[...], (ch, nconv))
        for j in range(SSD_CONV):
            shift = (SSD_CONV // 2 - j) % nwin
            tap = win if shift == 0 else pltpu.roll(win, shift, axis=0)
            acc = acc + cw_ref[j:j + 1, :] * tap[pad:pad + ch, :]
        xc_ref[pl.ds(r0, ch), :] = _silu(acc)
        return carry

    lax.fori_loop(0, nc, conv_chunk, 0)

    lane = lax.broadcasted_iota(I32, (s, LANES), 1)
    pre = dtr_ref[0] + dtb_ref[...]
    dt = jnp.maximum(pre, 0.0) + jnp.log1p(jnp.exp(-jnp.abs(pre)))
    dt = jnp.where(lane < 2 * SSD_HEADS, dt, 0.0)
    dt_ref[...] = dt
    da_ref[...] = dt * (-jnp.exp(alog_ref[...]))

    row = lax.broadcasted_iota(I32, (ch, ch), 0)
    col = lax.broadcasted_iota(I32, (ch, ch), 1)
    tri = jnp.where(col <= row, 1.0, 0.0).astype(F32)
    lower = col <= row
    upper = col >= row
    lane_c = lax.broadcasted_iota(I32, (ch, LANES), 1)
    st_rows = lax.broadcasted_iota(I32, (SSD_BC, SSD_WIDTH), 0) // SSD_STATE
    st_cols = lax.broadcasted_iota(I32, (SSD_BC, SSD_WIDTH), 1) // (SSD_WIDTH // SSD_GROUPS)
    same_group = st_rows == st_cols
    sel_f = _head_selector(0)
    sel_b = _head_selector(SSD_HEADS)

    hf_ref[...] = jnp.zeros_like(hf_ref)
    hb_ref[...] = jnp.zeros_like(hb_ref)

    def wide(x):
        return jnp.concatenate([x[c * ch:(c + 1) * ch, :] for c in range(nc)], axis=1)

    da_wide = wide(da_ref[...])
    cs_wide = jnp.dot(tri, da_wide, precision=HIGHEST, preferred_element_type=F32)
    for c in range(nc):
        cs_ref[c * ch:(c + 1) * ch, :] = cs_wide[:, c * LANES:(c + 1) * LANES]
    cst_ref[...] = cs_wide.T
    ecst_ref[...] = (cs_wide - da_wide).T
    dtt_ref[...] = wide(dt_ref[...]).T
    bt_ref[...] = wide(xc_ref[:, SSD_WIDTH:SSD_WIDTH + SSD_BC]).T.astype(BF16)

    def fwd_chunk(c, carry):
        r0 = pl.multiple_of(c * ch, ch)
        da = da_ref[pl.ds(r0, ch), :]
        dtc = dt_ref[pl.ds(r0, ch), :]
        cs = cs_ref[pl.ds(r0, ch), :]
        tot = cs[ch - 1:ch, :]
        ecs = cs - da
        nh2 = 2 * SSD_HEADS
        cs_t = cst_ref[pl.ds(r0, nh2), :]
        ecs_t = ecst_ref[pl.ds(r0, nh2), :]
        dt_t = dtt_ref[pl.ds(r0, nh2), :]
        xs = xc_ref[pl.ds(r0, ch), 0:SSD_WIDTH]
        cm = xc_ref[pl.ds(r0, ch), SSD_WIDTH + SSD_BC:SSD_WIDTH + 2 * SSD_BC]
        bt = bt_ref[pl.ds(r0, ch), :]
        cmb = cm.astype(BF16)
        xsb = xs.astype(BF16)

        g_mats = []
        for g in range(SSD_GROUPS):
            cg = jnp.where(lane_c // SSD_STATE == g, cmb, jnp.zeros_like(cmb))
            g_mats.append(jnp.dot(cg, bt, preferred_element_type=F32))
        for pair in range(SSD_HEADS // 2):
            xpair = xsb[:, pair * LANES:(pair + 1) * LANES]
            ypair = jnp.zeros((ch, LANES), F32)
            for sub in range(2):
                hh = 2 * pair + sub
                g = hh // (SSD_HEADS // SSD_GROUPS)
                d_f = cs[:, hh:hh + 1] - cs_t[hh:hh + 1, :]
                d_b = ecs_t[SSD_HEADS + hh:SSD_HEADS + hh + 1, :] - ecs[:, SSD_HEADS + hh:SSD_HEADS + hh + 1]
                decay = jnp.exp(jnp.where(lower, d_f, d_b))
                wgt = (jnp.where(lower, dt_t[hh:hh + 1, :], 0.0)
                       + jnp.where(upper, dt_t[SSD_HEADS + hh:SSD_HEADS + hh + 1, :], 0.0))
                m = (g_mats[g] * decay * wgt).astype(BF16)
                xh = jnp.where(lane_c // SSD_HEADDIM == sub, xpair, jnp.zeros_like(xpair))
                ypair = ypair + jnp.dot(m, xh, preferred_element_type=F32)
            y_ref[pl.ds(r0, ch), pair * LANES:(pair + 1) * LANES] = ypair

        hf = hf_ref[...]
        factors = _expand_heads(jnp.concatenate(
            [jnp.exp(cs), dtc * jnp.exp(tot - cs), jnp.broadcast_to(jnp.exp(tot), (8, LANES))], axis=0), sel_f)
        y_off = jnp.dot(cmb, hf.astype(BF16), preferred_element_type=F32) * factors[0:ch]
        y_ref[pl.ds(r0, ch), :] = y_ref[pl.ds(r0, ch), :] + y_off
        xd = (xs * factors[ch:2 * ch]).astype(BF16)
        upd = jnp.dot(bt, xd, preferred_element_type=F32)
        hf_ref[...] = hf * factors[2 * ch:2 * ch + 1] + jnp.where(same_group, upd, 0.0)
        return carry

    lax.fori_loop(0, nc, fwd_chunk, 0, unroll=4)

    def bwd_chunk(i, carry):
        c = nc - 1 - i
        r0 = pl.multiple_of(c * ch, ch)
        da = da_ref[pl.ds(r0, ch), :]
        dtc = dt_ref[pl.ds(r0, ch), :]
        cs = cs_ref[pl.ds(r0, ch), :]
        tot = cs[ch - 1:ch, :]
        ecs = cs - da
        xs = xc_ref[pl.ds(r0, ch), 0:SSD_WIDTH]
        cm = xc_ref[pl.ds(r0, ch), SSD_WIDTH + SSD_BC:SSD_WIDTH + 2 * SSD_BC]
        bt = bt_ref[pl.ds(r0, ch), :]

        hb = hb_ref[...]
        factors = _expand_heads(jnp.concatenate(
            [jnp.exp(tot - ecs), dtc * jnp.exp(ecs), jnp.broadcast_to(jnp.exp(tot), (8, LANES))], axis=0), sel_b)
        y_off = jnp.dot(cm.astype(BF16), hb.astype(BF16), preferred_element_type=F32) * factors[0:ch]
        xd = (xs * factors[ch:2 * ch]).astype(BF16)
        upd = jnp.dot(bt, xd, preferred_element_type=F32)
        hb_ref[...] = hb * factors[2 * ch:2 * ch + 1] + jnp.where(same_group, upd, 0.0)

        y = y_ref[pl.ds(r0, ch), :] + y_off + dskip_ref[...] * xs
        y = y * _silu(z_ref[0, pl.ds(r0, ch), :])
        gw = SSD_WIDTH // SSD_GROUPS
        yn = jnp.concatenate([_rms(y[:, g * gw:(g + 1) * gw]) for g in range(SSD_GROUPS)], axis=-1)
        o_ref[0, pl.ds(r0, ch), :] = (yn * ng_ref[...]).astype(BF16)
        return carry

    lax.fori_loop(0, nc, bwd_chunk, 0, unroll=4)


def _ssd(z, xbc, dt_raw, conv_w, conv_b, dt_bias, a_log, d_skip, norm_g):
    bsz, s, nconv = xbc.shape
    tok = lambda wd: pl.BlockSpec((1, s, wd), lambda b: (b, 0, 0))
    full = lambda a: pl.BlockSpec(a.shape, lambda b: (0, 0))
    return pl.pallas_call(
        _ssd_body,
        grid=(bsz,),
        in_specs=[tok(SSD_WIDTH), tok(nconv), tok(LANES), full(conv_w), full(conv_b), full(dt_bias),
                  full(a_log), full(d_skip), full(norm_g)],
        out_specs=tok(SSD_WIDTH),
        out_shape=jax.ShapeDtypeStruct((bsz, s, SSD_WIDTH), BF16),
        scratch_shapes=[pltpu.VMEM((s + 16, nconv), F32),
                        pltpu.VMEM((s, nconv), F32),
                        pltpu.VMEM((s, LANES), F32),
                        pltpu.VMEM((s, LANES), F32),
                        pltpu.VMEM((s, LANES), F32),
                        pltpu.VMEM((s, LANES), F32),
                        pltpu.VMEM((s, LANES), F32),
                        pltpu.VMEM((s, LANES), F32),
                        pltpu.VMEM((s, SSD_BC), BF16),
                        pltpu.VMEM((s, SSD_WIDTH), F32),
                        pltpu.VMEM((SSD_BC, SSD_WIDTH), F32),
                        pltpu.VMEM((SSD_BC, SSD_WIDTH), F32)],
        compiler_params=_params("parallel"),
        name="ssd_mixer",
    )(z, xbc, dt_raw, conv_w, conv_b, dt_bias, a_log, d_skip, norm_g)


def _outproj_body(att_ref, ssd_ref, x_ref, g1_ref, sh_ref, sc_ref, ng_ref, w_ref, rw_ref,
                  x1_ref, h2_ref, aff_ref):
    na = att_ref.shape[2]
    mix = (jnp.dot(att_ref[0], w_ref[0:na, :], preferred_element_type=F32)
           + jnp.dot(ssd_ref[0], w_ref[na:, :], preferred_element_type=F32))
    x1 = x_ref[0] + g1_ref[0] * mix
    x1_ref[0] = x1
    h2 = _rms(x1) * ng_ref[...]
    h2 = h2 * (1.0 + sc_ref[0]) + sh_ref[0]
    h2_hi = h2.astype(BF16)
    h2_ref[0] = h2_hi
    h2_lo = (h2 - h2_hi.astype(F32)).astype(BF16)
    nt = lambda a, b: lax.dot_general(a, b, (((1,), (1,)), ((), ())), preferred_element_type=F32)
    logits = nt(rw_ref[0], h2_hi) + (nt(rw_ref[0], h2_lo) + nt(rw_ref[1], h2_hi))
    e = jnp.exp(logits - jnp.max(logits, axis=0, keepdims=True))
    aff_ref[0] = e / jnp.sum(e, axis=0, keepdims=True)


def _out_proj(att, ssd, x, g1, sh2, sc2, ng, w_out, rw_t, ts):
    bsz, s, d = x.shape
    tok = lambda wd: pl.BlockSpec((1, ts, wd), lambda b, t: (b, t, 0))
    vec = pl.BlockSpec((1, 1, d), lambda b, t: (b, 0, 0))
    full = lambda a: pl.BlockSpec(a.shape, lambda b, t: (0, 0))
    ne = rw_t.shape[1]
    return pl.pallas_call(
        _outproj_body,
        grid=(bsz, s // ts),
        in_specs=[tok(att.shape[2]), tok(ssd.shape[2]), tok(d), vec, vec, vec, full(ng), full(w_out),
                  pl.BlockSpec(rw_t.shape, lambda b, t: (0, 0, 0))],
        out_specs=[tok(d), tok(d), pl.BlockSpec((1, ne, ts), lambda b, t: (b, 0, t))],
        out_shape=[jax.ShapeDtypeStruct((bsz, s, d), F32),
                   jax.ShapeDtypeStruct((bsz, s, d), BF16),
                   jax.ShapeDtypeStruct((bsz, ne, s), F32)],
        compiler_params=_params("parallel", "arbitrary"),
        name="out_proj_router",
    )(att, ssd, x, g1, sh2, sc2, ng, w_out, rw_t)


def _topk_body(aff_ref, pos_ref, post_ref, ut_ref, *, cap):
    nrow, s = aff_ref.shape
    r = lax.broadcasted_iota(I32, (s, s), 0)
    c = lax.broadcasted_iota(I32, (s, s), 1)
    ut_ref[...] = jnp.where(r < c, 1.0, 0.0).astype(BF16)

    bits = pltpu.bitcast(aff_ref[...], I32)

    def count(mask):
        return jnp.sum(jnp.where(mask, 1, 0), axis=-1, keepdims=True)

    thr = jnp.zeros((nrow, 1), I32)
    for bit in range(30, -1, -1):
        cand = thr | (1 << bit)
        thr = jnp.where(count(bits >= cand) >= cap, cand, thr)
    gt = bits > thr
    eq = bits == thr
    need = cap - count(gt)
    eq_rank = jnp.dot(jnp.where(eq, 1.0, 0.0).astype(BF16), ut_ref[...], preferred_element_type=F32)
    sel = gt | (eq & (eq_rank < need.astype(F32)))
    slot = jnp.dot(jnp.where(sel, 1.0, 0.0).astype(BF16), ut_ref[...], preferred_element_type=F32)
    pos = jnp.where(sel, slot, -1.0)
    pos_ref[...] = pos.astype(I32)
    post_ref[...] = pos.T.astype(I32)


def _topk(aff_t, cap):
    bsz, ne, s = aff_t.shape
    nrow = bsz * ne
    pos, pos_t = pl.pallas_call(
        functools.partial(_topk_body, cap=cap),
        grid=(1,),
        in_specs=[pl.BlockSpec((nrow, s), lambda i: (0, 0))],
        out_specs=[pl.BlockSpec((nrow, s), lambda i: (0, 0)),
                   pl.BlockSpec((s, nrow), lambda i: (0, 0))],
        out_shape=[jax.ShapeDtypeStruct((nrow, s), I32),
                   jax.ShapeDtypeStruct((s, nrow), I32)],
        scratch_shapes=[pltpu.VMEM((s, s), BF16)],
        compiler_params=_params("arbitrary"),
        name="expert_topk",
    )(aff_t.reshape(nrow, s))
    return pos.reshape(bsz, ne, s), pos_t


def _gather_body(h2_ref, pos_ref, aff_ref, xg_ref, gc_ref, *, cap):
    ne, s = pos_ref.shape[1], pos_ref.shape[2]
    slot = lax.broadcasted_iota(I32, (cap, s), 0)
    h2 = h2_ref[0]
    for e in range(ne):
        hit = pos_ref[0, e:e + 1, :] == slot
        xg_ref[e] = jnp.dot(jnp.where(hit, 1.0, 0.0).astype(BF16), h2,
                            preferred_element_type=F32).astype(BF16)
        gate = jnp.sum(jnp.where(hit, aff_ref[0, e:e + 1, :], 0.0), axis=-1, keepdims=True)
        gc_ref[e] = jnp.broadcast_to(gate, (cap, LANES))


def _gather(h2, pos, aff_t, cap):
    bsz, s, d = h2.shape
    ne = pos.shape[1]
    return pl.pallas_call(
        functools.partial(_gather_body, cap=cap),
        grid=(bsz,),
        in_specs=[pl.BlockSpec((1, s, d), lambda b: (b, 0, 0)),
                  pl.BlockSpec((1, ne, s), lambda b: (b, 0, 0)),
                  pl.BlockSpec((1, ne, s), lambda b: (b, 0, 0))],
        out_specs=[pl.BlockSpec((ne, cap, d), lambda b: (0, b, 0)),
                   pl.BlockSpec((ne, cap, LANES), lambda b: (0, b, 0))],
        out_shape=[jax.ShapeDtypeStruct((ne, bsz * cap, d), BF16),
                   jax.ShapeDtypeStruct((ne, bsz * cap, LANES), F32)],
        compiler_params=_params("parallel"),
        name="moe_gather",
    )(h2, pos, aff_t)


def _ffn_body(xg_ref, gc_ref, wg_ref, wu_ref, wd_ref, y_ref, acc_ref):
    f = pl.program_id(1)

    @pl.when(f == 0)
    def _():
        acc_ref[...] = jnp.zeros_like(acc_ref)

    wg = wg_ref[0].astype(BF16)
    wu = wu_ref[0].astype(BF16)
    wd = wd_ref[0].astype(BF16)
    half = xg_ref.shape[1] // 2
    for r in range(2):
        rows = slice(r * half, (r + 1) * half)
        xg = xg_ref[0, rows, :]
        gate = jnp.dot(xg, wg, preferred_element_type=F32)
        up = jnp.dot(xg, wu, preferred_element_type=F32)
        hid = (_silu(gate) * up).astype(BF16)
        acc_ref[rows, :] += jnp.dot(hid, wd, preferred_element_type=F32)

    @pl.when(f == pl.num_programs(1) - 1)
    def _():
        m, d = acc_ref.shape
        gcol = gc_ref[0]
        y_ref[0] = (acc_ref[...] * jnp.concatenate([gcol] * (d // LANES), axis=-1)).astype(BF16)


def _ffn(xg, gc, w_gate, w_up, w_down, tf):
    ne, m, d = xg.shape
    ff = w_gate.shape[2]
    return pl.pallas_call(
        _ffn_body,
        grid=(ne, ff // tf),
        in_specs=[pl.BlockSpec((1, m, d), lambda e, f: (e, 0, 0)),
                  pl.BlockSpec((1, m, LANES), lambda e, f: (e, 0, 0)),
                  pl.BlockSpec((1, d, tf), lambda e, f: (e, 0, f)),
                  pl.BlockSpec((1, d, tf), lambda e, f: (e, 0, f)),
                  pl.BlockSpec((1, tf, d), lambda e, f: (e, f, 0))],
        out_specs=pl.BlockSpec((1, m, d), lambda e, f: (e, 0, 0)),
        out_shape=jax.ShapeDtypeStruct((ne, m, d), BF16),
        scratch_shapes=[pltpu.VMEM((m, d), F32)],
        compiler_params=_params("parallel", "arbitrary"),
        name="moe_ffn",
    )(xg, gc, w_gate, w_up, w_down)


def _scatter_body(post_ref, y_ref, x1_ref, g2_ref, ng_ref, o_ref, *, cap):
    ne = y_ref.shape[0]
    ts, nrow = post_ref.shape
    d = y_ref.shape[2]
    r = lax.broadcasted_iota(I32, (nrow, LANES), 0)
    c = lax.broadcasted_iota(I32, (nrow, LANES), 1)
    pick = jnp.where(r == pl.program_id(0) * ne + c, 1.0, 0.0).astype(BF16)
    post = jnp.dot(post_ref[...].astype(F32).astype(BF16), pick, preferred_element_type=F32)
    slot = lax.broadcasted_iota(I32, (ts, cap), 1).astype(F32)
    onehot = jnp.concatenate(
        [jnp.where(post[:, e:e + 1] == slot, 1.0, 0.0).astype(BF16) for e in range(ne)], axis=-1)
    moe = jnp.dot(onehot, y_ref[...].reshape(ne * cap, d), preferred_element_type=F32)
    x2 = x1_ref[0] + g2_ref[0] * moe
    o_ref[0] = _rms(x2) * ng_ref[...]


def _scatter(post, y, x1, g2, ng, cap, ts):
    bsz, s, d = x1.shape
    ne = y.shape[0]
    return pl.pallas_call(
        functools.partial(_scatter_body, cap=cap),
        grid=(bsz, s // ts),
        in_specs=[pl.BlockSpec((ts, post.shape[1]), lambda b, t: (t, 0)),
                  pl.BlockSpec((ne, cap, d), lambda b, t: (0, b, 0)),
                  pl.BlockSpec((1, ts, d), lambda b, t: (b, t, 0)),
                  pl.BlockSpec((1, 1, d), lambda b, t: (b, 0, 0)),
                  pl.BlockSpec((1, d), lambda b, t: (0, 0))],
        out_specs=pl.BlockSpec((1, ts, d), lambda b, t: (b, t, 0)),
        out_shape=jax.ShapeDtypeStruct((bsz, s, d), F32),
        compiler_params=_params("parallel", "arbitrary"),
        name="moe_scatter_final",
    )(post, y, x1, g2, ng)


def _pad_lanes(a, n=LANES):
    return jnp.pad(a, [(0, 0)] * (a.ndim - 1) + [(0, n - a.shape[-1])])


def kernel(x, c, ada_w, ada_b, norm_mix_g, norm_ffn_g, norm_final_g, w_in, lambda_q1, lambda_k1, lambda_q2, lambda_k2, attn_subln_g, rel_bias_table, conv_w, conv_b, dt_bias_f, dt_bias_b, A_log_f, A_log_b, D_skip, ssm_norm_g, w_out, router_w, w_gate, w_up, w_down):
    bsz, s, d = x.shape
    depth = ada_w.shape[0]
    ne = router_w.shape[2]
    cap = CAPACITY_FACTOR * s // ne
    att_w = ATT_HEADS * ATT_V_DIM
    nconv = conv_w.shape[3]
    splits = (att_w, att_w, att_w, SSD_WIDTH, nconv, 2 * SSD_HEADS)
    offs = [0]
    for wd in splits:
        offs.append(offs[-1] + wd)
    assert offs[-1] == w_in.shape[2]
    row = lambda a: a.reshape(1, -1)

    assert depth == 1, "the final RMSNorm is fused into the (single) layer's scatter kernel"
    for l in range(depth):
        lam_init = 0.8 - 0.6 * math.exp(-0.3 * l)
        mod = _ada_mod(c, ada_w[l], ada_b[l])
        sh1, sc1, g1, sh2, sc2, g2 = [m.reshape(bsz, 1, d) for m in jnp.split(mod, 6, axis=-1)]

        q, k, v_t, z, xbc, dt_raw = _in_proj(x, sh1, sc1, row(norm_mix_g[l]), w_in[l].astype(BF16), offs, ts=1024)

        att = _attention(q, k, v_t, rel_bias_table, row(lambda_q1[l]), row(lambda_k1[l]),
                         row(lambda_q2[l]), row(lambda_k2[l]), row(attn_subln_g[l]), lam_init, tq=2048)

        dt_bias = _pad_lanes(jnp.concatenate([dt_bias_f[l], dt_bias_b[l]]).reshape(1, -1))
        a_log = _pad_lanes(jnp.concatenate([A_log_f[l], A_log_b[l]]).reshape(1, -1))
        d_skip = jnp.repeat(D_skip[l], SSD_HEADDIM).reshape(1, -1)
        ssd = _ssd(z, xbc, dt_raw, conv_w[l].reshape(SSD_CONV, nconv), row(conv_b[l]), dt_bias, a_log,
                   d_skip, row(ssm_norm_g[l]))

        rw_t = router_w[l].T
        rw_hi = rw_t.astype(BF16)
        rw_split = jnp.stack([rw_hi, (rw_t - rw_hi.astype(F32)).astype(BF16)])
        x1, h2, aff_t = _out_proj(att, ssd, x, g1, sh2, sc2, row(norm_ffn_g[l]), w_out[l].astype(BF16),
                                  rw_split, ts=1024)
        pos, pos_t = _topk(aff_t, cap)
        xg, gc = _gather(h2, pos, aff_t, cap)
        y = _ffn(xg, gc, w_gate[l], w_up[l], w_down[l], tf=256)
        x = _scatter(pos_t, y, x1, g2, row(norm_final_g), cap, ts=1024)
    return x
```

```python
import functools
import math

import jax
import jax.numpy as jnp
from jax import lax
from jax.experimental import pallas as pl
from jax.experimental.pallas import tpu as pltpu

F32 = jnp.float32
BF16 = jnp.bfloat16
I32 = jnp.int32
HIGHEST = lax.Precision.HIGHEST

ATT_HEADS = 4
ATT_QK_DIM = 64
ATT_V_DIM = 128
N_BUCKETS = 32
SSD_HEADS = 8
SSD_HEADDIM = 64
SSD_GROUPS = 2
SSD_STATE = 64
SSD_CONV = 5
SSD_CHUNK = 128
SSD_WIDTH = SSD_HEADS * SSD_HEADDIM
SSD_BC = SSD_GROUPS * SSD_STATE
N_EXPERTS = 16
CAPACITY_FACTOR = 2
EPS = 1e-6
LANES = 128
NEG_BIG = -1e30
LOG2E = math.log2(math.e)

VMEM_LIMIT = 56 * 1024 * 1024


def _params(*semantics):
    return pltpu.CompilerParams(dimension_semantics=semantics, vmem_limit_bytes=VMEM_LIMIT)


def _silu(v):
    return v * jax.nn.sigmoid(v)


def _rms(v, eps=EPS):
    return v * lax.rsqrt(jnp.mean(v * v, axis=-1, keepdims=True) + eps)


def _split_bf16(v):
    hi = v.astype(BF16)
    return hi, (v - hi.astype(F32)).astype(BF16)


def _mod_body(c_ref, w_ref, b_ref, o_ref):
    a_hi, a_lo = _split_bf16(_silu(c_ref[...]))
    w_hi, w_lo = _split_bf16(w_ref[...])
    dot = functools.partial(jnp.dot, preferred_element_type=F32)
    o_ref[...] = dot(a_hi, w_hi) + (dot(a_lo, w_hi) + dot(a_hi, w_lo)) + b_ref[...]


def _ada_mod(c, w, b):
    bsz, d = c.shape
    n = w.shape[1]
    tn = n // 4
    return pl.pallas_call(
        _mod_body,
        grid=(n // tn,),
        in_specs=[pl.BlockSpec((bsz, d), lambda j: (0, 0)),
                  pl.BlockSpec((d, tn), lambda j: (0, j)),
                  pl.BlockSpec((1, tn), lambda j: (0, j))],
        out_specs=pl.BlockSpec((bsz, tn), lambda j: (0, j)),
        out_shape=jax.ShapeDtypeStruct((bsz, n), F32),
        compiler_params=_params("arbitrary"),
        name="ada_mod",
    )(c, w, b.reshape(1, n))


def _inproj_body(x_ref, sh_ref, sc_ref, g_ref, w_ref, q_ref, k_ref, vt_ref, z_ref, xbc_ref, dt_ref, wvt_ref,
                 *, offs, qscale):
    @pl.when((pl.program_id(0) == 0) & (pl.program_id(1) == 0))
    def _():
        wvt_ref[...] = w_ref[:, offs[2]:offs[3]].astype(F32).T.astype(BF16)

    h = _rms(x_ref[0]) * g_ref[...]
    h = (h * (1.0 + sc_ref[0]) + sh_ref[0]).astype(BF16)

    def proj(i):
        return jnp.dot(h, w_ref[:, offs[i]:offs[i + 1]], preferred_element_type=F32)

    q_ref[0] = (proj(0) * qscale).astype(BF16)
    k_ref[0] = proj(1).astype(BF16)
    vt_ref[0, 0] = lax.dot_general(wvt_ref[...], h, (((1,), (1,)), ((), ())),
                                   preferred_element_type=F32).astype(BF16)
    z_ref[0] = proj(3)
    xbc_ref[0] = proj(4)
    dt = proj(5)
    dt_ref[0] = jnp.concatenate([dt, jnp.zeros((dt.shape[0], LANES - dt.shape[1]), F32)], axis=1)


def _in_proj(x, sh, sc, g, w, offs, ts):
    bsz, s, d = x.shape
    wq, wk, wv, wz, wxbc = [offs[i + 1] - offs[i] for i in range(5)]
    tok = lambda wd: pl.BlockSpec((1, ts, wd), lambda b, t: (b, t, 0))
    vec = pl.BlockSpec((1, 1, d), lambda b, t: (b, 0, 0))
    tok_out = lambda wd, dt: (tok(wd), jax.ShapeDtypeStruct((bsz, s, wd), dt))
    outs = [tok_out(wq, BF16), tok_out(wk, BF16),
            (pl.BlockSpec((1, 1, wv, ts), lambda b, t: (b, t, 0, 0)), jax.ShapeDtypeStruct((bsz, s // ts, wv, ts), BF16)),
            tok_out(wz, F32), tok_out(wxbc, F32), tok_out(LANES, F32)]
    return pl.pallas_call(
        functools.partial(_inproj_body, offs=tuple(offs), qscale=ATT_QK_DIM ** -0.5 * LOG2E),
        grid=(bsz, s // ts),
        in_specs=[tok(d), vec, vec,
                  pl.BlockSpec((1, d), lambda b, t: (0, 0)),
                  pl.BlockSpec(w.shape, lambda b, t: (0, 0))],
        out_specs=[o[0] for o in outs],
        out_shape=[o[1] for o in outs],
        scratch_shapes=[pltpu.VMEM((wv, d), BF16)],
        compiler_params=_params("arbitrary", "arbitrary"),
        name="in_proj",
    )(x, sh, sc, g, w)


def _t5_bucket(rel):
    n = jnp.abs(rel)
    large = jnp.full(rel.shape, 8, I32)
    for t in (12, 16, 23, 32, 46, 64, 91):
        large = large + jnp.where(n >= t, 1, 0)
    return jnp.where(rel > 0, 16, 0) + jnp.where(n < 8, n, large)


def _attn_body(tbl_ref, lq1_ref, lk1_ref, lq2_ref, lk2_ref, g_ref, q_ref, k_ref, vt_ref, o_ref,
               bias_ref, acc_ref, *, lam_init, tk):
    h = pl.program_id(0)
    qi = pl.program_id(1)
    s, tq = bias_ref.shape
    tb = 2 * LANES
    ring = pl.next_power_of_2(tb + tq - 1)

    @pl.when(pl.program_id(2) == 0)
    def _():
        j = lax.broadcasted_iota(I32, (8, ring), 1)
        for kt in range(s // tb):
            bucket = _t5_bucket((kt * tb + tb - 1) - qi * tq - j)
            w = jnp.zeros((8, ring), F32)
            for bb in range(N_BUCKETS):
                w = jnp.where(bucket == bb, tbl_ref[bb, h] * LOG2E, w)
            big = jnp.broadcast_to(w[0:1], (tb, ring))
            rolled = pltpu.roll(big, ring - (tb - 1), axis=1, stride=1, stride_axis=0)
            bias_ref[kt * tb:(kt + 1) * tb, :] = rolled[:, :tq]

    lam = (jnp.exp(jnp.sum(lq1_ref[...] * lk1_ref[...], axis=-1, keepdims=True))
           - jnp.exp(jnp.sum(lq2_ref[...] * lk2_ref[...], axis=-1, keepdims=True)) + lam_init)
    q = q_ref[0]
    lane = lax.broadcasted_iota(I32, q.shape, 1)
    qms = [jnp.where((lane >= m * ATT_QK_DIM) & (lane < (m + 1) * ATT_QK_DIM), q, jnp.zeros_like(q))
           for m in range(2)]
    qw = 2 * LANES
    nqc = tq // qw
    nkb, kb = vt_ref.shape[1], vt_ref.shape[3]
    ones_rows = 16

    streams = [(m, c) for m in range(2) for c in range(nqc)]
    run_max = {st: jnp.full((1, qw), NEG_BIG, F32) for st in streams}
    nt = s // tk
    items = [(t, st) for t in range(nt) for st in streams]
    pending = {}
    depth = min(7, len(streams) - 1)

    def logits_part(item):
        t, (m, c) = item
        rows, cols = slice(t * tk, (t + 1) * tk), slice(c * qw, (c + 1) * qw)
        logit = bias_ref[rows, cols] + lax.dot_general(k_ref[0, rows, :], qms[m][cols, :], (((1,), (1,)), ((), ())),
                                                       preferred_element_type=F32)
        mx = logit[0:8, :]
        for r in range(1, tk // 8):
            mx = jnp.maximum(mx, logit[r * 8:(r + 1) * 8, :])
        pending[item] = (logit, jnp.max(mx, axis=0, keepdims=True))

    def pv_part(item):
        t, st = item
        m, c = st
        cols = slice(c * qw, (c + 1) * qw)
        logit, tile_max = pending.pop(item)
        m_old = run_max[st]
        m_new = jnp.maximum(m_old, tile_max)
        run_max[st] = m_new
        probs = jnp.exp2(logit - m_new).astype(BF16)
        blk, off = (t * tk) // kb, (t * tk) % kb
        vt_ext = jnp.concatenate([vt_ref[0, blk][:, off:off + tk], jnp.ones((ones_rows, tk), BF16)], axis=0)
        part = jnp.dot(vt_ext, probs, preferred_element_type=F32)
        if t == 0:
            acc_ref[m, :, cols] = part
        else:
            acc_ref[m, :, cols] = acc_ref[m, :, cols] * jnp.exp2(m_old - m_new) + part

    for i in range(len(items) + depth):
        if i < len(items):
            logits_part(items[i])
        if i >= depth:
            pv_part(items[i - depth])
    outs = [acc_ref[m, :ATT_V_DIM, :] / acc_ref[m, ATT_V_DIM:ATT_V_DIM + 1, :] for m in range(2)]
    att = outs[0] - lam * outs[1]
    att = att * lax.rsqrt(jnp.mean(att * att, axis=0, keepdims=True) + EPS)
    o_ref[0] = (att * (g_ref[...] * (1.0 - lam_init))).T.astype(BF16)


def _attention(q, k, v_t, tbl, lq1, lk1, lq2, lk2, subln_g, lam_init, tq):
    bsz, s, _ = q.shape
    nkb, kb = v_t.shape[1], v_t.shape[3]
    small = lambda n: pl.BlockSpec((1, n), lambda h, i, b: (0, 0))
    return pl.pallas_call(
        functools.partial(_attn_body, lam_init=lam_init, tk=2 * LANES),
        grid=(ATT_HEADS, s // tq, bsz),
        in_specs=[pl.BlockSpec(memory_space=pltpu.SMEM),
                  small(ATT_QK_DIM), small(ATT_QK_DIM), small(ATT_QK_DIM), small(ATT_QK_DIM),
                  pl.BlockSpec((ATT_V_DIM, 1), lambda h, i, b: (0, 0)),
                  pl.BlockSpec((1, tq, ATT_V_DIM), lambda h, i, b: (b, i, h)),
                  pl.BlockSpec((1, s, ATT_V_DIM), lambda h, i, b: (b, 0, h)),
                  pl.BlockSpec((1, nkb, ATT_V_DIM, kb), lambda h, i, b: (b, 0, h, 0))],
        out_specs=pl.BlockSpec((1, tq, ATT_V_DIM), lambda h, i, b: (b, i, h)),
        out_shape=jax.ShapeDtypeStruct((bsz, s, ATT_HEADS * ATT_V_DIM), BF16),
        scratch_shapes=[pltpu.VMEM((s, tq), F32),
                        pltpu.VMEM((2, ATT_V_DIM + 16, tq), F32)],
        compiler_params=_params("parallel", "parallel", "arbitrary"),
        name="diff_attention",
    )(tbl, lq1, lk1, lq2, lk2, subln_g.reshape(ATT_V_DIM, 1), q, k, v_t)


def _head_selector(off):
    r = lax.broadcasted_iota(I32, (LANES, SSD_WIDTH), 0)
    c = lax.broadcasted_iota(I32, (LANES, SSD_WIDTH), 1)
    return jnp.where(r - off == c // SSD_HEADDIM, 1.0, 0.0).astype(BF16)


def _expand_heads(cols, selector):
    hi = cols.astype(BF16)
    lo = (cols - hi.astype(F32)).astype(BF16)
    return (jnp.dot(hi, selector, preferred_element_type=F32)
            + jnp.dot(lo, selector, preferred_element_type=F32))


def _ssd_body(z_ref, xbc_ref, dtr_ref, cw_ref, cb_ref, dtb_ref, alog_ref, dskip_ref, ng_ref, o_ref,
              xp_ref, xc_ref, dt_ref, da_ref, cs_ref, cst_ref, ecst_ref, dtt_ref, bt_ref, y_ref, hf_ref, hb_ref):
    s = xbc_ref.shape[1]
    ch = SSD_CHUNK
    assert ch == LANES
    nc = s // ch
    pad = 8
    nconv = xbc_ref.shape[2]

    xp_ref[0:pad, :] = jnp.zeros((pad, nconv), F32)
    xp_ref[pad + s:pad + s + pad, :] = jnp.zeros((pad, nconv), F32)
    xp_ref[pad:pad + s, :] = xbc_ref[0]

    def conv_chunk(c, carry):
        r0 = pl.multiple_of(c * ch, ch)
        nwin = ch + 2 * pad
        win = xp_ref[pl.ds(r0, nwin), :]
        acc = jnp.broadcast_to(cb_ref[...], (ch, nconv))
        for j in range(SSD_CONV):
            shift = (SSD_CONV // 2 - j) % nwin
            tap = win if shift == 0 else pltpu.roll(win, shift, axis=0)
            acc = acc + cw_ref[j:j + 1, :] * tap[pad:pad + ch, :]
        xc_ref[pl.ds(r0, ch), :] = _silu(acc)
        return carry

    lax.fori_loop(0, nc, conv_chunk, 0)

    lane = lax.broadcasted_iota(I32, (s, LANES), 1)
    pre = dtr_ref[0] + dtb_ref[...]
    dt = jnp.maximum(pre, 0.0) + jnp.log1p(jnp.exp(-jnp.abs(pre)))
    dt = jnp.where(lane < 2 * SSD_HEADS, dt, 0.0)
    dt_ref[...] = dt
    da_ref[...] = dt * (-jnp.exp(alog_ref[...]))

    row = lax.broadcasted_iota(I32, (ch, ch), 0)
    col = lax.broadcasted_iota(I32, (ch, ch), 1)
    tri = jnp.where(col <= row, 1.0, 0.0).astype(F32)
    lower = col <= row
    upper = col >= row
    lane_c = lax.broadcasted_iota(I32, (ch, LANES), 1)
    st_rows = lax.broadcasted_iota(I32, (SSD_BC, SSD_WIDTH), 0) // SSD_STATE
    st_cols = lax.broadcasted_iota(I32, (SSD_BC, SSD_WIDTH), 1) // (SSD_WIDTH // SSD_GROUPS)
    same_group = st_rows == st_cols
    sel_f = _head_selector(0)
    sel_b = _head_selector(SSD_HEADS)

    hf_ref[...] = jnp.zeros_like(hf_ref)
    hb_ref[...] = jnp.zeros_like(hb_ref)

    def wide(x):
        return jnp.concatenate([x[c * ch:(c + 1) * ch, :] for c in range(nc)], axis=1)

    da_wide = wide(da_ref[...])
    cs_wide = jnp.dot(tri, da_wide, precision=HIGHEST, preferred_element_type=F32)
    for c in range(nc):
        cs_ref[c * ch:(c + 1) * ch, :] = cs_wide[:, c * LANES:(c + 1) * LANES]
    cst_ref[...] = cs_wide.T
    ecst_ref[...] = (cs_wide - da_wide).T
    dtt_ref[...] = wide(dt_ref[...]).T
    bt_ref[...] = wide(xc_ref[:, SSD_WIDTH:SSD_WIDTH + SSD_BC]).T.astype(BF16)

    def fwd_chunk(c, carry):
        r0 = pl.multiple_of(c * ch, ch)
        da = da_ref[pl.ds(r0, ch), :]
        dtc = dt_ref[pl.ds(r0, ch), :]
        cs = cs_ref[pl.ds(r0, ch), :]
        tot = cs[ch - 1:ch, :]
        ecs = cs - da
        nh2 = 2 * SSD_HEADS
        cs_t = cst_ref[pl.ds(r0, nh2), :]
        ecs_t = ecst_ref[pl.ds(r0, nh2), :]
        dt_t = dtt_ref[pl.ds(r0, nh2), :]
        xs = xc_ref[pl.ds(r0, ch), 0:SSD_WIDTH]
        cm = xc_ref[pl.ds(r0, ch), SSD_WIDTH + SSD_BC:SSD_WIDTH + 2 * SSD_BC]
        bt = bt_ref[pl.ds(r0, ch), :]
        cmb = cm.astype(BF16)
        xsb = xs.astype(BF16)

        hf = hf_ref[...]
        factors = _expand_heads(jnp.concatenate(
            [jnp.exp(cs), dtc * jnp.exp(tot - cs), jnp.broadcast_to(jnp.exp(tot), (8, LANES))], axis=0), sel_f)
        y_ref[pl.ds(r0, ch), :] = jnp.dot(cmb, hf.astype(BF16), preferred_element_type=F32) * factors[0:ch]
        xd = (xs * factors[ch:2 * ch]).astype(BF16)
        upd = jnp.dot(bt, xd, preferred_element_type=F32)
        hf_ref[...] = hf * factors[2 * ch:2 * ch + 1] + jnp.where(same_group, upd, 0.0)

        g_mats = []
        for g in range(SSD_GROUPS):
            cg = jnp.where(lane_c // SSD_STATE == g, cmb, jnp.zeros_like(cmb))
            g_mats.append(jnp.dot(cg, bt, preferred_element_type=F32))
        for pair in range(SSD_HEADS // 2):
            xpair = xsb[:, pair * LANES:(pair + 1) * LANES]
            ypair = jnp.zeros((ch, LANES), F32)
            for sub in range(2):
                hh = 2 * pair + sub
                g = hh // (SSD_HEADS // SSD_GROUPS)
                d_f = cs[:, hh:hh + 1] - cs_t[hh:hh + 1, :]
                d_b = ecs_t[SSD_HEADS + hh:SSD_HEADS + hh + 1, :] - ecs[:, SSD_HEADS + hh:SSD_HEADS + hh + 1]
                decay = jnp.exp(jnp.where(lower, d_f, d_b))
                wgt = (jnp.where(lower, dt_t[hh:hh + 1, :], 0.0)
                       + jnp.where(upper, dt_t[SSD_HEADS + hh:SSD_HEADS + hh + 1, :], 0.0))
                m = (g_mats[g] * decay * wgt).astype(BF16)
                xh = jnp.where(lane_c // SSD_HEADDIM == sub, xpair, jnp.zeros_like(xpair))
                ypair = ypair + jnp.dot(m, xh, preferred_element_type=F32)
            y_ref[pl.ds(r0, ch), pair * LANES:(pair + 1) * LANES] += ypair
        return carry

    lax.fori_loop(0, nc, fwd_chunk, 0, unroll=4)

    def bwd_chunk(i, carry):
        c = nc - 1 - i
        r0 = pl.multiple_of(c * ch, ch)
        da = da_ref[pl.ds(r0, ch), :]
        dtc = dt_ref[pl.ds(r0, ch), :]
        cs = cs_ref[pl.ds(r0, ch), :]
        tot = cs[ch - 1:ch, :]
        ecs = cs - da
        xs = xc_ref[pl.ds(r0, ch), 0:SSD_WIDTH]
        cm = xc_ref[pl.ds(r0, ch), SSD_WIDTH + SSD_BC:SSD_WIDTH + 2 * SSD_BC]
        bt = bt_ref[pl.ds(r0, ch), :]

        hb = hb_ref[...]
        factors = _expand_heads(jnp.concatenate(
            [jnp.exp(tot - ecs), dtc * jnp.exp(ecs), jnp.broadcast_to(jnp.exp(tot), (8, LANES))], axis=0), sel_b)
        y_off = jnp.dot(cm.astype(BF16), hb.astype(BF16), preferred_element_type=F32) * factors[0:ch]
        xd = (xs * factors[ch:2 * ch]).astype(BF16)
        upd = jnp.dot(bt, xd, preferred_element_type=F32)
        hb_ref[...] = hb * factors[2 * ch:2 * ch + 1] + jnp.where(same_group, upd, 0.0)

        y = y_ref[pl.ds(r0, ch), :] + y_off + dskip_ref[...] * xs
        y = y * _silu(z_ref[0, pl.ds(r0, ch), :])
        gw = SSD_WIDTH // SSD_GROUPS
        yn = jnp.concatenate([_rms(y[:, g * gw:(g + 1) * gw]) for g in range(SSD_GROUPS)], axis=-1)
        o_ref[0, pl.ds(r0, ch), :] = (yn * ng_ref[...]).astype(BF16)
        return carry

    lax.fori_loop(0, nc, bwd_chunk, 0, unroll=4)


def _ssd(z, xbc, dt_raw, conv_w, conv_b, dt_bias, a_log, d_skip, norm_g):
    bsz, s, nconv = xbc.shape
    tok = lambda wd: pl.BlockSpec((1, s, wd), lambda b: (b, 0, 0))
    full = lambda a: pl.BlockSpec(a.shape, lambda b: (0, 0))
    return pl.pallas_call(
        _ssd_body,
        grid=(bsz,),
        in_specs=[tok(SSD_WIDTH), tok(nconv), tok(LANES), full(conv_w), full(conv_b), full(dt_bias),
                  full(a_log), full(d_skip), full(norm_g)],
        out_specs=tok(SSD_WIDTH),
        out_shape=jax.ShapeDtypeStruct((bsz, s, SSD_WIDTH), BF16),
        scratch_shapes=[pltpu.VMEM((s + 16, nconv), F32),
                        pltpu.VMEM((s, nconv), F32),
                        pltpu.VMEM((s, LANES), F32),
                        pltpu.VMEM((s, LANES), F32),
                        pltpu.VMEM((s, LANES), F32),
                        pltpu.VMEM((s, LANES), F32),
                        pltpu.VMEM((s, LANES), F32),
                        pltpu.VMEM((s, LANES), F32),
                        pltpu.VMEM((s, SSD_BC), BF16),
                        pltpu.VMEM((s, SSD_WIDTH), F32),
                        pltpu.VMEM((SSD_BC, SSD_WIDTH), F32),
                        pltpu.VMEM((SSD_BC, SSD_WIDTH), F32)],
        compiler_params=_params("parallel"),
        name="ssd_mixer",
    )(z, xbc, dt_raw, conv_w, conv_b, dt_bias, a_log, d_skip, norm_g)


def _outproj_body(att_ref, ssd_ref, x_ref, g1_ref, sh_ref, sc_ref, ng_ref, w_ref, rw_ref,
                  x1_ref, h2_ref, aff_ref):
    na = att_ref.shape[2]
    mix = (jnp.dot(att_ref[0], w_ref[0:na, :], preferred_element_type=F32)
           + jnp.dot(ssd_ref[0], w_ref[na:, :], preferred_element_type=F32))
    x1 = x_ref[0] + g1_ref[0] * mix
    x1_ref[0] = x1
    h2 = _rms(x1) * ng_ref[...]
    h2 = h2 * (1.0 + sc_ref[0]) + sh_ref[0]
    h2_hi = h2.astype(BF16)
    h2_ref[0] = h2_hi
    h2_lo = (h2 - h2_hi.astype(F32)).astype(BF16)
    nt = lambda a, b: lax.dot_general(a, b, (((1,), (1,)), ((), ())), preferred_element_type=F32)
    logits = nt(rw_ref[0], h2_hi) + (nt(rw_ref[0], h2_lo) + nt(rw_ref[1], h2_hi))
    e = jnp.exp(logits - jnp.max(logits, axis=0, keepdims=True))
    aff_ref[0] = e / jnp.sum(e, axis=0, keepdims=True)


def _out_proj(att, ssd, x, g1, sh2, sc2, ng, w_out, rw_t, ts):
    bsz, s, d = x.shape
    tok = lambda wd: pl.BlockSpec((1, ts, wd), lambda b, t: (b, t, 0))
    vec = pl.BlockSpec((1, 1, d), lambda b, t: (b, 0, 0))
    full = lambda a: pl.BlockSpec(a.shape, lambda b, t: (0, 0))
    ne = rw_t.shape[1]
    return pl.pallas_call(
        _outproj_body,
        grid=(bsz, s // ts),
        in_specs=[tok(att.shape[2]), tok(ssd.shape[2]), tok(d), vec, vec, vec, full(ng), full(w_out),
                  pl.BlockSpec(rw_t.shape, lambda b, t: (0, 0, 0))],
        out_specs=[tok(d), tok(d), pl.BlockSpec((1, ne, ts), lambda b, t: (b, 0, t))],
        out_shape=[jax.ShapeDtypeStruct((bsz, s, d), F32),
                   jax.ShapeDtypeStruct((bsz, s, d), BF16),
                   jax.ShapeDtypeStruct((bsz, ne, s), F32)],
        compiler_params=_params("parallel", "arbitrary"),
        name="out_proj_router",
    )(att, ssd, x, g1, sh2, sc2, ng, w_out, rw_t)


def _topk_body(aff_ref, pos_ref, post_ref, ut_ref, *, cap):
    nrow, s = aff_ref.shape
    r = lax.broadcasted_iota(I32, (s, s), 0)
    c = lax.broadcasted_iota(I32, (s, s), 1)
    ut_ref[...] = jnp.where(r < c, 1.0, 0.0).astype(BF16)

    bits = pltpu.bitcast(aff_ref[...], I32)

    def count(mask):
        return jnp.sum(jnp.where(mask, 1, 0), axis=-1, keepdims=True)

    thr = jnp.zeros((nrow, 1), I32)
    for bit in range(30, -1, -1):
        cand = thr | (1 << bit)
        thr = jnp.where(count(bits >= cand) >= cap, cand, thr)
    gt = bits > thr
    eq = bits == thr
    need = cap - count(gt)
    eq_rank = jnp.dot(jnp.where(eq, 1.0, 0.0).astype(BF16), ut_ref[...], preferred_element_type=F32)
    sel = gt | (eq & (eq_rank < need.astype(F32)))
    slot = jnp.dot(jnp.where(sel, 1.0, 0.0).astype(BF16), ut_ref[...], preferred_element_type=F32)
    pos = jnp.where(sel, slot, -1.0)
    pos_ref[...] = pos.astype(I32)
    post_ref[...] = pos.T.astype(I32)


def _topk(aff_t, cap):
    bsz, ne, s = aff_t.shape
    nrow = bsz * ne
    pos, pos_t = pl.pallas_call(
        functools.partial(_topk_body, cap=cap),
        grid=(1,),
        in_specs=[pl.BlockSpec((nrow, s), lambda i: (0, 0))],
        out_specs=[pl.BlockSpec((nrow, s), lambda i: (0, 0)),
                   pl.BlockSpec((s, nrow), lambda i: (0, 0))],
        out_shape=[jax.ShapeDtypeStruct((nrow, s), I32),
                   jax.ShapeDtypeStruct((s, nrow), I32)],
        scratch_shapes=[pltpu.VMEM((s, s), BF16)],
        compiler_params=_params("arbitrary"),
        name="expert_topk",
    )(aff_t.reshape(nrow, s))
    return pos.reshape(bsz, ne, s), pos_t


def _gather_body(h2_ref, pos_ref, aff_ref, xg_ref, gc_ref, *, cap):
    ne, s = pos_ref.shape[1], pos_ref.shape[2]
    slot = lax.broadcasted_iota(I32, (cap, s), 0)
    h2 = h2_ref[0]
    for e in range(ne):
        hit = pos_ref[0, e:e + 1, :] == slot
        xg_ref[e] = jnp.dot(jnp.where(hit, 1.0, 0.0).astype(BF16), h2,
                            preferred_element_type=F32).astype(BF16)
        gate = jnp.sum(jnp.where(hit, aff_ref[0, e:e + 1, :], 0.0), axis=-1, keepdims=True)
        gc_ref[e] = jnp.broadcast_to(gate, (cap, LANES))


def _gather(h2, pos, aff_t, cap):
    bsz, s, d = h2.shape
    ne = pos.shape[1]
    return pl.pallas_call(
        functools.partial(_gather_body, cap=cap),
        grid=(bsz,),
        in_specs=[pl.BlockSpec((1, s, d), lambda b: (b, 0, 0)),
                  pl.BlockSpec((1, ne, s), lambda b: (b, 0, 0)),
                  pl.BlockSpec((1, ne, s), lambda b: (b, 0, 0))],
        out_specs=[pl.BlockSpec((ne, cap, d), lambda b: (0, b, 0)),
                   pl.BlockSpec((ne, cap, LANES), lambda b: (0, b, 0))],
        out_shape=[jax.ShapeDtypeStruct((ne, bsz * cap, d), BF16),
                   jax.ShapeDtypeStruct((ne, bsz * cap, LANES), F32)],
        compiler_params=_params("parallel"),
        name="moe_gather",
    )(h2, pos, aff_t)


def _ffn_body(xg_ref, gc_ref, wg_ref, wu_ref, wd_ref, y_ref, acc_ref):
    f = pl.program_id(1)

    @pl.when(f == 0)
    def _():
        acc_ref[...] = jnp.zeros_like(acc_ref)

    wg = wg_ref[0].astype(BF16)
    wu = wu_ref[0].astype(BF16)
    wd = wd_ref[0].astype(BF16)
    nq = 4
    rq = xg_ref.shape[1] // nq
    hidden = {}

    def gate_up(r):
        xg = xg_ref[0, r * rq:(r + 1) * rq, :]
        gate = jnp.dot(xg, wg, preferred_element_type=F32)
        up = jnp.dot(xg, wu, preferred_element_type=F32)
        hidden[r] = (_silu(gate) * up).astype(BF16)

    def down(r):
        acc_ref[r * rq:(r + 1) * rq, :] += jnp.dot(hidden.pop(r), wd, preferred_element_type=F32)

    gate_up(0)
    for r in range(nq):
        if r + 1 < nq:
            gate_up(r + 1)
        down(r)

    @pl.when(f == pl.num_programs(1) - 1)
    def _():
        m, d = acc_ref.shape
        gcol = gc_ref[0]
        y_ref[0] = (acc_ref[...] * jnp.concatenate([gcol] * (d // LANES), axis=-1)).astype(BF16)


def _ffn(xg, gc, w_gate, w_up, w_down, tf):
    ne, m, d = xg.shape
    ff = w_gate.shape[2]
    return pl.pallas_call(
        _ffn_body,
        grid=(ne, ff // tf),
        in_specs=[pl.BlockSpec((1, m, d), lambda e, f: (e, 0, 0)),
                  pl.BlockSpec((1, m, LANES), lambda e, f: (e, 0, 0)),
                  pl.BlockSpec((1, d, tf), lambda e, f: (e, 0, f)),
                  pl.BlockSpec((1, d, tf), lambda e, f: (e, 0, f)),
                  pl.BlockSpec((1, tf, d), lambda e, f: (e, f, 0))],
        out_specs=pl.BlockSpec((1, m, d), lambda e, f: (e, 0, 0)),
        out_shape=jax.ShapeDtypeStruct((ne, m, d), BF16),
        scratch_shapes=[pltpu.VMEM((m, d), F32)],
        compiler_params=_params("parallel", "arbitrary"),
        name="moe_ffn",
    )(xg, gc, w_gate, w_up, w_down)


def _scatter_body(post_ref, y_ref, x1_ref, g2_ref, ng_ref, o_ref, *, cap):
    ne = y_ref.shape[0]
    ts, nrow = post_ref.shape
    d = y_ref.shape[2]
    r = lax.broadcasted_iota(I32, (nrow, LANES), 0)
    c = lax.broadcasted_iota(I32, (nrow, LANES), 1)
    pick = jnp.where(r == pl.program_id(0) * ne + c, 1.0, 0.0).astype(BF16)
    post = jnp.dot(post_ref[...].astype(F32).astype(BF16), pick, preferred_element_type=F32)
    slot = lax.broadcasted_iota(I32, (ts, cap), 1).astype(F32)
    onehot = jnp.concatenate(
        [jnp.where(post[:, e:e + 1] == slot, 1.0, 0.0).astype(BF16) for e in range(ne)], axis=-1)
    moe = jnp.dot(onehot, y_ref[...].reshape(ne * cap, d), preferred_element_type=F32)
    x2 = x1_ref[0] + g2_ref[0] * moe
    o_ref[0] = _rms(x2) * ng_ref[...]


def _scatter(post, y, x1, g2, ng, cap, ts):
    bsz, s, d = x1.shape
    ne = y.shape[0]
    return pl.pallas_call(
        functools.partial(_scatter_body, cap=cap),
        grid=(bsz, s // ts),
        in_specs=[pl.BlockSpec((ts, post.shape[1]), lambda b, t: (t, 0)),
                  pl.BlockSpec((ne, cap, d), lambda b, t: (0, b, 0)),
                  pl.BlockSpec((1, ts, d), lambda b, t: (b, t, 0)),
                  pl.BlockSpec((1, 1, d), lambda b, t: (b, 0, 0)),
                  pl.BlockSpec((1, d), lambda b, t: (0, 0))],
        out_specs=pl.BlockSpec((1, ts, d), lambda b, t: (b, t, 0)),
        out_shape=jax.ShapeDtypeStruct((bsz, s, d), F32),
        compiler_params=_params("parallel", "arbitrary"),
        name="moe_scatter_final",
    )(post, y, x1, g2, ng)


def _pad_lanes(a, n=LANES):
    return jnp.pad(a, [(0, 0)] * (a.ndim - 1) + [(0, n - a.shape[-1])])


def kernel(x, c, ada_w, ada_b, norm_mix_g, norm_ffn_g, norm_final_g, w_in, lambda_q1, lambda_k1, lambda_q2, lambda_k2, attn_subln_g, rel_bias_table, conv_w, conv_b, dt_bias_f, dt_bias_b, A_log_f, A_log_b, D_skip, ssm_norm_g, w_out, router_w, w_gate, w_up, w_down):
    bsz, s, d = x.shape
    depth = ada_w.shape[0]
    ne = router_w.shape[2]
    cap = CAPACITY_FACTOR * s // ne
    att_w = ATT_HEADS * ATT_V_DIM
    nconv = conv_w.shape[3]
    splits = (att_w, att_w, att_w, SSD_WIDTH, nconv, 2 * SSD_HEADS)
    offs = [0]
    for wd in splits:
        offs.append(offs[-1] + wd)
    assert offs[-1] == w_in.shape[2]
    row = lambda a: a.reshape(1, -1)

    assert depth == 1, "the final RMSNorm is fused into the (single) layer's scatter kernel"
    for l in range(depth):
        lam_init = 0.8 - 0.6 * math.exp(-0.3 * l)
        mod = _ada_mod(c, ada_w[l], ada_b[l])
        sh1, sc1, g1, sh2, sc2, g2 = [m.reshape(bsz, 1, d) for m in jnp.split(mod, 6, axis=-1)]

        q, k, v_t, z, xbc, dt_raw = _in_proj(x, sh1, sc1, row(norm_mix_g[l]), w_in[l].astype(BF16), offs, ts=1024)

        att = _attention(q, k, v_t, rel_bias_table, row(lambda_q1[l]), row(lambda_k1[l]),
                         row(lambda_q2[l]), row(lambda_k2[l]), row(attn_subln_g[l]), lam_init, tq=2048)

        dt_bias = _pad_lanes(jnp.concatenate([dt_bias_f[l], dt_bias_b[l]]).reshape(1, -1))
        a_log = _pad_lanes(jnp.concatenate([A_log_f[l], A_log_b[l]]).reshape(1, -1))
        d_skip = jnp.repeat(D_skip[l], SSD_HEADDIM).reshape(1, -1)
        ssd = _ssd(z, xbc, dt_raw, conv_w[l].reshape(SSD_CONV, nconv), row(conv_b[l]), dt_bias, a_log,
                   d_skip, row(ssm_norm_g[l]))

        rw_t = router_w[l].T
        rw_hi = rw_t.astype(BF16)
        rw_split = jnp.stack([rw_hi, (rw_t - rw_hi.astype(F32)).astype(BF16)])
        x1, h2, aff_t = _out_proj(att, ssd, x, g1, sh2, sc2, row(norm_ffn_g[l]), w_out[l].astype(BF16),
                                  rw_split, ts=1024)
        pos, pos_t = _topk(aff_t, cap)
        xg, gc = _gather(h2, pos, aff_t, cap)
        y = _ffn(xg, gc, w_gate[l], w_up[l], w_down[l], tf=256)
        x = _scatter(pos_t, y, x1, g2, row(norm_final_g), cap, ts=1024)
    return x
```

```python
import functools
import math

import jax
import jax.numpy as jnp
from jax import lax
from jax.experimental import pallas as pl
from jax.experimental.pallas import tpu as pltpu

F32 = jnp.float32
BF16 = jnp.bfloat16
I32 = jnp.int32
HIGHEST = lax.Precision.HIGHEST

ATT_HEADS = 4
ATT_QK_DIM = 64
ATT_V_DIM = 128
N_BUCKETS = 32
MAX_DISTANCE = 128
SSD_HEADS = 8
SSD_HEADDIM = 64
SSD_GROUPS = 2
SSD_STATE = 64
SSD_CONV = 5
SSD_CHUNK = 128
SSD_WIDTH = SSD_HEADS * SSD_HEADDIM
SSD_BC = SSD_GROUPS * SSD_STATE
N_EXPERTS = 16
CAPACITY_FACTOR = 2
EPS = 1e-6
LANES = 128
NEG_BIG = -1e30
LOG2E = math.log2(math.e)

VMEM_LIMIT = 56 * 1024 * 1024


def _params(*semantics):
    return pltpu.CompilerParams(dimension_semantics=semantics, vmem_limit_bytes=VMEM_LIMIT)


def _silu(v):
    return v * jax.nn.sigmoid(v)


def _rms(v, eps=EPS):
    return v * lax.rsqrt(jnp.mean(v * v, axis=-1, keepdims=True) + eps)


def _split_bf16(v):
    hi = v.astype(BF16)
    return hi, (v - hi.astype(F32)).astype(BF16)


def _mod_body(c_ref, w_ref, b_ref, o_ref):
    a_hi, a_lo = _split_bf16(_silu(c_ref[...]))
    w_hi, w_lo = _split_bf16(w_ref[...])
    dot = functools.partial(jnp.dot, preferred_element_type=F32)
    o_ref[...] = dot(a_hi, w_hi) + (dot(a_lo, w_hi) + dot(a_hi, w_lo)) + b_ref[...]


def _ada_mod(c, w, b):
    bsz, d = c.shape
    n = w.shape[1]
    tn = n // 4
    return pl.pallas_call(
        _mod_body,
        grid=(n // tn,),
        in_specs=[pl.BlockSpec((bsz, d), lambda j: (0, 0)),
                  pl.BlockSpec((d, tn), lambda j: (0, j)),
                  pl.BlockSpec((1, tn), lambda j: (0, j))],
        out_specs=pl.BlockSpec((bsz, tn), lambda j: (0, j)),
        out_shape=jax.ShapeDtypeStruct((bsz, n), F32),
        compiler_params=_params("arbitrary"),
        name="ada_mod",
    )(c, w, b.reshape(1, n))


def _inproj_body(x_ref, sh_ref, sc_ref, g_ref, w_ref, q_ref, k_ref, vt_ref, z_ref, xbc_ref, dt_ref, wvt_ref,
                 *, offs, qscale):
    @pl.when((pl.program_id(0) == 0) & (pl.program_id(1) == 0))
    def _():
        wvt_ref[...] = w_ref[:, offs[2]:offs[3]].astype(F32).T.astype(BF16)

    h = _rms(x_ref[0]) * g_ref[...]
    h = (h * (1.0 + sc_ref[0]) + sh_ref[0]).astype(BF16)

    def proj(i):
        return jnp.dot(h, w_ref[:, offs[i]:offs[i + 1]], preferred_element_type=F32)

    q_ref[0] = (proj(0) * qscale).astype(BF16)
    k_ref[0] = proj(1).astype(BF16)
    vt_ref[0, 0] = lax.dot_general(wvt_ref[...], h, (((1,), (1,)), ((), ())),
                                   preferred_element_type=F32).astype(BF16)
    z_ref[0] = proj(3)
    xbc_ref[0] = proj(4)
    dt = proj(5)
    dt_ref[0] = jnp.concatenate([dt, jnp.zeros((dt.shape[0], LANES - dt.shape[1]), F32)], axis=1)


def _in_proj(x, sh, sc, g, w, offs, ts):
    bsz, s, d = x.shape
    wq, wk, wv, wz, wxbc = [offs[i + 1] - offs[i] for i in range(5)]
    tok = lambda wd: pl.BlockSpec((1, ts, wd), lambda b, t: (b, t, 0))
    vec = pl.BlockSpec((1, 1, d), lambda b, t: (b, 0, 0))
    tok_out = lambda wd, dt: (tok(wd), jax.ShapeDtypeStruct((bsz, s, wd), dt))
    outs = [tok_out(wq, BF16), tok_out(wk, BF16),
            (pl.BlockSpec((1, 1, wv, ts), lambda b, t: (b, t, 0, 0)), jax.ShapeDtypeStruct((bsz, s // ts, wv, ts), BF16)),
            tok_out(wz, F32), tok_out(wxbc, F32), tok_out(LANES, F32)]
    return pl.pallas_call(
        functools.partial(_inproj_body, offs=tuple(offs), qscale=ATT_QK_DIM ** -0.5 * LOG2E),
        grid=(bsz, s // ts),
        in_specs=[tok(d), vec, vec,
                  pl.BlockSpec((1, d), lambda b, t: (0, 0)),
                  pl.BlockSpec(w.shape, lambda b, t: (0, 0))],
        out_specs=[o[0] for o in outs],
        out_shape=[o[1] for o in outs],
        scratch_shapes=[pltpu.VMEM((wv, d), BF16)],
        compiler_params=_params("arbitrary", "arbitrary"),
        name="in_proj",
    )(x, sh, sc, g, w)


def _t5_bucket(rel):
    n = jnp.abs(rel)
    large = jnp.full(rel.shape, 8, I32)
    for t in (12, 16, 23, 32, 46, 64, 91):
        large = large + jnp.where(n >= t, 1, 0)
    return jnp.where(rel > 0, 16, 0) + jnp.where(n < 8, n, large)


def _attn_body(tbl_ref, lq1_ref, lk1_ref, lq2_ref, lk2_ref, g_ref, q_ref, k_ref, vt_ref, o_ref,
               bias_ref, acc_ref, *, lam_init, tk):
    h = pl.program_id(0)
    qi = pl.program_id(1)
    s, tq = bias_ref.shape
    tb = 2 * LANES
    ring = pl.next_power_of_2(tb + tq - 1)

    @pl.when(pl.program_id(2) == 0)
    def _():
        j = lax.broadcasted_iota(I32, (8, ring), 1)
        for kt in range(s // tb):
            bucket = _t5_bucket((kt * tb + tb - 1) - qi * tq - j)
            w = jnp.zeros((8, ring), F32)
            for bb in range(N_BUCKETS):
                w = jnp.where(bucket == bb, tbl_ref[bb, h] * LOG2E, w)
            big = jnp.broadcast_to(w[0:1], (tb, ring))
            rolled = pltpu.roll(big, ring - (tb - 1), axis=1, stride=1, stride_axis=0)
            bias_ref[kt * tb:(kt + 1) * tb, :] = rolled[:, :tq]

    lam = (jnp.exp(jnp.sum(lq1_ref[...] * lk1_ref[...], axis=-1, keepdims=True))
           - jnp.exp(jnp.sum(lq2_ref[...] * lk2_ref[...], axis=-1, keepdims=True)) + lam_init)
    q = q_ref[0]
    k_all = k_ref[0]
    lane = lax.broadcasted_iota(I32, q.shape, 1)
    qms = [jnp.where((lane >= m * ATT_QK_DIM) & (lane < (m + 1) * ATT_QK_DIM), q, jnp.zeros_like(q))
           for m in range(2)]
    qw = 2 * LANES
    nqc = tq // qw

    assert tq == s, "tile sidedness is static only when one step covers all queries"
    far = jnp.full((8, LANES), MAX_DISTANCE, I32)
    side_const = []
    for rel in (-far, far):
        bucket = _t5_bucket(rel)
        cst = jnp.zeros((8, LANES), F32)
        for bb in range(N_BUCKETS):
            cst = jnp.where(bucket == bb, tbl_ref[bb, h] * LOG2E, cst)
        c_hi = cst.astype(BF16)
        rem = cst - c_hi.astype(F32)
        c_mid = rem.astype(BF16)
        c_lo = (rem - c_mid.astype(F32)).astype(BF16)
        side_const.append([jnp.broadcast_to(p[0:1], k_all.shape) for p in (c_hi, c_mid, c_lo)])
    q_aug, k_side = [], []
    for m in range(2):
        spare = (1 - m) * ATT_QK_DIM
        hot = [jnp.where(lane == spare + i, 1.0, 0.0).astype(BF16) for i in range(3)]
        keep = jnp.where((lane >= spare) & (lane < spare + 3), 0.0, 1.0).astype(BF16)
        q_aug.append(qms[m] + (hot[0] + hot[1] + hot[2]))
        k_kept = k_all * keep
        k_side.append([k_kept + (hot[0] * parts[0] + hot[1] * parts[1] + hot[2] * parts[2]) for parts in side_const])
    nkb, kb = vt_ref.shape[1], vt_ref.shape[3]
    ones_rows = 16

    streams = [(m, c) for m in range(2) for c in range(nqc)]
    run_max = {st: jnp.full((1, qw), NEG_BIG, F32) for st in streams}
    nt = s // tk
    items = [(t, st) for t in range(nt) for st in streams]
    pending = {}
    depth = min(7, len(streams) - 1)

    def logits_part(item):
        t, (m, c) = item
        rows, cols = slice(t * tk, (t + 1) * tk), slice(c * qw, (c + 1) * qw)
        nt_dot = lambda a, b: lax.dot_general(a, b, (((1,), (1,)), ((), ())), preferred_element_type=F32)
        max_off = (t + 1) * tk - 1 - c * qw
        min_off = t * tk - ((c + 1) * qw - 1)
        if max_off <= -MAX_DISTANCE:
            logit = nt_dot(k_side[m][0][rows, :], q_aug[m][cols, :])
        elif min_off >= MAX_DISTANCE:
            logit = nt_dot(k_side[m][1][rows, :], q_aug[m][cols, :])
        else:
            logit = bias_ref[rows, cols] + nt_dot(k_all[rows, :], qms[m][cols, :])
        mx = logit[0:8, :]
        for r in range(1, tk // 8):
            mx = jnp.maximum(mx, logit[r * 8:(r + 1) * 8, :])
        pending[item] = (logit, jnp.max(mx, axis=0, keepdims=True))

    def pv_part(item):
        t, st = item
        m, c = st
        cols = slice(c * qw, (c + 1) * qw)
        logit, tile_max = pending.pop(item)
        m_old = run_max[st]
        m_new = jnp.maximum(m_old, tile_max)
        run_max[st] = m_new
        probs = jnp.exp2(logit - m_new).astype(BF16)
        blk, off = (t * tk) // kb, (t * tk) % kb
        vt_ext = jnp.concatenate([vt_ref[0, blk][:, off:off + tk], jnp.ones((ones_rows, tk), BF16)], axis=0)
        part = jnp.dot(vt_ext, probs, preferred_element_type=F32)
        if t == 0:
            acc_ref[m, :, cols] = part
        else:
            acc_ref[m, :, cols] = acc_ref[m, :, cols] * jnp.exp2(m_old - m_new) + part

    for i in range(len(items) + depth):
        if i < len(items):
            logits_part(items[i])
        if i >= depth:
            pv_part(items[i - depth])
    outs = [acc_ref[m, :ATT_V_DIM, :] / acc_ref[m, ATT_V_DIM:ATT_V_DIM + 1, :] for m in range(2)]
    att = outs[0] - lam * outs[1]
    att = att * lax.rsqrt(jnp.mean(att * att, axis=0, keepdims=True) + EPS)
    o_ref[0] = (att * (g_ref[...] * (1.0 - lam_init))).T.astype(BF16)


def _attention(q, k, v_t, tbl, lq1, lk1, lq2, lk2, subln_g, lam_init, tq):
    bsz, s, _ = q.shape
    nkb, kb = v_t.shape[1], v_t.shape[3]
    small = lambda n: pl.BlockSpec((1, n), lambda h, i, b: (0, 0))
    return pl.pallas_call(
        functools.partial(_attn_body, lam_init=lam_init, tk=2 * LANES),
        grid=(ATT_HEADS, s // tq, bsz),
        in_specs=[pl.BlockSpec(memory_space=pltpu.SMEM),
                  small(ATT_QK_DIM), small(ATT_QK_DIM), small(ATT_QK_DIM), small(ATT_QK_DIM),
                  pl.BlockSpec((ATT_V_DIM, 1), lambda h, i, b: (0, 0)),
                  pl.BlockSpec((1, tq, ATT_V_DIM), lambda h, i, b: (b, i, h)),
                  pl.BlockSpec((1, s, ATT_V_DIM), lambda h, i, b: (b, 0, h)),
                  pl.BlockSpec((1, nkb, ATT_V_DIM, kb), lambda h, i, b: (b, 0, h, 0))],
        out_specs=pl.BlockSpec((1, tq, ATT_V_DIM), lambda h, i, b: (b, i, h)),
        out_shape=jax.ShapeDtypeStruct((bsz, s, ATT_HEADS * ATT_V_DIM), BF16),
        scratch_shapes=[pltpu.VMEM((s, tq), F32),
                        pltpu.VMEM((2, ATT_V_DIM + 16, tq), F32)],
        compiler_params=_params("parallel", "parallel", "arbitrary"),
        name="diff_attention",
    )(tbl, lq1, lk1, lq2, lk2, subln_g.reshape(ATT_V_DIM, 1), q, k, v_t)


def _head_selector(off):
    r = lax.broadcasted_iota(I32, (LANES, SSD_WIDTH), 0)
    c = lax.broadcasted_iota(I32, (LANES, SSD_WIDTH), 1)
    return jnp.where(r - off == c // SSD_HEADDIM, 1.0, 0.0).astype(BF16)


def _expand_heads(cols, selector):
    hi = cols.astype(BF16)
    lo = (cols - hi.astype(F32)).astype(BF16)
    return (jnp.dot(hi, selector, preferred_element_type=F32)
            + jnp.dot(lo, selector, preferred_element_type=F32))


def _ssd_body(z_ref, xbc_ref, dtr_ref, cw_ref, cb_ref, dtb_ref, alog_ref, dskip_ref, ng_ref, o_ref,
              xp_ref, xc_ref, dt_ref, da_ref, cs_ref, cst_ref, ecst_ref, dtt_ref, bt_ref, xdf_ref, xdb_ref,
              dtotf_ref, dtotb_ref, y_ref, hf_ref, hb_ref):
    s = xbc_ref.shape[1]
    ch = SSD_CHUNK
    assert ch == LANES
    nc = s // ch
    pad = 8
    nconv = xbc_ref.shape[2]

    xp_ref[0:pad, :] = jnp.zeros((pad, nconv), F32)
    xp_ref[pad + s:pad + s + pad, :] = jnp.zeros((pad, nconv), F32)
    xp_ref[pad:pad + s, :] = xbc_ref[0]

    def conv_chunk(c, carry):
        r0 = pl.multiple_of(c * ch, ch)
        nwin = ch + 2 * pad
        win = xp_ref[pl.ds(r0, nwin), :]
        acc = jnp.broadcast_to(cb_ref[...], (ch, nconv))
        for j in range(SSD_CONV):
            shift = (SSD_CONV // 2 - j) % nwin
            tap = win if shift == 0 else pltpu.roll(win, shift, axis=0)
            acc = acc + cw_ref[j:j + 1, :] * tap[pad:pad + ch, :]
        xc_ref[pl.ds(r0, ch), :] = _silu(acc)
        return carry

    lax.fori_loop(0, nc, conv_chunk, 0)

    lane = lax.broadcasted_iota(I32, (s, LANES), 1)
    pre = dtr_ref[0] + dtb_ref[...]
    dt = jnp.maximum(pre, 0.0) + jnp.log1p(jnp.exp(-jnp.abs(pre)))
    dt = jnp.where(lane < 2 * SSD_HEADS, dt, 0.0)
    dt_ref[...] = dt
    da_ref[...] = dt * (-jnp.exp(alog_ref[...]))

    row = lax.broadcasted_iota(I32, (ch, ch), 0)
    col = lax.broadcasted_iota(I32, (ch, ch), 1)
    tri = jnp.where(col <= row, 1.0, 0.0).astype(F32)
    lower = col <= row
    upper = col >= row
    lane_c = lax.broadcasted_iota(I32, (ch, LANES), 1)
    st_rows = lax.broadcasted_iota(I32, (SSD_BC, SSD_WIDTH), 0) // SSD_STATE
    st_cols = lax.broadcasted_iota(I32, (SSD_BC, SSD_WIDTH), 1) // (SSD_WIDTH // SSD_GROUPS)
    same_group = st_rows == st_cols
    sel_f = _head_selector(0)
    sel_b = _head_selector(SSD_HEADS)

    hf_ref[...] = jnp.zeros_like(hf_ref)
    hb_ref[...] = jnp.zeros_like(hb_ref)

    def wide(x):
        return jnp.concatenate([x[c * ch:(c + 1) * ch, :] for c in range(nc)], axis=1)

    da_wide = wide(da_ref[...])
    cs_wide = jnp.dot(tri, da_wide, precision=HIGHEST, preferred_element_type=F32)
    for c in range(nc):
        cs_ref[c * ch:(c + 1) * ch, :] = cs_wide[:, c * LANES:(c + 1) * LANES]
    cst_ref[...] = cs_wide.T
    ecst_ref[...] = (cs_wide - da_wide).T
    dtt_ref[...] = wide(dt_ref[...]).T
    bt_ref[...] = wide(xc_ref[:, SSD_WIDTH:SSD_WIDTH + SSD_BC]).T.astype(BF16)

    def prep_chunk(c, carry):
        r0 = pl.multiple_of(c * ch, ch)
        r8 = pl.multiple_of(c * 8, 8)
        da = da_ref[pl.ds(r0, ch), :]
        dtc = dt_ref[pl.ds(r0, ch), :]
        cs = cs_ref[pl.ds(r0, ch), :]
        tot = cs[ch - 1:ch, :]
        xs = xc_ref[pl.ds(r0, ch), 0:SSD_WIDTH]
        tot8 = jnp.broadcast_to(jnp.exp(tot), (8, LANES))
        ff = _expand_heads(jnp.concatenate([dtc * jnp.exp(tot - cs), tot8], axis=0), sel_f)
        xdf_ref[pl.ds(r0, ch), :] = (xs * ff[0:ch]).astype(BF16)
        dtotf_ref[pl.ds(r8, 8), :] = ff[ch:ch + 8]
        fb = _expand_heads(jnp.concatenate([dtc * jnp.exp(cs - da), tot8], axis=0), sel_b)
        xdb_ref[pl.ds(r0, ch), :] = (xs * fb[0:ch]).astype(BF16)
        dtotb_ref[pl.ds(r8, 8), :] = fb[ch:ch + 8]
        return carry

    lax.fori_loop(0, nc, prep_chunk, 0, unroll=2)

    def fwd_chunk(c, carry):
        r0 = pl.multiple_of(c * ch, ch)
        da = da_ref[pl.ds(r0, ch), :]
        cs = cs_ref[pl.ds(r0, ch), :]
        ecs = cs - da
        nh2 = 2 * SSD_HEADS
        cs_t = cst_ref[pl.ds(r0, nh2), :]
        ecs_t = ecst_ref[pl.ds(r0, nh2), :]
        dt_t = dtt_ref[pl.ds(r0, nh2), :]
        xs = xc_ref[pl.ds(r0, ch), 0:SSD_WIDTH]
        cm = xc_ref[pl.ds(r0, ch), SSD_WIDTH + SSD_BC:SSD_WIDTH + 2 * SSD_BC]
        bt = bt_ref[pl.ds(r0, ch), :]
        cmb = cm.astype(BF16)
        xsb = xs.astype(BF16)

        hf = hf_ref[...]
        y_ref[pl.ds(r0, ch), :] = (jnp.dot(cmb, hf.astype(BF16), preferred_element_type=F32)
                                   * _expand_heads(jnp.exp(cs), sel_f))
        upd = jnp.dot(bt, xdf_ref[pl.ds(r0, ch), :], preferred_element_type=F32)
        hf_ref[...] = hf * dtotf_ref[pl.ds(pl.multiple_of(c * 8, 8), 1), :] + jnp.where(same_group, upd, 0.0)

        g_mats = []
        for g in range(SSD_GROUPS):
            cg = jnp.where(lane_c // SSD_STATE == g, cmb, jnp.zeros_like(cmb))
            g_mats.append(jnp.dot(cg, bt, preferred_element_type=F32))
        for pair in range(SSD_HEADS // 2):
            xpair = xsb[:, pair * LANES:(pair + 1) * LANES]
            ypair = jnp.zeros((ch, LANES), F32)
            for sub in range(2):
                hh = 2 * pair + sub
                g = hh // (SSD_HEADS // SSD_GROUPS)
                d_f = cs[:, hh:hh + 1] - cs_t[hh:hh + 1, :]
                d_b = ecs_t[SSD_HEADS + hh:SSD_HEADS + hh + 1, :] - ecs[:, SSD_HEADS + hh:SSD_HEADS + hh + 1]
                decay = jnp.exp(jnp.where(lower, d_f, d_b))
                wgt = (jnp.where(lower, dt_t[hh:hh + 1, :], 0.0)
                       + jnp.where(upper, dt_t[SSD_HEADS + hh:SSD_HEADS + hh + 1, :], 0.0))
                m = (g_mats[g] * decay * wgt).astype(BF16)
                xh = jnp.where(lane_c // SSD_HEADDIM == sub, xpair, jnp.zeros_like(xpair))
                ypair = ypair + jnp.dot(m, xh, preferred_element_type=F32)
            y_ref[pl.ds(r0, ch), pair * LANES:(pair + 1) * LANES] += ypair
        return carry

    lax.fori_loop(0, nc, fwd_chunk, 0, unroll=4)

    def bwd_chunk(i, carry):
        c = nc - 1 - i
        r0 = pl.multiple_of(c * ch, ch)
        da = da_ref[pl.ds(r0, ch), :]
        cs = cs_ref[pl.ds(r0, ch), :]
        tot = cs[ch - 1:ch, :]
        ecs = cs - da
        xs = xc_ref[pl.ds(r0, ch), 0:SSD_WIDTH]
        cm = xc_ref[pl.ds(r0, ch), SSD_WIDTH + SSD_BC:SSD_WIDTH + 2 * SSD_BC]
        bt = bt_ref[pl.ds(r0, ch), :]

        hb = hb_ref[...]
        y_off = (jnp.dot(cm.astype(BF16), hb.astype(BF16), preferred_element_type=F32)
                 * _expand_heads(jnp.exp(tot - ecs), sel_b))
        upd = jnp.dot(bt, xdb_ref[pl.ds(r0, ch), :], preferred_element_type=F32)
        hb_ref[...] = hb * dtotb_ref[pl.ds(pl.multiple_of(c * 8, 8), 1), :] + jnp.where(same_group, upd, 0.0)

        y = y_ref[pl.ds(r0, ch), :] + y_off + dskip_ref[...] * xs
        y = y * _silu(z_ref[0, pl.ds(r0, ch), :])
        gw = SSD_WIDTH // SSD_GROUPS
        yn = jnp.concatenate([_rms(y[:, g * gw:(g + 1) * gw]) for g in range(SSD_GROUPS)], axis=-1)
        o_ref[0, pl.ds(r0, ch), :] = (yn * ng_ref[...]).astype(BF16)
        return carry

    lax.fori_loop(0, nc, bwd_chunk, 0, unroll=4)


def _ssd(z, xbc, dt_raw, conv_w, conv_b, dt_bias, a_log, d_skip, norm_g):
    bsz, s, nconv = xbc.shape
    tok = lambda wd: pl.BlockSpec((1, s, wd), lambda b: (b, 0, 0))
    full = lambda a: pl.BlockSpec(a.shape, lambda b: (0, 0))
    return pl.pallas_call(
        _ssd_body,
        grid=(bsz,),
        in_specs=[tok(SSD_WIDTH), tok(nconv), tok(LANES), full(conv_w), full(conv_b), full(dt_bias),
                  full(a_log), full(d_skip), full(norm_g)],
        out_specs=tok(SSD_WIDTH),
        out_shape=jax.ShapeDtypeStruct((bsz, s, SSD_WIDTH), BF16),
        scratch_shapes=[pltpu.VMEM((s + 16, nconv), F32),
                        pltpu.VMEM((s, nconv), F32),
                        pltpu.VMEM((s, LANES), F32),
                        pltpu.VMEM((s, LANES), F32),
                        pltpu.VMEM((s, LANES), F32),
                        pltpu.VMEM((s, LANES), F32),
                        pltpu.VMEM((s, LANES), F32),
                        pltpu.VMEM((s, LANES), F32),
                        pltpu.VMEM((s, SSD_BC), BF16),
                        pltpu.VMEM((s, SSD_WIDTH), BF16),
                        pltpu.VMEM((s, SSD_WIDTH), BF16),
                        pltpu.VMEM((s // SSD_CHUNK * 8, SSD_WIDTH), F32),
                        pltpu.VMEM((s // SSD_CHUNK * 8, SSD_WIDTH), F32),
                        pltpu.VMEM((s, SSD_WIDTH), F32),
                        pltpu.VMEM((SSD_BC, SSD_WIDTH), F32),
                        pltpu.VMEM((SSD_BC, SSD_WIDTH), F32)],
        compiler_params=_params("parallel"),
        name="ssd_mixer",
    )(z, xbc, dt_raw, conv_w, conv_b, dt_bias, a_log, d_skip, norm_g)


def _outproj_body(att_ref, ssd_ref, x_ref, g1_ref, sh_ref, sc_ref, ng_ref, w_ref, rw_ref,
                  x1_ref, h2_ref, aff_ref):
    na = att_ref.shape[2]
    mix = (jnp.dot(att_ref[0], w_ref[0:na, :], preferred_element_type=F32)
           + jnp.dot(ssd_ref[0], w_ref[na:, :], preferred_element_type=F32))
    x1 = x_ref[0] + g1_ref[0] * mix
    x1_ref[0] = x1
    h2 = _rms(x1) * ng_ref[...]
    h2 = h2 * (1.0 + sc_ref[0]) + sh_ref[0]
    h2_hi = h2.astype(BF16)
    h2_ref[0] = h2_hi
    h2_lo = (h2 - h2_hi.astype(F32)).astype(BF16)
    nt = lambda a, b: lax.dot_general(a, b, (((1,), (1,)), ((), ())), preferred_element_type=F32)
    logits = nt(rw_ref[0], h2_hi) + (nt(rw_ref[0], h2_lo) + nt(rw_ref[1], h2_hi))
    e = jnp.exp(logits - jnp.max(logits, axis=0, keepdims=True))
    aff_ref[0] = e / jnp.sum(e, axis=0, keepdims=True)


def _out_proj(att, ssd, x, g1, sh2, sc2, ng, w_out, rw_t, ts):
    bsz, s, d = x.shape
    tok = lambda wd: pl.BlockSpec((1, ts, wd), lambda b, t: (b, t, 0))
    vec = pl.BlockSpec((1, 1, d), lambda b, t: (b, 0, 0))
    full = lambda a: pl.BlockSpec(a.shape, lambda b, t: (0, 0))
    ne = rw_t.shape[1]
    return pl.pallas_call(
        _outproj_body,
        grid=(bsz, s // ts),
        in_specs=[tok(att.shape[2]), tok(ssd.shape[2]), tok(d), vec, vec, vec, full(ng), full(w_out),
                  pl.BlockSpec(rw_t.shape, lambda b, t: (0, 0, 0))],
        out_specs=[tok(d), tok(d), pl.BlockSpec((1, ne, ts), lambda b, t: (b, 0, t))],
        out_shape=[jax.ShapeDtypeStruct((bsz, s, d), F32),
                   jax.ShapeDtypeStruct((bsz, s, d), BF16),
                   jax.ShapeDtypeStruct((bsz, ne, s), F32)],
        compiler_params=_params("parallel", "arbitrary"),
        name="out_proj_router",
    )(att, ssd, x, g1, sh2, sc2, ng, w_out, rw_t)


def _topk_body(aff_ref, pos_ref, post_ref, ut_ref, *, cap):
    nrow, s = aff_ref.shape
    r = lax.broadcasted_iota(I32, (s, s), 0)
    c = lax.broadcasted_iota(I32, (s, s), 1)
    ut_ref[...] = jnp.where(r < c, 1.0, 0.0).astype(BF16)

    bits = pltpu.bitcast(aff_ref[...], I32)

    def count(mask):
        return jnp.sum(jnp.where(mask, 1, 0), axis=-1, keepdims=True)

    thr = jnp.zeros((nrow, 1), I32)
    for bit in range(30, -1, -1):
        cand = thr | (1 << bit)
        thr = jnp.where(count(bits >= cand) >= cap, cand, thr)
    gt = bits > thr
    eq = bits == thr
    need = cap - count(gt)
    eq_rank = jnp.dot(jnp.where(eq, 1.0, 0.0).astype(BF16), ut_ref[...], preferred_element_type=F32)
    sel = gt | (eq & (eq_rank < need.astype(F32)))
    slot = jnp.dot(jnp.where(sel, 1.0, 0.0).astype(BF16), ut_ref[...], preferred_element_type=F32)
    pos = jnp.where(sel, slot, -1.0)
    pos_ref[...] = pos.astype(I32)
    post_ref[...] = pos.T.astype(I32)


def _topk(aff_t, cap):
    bsz, ne, s = aff_t.shape
    nrow = bsz * ne
    pos, pos_t = pl.pallas_call(
        functools.partial(_topk_body, cap=cap),
        grid=(1,),
        in_specs=[pl.BlockSpec((nrow, s), lambda i: (0, 0))],
        out_specs=[pl.BlockSpec((nrow, s), lambda i: (0, 0)),
                   pl.BlockSpec((s, nrow), lambda i: (0, 0))],
        out_shape=[jax.ShapeDtypeStruct((nrow, s), I32),
                   jax.ShapeDtypeStruct((s, nrow), I32)],
        scratch_shapes=[pltpu.VMEM((s, s), BF16)],
        compiler_params=_params("arbitrary"),
        name="expert_topk",
    )(aff_t.reshape(nrow, s))
    return pos.reshape(bsz, ne, s), pos_t


def _gather_body(h2_ref, pos_ref, aff_ref, xg_ref, gc_ref, *, cap):
    ne, s = pos_ref.shape[1], pos_ref.shape[2]
    slot = lax.broadcasted_iota(I32, (cap, s), 0)
    h2 = h2_ref[0]
    for e in range(ne):
        hit = pos_ref[0, e:e + 1, :] == slot
        xg_ref[e] = jnp.dot(jnp.where(hit, 1.0, 0.0).astype(BF16), h2,
                            preferred_element_type=F32).astype(BF16)
        gate = jnp.sum(jnp.where(hit, aff_ref[0, e:e + 1, :], 0.0), axis=-1, keepdims=True)
        gc_ref[e] = jnp.broadcast_to(gate, (cap, LANES))


def _gather(h2, pos, aff_t, cap):
    bsz, s, d = h2.shape
    ne = pos.shape[1]
    return pl.pallas_call(
        functools.partial(_gather_body, cap=cap),
        grid=(bsz,),
        in_specs=[pl.BlockSpec((1, s, d), lambda b: (b, 0, 0)),
                  pl.BlockSpec((1, ne, s), lambda b: (b, 0, 0)),
                  pl.BlockSpec((1, ne, s), lambda b: (b, 0, 0))],
        out_specs=[pl.BlockSpec((ne, cap, d), lambda b: (0, b, 0)),
                   pl.BlockSpec((ne, cap, LANES), lambda b: (0, b, 0))],
        out_shape=[jax.ShapeDtypeStruct((ne, bsz * cap, d), BF16),
                   jax.ShapeDtypeStruct((ne, bsz * cap, LANES), F32)],
        compiler_params=_params("parallel"),
        name="moe_gather",
    )(h2, pos, aff_t)


def _ffn_body(xg_ref, gc_ref, wg_ref, wu_ref, wd_ref, y_ref, acc_ref):
    f = pl.program_id(1)

    @pl.when(f == 0)
    def _():
        acc_ref[...] = jnp.zeros_like(acc_ref)

    wg = wg_ref[0].astype(BF16)
    wu = wu_ref[0].astype(BF16)
    wd = wd_ref[0].astype(BF16)
    nq = 4
    rq = xg_ref.shape[1] // nq
    hidden = {}

    def gate_up(r):
        xg = xg_ref[0, r * rq:(r + 1) * rq, :]
        gate = jnp.dot(xg, wg, preferred_element_type=F32)
        up = jnp.dot(xg, wu, preferred_element_type=F32)
        hidden[r] = (_silu(gate) * up).astype(BF16)

    def down(r):
        acc_ref[r * rq:(r + 1) * rq, :] += jnp.dot(hidden.pop(r), wd, preferred_element_type=F32)

    gate_up(0)
    for r in range(nq):
        if r + 1 < nq:
            gate_up(r + 1)
        down(r)

    @pl.when(f == pl.num_programs(1) - 1)
    def _():
        m, d = acc_ref.shape
        gcol = gc_ref[0]
        y_ref[0] = (acc_ref[...] * jnp.concatenate([gcol] * (d // LANES), axis=-1)).astype(BF16)


def _ffn(xg, gc, w_gate, w_up, w_down, tf):
    ne, m, d = xg.shape
    ff = w_gate.shape[2]
    return pl.pallas_call(
        _ffn_body,
        grid=(ne, ff // tf),
        in_specs=[pl.BlockSpec((1, m, d), lambda e, f: (e, 0, 0)),
                  pl.BlockSpec((1, m, LANES), lambda e, f: (e, 0, 0)),
                  pl.BlockSpec((1, d, tf), lambda e, f: (e, 0, f)),
                  pl.BlockSpec((1, d, tf), lambda e, f: (e, 0, f)),
                  pl.BlockSpec((1, tf, d), lambda e, f: (e, f, 0))],
        out_specs=pl.BlockSpec((1, m, d), lambda e, f: (e, 0, 0)),
        out_shape=jax.ShapeDtypeStruct((ne, m, d), BF16),
        scratch_shapes=[pltpu.VMEM((m, d), F32)],
        compiler_params=_params("parallel", "arbitrary"),
        name="moe_ffn",
    )(xg, gc, w_gate, w_up, w_down)


def _scatter_body(post_ref, y_ref, x1_ref, g2_ref, ng_ref, o_ref, *, cap):
    ne = y_ref.shape[0]
    ts, nrow = post_ref.shape
    d = y_ref.shape[2]
    r = lax.broadcasted_iota(I32, (nrow, LANES), 0)
    c = lax.broadcasted_iota(I32, (nrow, LANES), 1)
    pick = jnp.where(r == pl.program_id(0) * ne + c, 1.0, 0.0).astype(BF16)
    post = jnp.dot(post_ref[...].astype(F32).astype(BF16), pick, preferred_element_type=F32)
    slot = lax.broadcasted_iota(I32, (ts, cap), 1).astype(F32)
    onehot = jnp.concatenate(
        [jnp.where(post[:, e:e + 1] == slot, 1.0, 0.0).astype(BF16) for e in range(ne)], axis=-1)
    moe = jnp.dot(onehot, y_ref[...].reshape(ne * cap, d), preferred_element_type=F32)
    x2 = x1_ref[0] + g2_ref[0] * moe
    o_ref[0] = _rms(x2) * ng_ref[...]


def _scatter(post, y, x1, g2, ng, cap, ts):
    bsz, s, d = x1.shape
    ne = y.shape[0]
    return pl.pallas_call(
        functools.partial(_scatter_body, cap=cap),
        grid=(bsz, s // ts),
        in_specs=[pl.BlockSpec((ts, post.shape[1]), lambda b, t: (t, 0)),
                  pl.BlockSpec((ne, cap, d), lambda b, t: (0, b, 0)),
                  pl.BlockSpec((1, ts, d), lambda b, t: (b, t, 0)),
                  pl.BlockSpec((1, 1, d), lambda b, t: (b, 0, 0)),
                  pl.BlockSpec((1, d), lambda b, t: (0, 0))],
        out_specs=pl.BlockSpec((1, ts, d), lambda b, t: (b, t, 0)),
        out_shape=jax.ShapeDtypeStruct((bsz, s, d), F32),
        compiler_params=_params("parallel", "arbitrary"),
        name="moe_scatter_final",
    )(post, y, x1, g2, ng)


def _pad_lanes(a, n=LANES):
    return jnp.pad(a, [(0, 0)] * (a.ndim - 1) + [(0, n - a.shape[-1])])


def kernel(x, c, ada_w, ada_b, norm_mix_g, norm_ffn_g, norm_final_g, w_in, lambda_q1, lambda_k1, lambda_q2, lambda_k2, attn_subln_g, rel_bias_table, conv_w, conv_b, dt_bias_f, dt_bias_b, A_log_f, A_log_b, D_skip, ssm_norm_g, w_out, router_w, w_gate, w_up, w_down):
    bsz, s, d = x.shape
    depth = ada_w.shape[0]
    ne = router_w.shape[2]
    cap = CAPACITY_FACTOR * s // ne
    att_w = ATT_HEADS * ATT_V_DIM
    nconv = conv_w.shape[3]
    splits = (att_w, att_w, att_w, SSD_WIDTH, nconv, 2 * SSD_HEADS)
    offs = [0]
    for wd in splits:
        offs.append(offs[-1] + wd)
    assert offs[-1] == w_in.shape[2]
    row = lambda a: a.reshape(1, -1)

    assert depth == 1, "the final RMSNorm is fused into the (single) layer's scatter kernel"
    for l in range(depth):
        lam_init = 0.8 - 0.6 * math.exp(-0.3 * l)
        mod = _ada_mod(c, ada_w[l], ada_b[l])
        sh1, sc1, g1, sh2, sc2, g2 = [m.reshape(bsz, 1, d) for m in jnp.split(mod, 6, axis=-1)]

        q, k, v_t, z, xbc, dt_raw = _in_proj(x, sh1, sc1, row(norm_mix_g[l]), w_in[l].astype(BF16), offs, ts=1024)

        att = _attention(q, k, v_t, rel_bias_table, row(lambda_q1[l]), row(lambda_k1[l]),
                         row(lambda_q2[l]), row(lambda_k2[l]), row(attn_subln_g[l]), lam_init, tq=2048)

        dt_bias = _pad_lanes(jnp.concatenate([dt_bias_f[l], dt_bias_b[l]]).reshape(1, -1))
        a_log = _pad_lanes(jnp.concatenate([A_log_f[l], A_log_b[l]]).reshape(1, -1))
        d_skip = jnp.repeat(D_skip[l], SSD_HEADDIM).reshape(1, -1)
        ssd = _ssd(z, xbc, dt_raw, conv_w[l].reshape(SSD_CONV, nconv), row(conv_b[l]), dt_bias, a_log,
                   d_skip, row(ssm_norm_g[l]))

        rw_t = router_w[l].T
        rw_hi = rw_t.astype(BF16)
        rw_split = jnp.stack([rw_hi, (rw_t - rw_hi.astype(F32)).astype(BF16)])
        x1, h2, aff_t = _out_proj(att, ssd, x, g1, sh2, sc2, row(norm_ffn_g[l]), w_out[l].astype(BF16),
                                  rw_split, ts=1024)
        pos, pos_t = _topk(aff_t, cap)
        xg, gc = _gather(h2, pos, aff_t, cap)
        y = _ffn(xg, gc, w_gate[l], w_up[l], w_down[l], tf=256)
        x = _scatter(pos_t, y, x1, g2, row(norm_final_g), cap, ts=1024)
    return x
```

```python
import functools
import math

import jax
import jax.numpy as jnp
from jax import lax
from jax.experimental import pallas as pl
from jax.experimental.pallas import tpu as pltpu

F32 = jnp.float32
BF16 = jnp.bfloat16
I32 = jnp.int32
HIGHEST = lax.Precision.HIGHEST

ATT_HEADS = 4
ATT_QK_DIM = 64
ATT_V_DIM = 128
N_BUCKETS = 32
MAX_DISTANCE = 128
SSD_HEADS = 8
SSD_HEADDIM = 64
SSD_GROUPS = 2
SSD_STATE = 64
SSD_CONV = 5
SSD_CHUNK = 128
SSD_WIDTH = SSD_HEADS * SSD_HEADDIM
SSD_BC = SSD_GROUPS * SSD_STATE
N_EXPERTS = 16
CAPACITY_FACTOR = 2
EPS = 1e-6
LANES = 128
NEG_BIG = -1e30
LOG2E = math.log2(math.e)

VMEM_LIMIT = 56 * 1024 * 1024


def _params(*semantics):
    return pltpu.CompilerParams(dimension_semantics=semantics, vmem_limit_bytes=VMEM_LIMIT)


def _silu(v):
    return v * jax.nn.sigmoid(v)


def _rms(v, eps=EPS):
    return v * lax.rsqrt(jnp.mean(v * v, axis=-1, keepdims=True) + eps)


def _split_bf16(v):
    hi = v.astype(BF16)
    return hi, (v - hi.astype(F32)).astype(BF16)


def _mod_body(c_ref, w_ref, b_ref, o_ref):
    a_hi, a_lo = _split_bf16(_silu(c_ref[...]))
    w_hi, w_lo = _split_bf16(w_ref[...])
    dot = functools.partial(jnp.dot, preferred_element_type=F32)
    o_ref[...] = dot(a_hi, w_hi) + (dot(a_lo, w_hi) + dot(a_hi, w_lo)) + b_ref[...]


def _ada_mod(c, w, b):
    bsz, d = c.shape
    n = w.shape[1]
    tn = n // 4
    return pl.pallas_call(
        _mod_body,
        grid=(n // tn,),
        in_specs=[pl.BlockSpec((bsz, d), lambda j: (0, 0)),
                  pl.BlockSpec((d, tn), lambda j: (0, j)),
                  pl.BlockSpec((1, tn), lambda j: (0, j))],
        out_specs=pl.BlockSpec((bsz, tn), lambda j: (0, j)),
        out_shape=jax.ShapeDtypeStruct((bsz, n), F32),
        compiler_params=_params("arbitrary"),
        name="ada_mod",
    )(c, w, b.reshape(1, n))


def _inproj_body(x_ref, sh_ref, sc_ref, g_ref, w_ref, q_ref, k_ref, vt_ref, z_ref, xbc_ref, dt_ref, wvt_ref,
                 *, offs, qscale):
    @pl.when((pl.program_id(0) == 0) & (pl.program_id(1) == 0))
    def _():
        wvt_ref[...] = w_ref[:, offs[2]:offs[3]].astype(F32).T.astype(BF16)

    h = _rms(x_ref[0]) * g_ref[...]
    h = (h * (1.0 + sc_ref[0]) + sh_ref[0]).astype(BF16)

    def proj(i):
        return jnp.dot(h, w_ref[:, offs[i]:offs[i + 1]], preferred_element_type=F32)

    q_ref[0] = (proj(0) * qscale).astype(BF16)
    k_ref[0] = proj(1).astype(BF16)
    vt_ref[0, 0] = lax.dot_general(wvt_ref[...], h, (((1,), (1,)), ((), ())),
                                   preferred_element_type=F32).astype(BF16)
    z_ref[0] = proj(3)
    xbc_ref[0] = proj(4)
    dt = proj(5)
    dt_ref[0] = jnp.concatenate([dt, jnp.zeros((dt.shape[0], LANES - dt.shape[1]), F32)], axis=1)


def _in_proj(x, sh, sc, g, w, offs, ts):
    bsz, s, d = x.shape
    wq, wk, wv, wz, wxbc = [offs[i + 1] - offs[i] for i in range(5)]
    tok = lambda wd: pl.BlockSpec((1, ts, wd), lambda b, t: (b, t, 0))
    vec = pl.BlockSpec((1, 1, d), lambda b, t: (b, 0, 0))
    tok_out = lambda wd, dt: (tok(wd), jax.ShapeDtypeStruct((bsz, s, wd), dt))
    outs = [tok_out(wq, BF16), tok_out(wk, BF16),
            (pl.BlockSpec((1, 1, wv, ts), lambda b, t: (b, t, 0, 0)), jax.ShapeDtypeStruct((bsz, s // ts, wv, ts), BF16)),
            tok_out(wz, F32), tok_out(wxbc, F32), tok_out(LANES, F32)]
    return pl.pallas_call(
        functools.partial(_inproj_body, offs=tuple(offs), qscale=ATT_QK_DIM ** -0.5 * LOG2E),
        grid=(bsz, s // ts),
        in_specs=[tok(d), vec, vec,
                  pl.BlockSpec((1, d), lambda b, t: (0, 0)),
                  pl.BlockSpec(w.shape, lambda b, t: (0, 0))],
        out_specs=[o[0] for o in outs],
        out_shape=[o[1] for o in outs],
        scratch_shapes=[pltpu.VMEM((wv, d), BF16)],
        compiler_params=_params("arbitrary", "arbitrary"),
        name="in_proj",
    )(x, sh, sc, g, w)


def _t5_bucket(rel):
    n = jnp.abs(rel)
    large = jnp.full(rel.shape, 8, I32)
    for t in (12, 16, 23, 32, 46, 64, 91):
        large = large + jnp.where(n >= t, 1, 0)
    return jnp.where(rel > 0, 16, 0) + jnp.where(n < 8, n, large)


def _attn_body(tbl_ref, lq1_ref, lk1_ref, lq2_ref, lk2_ref, g_ref, q_ref, k_ref, vt_ref, o_ref,
               bias_ref, acc_ref, *, lam_init, tk):
    h = pl.program_id(0)
    qi = pl.program_id(1)
    s, tq = bias_ref.shape
    qw = 2 * LANES
    assert tq == s and tk == qw, "near / far tile classification below is static for one full-width step"
    near_w = 3 * qw
    ring = pl.next_power_of_2(tk + near_w - 1)

    @pl.when(pl.program_id(2) == 0)
    def _():
        j = lax.broadcasted_iota(I32, (8, ring), 1)
        for kt in range(s // tk):
            w0 = min(max((kt - 1) * qw, 0), tq - near_w)
            bucket = _t5_bucket((kt * tk + tk - 1) - w0 - j)
            w = jnp.zeros((8, ring), F32)
            for bb in range(N_BUCKETS):
                w = jnp.where(bucket == bb, tbl_ref[bb, h] * LOG2E, w)
            big = jnp.broadcast_to(w[0:1], (tk, ring))
            rolled = pltpu.roll(big, ring - (tk - 1), axis=1, stride=1, stride_axis=0)
            bias_ref[kt * tk:(kt + 1) * tk, w0:w0 + near_w] = rolled[:, :near_w]

    lam = (jnp.exp(jnp.sum(lq1_ref[...] * lk1_ref[...], axis=-1, keepdims=True))
           - jnp.exp(jnp.sum(lq2_ref[...] * lk2_ref[...], axis=-1, keepdims=True)) + lam_init)
    q = q_ref[0]
    k_all = k_ref[0]
    lane = lax.broadcasted_iota(I32, q.shape, 1)
    qms = [jnp.where((lane >= m * ATT_QK_DIM) & (lane < (m + 1) * ATT_QK_DIM), q, jnp.zeros_like(q))
           for m in range(2)]
    nqc = tq // qw

    far = jnp.full((8, LANES), MAX_DISTANCE, I32)
    side_const = []
    for rel in (-far, far):
        bucket = _t5_bucket(rel)
        cst = jnp.zeros((8, LANES), F32)
        for bb in range(N_BUCKETS):
            cst = jnp.where(bucket == bb, tbl_ref[bb, h] * LOG2E, cst)
        c_hi = cst.astype(BF16)
        rem = cst - c_hi.astype(F32)
        c_mid = rem.astype(BF16)
        c_lo = (rem - c_mid.astype(F32)).astype(BF16)
        side_const.append([jnp.broadcast_to(p[0:1], k_all.shape) for p in (c_hi, c_mid, c_lo)])
    q_aug, k_side = [], []
    for m in range(2):
        spare = (1 - m) * ATT_QK_DIM
        hot = [jnp.where(lane == spare + i, 1.0, 0.0).astype(BF16) for i in range(3)]
        keep = jnp.where((lane >= spare) & (lane < spare + 3), 0.0, 1.0).astype(BF16)
        q_aug.append(qms[m] + (hot[0] + hot[1] + hot[2]))
        k_kept = k_all * keep
        k_side.append([k_kept + (hot[0] * parts[0] + hot[1] * parts[1] + hot[2] * parts[2]) for parts in side_const])
    nkb, kb = vt_ref.shape[1], vt_ref.shape[3]
    ones_rows = 16

    streams = [(m, c) for m in range(2) for c in range(nqc)]
    run_max = {st: jnp.full((1, qw), NEG_BIG, F32) for st in streams}
    nt = s // tk
    items = [(t, st) for t in range(nt) for st in streams]
    pending = {}
    depth = min(7, len(streams) - 1)

    def logits_part(item):
        t, (m, c) = item
        rows, cols = slice(t * tk, (t + 1) * tk), slice(c * qw, (c + 1) * qw)
        nt_dot = lambda a, b: lax.dot_general(a, b, (((1,), (1,)), ((), ())), preferred_element_type=F32)
        max_off = (t + 1) * tk - 1 - c * qw
        min_off = t * tk - ((c + 1) * qw - 1)
        if max_off <= -MAX_DISTANCE:
            logit = nt_dot(k_side[m][0][rows, :], q_aug[m][cols, :])
        elif min_off >= MAX_DISTANCE:
            logit = nt_dot(k_side[m][1][rows, :], q_aug[m][cols, :])
        else:
            logit = bias_ref[rows, cols] + nt_dot(k_all[rows, :], qms[m][cols, :])
        mx = logit[0:8, :]
        for r in range(1, tk // 8):
            mx = jnp.maximum(mx, logit[r * 8:(r + 1) * 8, :])
        pending[item] = (logit, jnp.max(mx, axis=0, keepdims=True))

    def pv_part(item):
        t, st = item
        m, c = st
        cols = slice(c * qw, (c + 1) * qw)
        logit, tile_max = pending.pop(item)
        m_old = run_max[st]
        m_new = jnp.maximum(m_old, tile_max)
        run_max[st] = m_new
        probs = jnp.exp2(logit - m_new).astype(BF16)
        blk, off = (t * tk) // kb, (t * tk) % kb
        vt_ext = jnp.concatenate([vt_ref[0, blk][:, off:off + tk], jnp.ones((ones_rows, tk), BF16)], axis=0)
        part = jnp.dot(vt_ext, probs, preferred_element_type=F32)
        if t == 0:
            acc_ref[m, :, cols] = part
        else:
            acc_ref[m, :, cols] = acc_ref[m, :, cols] * jnp.exp2(m_old - m_new) + part

    for i in range(len(items) + depth):
        if i < len(items):
            logits_part(items[i])
        if i >= depth:
            pv_part(items[i - depth])
    outs = [acc_ref[m, :ATT_V_DIM, :] / acc_ref[m, ATT_V_DIM:ATT_V_DIM + 1, :] for m in range(2)]
    att = outs[0] - lam * outs[1]
    att = att * lax.rsqrt(jnp.mean(att * att, axis=0, keepdims=True) + EPS)
    o_ref[0] = (att * (g_ref[...] * (1.0 - lam_init))).T.astype(BF16)


def _attention(q, k, v_t, tbl, lq1, lk1, lq2, lk2, subln_g, lam_init, tq):
    bsz, s, _ = q.shape
    nkb, kb = v_t.shape[1], v_t.shape[3]
    small = lambda n: pl.BlockSpec((1, n), lambda h, i, b: (0, 0))
    return pl.pallas_call(
        functools.partial(_attn_body, lam_init=lam_init, tk=2 * LANES),
        grid=(ATT_HEADS, s // tq, bsz),
        in_specs=[pl.BlockSpec(memory_space=pltpu.SMEM),
                  small(ATT_QK_DIM), small(ATT_QK_DIM), small(ATT_QK_DIM), small(ATT_QK_DIM),
                  pl.BlockSpec((ATT_V_DIM, 1), lambda h, i, b: (0, 0)),
                  pl.BlockSpec((1, tq, ATT_V_DIM), lambda h, i, b: (b, i, h)),
                  pl.BlockSpec((1, s, ATT_V_DIM), lambda h, i, b: (b, 0, h)),
                  pl.BlockSpec((1, nkb, ATT_V_DIM, kb), lambda h, i, b: (b, 0, h, 0))],
        out_specs=pl.BlockSpec((1, tq, ATT_V_DIM), lambda h, i, b: (b, i, h)),
        out_shape=jax.ShapeDtypeStruct((bsz, s, ATT_HEADS * ATT_V_DIM), BF16),
        scratch_shapes=[pltpu.VMEM((s, tq), F32),
                        pltpu.VMEM((2, ATT_V_DIM + 16, tq), F32)],
        compiler_params=_params("parallel", "parallel", "arbitrary"),
        name="diff_attention",
    )(tbl, lq1, lk1, lq2, lk2, subln_g.reshape(ATT_V_DIM, 1), q, k, v_t)


def _head_selector(off):
    r = lax.broadcasted_iota(I32, (LANES, SSD_WIDTH), 0)
    c = lax.broadcasted_iota(I32, (LANES, SSD_WIDTH), 1)
    return jnp.where(r - off == c // SSD_HEADDIM, 1.0, 0.0).astype(BF16)


def _expand_heads(cols, selector):
    hi = cols.astype(BF16)
    lo = (cols - hi.astype(F32)).astype(BF16)
    return (jnp.dot(hi, selector, preferred_element_type=F32)
            + jnp.dot(lo, selector, preferred_element_type=F32))


def _ssd_body(z_ref, xbc_ref, dtr_ref, cw_ref, cb_ref, dtb_ref, alog_ref, dskip_ref, ng_ref, o_ref,
              xp_ref, xc_ref, dt_ref, da_ref, cs_ref, cst_ref, ecst_ref, dtt_ref, bt_ref, xdf_ref, xdb_ref,
              dtotf_ref, dtotb_ref, y_ref, hf_ref, hb_ref):
    s = xbc_ref.shape[1]
    ch = SSD_CHUNK
    assert ch == LANES
    nc = s // ch
    pad = 8
    nconv = xbc_ref.shape[2]

    xp_ref[0:pad, :] = jnp.zeros((pad, nconv), F32)
    xp_ref[pad + s:pad + s + pad, :] = jnp.zeros((pad, nconv), F32)
    xp_ref[pad:pad + s, :] = xbc_ref[0]

    def conv_chunk(c, carry):
        r0 = pl.multiple_of(c * ch, ch)
        nwin = ch + 2 * pad
        win = xp_ref[pl.ds(r0, nwin), :]
        acc = jnp.broadcast_to(cb_ref[...], (ch, nconv))
        for j in range(SSD_CONV):
            shift = (SSD_CONV // 2 - j) % nwin
            tap = win if shift == 0 else pltpu.roll(win, shift, axis=0)
            acc = acc + cw_ref[j:j + 1, :] * tap[pad:pad + ch, :]
        xc_ref[pl.ds(r0, ch), :] = _silu(acc)
        return carry

    lax.fori_loop(0, nc, conv_chunk, 0)

    lane = lax.broadcasted_iota(I32, (s, LANES), 1)
    pre = dtr_ref[0] + dtb_ref[...]
    dt = jnp.maximum(pre, 0.0) + jnp.log1p(jnp.exp(-jnp.abs(pre)))
    dt = jnp.where(lane < 2 * SSD_HEADS, dt, 0.0)
    dt_ref[...] = dt
    da_ref[...] = dt * (-jnp.exp(alog_ref[...]))

    row = lax.broadcasted_iota(I32, (ch, ch), 0)
    col = lax.broadcasted_iota(I32, (ch, ch), 1)
    tri = jnp.where(col <= row, 1.0, 0.0).astype(F32)
    lower = col <= row
    upper = col >= row
    lane_c = lax.broadcasted_iota(I32, (ch, LANES), 1)
    st_rows = lax.broadcasted_iota(I32, (SSD_BC, SSD_WIDTH), 0) // SSD_STATE
    st_cols = lax.broadcasted_iota(I32, (SSD_BC, SSD_WIDTH), 1) // (SSD_WIDTH // SSD_GROUPS)
    same_group = st_rows == st_cols
    sel_f = _head_selector(0)
    sel_b = _head_selector(SSD_HEADS)

    hf_ref[...] = jnp.zeros_like(hf_ref)
    hb_ref[...] = jnp.zeros_like(hb_ref)

    def wide(x):
        return jnp.concatenate([x[c * ch:(c + 1) * ch, :] for c in range(nc)], axis=1)

    da_wide = wide(da_ref[...])
    cs_wide = jnp.dot(tri, da_wide, precision=HIGHEST, preferred_element_type=F32)
    for c in range(nc):
        cs_ref[c * ch:(c + 1) * ch, :] = cs_wide[:, c * LANES:(c + 1) * LANES]
    cst_ref[...] = cs_wide.T
    ecst_ref[...] = (cs_wide - da_wide).T
    dtt_ref[...] = wide(dt_ref[...]).T
    bt_ref[...] = wide(xc_ref[:, SSD_WIDTH:SSD_WIDTH + SSD_BC]).T.astype(BF16)

    def prep_chunk(c, carry):
        r0 = pl.multiple_of(c * ch, ch)
        r8 = pl.multiple_of(c * 8, 8)
        da = da_ref[pl.ds(r0, ch), :]
        dtc = dt_ref[pl.ds(r0, ch), :]
        cs = cs_ref[pl.ds(r0, ch), :]
        tot = cs[ch - 1:ch, :]
        xs = xc_ref[pl.ds(r0, ch), 0:SSD_WIDTH]
        tot8 = jnp.broadcast_to(jnp.exp(tot), (8, LANES))
        ff = _expand_heads(jnp.concatenate([dtc * jnp.exp(tot - cs), tot8], axis=0), sel_f)
        xdf_ref[pl.ds(r0, ch), :] = (xs * ff[0:ch]).astype(BF16)
        dtotf_ref[pl.ds(r8, 8), :] = ff[ch:ch + 8]
        fb = _expand_heads(jnp.concatenate([dtc * jnp.exp(cs - da), tot8], axis=0), sel_b)
        xdb_ref[pl.ds(r0, ch), :] = (xs * fb[0:ch]).astype(BF16)
        dtotb_ref[pl.ds(r8, 8), :] = fb[ch:ch + 8]
        return carry

    lax.fori_loop(0, nc, prep_chunk, 0, unroll=2)

    def fwd_chunk(c, carry):
        r0 = pl.multiple_of(c * ch, ch)
        da = da_ref[pl.ds(r0, ch), :]
        cs = cs_ref[pl.ds(r0, ch), :]
        ecs = cs - da
        nh2 = 2 * SSD_HEADS
        cs_t = cst_ref[pl.ds(r0, nh2), :]
        ecs_t = ecst_ref[pl.ds(r0, nh2), :]
        dt_t = dtt_ref[pl.ds(r0, nh2), :]
        xs = xc_ref[pl.ds(r0, ch), 0:SSD_WIDTH]
        cm = xc_ref[pl.ds(r0, ch), SSD_WIDTH + SSD_BC:SSD_WIDTH + 2 * SSD_BC]
        bt = bt_ref[pl.ds(r0, ch), :]
        cmb = cm.astype(BF16)
        xsb = xs.astype(BF16)

        hf = hf_ref[...]
        y_ref[pl.ds(r0, ch), :] = (jnp.dot(cmb, hf.astype(BF16), preferred_element_type=F32)
                                   * _expand_heads(jnp.exp(cs), sel_f))
        upd = jnp.dot(bt, xdf_ref[pl.ds(r0, ch), :], preferred_element_type=F32)
        hf_ref[...] = hf * dtotf_ref[pl.ds(pl.multiple_of(c * 8, 8), 1), :] + jnp.where(same_group, upd, 0.0)

        g_mats = []
        for g in range(SSD_GROUPS):
            cg = jnp.where(lane_c // SSD_STATE == g, cmb, jnp.zeros_like(cmb))
            g_mats.append(jnp.dot(cg, bt, preferred_element_type=F32))
        for pair in range(SSD_HEADS // 2):
            xpair = xsb[:, pair * LANES:(pair + 1) * LANES]
            ypair = jnp.zeros((ch, LANES), F32)
            for sub in range(2):
                hh = 2 * pair + sub
                g = hh // (SSD_HEADS // SSD_GROUPS)
                d_f = cs[:, hh:hh + 1] - cs_t[hh:hh + 1, :]
                d_b = ecs_t[SSD_HEADS + hh:SSD_HEADS + hh + 1, :] - ecs[:, SSD_HEADS + hh:SSD_HEADS + hh + 1]
                decay = jnp.exp(jnp.where(lower, d_f, d_b))
                wgt = (jnp.where(lower, dt_t[hh:hh + 1, :], 0.0)
                       + jnp.where(upper, dt_t[SSD_HEADS + hh:SSD_HEADS + hh + 1, :], 0.0))
                m = (g_mats[g] * decay * wgt).astype(BF16)
                xh = jnp.where(lane_c // SSD_HEADDIM == sub, xpair, jnp.zeros_like(xpair))
                ypair = ypair + jnp.dot(m, xh, preferred_element_type=F32)
            y_ref[pl.ds(r0, ch), pair * LANES:(pair + 1) * LANES] += ypair
        return carry

    lax.fori_loop(0, nc, fwd_chunk, 0, unroll=4)

    def bwd_chunk(i, carry):
        c = nc - 1 - i
        r0 = pl.multiple_of(c * ch, ch)
        da = da_ref[pl.ds(r0, ch), :]
        cs = cs_ref[pl.ds(r0, ch), :]
        tot = cs[ch - 1:ch, :]
        ecs = cs - da
        xs = xc_ref[pl.ds(r0, ch), 0:SSD_WIDTH]
        cm = xc_ref[pl.ds(r0, ch), SSD_WIDTH + SSD_BC:SSD_WIDTH + 2 * SSD_BC]
        bt = bt_ref[pl.ds(r0, ch), :]

        hb = hb_ref[...]
        y_off = (jnp.dot(cm.astype(BF16), hb.astype(BF16), preferred_element_type=F32)
                 * _expand_heads(jnp.exp(tot - ecs), sel_b))
        upd = jnp.dot(bt, xdb_ref[pl.ds(r0, ch), :], preferred_element_type=F32)
        hb_ref[...] = hb * dtotb_ref[pl.ds(pl.multiple_of(c * 8, 8), 1), :] + jnp.where(same_group, upd, 0.0)

        y = y_ref[pl.ds(r0, ch), :] + y_off + dskip_ref[...] * xs
        y = y * _silu(z_ref[0, pl.ds(r0, ch), :])
        gw = SSD_WIDTH // SSD_GROUPS
        yn = jnp.concatenate([_rms(y[:, g * gw:(g + 1) * gw]) for g in range(SSD_GROUPS)], axis=-1)
        o_ref[0, pl.ds(r0, ch), :] = (yn * ng_ref[...]).astype(BF16)
        return carry

    lax.fori_loop(0, nc, bwd_chunk, 0, unroll=4)


def _ssd(z, xbc, dt_raw, conv_w, conv_b, dt_bias, a_log, d_skip, norm_g):
    bsz, s, nconv = xbc.shape
    tok = lambda wd: pl.BlockSpec((1, s, wd), lambda b: (b, 0, 0))
    full = lambda a: pl.BlockSpec(a.shape, lambda b: (0, 0))
    return pl.pallas_call(
        _ssd_body,
        grid=(bsz,),
        in_specs=[tok(SSD_WIDTH), tok(nconv), tok(LANES), full(conv_w), full(conv_b), full(dt_bias),
                  full(a_log), full(d_skip), full(norm_g)],
        out_specs=tok(SSD_WIDTH),
        out_shape=jax.ShapeDtypeStruct((bsz, s, SSD_WIDTH), BF16),
        scratch_shapes=[pltpu.VMEM((s + 16, nconv), F32),
                        pltpu.VMEM((s, nconv), F32),
                        pltpu.VMEM((s, LANES), F32),
                        pltpu.VMEM((s, LANES), F32),
                        pltpu.VMEM((s, LANES), F32),
                        pltpu.VMEM((s, LANES), F32),
                        pltpu.VMEM((s, LANES), F32),
                        pltpu.VMEM((s, LANES), F32),
                        pltpu.VMEM((s, SSD_BC), BF16),
                        pltpu.VMEM((s, SSD_WIDTH), BF16),
                        pltpu.VMEM((s, SSD_WIDTH), BF16),
                        pltpu.VMEM((s // SSD_CHUNK * 8, SSD_WIDTH), F32),
                        pltpu.VMEM((s // SSD_CHUNK * 8, SSD_WIDTH), F32),
                        pltpu.VMEM((s, SSD_WIDTH), F32),
                        pltpu.VMEM((SSD_BC, SSD_WIDTH), F32),
                        pltpu.VMEM((SSD_BC, SSD_WIDTH), F32)],
        compiler_params=_params("parallel"),
        name="ssd_mixer",
    )(z, xbc, dt_raw, conv_w, conv_b, dt_bias, a_log, d_skip, norm_g)


def _outproj_body(att_ref, ssd_ref, x_ref, g1_ref, sh_ref, sc_ref, ng_ref, w_ref, rw_ref,
                  x1_ref, h2_ref, aff_ref):
    na = att_ref.shape[2]
    mix = (jnp.dot(att_ref[0], w_ref[0:na, :], preferred_element_type=F32)
           + jnp.dot(ssd_ref[0], w_ref[na:, :], preferred_element_type=F32))
    x1 = x_ref[0] + g1_ref[0] * mix
    x1_ref[0] = x1
    h2 = _rms(x1) * ng_ref[...]
    h2 = h2 * (1.0 + sc_ref[0]) + sh_ref[0]
    h2_hi = h2.astype(BF16)
    h2_ref[0] = h2_hi
    h2_lo = (h2 - h2_hi.astype(F32)).astype(BF16)
    nt = lambda a, b: lax.dot_general(a, b, (((1,), (1,)), ((), ())), preferred_element_type=F32)
    logits = nt(rw_ref[0], h2_hi) + (nt(rw_ref[0], h2_lo) + nt(rw_ref[1], h2_hi))
    e = jnp.exp(logits - jnp.max(logits, axis=0, keepdims=True))
    aff_ref[0] = e / jnp.sum(e, axis=0, keepdims=True)


def _out_proj(att, ssd, x, g1, sh2, sc2, ng, w_out, rw_t, ts):
    bsz, s, d = x.shape
    tok = lambda wd: pl.BlockSpec((1, ts, wd), lambda b, t: (b, t, 0))
    vec = pl.BlockSpec((1, 1, d), lambda b, t: (b, 0, 0))
    full = lambda a: pl.BlockSpec(a.shape, lambda b, t: (0, 0))
    ne = rw_t.shape[1]
    return pl.pallas_call(
        _outproj_body,
        grid=(bsz, s // ts),
        in_specs=[tok(att.shape[2]), tok(ssd.shape[2]), tok(d), vec, vec, vec, full(ng), full(w_out),
                  pl.BlockSpec(rw_t.shape, lambda b, t: (0, 0, 0))],
        out_specs=[tok(d), tok(d), pl.BlockSpec((1, ne, ts), lambda b, t: (b, 0, t))],
        out_shape=[jax.ShapeDtypeStruct((bsz, s, d), F32),
                   jax.ShapeDtypeStruct((bsz, s, d), BF16),
                   jax.ShapeDtypeStruct((bsz, ne, s), F32)],
        compiler_params=_params("parallel", "arbitrary"),
        name="out_proj_router",
    )(att, ssd, x, g1, sh2, sc2, ng, w_out, rw_t)


def _topk_body(aff_ref, pos_ref, post_ref, ut_ref, *, cap):
    nrow, s = aff_ref.shape
    r = lax.broadcasted_iota(I32, (s, s), 0)
    c = lax.broadcasted_iota(I32, (s, s), 1)
    ut_ref[...] = jnp.where(r < c, 1.0, 0.0).astype(BF16)

    bits = pltpu.bitcast(aff_ref[...], I32)

    def count(mask):
        return jnp.sum(jnp.where(mask, 1, 0), axis=-1, keepdims=True)

    thr = jnp.zeros((nrow, 1), I32)
    for bit in range(30, -1, -1):
        cand = thr | (1 << bit)
        thr = jnp.where(count(bits >= cand) >= cap, cand, thr)
    gt = bits > thr
    eq = bits == thr
    need = cap - count(gt)
    eq_rank = jnp.dot(jnp.where(eq, 1.0, 0.0).astype(BF16), ut_ref[...], preferred_element_type=F32)
    sel = gt | (eq & (eq_rank < need.astype(F32)))
    slot = jnp.dot(jnp.where(sel, 1.0, 0.0).astype(BF16), ut_ref[...], preferred_element_type=F32)
    pos = jnp.where(sel, slot, -1.0)
    pos_ref[...] = pos.astype(I32)
    post_ref[...] = pos.T.astype(I32)


def _topk(aff_t, cap):
    bsz, ne, s = aff_t.shape
    nrow = bsz * ne
    pos, pos_t = pl.pallas_call(
        functools.partial(_topk_body, cap=cap),
        grid=(1,),
        in_specs=[pl.BlockSpec((nrow, s), lambda i: (0, 0))],
        out_specs=[pl.BlockSpec((nrow, s), lambda i: (0, 0)),
                   pl.BlockSpec((s, nrow), lambda i: (0, 0))],
        out_shape=[jax.ShapeDtypeStruct((nrow, s), I32),
                   jax.ShapeDtypeStruct((s, nrow), I32)],
        scratch_shapes=[pltpu.VMEM((s, s), BF16)],
        compiler_params=_params("arbitrary"),
        name="expert_topk",
    )(aff_t.reshape(nrow, s))
    return pos.reshape(bsz, ne, s), pos_t


def _gather_body(h2_ref, pos_ref, aff_ref, xg_ref, gc_ref, *, cap):
    ne, s = pos_ref.shape[1], pos_ref.shape[2]
    slot = lax.broadcasted_iota(I32, (cap, s), 0)
    h2 = h2_ref[0]
    for e in range(ne):
        hit = pos_ref[0, e:e + 1, :] == slot
        xg_ref[e] = jnp.dot(jnp.where(hit, 1.0, 0.0).astype(BF16), h2,
                            preferred_element_type=F32).astype(BF16)
        gate = jnp.sum(jnp.where(hit, aff_ref[0, e:e + 1, :], 0.0), axis=-1, keepdims=True)
        gc_ref[e] = jnp.broadcast_to(gate, (cap, LANES))


def _gather(h2, pos, aff_t, cap):
    bsz, s, d = h2.shape
    ne = pos.shape[1]
    return pl.pallas_call(
        functools.partial(_gather_body, cap=cap),
        grid=(bsz,),
        in_specs=[pl.BlockSpec((1, s, d), lambda b: (b, 0, 0)),
                  pl.BlockSpec((1, ne, s), lambda b: (b, 0, 0)),
                  pl.BlockSpec((1, ne, s), lambda b: (b, 0, 0))],
        out_specs=[pl.BlockSpec((ne, cap, d), lambda b: (0, b, 0)),
                   pl.BlockSpec((ne, cap, LANES), lambda b: (0, b, 0))],
        out_shape=[jax.ShapeDtypeStruct((ne, bsz * cap, d), BF16),
                   jax.ShapeDtypeStruct((ne, bsz * cap, LANES), F32)],
        compiler_params=_params("parallel"),
        name="moe_gather",
    )(h2, pos, aff_t)


def _ffn_body(xg_ref, gc_ref, wg_ref, wu_ref, wd_ref, y_ref, acc_ref):
    f = pl.program_id(1)

    @pl.when(f == 0)
    def _():
        acc_ref[...] = jnp.zeros_like(acc_ref)

    wg = wg_ref[0].astype(BF16)
    wu = wu_ref[0].astype(BF16)
    wd = wd_ref[0].astype(BF16)
    nq = 4
    rq = xg_ref.shape[1] // nq
    hidden = {}

    def gate_up(r):
        xg = xg_ref[0, r * rq:(r + 1) * rq, :]
        gate = jnp.dot(xg, wg, preferred_element_type=F32)
        up = jnp.dot(xg, wu, preferred_element_type=F32)
        hidden[r] = (_silu(gate) * up).astype(BF16)

    def down(r):
        acc_ref[r * rq:(r + 1) * rq, :] += jnp.dot(hidden.pop(r), wd, preferred_element_type=F32)

    gate_up(0)
    for r in range(nq):
        if r + 1 < nq:
            gate_up(r + 1)
        down(r)

    @pl.when(f == pl.num_programs(1) - 1)
    def _():
        m, d = acc_ref.shape
        gcol = gc_ref[0]
        y_ref[0] = (acc_ref[...] * jnp.concatenate([gcol] * (d // LANES), axis=-1)).astype(BF16)


def _ffn(xg, gc, w_gate, w_up, w_down, tf):
    ne, m, d = xg.shape
    ff = w_gate.shape[2]
    return pl.pallas_call(
        _ffn_body,
        grid=(ne, ff // tf),
        in_specs=[pl.BlockSpec((1, m, d), lambda e, f: (e, 0, 0)),
                  pl.BlockSpec((1, m, LANES), lambda e, f: (e, 0, 0)),
                  pl.BlockSpec((1, d, tf), lambda e, f: (e, 0, f)),
                  pl.BlockSpec((1, d, tf), lambda e, f: (e, 0, f)),
                  pl.BlockSpec((1, tf, d), lambda e, f: (e, f, 0))],
        out_specs=pl.BlockSpec((1, m, d), lambda e, f: (e, 0, 0)),
        out_shape=jax.ShapeDtypeStruct((ne, m, d), BF16),
        scratch_shapes=[pltpu.VMEM((m, d), F32)],
        compiler_params=_params("parallel", "arbitrary"),
        name="moe_ffn",
    )(xg, gc, w_gate, w_up, w_down)


def _scatter_body(post_ref, y_ref, x1_ref, g2_ref, ng_ref, o_ref, *, cap):
    ne = y_ref.shape[0]
    ts, nrow = post_ref.shape
    d = y_ref.shape[2]
    r = lax.broadcasted_iota(I32, (nrow, LANES), 0)
    c = lax.broadcasted_iota(I32, (nrow, LANES), 1)
    pick = jnp.where(r == pl.program_id(0) * ne + c, 1.0, 0.0).astype(BF16)
    post = jnp.dot(post_ref[...].astype(F32).astype(BF16), pick, preferred_element_type=F32)
    slot = lax.broadcasted_iota(I32, (ts, cap), 1).astype(F32)
    onehot = jnp.concatenate(
        [jnp.where(post[:, e:e + 1] == slot, 1.0, 0.0).astype(BF16) for e in range(ne)], axis=-1)
    moe = jnp.dot(onehot, y_ref[...].reshape(ne * cap, d), preferred_element_type=F32)
    x2 = x1_ref[0] + g2_ref[0] * moe
    o_ref[0] = _rms(x2) * ng_ref[...]


def _scatter(post, y, x1, g2, ng, cap, ts):
    bsz, s, d = x1.shape
    ne = y.shape[0]
    return pl.pallas_call(
        functools.partial(_scatter_body, cap=cap),
        grid=(bsz, s // ts),
        in_specs=[pl.BlockSpec((ts, post.shape[1]), lambda b, t: (t, 0)),
                  pl.BlockSpec((ne, cap, d), lambda b, t: (0, b, 0)),
                  pl.BlockSpec((1, ts, d), lambda b, t: (b, t, 0)),
                  pl.BlockSpec((1, 1, d), lambda b, t: (b, 0, 0)),
                  pl.BlockSpec((1, d), lambda b, t: (0, 0))],
        out_specs=pl.BlockSpec((1, ts, d), lambda b, t: (b, t, 0)),
        out_shape=jax.ShapeDtypeStruct((bsz, s, d), F32),
        compiler_params=_params("parallel", "arbitrary"),
        name="moe_scatter_final",
    )(post, y, x1, g2, ng)


def _pad_lanes(a, n=LANES):
    return jnp.pad(a, [(0, 0)] * (a.ndim - 1) + [(0, n - a.shape[-1])])


def kernel(x, c, ada_w, ada_b, norm_mix_g, norm_ffn_g, norm_final_g, w_in, lambda_q1, lambda_k1, lambda_q2, lambda_k2, attn_subln_g, rel_bias_table, conv_w, conv_b, dt_bias_f, dt_bias_b, A_log_f, A_log_b, D_skip, ssm_norm_g, w_out, router_w, w_gate, w_up, w_down):
    bsz, s, d = x.shape
    depth = ada_w.shape[0]
    ne = router_w.shape[2]
    cap = CAPACITY_FACTOR * s // ne
    att_w = ATT_HEADS * ATT_V_DIM
    nconv = conv_w.shape[3]
    splits = (att_w, att_w, att_w, SSD_WIDTH, nconv, 2 * SSD_HEADS)
    offs = [0]
    for wd in splits:
        offs.append(offs[-1] + wd)
    assert offs[-1] == w_in.shape[2]
    row = lambda a: a.reshape(1, -1)

    assert depth == 1, "the final RMSNorm is fused into the (single) layer's scatter kernel"
    for l in range(depth):
        lam_init = 0.8 - 0.6 * math.exp(-0.3 * l)
        mod = _ada_mod(c, ada_w[l], ada_b[l])
        sh1, sc1, g1, sh2, sc2, g2 = [m.reshape(bsz, 1, d) for m in jnp.split(mod, 6, axis=-1)]

        q, k, v_t, z, xbc, dt_raw = _in_proj(x, sh1, sc1, row(norm_mix_g[l]), w_in[l].astype(BF16), offs, ts=1024)

        att = _attention(q, k, v_t, rel_bias_table, row(lambda_q1[l]), row(lambda_k1[l]),
                         row(lambda_q2[l]), row(lambda_k2[l]), row(attn_subln_g[l]), lam_init, tq=2048)

        dt_bias = _pad_lanes(jnp.concatenate([dt_bias_f[l], dt_bias_b[l]]).reshape(1, -1))
        a_log = _pad_lanes(jnp.concatenate([A_log_f[l], A_log_b[l]]).reshape(1, -1))
        d_skip = jnp.repeat(D_skip[l], SSD_HEADDIM).reshape(1, -1)
        ssd = _ssd(z, xbc, dt_raw, conv_w[l].reshape(SSD_CONV, nconv), row(conv_b[l]), dt_bias, a_log,
                   d_skip, row(ssm_norm_g[l]))

        rw_t = router_w[l].T
        rw_hi = rw_t.astype(BF16)
        rw_split = jnp.stack([rw_hi, (rw_t - rw_hi.astype(F32)).astype(BF16)])
        x1, h2, aff_t = _out_proj(att, ssd, x, g1, sh2, sc2, row(norm_ffn_g[l]), w_out[l].astype(BF16),
                                  rw_split, ts=1024)
        pos, pos_t = _topk(aff_t, cap)
        xg, gc = _gather(h2, pos, aff_t, cap)
        y = _ffn(xg, gc, w_gate[l], w_up[l], w_down[l], tf=256)
        x = _scatter(pos_t, y, x1, g2, row(norm_final_g), cap, ts=1024)
    return x
```

```python
import functools
import math

import jax
import jax.numpy as jnp
from jax import lax
from jax.experimental import pallas as pl
from jax.experimental.pallas import tpu as pltpu

F32 = jnp.float32
BF16 = jnp.bfloat16
I32 = jnp.int32
HIGHEST = lax.Precision.HIGHEST

ATT_HEADS = 4
ATT_QK_DIM = 64
ATT_V_DIM = 128
N_BUCKETS = 32
MAX_DISTANCE = 128
SSD_HEADS = 8
SSD_HEADDIM = 64
SSD_GROUPS = 2
SSD_STATE = 64
SSD_CONV = 5
SSD_CHUNK = 128
SSD_WIDTH = SSD_HEADS * SSD_HEADDIM
SSD_BC = SSD_GROUPS * SSD_STATE
N_EXPERTS = 16
CAPACITY_FACTOR = 2
EPS = 1e-6
LANES = 128
NEG_BIG = -1e30
LOG2E = math.log2(math.e)

VMEM_LIMIT = 56 * 1024 * 1024


def _params(*semantics):
    return pltpu.CompilerParams(dimension_semantics=semantics, vmem_limit_bytes=VMEM_LIMIT)


def _silu(v):
    return v * jax.nn.sigmoid(v)


def _rms(v, eps=EPS):
    return v * lax.rsqrt(jnp.mean(v * v, axis=-1, keepdims=True) + eps)


def _split_bf16(v):
    hi = v.astype(BF16)
    return hi, (v - hi.astype(F32)).astype(BF16)


def _mod_body(c_ref, w_ref, b_ref, o_ref):
    a_hi, a_lo = _split_bf16(_silu(c_ref[...]))
    w_hi, w_lo = _split_bf16(w_ref[...])
    dot = functools.partial(jnp.dot, preferred_element_type=F32)
    o_ref[...] = dot(a_hi, w_hi) + (dot(a_lo, w_hi) + dot(a_hi, w_lo)) + b_ref[...]


def _ada_mod(c, w, b):
    bsz, d = c.shape
    n = w.shape[1]
    tn = n // 4
    return pl.pallas_call(
        _mod_body,
        grid=(n // tn,),
        in_specs=[pl.BlockSpec((bsz, d), lambda j: (0, 0)),
                  pl.BlockSpec((d, tn), lambda j: (0, j)),
                  pl.BlockSpec((1, tn), lambda j: (0, j))],
        out_specs=pl.BlockSpec((bsz, tn), lambda j: (0, j)),
        out_shape=jax.ShapeDtypeStruct((bsz, n), F32),
        compiler_params=_params("arbitrary"),
        name="ada_mod",
    )(c, w, b.reshape(1, n))


def _inproj_body(x_ref, sh_ref, sc_ref, g_ref, w_ref, q_ref, k_ref, vt_ref, z_ref, xbc_ref, dt_ref, wvt_ref,
                 *, offs, qscale):
    @pl.when((pl.program_id(0) == 0) & (pl.program_id(1) == 0))
    def _():
        wvt_ref[...] = w_ref[:, offs[2]:offs[3]].astype(F32).T.astype(BF16)

    h = _rms(x_ref[0]) * g_ref[...]
    h = (h * (1.0 + sc_ref[0]) + sh_ref[0]).astype(BF16)

    def proj(i):
        return jnp.dot(h, w_ref[:, offs[i]:offs[i + 1]], preferred_element_type=F32)

    q_ref[0] = (proj(0) * qscale).astype(BF16)
    k_ref[0] = proj(1).astype(BF16)
    vt_ref[0, 0] = lax.dot_general(wvt_ref[...], h, (((1,), (1,)), ((), ())),
                                   preferred_element_type=F32).astype(BF16)
    z_ref[0] = proj(3)
    xbc_ref[0] = proj(4)
    dt = proj(5)
    dt_ref[0] = jnp.concatenate([dt, jnp.zeros((dt.shape[0], LANES - dt.shape[1]), F32)], axis=1)


def _in_proj(x, sh, sc, g, w, offs, ts):
    bsz, s, d = x.shape
    wq, wk, wv, wz, wxbc = [offs[i + 1] - offs[i] for i in range(5)]
    tok = lambda wd: pl.BlockSpec((1, ts, wd), lambda b, t: (b, t, 0))
    vec = pl.BlockSpec((1, 1, d), lambda b, t: (b, 0, 0))
    tok_out = lambda wd, dt: (tok(wd), jax.ShapeDtypeStruct((bsz, s, wd), dt))
    outs = [tok_out(wq, BF16), tok_out(wk, BF16),
            (pl.BlockSpec((1, 1, wv, ts), lambda b, t: (b, t, 0, 0)), jax.ShapeDtypeStruct((bsz, s // ts, wv, ts), BF16)),
            tok_out(wz, F32), tok_out(wxbc, F32), tok_out(LANES, F32)]
    return pl.pallas_call(
        functools.partial(_inproj_body, offs=tuple(offs), qscale=ATT_QK_DIM ** -0.5 * LOG2E),
        grid=(bsz, s // ts),
        in_specs=[tok(d), vec, vec,
                  pl.BlockSpec((1, d), lambda b, t: (0, 0)),
                  pl.BlockSpec(w.shape, lambda b, t: (0, 0))],
        out_specs=[o[0] for o in outs],
        out_shape=[o[1] for o in outs],
        scratch_shapes=[pltpu.VMEM((wv, d), BF16)],
        compiler_params=_params("arbitrary", "arbitrary"),
        name="in_proj",
    )(x, sh, sc, g, w)


def _t5_bucket(rel):
    n = jnp.abs(rel)
    large = jnp.full(rel.shape, 8, I32)
    for t in (12, 16, 23, 32, 46, 64, 91):
        large = large + jnp.where(n >= t, 1, 0)
    return jnp.where(rel > 0, 16, 0) + jnp.where(n < 8, n, large)


def _attn_body(tbl_ref, lq1_ref, lk1_ref, lq2_ref, lk2_ref, g_ref, q_ref, k_ref, vt_ref, o_ref,
               bias_ref, acc_ref, *, lam_init, tk):
    h = pl.program_id(0)
    qi = pl.program_id(1)
    s, tq = bias_ref.shape
    qw = 2 * LANES
    assert tq == s and tk == qw, "near / far tile classification below is static for one full-width step"
    near_w = 3 * qw
    ring = pl.next_power_of_2(tk + near_w - 1)

    @pl.when(pl.program_id(2) == 0)
    def _():
        j = lax.broadcasted_iota(I32, (8, ring), 1)
        for kt in range(s // tk):
            w0 = min(max((kt - 1) * qw, 0), tq - near_w)
            bucket = _t5_bucket((kt * tk + tk - 1) - w0 - j)
            w = jnp.zeros((8, ring), F32)
            for bb in range(N_BUCKETS):
                w = jnp.where(bucket == bb, tbl_ref[bb, h] * LOG2E, w)
            big = jnp.broadcast_to(w[0:1], (tk, ring))
            rolled = pltpu.roll(big, ring - (tk - 1), axis=1, stride=1, stride_axis=0)
            bias_ref[kt * tk:(kt + 1) * tk, w0:w0 + near_w] = rolled[:, :near_w]

    lam = (jnp.exp(jnp.sum(lq1_ref[...] * lk1_ref[...], axis=-1, keepdims=True))
           - jnp.exp(jnp.sum(lq2_ref[...] * lk2_ref[...], axis=-1, keepdims=True)) + lam_init)
    q = q_ref[0]
    k_all = k_ref[0]
    lane = lax.broadcasted_iota(I32, q.shape, 1)
    qms = [jnp.where((lane >= m * ATT_QK_DIM) & (lane < (m + 1) * ATT_QK_DIM), q, jnp.zeros_like(q))
           for m in range(2)]
    nqc = tq // qw

    far = jnp.full((8, LANES), MAX_DISTANCE, I32)
    side_const = []
    for rel in (-far, far):
        bucket = _t5_bucket(rel)
        cst = jnp.zeros((8, LANES), F32)
        for bb in range(N_BUCKETS):
            cst = jnp.where(bucket == bb, tbl_ref[bb, h] * LOG2E, cst)
        c_hi = cst.astype(BF16)
        rem = cst - c_hi.astype(F32)
        c_mid = rem.astype(BF16)
        c_lo = (rem - c_mid.astype(F32)).astype(BF16)
        side_const.append([jnp.broadcast_to(p[0:1], k_all.shape) for p in (c_hi, c_mid, c_lo)])
    q_aug, k_side = [], []
    for m in range(2):
        spare = (1 - m) * ATT_QK_DIM
        hot = [jnp.where(lane == spare + i, 1.0, 0.0).astype(BF16) for i in range(3)]
        keep = jnp.where((lane >= spare) & (lane < spare + 3), 0.0, 1.0).astype(BF16)
        q_aug.append(qms[m] + (hot[0] + hot[1] + hot[2]))
        k_kept = k_all * keep
        k_side.append([k_kept + (hot[0] * parts[0] + hot[1] * parts[1] + hot[2] * parts[2]) for parts in side_const])
    nkb, kb = vt_ref.shape[1], vt_ref.shape[3]
    ones_rows = 16

    streams = [(m, c) for m in range(2) for c in range(nqc)]
    run_max = {st: jnp.full((1, qw), NEG_BIG, F32) for st in streams}
    nt = s // tk
    items = [(t, st) for t in range(nt) for st in streams]
    pending = {}
    depth = min(7, len(streams) - 1)

    def logits_part(item):
        t, (m, c) = item
        rows, cols = slice(t * tk, (t + 1) * tk), slice(c * qw, (c + 1) * qw)
        nt_dot = lambda a, b: lax.dot_general(a, b, (((1,), (1,)), ((), ())), preferred_element_type=F32)
        max_off = (t + 1) * tk - 1 - c * qw
        min_off = t * tk - ((c + 1) * qw - 1)
        if max_off <= -MAX_DISTANCE:
            logit = nt_dot(k_side[m][0][rows, :], q_aug[m][cols, :])
        elif min_off >= MAX_DISTANCE:
            logit = nt_dot(k_side[m][1][rows, :], q_aug[m][cols, :])
        else:
            logit = bias_ref[rows, cols] + nt_dot(k_all[rows, :], qms[m][cols, :])
        mx = logit[0:8, :]
        for r in range(1, tk // 8):
            mx = jnp.maximum(mx, logit[r * 8:(r + 1) * 8, :])
        pending[item] = (logit, jnp.max(mx, axis=0, keepdims=True))

    def pv_part(item):
        t, st = item
        m, c = st
        cols = slice(c * qw, (c + 1) * qw)
        logit, tile_max = pending.pop(item)
        m_old = run_max[st]
        m_new = jnp.maximum(m_old, tile_max)
        run_max[st] = m_new
        probs = jnp.exp2(logit - m_new).astype(BF16)
        blk, off = (t * tk) // kb, (t * tk) % kb
        vt_ext = jnp.concatenate([vt_ref[0, blk][:, off:off + tk], jnp.ones((ones_rows, tk), BF16)], axis=0)
        part = jnp.dot(vt_ext, probs, preferred_element_type=F32)
        if t == 0:
            acc_ref[m, :, cols] = part
        else:
            acc_ref[m, :, cols] = acc_ref[m, :, cols] * jnp.exp2(m_old - m_new) + part

    for i in range(len(items) + depth):
        if i < len(items):
            logits_part(items[i])
        if i >= depth:
            pv_part(items[i - depth])
    outs = [acc_ref[m, :ATT_V_DIM, :] / acc_ref[m, ATT_V_DIM:ATT_V_DIM + 1, :] for m in range(2)]
    att = outs[0] - lam * outs[1]
    att = att * lax.rsqrt(jnp.mean(att * att, axis=0, keepdims=True) + EPS)
    o_ref[0] = (att * (g_ref[...] * (1.0 - lam_init))).T.astype(BF16)


def _attention(q, k, v_t, tbl, lq1, lk1, lq2, lk2, subln_g, lam_init, tq):
    bsz, s, _ = q.shape
    nkb, kb = v_t.shape[1], v_t.shape[3]
    small = lambda n: pl.BlockSpec((1, n), lambda h, i, b: (0, 0))
    return pl.pallas_call(
        functools.partial(_attn_body, lam_init=lam_init, tk=2 * LANES),
        grid=(ATT_HEADS, s // tq, bsz),
        in_specs=[pl.BlockSpec(memory_space=pltpu.SMEM),
                  small(ATT_QK_DIM), small(ATT_QK_DIM), small(ATT_QK_DIM), small(ATT_QK_DIM),
                  pl.BlockSpec((ATT_V_DIM, 1), lambda h, i, b: (0, 0)),
                  pl.BlockSpec((1, tq, ATT_V_DIM), lambda h, i, b: (b, i, h)),
                  pl.BlockSpec((1, s, ATT_V_DIM), lambda h, i, b: (b, 0, h)),
                  pl.BlockSpec((1, nkb, ATT_V_DIM, kb), lambda h, i, b: (b, 0, h, 0))],
        out_specs=pl.BlockSpec((1, tq, ATT_V_DIM), lambda h, i, b: (b, i, h)),
        out_shape=jax.ShapeDtypeStruct((bsz, s, ATT_HEADS * ATT_V_DIM), BF16),
        scratch_shapes=[pltpu.VMEM((s, tq), F32),
                        pltpu.VMEM((2, ATT_V_DIM + 16, tq), F32)],
        compiler_params=_params("parallel", "parallel", "arbitrary"),
        name="diff_attention",
    )(tbl, lq1, lk1, lq2, lk2, subln_g.reshape(ATT_V_DIM, 1), q, k, v_t)


def _head_selector(off):
    r = lax.broadcasted_iota(I32, (LANES, SSD_WIDTH), 0)
    c = lax.broadcasted_iota(I32, (LANES, SSD_WIDTH), 1)
    return jnp.where(r - off == c // SSD_HEADDIM, 1.0, 0.0).astype(BF16)


def _expand_heads(cols, selector):
    hi = cols.astype(BF16)
    lo = (cols - hi.astype(F32)).astype(BF16)
    return (jnp.dot(hi, selector, preferred_element_type=F32)
            + jnp.dot(lo, selector, preferred_element_type=F32))


def _ssd_body(z_ref, xbc_ref, dtr_ref, cw_ref, cb_ref, dtb_ref, alog_ref, dskip_ref, ng_ref, o_ref,
              xp_ref, xc_ref, dt_ref, da_ref, cs_ref, cst_ref, ecst_ref, dtt_ref, bt_ref, xdf_ref, xdb_ref,
              dtotf_ref, dtotb_ref, y_ref, hf_ref, hb_ref):
    s = xbc_ref.shape[1]
    ch = SSD_CHUNK
    assert ch == LANES
    nc = s // ch
    pad = 8
    nconv = xbc_ref.shape[2]

    xp_ref[0:pad, :] = jnp.zeros((pad, nconv), F32)
    xp_ref[pad + s:pad + s + pad, :] = jnp.zeros((pad, nconv), F32)
    xp_ref[pad:pad + s, :] = xbc_ref[0]

    def conv_chunk(c, carry):
        r0 = pl.multiple_of(c * ch, ch)
        nwin = ch + 2 * pad
        win = xp_ref[pl.ds(r0, nwin), :]
        acc = jnp.broadcast_to(cb_ref[...], (ch, nconv))
        for j in range(SSD_CONV):
            shift = (SSD_CONV // 2 - j) % nwin
            tap = win if shift == 0 else pltpu.roll(win, shift, axis=0)
            acc = acc + cw_ref[j:j + 1, :] * tap[pad:pad + ch, :]
        xc_ref[pl.ds(r0, ch), :] = _silu(acc)
        return carry

    lax.fori_loop(0, nc, conv_chunk, 0)

    lane = lax.broadcasted_iota(I32, (s, LANES), 1)
    pre = dtr_ref[0] + dtb_ref[...]
    dt = jnp.maximum(pre, 0.0) + jnp.log1p(jnp.exp(-jnp.abs(pre)))
    dt = jnp.where(lane < 2 * SSD_HEADS, dt, 0.0)
    dt_ref[...] = dt
    da_ref[...] = dt * (-jnp.exp(alog_ref[...]))

    row = lax.broadcasted_iota(I32, (ch, ch), 0)
    col = lax.broadcasted_iota(I32, (ch, ch), 1)
    tri = jnp.where(col <= row, 1.0, 0.0).astype(F32)
    lower = col <= row
    upper = col >= row
    lane_c = lax.broadcasted_iota(I32, (ch, LANES), 1)
    st_rows = lax.broadcasted_iota(I32, (SSD_BC, SSD_WIDTH), 0) // SSD_STATE
    st_cols = lax.broadcasted_iota(I32, (SSD_BC, SSD_WIDTH), 1) // (SSD_WIDTH // SSD_GROUPS)
    same_group = st_rows == st_cols
    sel_f = _head_selector(0)
    sel_b = _head_selector(SSD_HEADS)

    hf_ref[...] = jnp.zeros_like(hf_ref)
    hb_ref[...] = jnp.zeros_like(hb_ref)

    def wide(x):
        return jnp.concatenate([x[c * ch:(c + 1) * ch, :] for c in range(nc)], axis=1)

    da_wide = wide(da_ref[...])
    cs_wide = jnp.dot(tri, da_wide, precision=HIGHEST, preferred_element_type=F32)
    for c in range(nc):
        cs_ref[c * ch:(c + 1) * ch, :] = cs_wide[:, c * LANES:(c + 1) * LANES]
    cst_ref[...] = cs_wide.T
    ecst_ref[...] = (cs_wide - da_wide).T
    dtt_ref[...] = wide(dt_ref[...]).T
    bt_ref[...] = wide(xc_ref[:, SSD_WIDTH:SSD_WIDTH + SSD_BC]).T.astype(BF16)

    def prep_chunk(c, carry):
        r0 = pl.multiple_of(c * ch, ch)
        r8 = pl.multiple_of(c * 8, 8)
        da = da_ref[pl.ds(r0, ch), :]
        dtc = dt_ref[pl.ds(r0, ch), :]
        cs = cs_ref[pl.ds(r0, ch), :]
        tot = cs[ch - 1:ch, :]
        xs = xc_ref[pl.ds(r0, ch), 0:SSD_WIDTH]
        tot8 = jnp.broadcast_to(jnp.exp(tot), (8, LANES))
        ff = _expand_heads(jnp.concatenate([dtc * jnp.exp(tot - cs), tot8], axis=0), sel_f)
        xdf_ref[pl.ds(r0, ch), :] = (xs * ff[0:ch]).astype(BF16)
        dtotf_ref[pl.ds(r8, 8), :] = ff[ch:ch + 8]
        fb = _expand_heads(jnp.concatenate([dtc * jnp.exp(cs - da), tot8], axis=0), sel_b)
        xdb_ref[pl.ds(r0, ch), :] = (xs * fb[0:ch]).astype(BF16)
        dtotb_ref[pl.ds(r8, 8), :] = fb[ch:ch + 8]
        return carry

    lax.fori_loop(0, nc, prep_chunk, 0, unroll=2)

    def fwd_chunk(c, carry):
        r0 = pl.multiple_of(c * ch, ch)
        da = da_ref[pl.ds(r0, ch), :]
        cs = cs_ref[pl.ds(r0, ch), :]
        ecs = cs - da
        nh2 = 2 * SSD_HEADS
        cs_t = cst_ref[pl.ds(r0, nh2), :]
        ecs_t = ecst_ref[pl.ds(r0, nh2), :]
        dt_t = dtt_ref[pl.ds(r0, nh2), :]
        xs = xc_ref[pl.ds(r0, ch), 0:SSD_WIDTH]
        cm = xc_ref[pl.ds(r0, ch), SSD_WIDTH + SSD_BC:SSD_WIDTH + 2 * SSD_BC]
        bt = bt_ref[pl.ds(r0, ch), :]
        cmb = cm.astype(BF16)
        xsb = xs.astype(BF16)

        hf = hf_ref[...]
        y_ref[pl.ds(r0, ch), :] = (jnp.dot(cmb, hf.astype(BF16), preferred_element_type=F32)
                                   * _expand_heads(jnp.exp(cs), sel_f))
        upd = jnp.dot(bt, xdf_ref[pl.ds(r0, ch), :], preferred_element_type=F32)
        hf_ref[...] = hf * dtotf_ref[pl.ds(pl.multiple_of(c * 8, 8), 1), :] + jnp.where(same_group, upd, 0.0)

        g_mats = []
        for g in range(SSD_GROUPS):
            cg = jnp.where(lane_c // SSD_STATE == g, cmb, jnp.zeros_like(cmb))
            g_mats.append(jnp.dot(cg, bt, preferred_element_type=F32))
        for pair in range(SSD_HEADS // 2):
            xpair = xsb[:, pair * LANES:(pair + 1) * LANES]
            ypair = jnp.zeros((ch, LANES), F32)
            for sub in range(2):
                hh = 2 * pair + sub
                g = hh // (SSD_HEADS // SSD_GROUPS)
                d_f = cs[:, hh:hh + 1] - cs_t[hh:hh + 1, :]
                d_b = ecs_t[SSD_HEADS + hh:SSD_HEADS + hh + 1, :] - ecs[:, SSD_HEADS + hh:SSD_HEADS + hh + 1]
                decay = jnp.exp(jnp.where(lower, d_f, d_b))
                wgt = (jnp.where(lower, dt_t[hh:hh + 1, :], 0.0)
                       + jnp.where(upper, dt_t[SSD_HEADS + hh:SSD_HEADS + hh + 1, :], 0.0))
                m = (g_mats[g] * decay * wgt).astype(BF16)
                xh = jnp.where(lane_c // SSD_HEADDIM == sub, xpair, jnp.zeros_like(xpair))
                ypair = ypair + jnp.dot(m, xh, preferred_element_type=F32)
            y_ref[pl.ds(r0, ch), pair * LANES:(pair + 1) * LANES] += ypair
        return carry

    lax.fori_loop(0, nc, fwd_chunk, 0, unroll=4)

    def bwd_chunk(i, carry):
        c = nc - 1 - i
        r0 = pl.multiple_of(c * ch, ch)
        da = da_ref[pl.ds(r0, ch), :]
        cs = cs_ref[pl.ds(r0, ch), :]
        tot = cs[ch - 1:ch, :]
        ecs = cs - da
        xs = xc_ref[pl.ds(r0, ch), 0:SSD_WIDTH]
        cm = xc_ref[pl.ds(r0, ch), SSD_WIDTH + SSD_BC:SSD_WIDTH + 2 * SSD_BC]
        bt = bt_ref[pl.ds(r0, ch), :]

        hb = hb_ref[...]
        y_off = (jnp.dot(cm.astype(BF16), hb.astype(BF16), preferred_element_type=F32)
                 * _expand_heads(jnp.exp(tot - ecs), sel_b))
        upd = jnp.dot(bt, xdb_ref[pl.ds(r0, ch), :], preferred_element_type=F32)
        hb_ref[...] = hb * dtotb_ref[pl.ds(pl.multiple_of(c * 8, 8), 1), :] + jnp.where(same_group, upd, 0.0)

        y = y_ref[pl.ds(r0, ch), :] + y_off + dskip_ref[...] * xs
        y = y * _silu(z_ref[0, pl.ds(r0, ch), :])
        gw = SSD_WIDTH // SSD_GROUPS
        yn = jnp.concatenate([_rms(y[:, g * gw:(g + 1) * gw]) for g in range(SSD_GROUPS)], axis=-1)
        o_ref[0, pl.ds(r0, ch), :] = (yn * ng_ref[...]).astype(BF16)
        return carry

    lax.fori_loop(0, nc, bwd_chunk, 0, unroll=4)


def _ssd(z, xbc, dt_raw, conv_w, conv_b, dt_bias, a_log, d_skip, norm_g):
    bsz, s, nconv = xbc.shape
    tok = lambda wd: pl.BlockSpec((1, s, wd), lambda b: (b, 0, 0))
    full = lambda a: pl.BlockSpec(a.shape, lambda b: (0, 0))
    return pl.pallas_call(
        _ssd_body,
        grid=(bsz,),
        in_specs=[tok(SSD_WIDTH), tok(nconv), tok(LANES), full(conv_w), full(conv_b), full(dt_bias),
                  full(a_log), full(d_skip), full(norm_g)],
        out_specs=tok(SSD_WIDTH),
        out_shape=jax.ShapeDtypeStruct((bsz, s, SSD_WIDTH), BF16),
        scratch_shapes=[pltpu.VMEM((s + 16, nconv), F32),
                        pltpu.VMEM((s, nconv), F32),
                        pltpu.VMEM((s, LANES), F32),
                        pltpu.VMEM((s, LANES), F32),
                        pltpu.VMEM((s, LANES), F32),
                        pltpu.VMEM((s, LANES), F32),
                        pltpu.VMEM((s, LANES), F32),
                        pltpu.VMEM((s, LANES), F32),
                        pltpu.VMEM((s, SSD_BC), BF16),
                        pltpu.VMEM((s, SSD_WIDTH), BF16),
                        pltpu.VMEM((s, SSD_WIDTH), BF16),
                        pltpu.VMEM((s // SSD_CHUNK * 8, SSD_WIDTH), F32),
                        pltpu.VMEM((s // SSD_CHUNK * 8, SSD_WIDTH), F32),
                        pltpu.VMEM((s, SSD_WIDTH), F32),
                        pltpu.VMEM((SSD_BC, SSD_WIDTH), F32),
                        pltpu.VMEM((SSD_BC, SSD_WIDTH), F32)],
        compiler_params=_params("parallel"),
        name="ssd_mixer",
    )(z, xbc, dt_raw, conv_w, conv_b, dt_bias, a_log, d_skip, norm_g)


def _outproj_body(att_ref, ssd_ref, x_ref, g1_ref, sh_ref, sc_ref, ng_ref, w_ref, rw_ref,
                  x1_ref, h2_ref, aff_ref):
    na = att_ref.shape[2]
    mix = (jnp.dot(att_ref[0], w_ref[0:na, :], preferred_element_type=F32)
           + jnp.dot(ssd_ref[0], w_ref[na:, :], preferred_element_type=F32))
    x1 = x_ref[0] + g1_ref[0] * mix
    x1_ref[0] = x1
    h2 = _rms(x1) * ng_ref[...]
    h2 = h2 * (1.0 + sc_ref[0]) + sh_ref[0]
    h2_hi = h2.astype(BF16)
    h2_ref[0] = h2_hi
    h2_lo = (h2 - h2_hi.astype(F32)).astype(BF16)
    nt = lambda a, b: lax.dot_general(a, b, (((1,), (1,)), ((), ())), preferred_element_type=F32)
    ne = rw_ref.shape[1]
    both = nt(rw_ref[...].reshape(2 * ne, rw_ref.shape[2]), h2_hi)
    logits = both[0:ne] + (nt(rw_ref[0], h2_lo) + both[ne:2 * ne])
    e = jnp.exp(logits - jnp.max(logits, axis=0, keepdims=True))
    aff_ref[0] = e / jnp.sum(e, axis=0, keepdims=True)


def _out_proj(att, ssd, x, g1, sh2, sc2, ng, w_out, rw_t, ts):
    bsz, s, d = x.shape
    tok = lambda wd: pl.BlockSpec((1, ts, wd), lambda b, t: (b, t, 0))
    vec = pl.BlockSpec((1, 1, d), lambda b, t: (b, 0, 0))
    full = lambda a: pl.BlockSpec(a.shape, lambda b, t: (0, 0))
    ne = rw_t.shape[1]
    return pl.pallas_call(
        _outproj_body,
        grid=(bsz, s // ts),
        in_specs=[tok(att.shape[2]), tok(ssd.shape[2]), tok(d), vec, vec, vec, full(ng), full(w_out),
                  pl.BlockSpec(rw_t.shape, lambda b, t: (0, 0, 0))],
        out_specs=[tok(d), tok(d), pl.BlockSpec((1, ne, ts), lambda b, t: (b, 0, t))],
        out_shape=[jax.ShapeDtypeStruct((bsz, s, d), F32),
                   jax.ShapeDtypeStruct((bsz, s, d), BF16),
                   jax.ShapeDtypeStruct((bsz, ne, s), F32)],
        compiler_params=_params("parallel", "arbitrary"),
        name="out_proj_router",
    )(att, ssd, x, g1, sh2, sc2, ng, w_out, rw_t)


def _topk_body(aff_ref, pos_ref, post_ref, ut_ref, *, cap):
    nrow, s = aff_ref.shape
    r = lax.broadcasted_iota(I32, (s, s), 0)
    c = lax.broadcasted_iota(I32, (s, s), 1)
    ut_ref[...] = jnp.where(r < c, 1.0, 0.0).astype(BF16)

    bits = pltpu.bitcast(aff_ref[...], I32)

    def count(mask):
        return jnp.sum(jnp.where(mask, 1, 0), axis=-1, keepdims=True)

    thr = jnp.zeros((nrow, 1), I32)
    for bit in range(30, -1, -1):
        cand = thr | (1 << bit)
        thr = jnp.where(count(bits >= cand) >= cap, cand, thr)
    gt = bits > thr
    eq = bits == thr
    need = cap - count(gt)
    eq_rank = jnp.dot(jnp.where(eq, 1.0, 0.0).astype(BF16), ut_ref[...], preferred_element_type=F32)
    sel = gt | (eq & (eq_rank < need.astype(F32)))
    slot = jnp.dot(jnp.where(sel, 1.0, 0.0).astype(BF16), ut_ref[...], preferred_element_type=F32)
    pos = jnp.where(sel, slot, -1.0)
    pos_ref[...] = pos.astype(I32)
    post_ref[...] = pos.T.astype(I32)


def _topk(aff_t, cap):
    bsz, ne, s = aff_t.shape
    nrow = bsz * ne
    pos, pos_t = pl.pallas_call(
        functools.partial(_topk_body, cap=cap),
        grid=(1,),
        in_specs=[pl.BlockSpec((nrow, s), lambda i: (0, 0))],
        out_specs=[pl.BlockSpec((nrow, s), lambda i: (0, 0)),
                   pl.BlockSpec((s, nrow), lambda i: (0, 0))],
        out_shape=[jax.ShapeDtypeStruct((nrow, s), I32),
                   jax.ShapeDtypeStruct((s, nrow), I32)],
        scratch_shapes=[pltpu.VMEM((s, s), BF16)],
        compiler_params=_params("arbitrary"),
        name="expert_topk",
    )(aff_t.reshape(nrow, s))
    return pos.reshape(bsz, ne, s), pos_t


def _gather_body(h2_ref, pos_ref, aff_ref, xg_ref, gc_ref, *, cap):
    ne, s = pos_ref.shape[1], pos_ref.shape[2]
    slot = lax.broadcasted_iota(I32, (cap, s), 0)
    h2 = h2_ref[0]
    for e in range(ne):
        hit = pos_ref[0, e:e + 1, :] == slot
        xg_ref[e] = jnp.dot(jnp.where(hit, 1.0, 0.0).astype(BF16), h2,
                            preferred_element_type=F32).astype(BF16)
        gate = jnp.sum(jnp.where(hit, aff_ref[0, e:e + 1, :], 0.0), axis=-1, keepdims=True)
        gc_ref[e] = jnp.broadcast_to(gate, (cap, LANES))


def _gather(h2, pos, aff_t, cap):
    bsz, s, d = h2.shape
    ne = pos.shape[1]
    return pl.pallas_call(
        functools.partial(_gather_body, cap=cap),
        grid=(bsz,),
        in_specs=[pl.BlockSpec((1, s, d), lambda b: (b, 0, 0)),
                  pl.BlockSpec((1, ne, s), lambda b: (b, 0, 0)),
                  pl.BlockSpec((1, ne, s), lambda b: (b, 0, 0))],
        out_specs=[pl.BlockSpec((ne, cap, d), lambda b: (0, b, 0)),
                   pl.BlockSpec((ne, cap, LANES), lambda b: (0, b, 0))],
        out_shape=[jax.ShapeDtypeStruct((ne, bsz * cap, d), BF16),
                   jax.ShapeDtypeStruct((ne, bsz * cap, LANES), F32)],
        compiler_params=_params("parallel"),
        name="moe_gather",
    )(h2, pos, aff_t)


def _ffn_body(xg_ref, gc_ref, wg_ref, wu_ref, wd_ref, y_ref, acc_ref):
    f = pl.program_id(1)

    @pl.when(f == 0)
    def _():
        acc_ref[...] = jnp.zeros_like(acc_ref)

    wg = wg_ref[0].astype(BF16)
    wu = wu_ref[0].astype(BF16)
    wd = wd_ref[0].astype(BF16)
    nq = 4
    rq = xg_ref.shape[1] // nq
    hidden = {}

    def gate_up(r):
        xg = xg_ref[0, r * rq:(r + 1) * rq, :]
        gate = jnp.dot(xg, wg, preferred_element_type=F32)
        up = jnp.dot(xg, wu, preferred_element_type=F32)
        hidden[r] = (_silu(gate) * up).astype(BF16)

    def down(r):
        acc_ref[r * rq:(r + 1) * rq, :] += jnp.dot(hidden.pop(r), wd, preferred_element_type=F32)

    gate_up(0)
    for r in range(nq):
        if r + 1 < nq:
            gate_up(r + 1)
        down(r)

    @pl.when(f == pl.num_programs(1) - 1)
    def _():
        m, d = acc_ref.shape
        gcol = gc_ref[0]
        y_ref[0] = (acc_ref[...] * jnp.concatenate([gcol] * (d // LANES), axis=-1)).astype(BF16)


def _ffn(xg, gc, w_gate, w_up, w_down, tf):
    ne, m, d = xg.shape
    ff = w_gate.shape[2]
    return pl.pallas_call(
        _ffn_body,
        grid=(ne, ff // tf),
        in_specs=[pl.BlockSpec((1, m, d), lambda e, f: (e, 0, 0)),
                  pl.BlockSpec((1, m, LANES), lambda e, f: (e, 0, 0)),
                  pl.BlockSpec((1, d, tf), lambda e, f: (e, 0, f)),
                  pl.BlockSpec((1, d, tf), lambda e, f: (e, 0, f)),
                  pl.BlockSpec((1, tf, d), lambda e, f: (e, f, 0))],
        out_specs=pl.BlockSpec((1, m, d), lambda e, f: (e, 0, 0)),
        out_shape=jax.ShapeDtypeStruct((ne, m, d), BF16),
        scratch_shapes=[pltpu.VMEM((m, d), F32)],
        compiler_params=_params("parallel", "arbitrary"),
        name="moe_ffn",
    )(xg, gc, w_gate, w_up, w_down)


def _scatter_body(post_ref, y_ref, x1_ref, g2_ref, ng_ref, o_ref, *, cap):
    ne = y_ref.shape[0]
    ts, nrow = post_ref.shape
    d = y_ref.shape[2]
    r = lax.broadcasted_iota(I32, (nrow, LANES), 0)
    c = lax.broadcasted_iota(I32, (nrow, LANES), 1)
    pick = jnp.where(r == pl.program_id(0) * ne + c, 1.0, 0.0).astype(BF16)
    post = jnp.dot(post_ref[...].astype(F32).astype(BF16), pick, preferred_element_type=F32)
    slot = lax.broadcasted_iota(I32, (ts, cap), 1).astype(F32)
    onehot = jnp.concatenate(
        [jnp.where(post[:, e:e + 1] == slot, 1.0, 0.0).astype(BF16) for e in range(ne)], axis=-1)
    moe = jnp.dot(onehot, y_ref[...].reshape(ne * cap, d), preferred_element_type=F32)
    x2 = x1_ref[0] + g2_ref[0] * moe
    o_ref[0] = _rms(x2) * ng_ref[...]


def _scatter(post, y, x1, g2, ng, cap, ts):
    bsz, s, d = x1.shape
    ne = y.shape[0]
    return pl.pallas_call(
        functools.partial(_scatter_body, cap=cap),
        grid=(bsz, s // ts),
        in_specs=[pl.BlockSpec((ts, post.shape[1]), lambda b, t: (t, 0)),
                  pl.BlockSpec((ne, cap, d), lambda b, t: (0, b, 0)),
                  pl.BlockSpec((1, ts, d), lambda b, t: (b, t, 0)),
                  pl.BlockSpec((1, 1, d), lambda b, t: (b, 0, 0)),
                  pl.BlockSpec((1, d), lambda b, t: (0, 0))],
        out_specs=pl.BlockSpec((1, ts, d), lambda b, t: (b, t, 0)),
        out_shape=jax.ShapeDtypeStruct((bsz, s, d), F32),
        compiler_params=_params("parallel", "arbitrary"),
        name="moe_scatter_final",
    )(post, y, x1, g2, ng)


def _pad_lanes(a, n=LANES):
    return jnp.pad(a, [(0, 0)] * (a.ndim - 1) + [(0, n - a.shape[-1])])


def kernel(x, c, ada_w, ada_b, norm_mix_g, norm_ffn_g, norm_final_g, w_in, lambda_q1, lambda_k1, lambda_q2, lambda_k2, attn_subln_g, rel_bias_table, conv_w, conv_b, dt_bias_f, dt_bias_b, A_log_f, A_log_b, D_skip, ssm_norm_g, w_out, router_w, w_gate, w_up, w_down):
    bsz, s, d = x.shape
    depth = ada_w.shape[0]
    ne = router_w.shape[2]
    cap = CAPACITY_FACTOR * s // ne
    att_w = ATT_HEADS * ATT_V_DIM
    nconv = conv_w.shape[3]
    splits = (att_w, att_w, att_w, SSD_WIDTH, nconv, 2 * SSD_HEADS)
    offs = [0]
    for wd in splits:
        offs.append(offs[-1] + wd)
    assert offs[-1] == w_in.shape[2]
    row = lambda a: a.reshape(1, -1)

    assert depth == 1, "the final RMSNorm is fused into the (single) layer's scatter kernel"
    for l in range(depth):
        lam_init = 0.8 - 0.6 * math.exp(-0.3 * l)
        mod = _ada_mod(c, ada_w[l], ada_b[l])
        sh1, sc1, g1, sh2, sc2, g2 = [m.reshape(bsz, 1, d) for m in jnp.split(mod, 6, axis=-1)]

        q, k, v_t, z, xbc, dt_raw = _in_proj(x, sh1, sc1, row(norm_mix_g[l]), w_in[l].astype(BF16), offs, ts=1024)

        att = _attention(q, k, v_t, rel_bias_table, row(lambda_q1[l]), row(lambda_k1[l]),
                         row(lambda_q2[l]), row(lambda_k2[l]), row(attn_subln_g[l]), lam_init, tq=2048)

        dt_bias = _pad_lanes(jnp.concatenate([dt_bias_f[l], dt_bias_b[l]]).reshape(1, -1))
        a_log = _pad_lanes(jnp.concatenate([A_log_f[l], A_log_b[l]]).reshape(1, -1))
        d_skip = jnp.repeat(D_skip[l], SSD_HEADDIM).reshape(1, -1)
        ssd = _ssd(z, xbc, dt_raw, conv_w[l].reshape(SSD_CONV, nconv), row(conv_b[l]), dt_bias, a_log,
                   d_skip, row(ssm_norm_g[l]))

        rw_t = router_w[l].T
        rw_hi = rw_t.astype(BF16)
        rw_split = jnp.stack([rw_hi, (rw_t - rw_hi.astype(F32)).astype(BF16)])
        x1, h2, aff_t = _out_proj(att, ssd, x, g1, sh2, sc2, row(norm_ffn_g[l]), w_out[l].astype(BF16),
                                  rw_split, ts=1024)
        pos, pos_t = _topk(aff_t, cap)
        xg, gc = _gather(h2, pos, aff_t, cap)
        y = _ffn(xg, gc, w_gate[l], w_up[l], w_down[l], tf=256)
        x = _scatter(pos_t, y, x1, g2, row(norm_final_g), cap, ts=1024)
    return x
```

```python
import functools
import math

import jax
import jax.numpy as jnp
from jax import lax
from jax.experimental import pallas as pl
from jax.experimental.pallas import tpu as pltpu

F32 = jnp.float32
BF16 = jnp.bfloat16
I32 = jnp.int32
HIGHEST = lax.Precision.HIGHEST

ATT_HEADS = 4
ATT_QK_DIM = 64
ATT_V_DIM = 128
N_BUCKETS = 32
MAX_DISTANCE = 128
SSD_HEADS = 8
SSD_HEADDIM = 64
SSD_GROUPS = 2
SSD_STATE = 64
SSD_CONV = 5
SSD_CHUNK = 128
SSD_WIDTH = SSD_HEADS * SSD_HEADDIM
SSD_BC = SSD_GROUPS * SSD_STATE
N_EXPERTS = 16
CAPACITY_FACTOR = 2
EPS = 1e-6
LANES = 128
NEG_BIG = -1e30
LOG2E = math.log2(math.e)

VMEM_LIMIT = 56 * 1024 * 1024


def _params(*semantics):
    return pltpu.CompilerParams(dimension_semantics=semantics, vmem_limit_bytes=VMEM_LIMIT)


def _silu(v):
    return v * jax.nn.sigmoid(v)


def _rms(v, eps=EPS):
    return v * lax.rsqrt(jnp.mean(v * v, axis=-1, keepdims=True) + eps)


def _split_bf16(v):
    hi = v.astype(BF16)
    return hi, (v - hi.astype(F32)).astype(BF16)


def _mod_body(c_ref, w_ref, b_ref, o_ref):
    a_hi, a_lo = _split_bf16(_silu(c_ref[...]))
    w_hi, w_lo = _split_bf16(w_ref[...])
    dot = functools.partial(jnp.dot, preferred_element_type=F32)
    o_ref[...] = dot(a_hi, w_hi) + (dot(a_lo, w_hi) + dot(a_hi, w_lo)) + b_ref[...]


def _ada_mod(c, w, b):
    bsz, d = c.shape
    n = w.shape[1]
    tn = n // 4
    return pl.pallas_call(
        _mod_body,
        grid=(n // tn,),
        in_specs=[pl.BlockSpec((bsz, d), lambda j: (0, 0)),
                  pl.BlockSpec((d, tn), lambda j: (0, j)),
                  pl.BlockSpec((1, tn), lambda j: (0, j))],
        out_specs=pl.BlockSpec((bsz, tn), lambda j: (0, j)),
        out_shape=jax.ShapeDtypeStruct((bsz, n), F32),
        compiler_params=_params("arbitrary"),
        name="ada_mod",
    )(c, w, b.reshape(1, n))


def _inproj_body(x_ref, sh_ref, sc_ref, g_ref, w_ref, q_ref, k_ref, vt_ref, z_ref, xbc_ref, dt_ref, wvt_ref,
                 *, offs, qscale):
    @pl.when((pl.program_id(0) == 0) & (pl.program_id(1) == 0))
    def _():
        wvt_ref[...] = w_ref[:, offs[2]:offs[3]].astype(F32).T.astype(BF16)

    h = _rms(x_ref[0]) * g_ref[...]
    h = (h * (1.0 + sc_ref[0]) + sh_ref[0]).astype(BF16)

    def proj(i):
        return jnp.dot(h, w_ref[:, offs[i]:offs[i + 1]], preferred_element_type=F32)

    q_ref[0] = (proj(0) * qscale).astype(BF16)
    k_ref[0] = proj(1).astype(BF16)
    vt_ref[0, 0] = lax.dot_general(wvt_ref[...], h, (((1,), (1,)), ((), ())),
                                   preferred_element_type=F32).astype(BF16)
    z_ref[0] = proj(3)
    xbc_ref[0] = proj(4)
    dt = proj(5)
    dt_ref[0] = jnp.concatenate([dt, jnp.zeros((dt.shape[0], LANES - dt.shape[1]), F32)], axis=1)


def _in_proj(x, sh, sc, g, w, offs, ts):
    bsz, s, d = x.shape
    wq, wk, wv, wz, wxbc = [offs[i + 1] - offs[i] for i in range(5)]
    tok = lambda wd: pl.BlockSpec((1, ts, wd), lambda b, t: (b, t, 0))
    vec = pl.BlockSpec((1, 1, d), lambda b, t: (b, 0, 0))
    tok_out = lambda wd, dt: (tok(wd), jax.ShapeDtypeStruct((bsz, s, wd), dt))
    outs = [tok_out(wq, BF16), tok_out(wk, BF16),
            (pl.BlockSpec((1, 1, wv, ts), lambda b, t: (b, t, 0, 0)), jax.ShapeDtypeStruct((bsz, s // ts, wv, ts), BF16)),
            tok_out(wz, F32), tok_out(wxbc, F32), tok_out(LANES, F32)]
    return pl.pallas_call(
        functools.partial(_inproj_body, offs=tuple(offs), qscale=ATT_QK_DIM ** -0.5 * LOG2E),
        grid=(bsz, s // ts),
        in_specs=[tok(d), vec, vec,
                  pl.BlockSpec((1, d), lambda b, t: (0, 0)),
                  pl.BlockSpec(w.shape, lambda b, t: (0, 0))],
        out_specs=[o[0] for o in outs],
        out_shape=[o[1] for o in outs],
        scratch_shapes=[pltpu.VMEM((wv, d), BF16)],
        compiler_params=_params("arbitrary", "arbitrary"),
        name="in_proj",
    )(x, sh, sc, g, w)


def _t5_bucket(rel):
    n = jnp.abs(rel)
    large = jnp.full(rel.shape, 8, I32)
    for t in (12, 16, 23, 32, 46, 64, 91):
        large = large + jnp.where(n >= t, 1, 0)
    return jnp.where(rel > 0, 16, 0) + jnp.where(n < 8, n, large)


def _attn_body(tbl_ref, lq1_ref, lk1_ref, lq2_ref, lk2_ref, g_ref, q_ref, k_ref, vt_ref, o_ref,
               bias_ref, acc_ref, *, lam_init, tk):
    h = pl.program_id(0)
    qi = pl.program_id(1)
    s, tq = bias_ref.shape
    qw = 2 * LANES
    assert tq == s and tk == qw, "near / far tile classification below is static for one full-width step"
    near_w = 3 * qw
    ring = pl.next_power_of_2(tk + near_w - 1)

    @pl.when(pl.program_id(2) == 0)
    def _():
        j = lax.broadcasted_iota(I32, (8, ring), 1)
        for kt in range(s // tk):
            w0 = min(max((kt - 1) * qw, 0), tq - near_w)
            bucket = _t5_bucket((kt * tk + tk - 1) - w0 - j)
            w = jnp.zeros((8, ring), F32)
            for bb in range(N_BUCKETS):
                w = jnp.where(bucket == bb, tbl_ref[bb, h] * LOG2E, w)
            big = jnp.broadcast_to(w[0:1], (tk, ring))
            rolled = pltpu.roll(big, ring - (tk - 1), axis=1, stride=1, stride_axis=0)
            bias_ref[kt * tk:(kt + 1) * tk, w0:w0 + near_w] = rolled[:, :near_w]

    lam = (jnp.exp(jnp.sum(lq1_ref[...] * lk1_ref[...], axis=-1, keepdims=True))
           - jnp.exp(jnp.sum(lq2_ref[...] * lk2_ref[...], axis=-1, keepdims=True)) + lam_init)
    q = q_ref[0]
    k_all = k_ref[0]
    lane = lax.broadcasted_iota(I32, q.shape, 1)
    qms = [jnp.where((lane >= m * ATT_QK_DIM) & (lane < (m + 1) * ATT_QK_DIM), q, jnp.zeros_like(q))
           for m in range(2)]
    nqc = tq // qw

    far = jnp.full((8, LANES), MAX_DISTANCE, I32)
    side_const = []
    for rel in (-far, far):
        bucket = _t5_bucket(rel)
        cst = jnp.zeros((8, LANES), F32)
        for bb in range(N_BUCKETS):
            cst = jnp.where(bucket == bb, tbl_ref[bb, h] * LOG2E, cst)
        c_hi = cst.astype(BF16)
        rem = cst - c_hi.astype(F32)
        c_mid = rem.astype(BF16)
        c_lo = (rem - c_mid.astype(F32)).astype(BF16)
        side_const.append([jnp.broadcast_to(p[0:1], k_all.shape) for p in (c_hi, c_mid, c_lo)])
    q_aug, k_side = [], []
    for m in range(2):
        spare = (1 - m) * ATT_QK_DIM
        hot = [jnp.where(lane == spare + i, 1.0, 0.0).astype(BF16) for i in range(3)]
        keep = jnp.where((lane >= spare) & (lane < spare + 3), 0.0, 1.0).astype(BF16)
        q_aug.append(qms[m] + (hot[0] + hot[1] + hot[2]))
        k_kept = k_all * keep
        k_side.append([k_kept + (hot[0] * parts[0] + hot[1] * parts[1] + hot[2] * parts[2]) for parts in side_const])
    nkb, kb = vt_ref.shape[1], vt_ref.shape[3]
    ones_rows = 16

    streams = [(m, c) for m in range(2) for c in range(nqc)]
    run_max = {st: jnp.full((1, qw), NEG_BIG, F32) for st in streams}
    nt = s // tk
    items = [(t, st) for t in range(nt) for st in streams]
    pending = {}
    depth = min(7, len(streams) - 1)

    def logits_part(item):
        t, (m, c) = item
        rows, cols = slice(t * tk, (t + 1) * tk), slice(c * qw, (c + 1) * qw)
        nt_dot = lambda a, b: lax.dot_general(a, b, (((1,), (1,)), ((), ())), preferred_element_type=F32)
        max_off = (t + 1) * tk - 1 - c * qw
        min_off = t * tk - ((c + 1) * qw - 1)
        if max_off <= -MAX_DISTANCE:
            logit = nt_dot(k_side[m][0][rows, :], q_aug[m][cols, :])
        elif min_off >= MAX_DISTANCE:
            logit = nt_dot(k_side[m][1][rows, :], q_aug[m][cols, :])
        else:
            logit = bias_ref[rows, cols] + nt_dot(k_all[rows, :], qms[m][cols, :])
        mx = logit[0:8, :]
        for r in range(1, tk // 8):
            mx = jnp.maximum(mx, logit[r * 8:(r + 1) * 8, :])
        pending[item] = (logit, jnp.max(mx, axis=0, keepdims=True))

    def pv_part(item):
        t, st = item
        m, c = st
        cols = slice(c * qw, (c + 1) * qw)
        logit, tile_max = pending.pop(item)
        m_old = run_max[st]
        m_new = jnp.maximum(m_old, tile_max)
        run_max[st] = m_new
        probs = jnp.exp2(logit - m_new).astype(BF16)
        blk, off = (t * tk) // kb, (t * tk) % kb
        vt_ext = jnp.concatenate([vt_ref[0, blk][:, off:off + tk], jnp.ones((ones_rows, tk), BF16)], axis=0)
        part = jnp.dot(vt_ext, probs, preferred_element_type=F32)
        if t == 0:
            acc_ref[m, :, cols] = part
        else:
            acc_ref[m, :, cols] = acc_ref[m, :, cols] * jnp.exp2(m_old - m_new) + part

    for i in range(len(items) + depth):
        if i < len(items):
            logits_part(items[i])
        if i >= depth:
            pv_part(items[i - depth])
    outs = [acc_ref[m, :ATT_V_DIM, :] / acc_ref[m, ATT_V_DIM:ATT_V_DIM + 1, :] for m in range(2)]
    att = outs[0] - lam * outs[1]
    att = att * lax.rsqrt(jnp.mean(att * att, axis=0, keepdims=True) + EPS)
    o_ref[0] = (att * (g_ref[...] * (1.0 - lam_init))).T.astype(BF16)


def _attention(q, k, v_t, tbl, lq1, lk1, lq2, lk2, subln_g, lam_init, tq):
    bsz, s, _ = q.shape
    nkb, kb = v_t.shape[1], v_t.shape[3]
    small = lambda n: pl.BlockSpec((1, n), lambda h, i, b: (0, 0))
    return pl.pallas_call(
        functools.partial(_attn_body, lam_init=lam_init, tk=2 * LANES),
        grid=(ATT_HEADS, s // tq, bsz),
        in_specs=[pl.BlockSpec(memory_space=pltpu.SMEM),
                  small(ATT_QK_DIM), small(ATT_QK_DIM), small(ATT_QK_DIM), small(ATT_QK_DIM),
                  pl.BlockSpec((ATT_V_DIM, 1), lambda h, i, b: (0, 0)),
                  pl.BlockSpec((1, tq, ATT_V_DIM), lambda h, i, b: (b, i, h)),
                  pl.BlockSpec((1, s, ATT_V_DIM), lambda h, i, b: (b, 0, h)),
                  pl.BlockSpec((1, nkb, ATT_V_DIM, kb), lambda h, i, b: (b, 0, h, 0))],
        out_specs=pl.BlockSpec((1, tq, ATT_V_DIM), lambda h, i, b: (b, i, h)),
        out_shape=jax.ShapeDtypeStruct((bsz, s, ATT_HEADS * ATT_V_DIM), BF16),
        scratch_shapes=[pltpu.VMEM((s, tq), F32),
                        pltpu.VMEM((2, ATT_V_DIM + 16, tq), F32)],
        compiler_params=_params("parallel", "parallel", "arbitrary"),
        name="diff_attention",
    )(tbl, lq1, lk1, lq2, lk2, subln_g.reshape(ATT_V_DIM, 1), q, k, v_t)


def _head_selector(off):
    r = lax.broadcasted_iota(I32, (LANES, SSD_WIDTH), 0)
    c = lax.broadcasted_iota(I32, (LANES, SSD_WIDTH), 1)
    return jnp.where(r - off == c // SSD_HEADDIM, 1.0, 0.0).astype(BF16)


def _expand_heads(cols, selector):
    hi = cols.astype(BF16)
    lo = (cols - hi.astype(F32)).astype(BF16)
    return (jnp.dot(hi, selector, preferred_element_type=F32)
            + jnp.dot(lo, selector, preferred_element_type=F32))


def _ssd_body(z_ref, xbc_ref, dtr_ref, cw_ref, cb_ref, dtb_ref, alog_ref, dskip_ref, ng_ref, o_ref,
              xp_ref, xc_ref, dt_ref, da_ref, cs_ref, cst_ref, ecst_ref, dtt_ref, bt_ref, xdf_ref, xdb_ref,
              dtotf_ref, dtotb_ref, y_ref, hf_ref, hb_ref):
    s = xbc_ref.shape[1]
    ch = SSD_CHUNK
    assert ch == LANES
    nc = s // ch
    pad = 8
    nconv = xbc_ref.shape[2]

    xp_ref[0:pad, :] = jnp.zeros((pad, nconv), F32)
    xp_ref[pad + s:pad + s + pad, :] = jnp.zeros((pad, nconv), F32)
    xp_ref[pad:pad + s, :] = xbc_ref[0]

    def conv_chunk(c, carry):
        r0 = pl.multiple_of(c * ch, ch)
        nwin = ch + 2 * pad
        win = xp_ref[pl.ds(r0, nwin), :]
        acc = jnp.broadcast_to(cb_ref[...], (ch, nconv))
        for j in range(SSD_CONV):
            shift = (SSD_CONV // 2 - j) % nwin
            tap = win if shift == 0 else pltpu.roll(win, shift, axis=0)
            acc = acc + cw_ref[j:j + 1, :] * tap[pad:pad + ch, :]
        xc_ref[pl.ds(r0, ch), :] = _silu(acc)
        return carry

    lax.fori_loop(0, nc, conv_chunk, 0)

    lane = lax.broadcasted_iota(I32, (s, LANES), 1)
    pre = dtr_ref[0] + dtb_ref[...]
    dt = jnp.maximum(pre, 0.0) + jnp.log1p(jnp.exp(-jnp.abs(pre)))
    dt = jnp.where(lane < 2 * SSD_HEADS, dt, 0.0)
    dt_ref[...] = dt
    da_ref[...] = dt * (-jnp.exp(alog_ref[...]))

    row = lax.broadcasted_iota(I32, (ch, ch), 0)
    col = lax.broadcasted_iota(I32, (ch, ch), 1)
    tri = jnp.where(col <= row, 1.0, 0.0).astype(F32)
    lower = col <= row
    upper = col >= row
    lane_c = lax.broadcasted_iota(I32, (ch, LANES), 1)
    st_rows = lax.broadcasted_iota(I32, (SSD_BC, SSD_WIDTH), 0) // SSD_STATE
    st_cols = lax.broadcasted_iota(I32, (SSD_BC, SSD_WIDTH), 1) // (SSD_WIDTH // SSD_GROUPS)
    same_group = st_rows == st_cols
    sel_f = _head_selector(0)
    sel_b = _head_selector(SSD_HEADS)

    hf_ref[...] = jnp.zeros_like(hf_ref)
    hb_ref[...] = jnp.zeros_like(hb_ref)

    def wide(x):
        return jnp.concatenate([x[c * ch:(c + 1) * ch, :] for c in range(nc)], axis=1)

    da_wide = wide(da_ref[...])
    cs_wide = jnp.dot(tri, da_wide, precision=HIGHEST, preferred_element_type=F32)
    for c in range(nc):
        cs_ref[c * ch:(c + 1) * ch, :] = cs_wide[:, c * LANES:(c + 1) * LANES]
    cst_ref[...] = cs_wide.T
    ecst_ref[...] = (cs_wide - da_wide).T
    dtt_ref[...] = wide(dt_ref[...]).T
    bt_ref[...] = wide(xc_ref[:, SSD_WIDTH:SSD_WIDTH + SSD_BC]).T.astype(BF16)

    def prep_chunk(c, carry):
        r0 = pl.multiple_of(c * ch, ch)
        r8 = pl.multiple_of(c * 8, 8)
        da = da_ref[pl.ds(r0, ch), :]
        dtc = dt_ref[pl.ds(r0, ch), :]
        cs = cs_ref[pl.ds(r0, ch), :]
        tot = cs[ch - 1:ch, :]
        xs = xc_ref[pl.ds(r0, ch), 0:SSD_WIDTH]
        tot8 = jnp.broadcast_to(jnp.exp(tot), (8, LANES))
        ff = _expand_heads(jnp.concatenate([dtc * jnp.exp(tot - cs), tot8], axis=0), sel_f)
        xdf_ref[pl.ds(r0, ch), :] = (xs * ff[0:ch]).astype(BF16)
        dtotf_ref[pl.ds(r8, 8), :] = ff[ch:ch + 8]
        fb = _expand_heads(jnp.concatenate([dtc * jnp.exp(cs - da), tot8], axis=0), sel_b)
        xdb_ref[pl.ds(r0, ch), :] = (xs * fb[0:ch]).astype(BF16)
        dtotb_ref[pl.ds(r8, 8), :] = fb[ch:ch + 8]
        return carry

    lax.fori_loop(0, nc, prep_chunk, 0, unroll=2)

    def fwd_chunk(c, carry):
        r0 = pl.multiple_of(c * ch, ch)
        da = da_ref[pl.ds(r0, ch), :]
        cs = cs_ref[pl.ds(r0, ch), :]
        ecs = cs - da
        nh2 = 2 * SSD_HEADS
        cs_t = cst_ref[pl.ds(r0, nh2), :]
        ecs_t = ecst_ref[pl.ds(r0, nh2), :]
        dt_t = dtt_ref[pl.ds(r0, nh2), :]
        xs = xc_ref[pl.ds(r0, ch), 0:SSD_WIDTH]
        cm = xc_ref[pl.ds(r0, ch), SSD_WIDTH + SSD_BC:SSD_WIDTH + 2 * SSD_BC]
        bt = bt_ref[pl.ds(r0, ch), :]
        cmb = cm.astype(BF16)
        xsb = xs.astype(BF16)

        hf = hf_ref[...]
        y_ref[pl.ds(r0, ch), :] = (jnp.dot(cmb, hf.astype(BF16), preferred_element_type=F32)
                                   * _expand_heads(jnp.exp(cs), sel_f))
        upd = jnp.dot(bt, xdf_ref[pl.ds(r0, ch), :], preferred_element_type=F32)
        hf_ref[...] = hf * dtotf_ref[pl.ds(pl.multiple_of(c * 8, 8), 1), :] + jnp.where(same_group, upd, 0.0)

        g_mats = []
        for g in range(SSD_GROUPS):
            cg = jnp.where(lane_c // SSD_STATE == g, cmb, jnp.zeros_like(cmb))
            g_mats.append(jnp.dot(cg, bt, preferred_element_type=F32))
        for pair in range(SSD_HEADS // 2):
            xpair = xsb[:, pair * LANES:(pair + 1) * LANES]
            ypair = jnp.zeros((ch, LANES), F32)
            for sub in range(2):
                hh = 2 * pair + sub
                g = hh // (SSD_HEADS // SSD_GROUPS)
                d_f = cs[:, hh:hh + 1] - cs_t[hh:hh + 1, :]
                d_b = ecs_t[SSD_HEADS + hh:SSD_HEADS + hh + 1, :] - ecs[:, SSD_HEADS + hh:SSD_HEADS + hh + 1]
                decay = jnp.exp(jnp.where(lower, d_f, d_b))
                wgt = (jnp.where(lower, dt_t[hh:hh + 1, :], 0.0)
                       + jnp.where(upper, dt_t[SSD_HEADS + hh:SSD_HEADS + hh + 1, :], 0.0))
                m = (g_mats[g] * decay * wgt).astype(BF16)
                xh = jnp.where(lane_c // SSD_HEADDIM == sub, xpair, jnp.zeros_like(xpair))
                ypair = ypair + jnp.dot(m, xh, preferred_element_type=F32)
            y_ref[pl.ds(r0, ch), pair * LANES:(pair + 1) * LANES] += ypair
        return carry

    lax.fori_loop(0, nc, fwd_chunk, 0, unroll=4)

    def bwd_chunk(i, carry):
        c = nc - 1 - i
        r0 = pl.multiple_of(c * ch, ch)
        da = da_ref[pl.ds(r0, ch), :]
        cs = cs_ref[pl.ds(r0, ch), :]
        tot = cs[ch - 1:ch, :]
        ecs = cs - da
        xs = xc_ref[pl.ds(r0, ch), 0:SSD_WIDTH]
        cm = xc_ref[pl.ds(r0, ch), SSD_WIDTH + SSD_BC:SSD_WIDTH + 2 * SSD_BC]
        bt = bt_ref[pl.ds(r0, ch), :]

        hb = hb_ref[...]
        y_off = (jnp.dot(cm.astype(BF16), hb.astype(BF16), preferred_element_type=F32)
                 * _expand_heads(jnp.exp(tot - ecs), sel_b))
        upd = jnp.dot(bt, xdb_ref[pl.ds(r0, ch), :], preferred_element_type=F32)
        hb_ref[...] = hb * dtotb_ref[pl.ds(pl.multiple_of(c * 8, 8), 1), :] + jnp.where(same_group, upd, 0.0)

        y = y_ref[pl.ds(r0, ch), :] + y_off + dskip_ref[...] * xs
        y = y * _silu(z_ref[0, pl.ds(r0, ch), :])
        gw = SSD_WIDTH // SSD_GROUPS
        yn = jnp.concatenate([_rms(y[:, g * gw:(g + 1) * gw]) for g in range(SSD_GROUPS)], axis=-1)
        o_ref[0, pl.ds(r0, ch), :] = (yn * ng_ref[...]).astype(BF16)
        return carry

    lax.fori_loop(0, nc, bwd_chunk, 0, unroll=4)


def _ssd(z, xbc, dt_raw, conv_w, conv_b, dt_bias, a_log, d_skip, norm_g):
    bsz, s, nconv = xbc.shape
    tok = lambda wd: pl.BlockSpec((1, s, wd), lambda b: (b, 0, 0))
    full = lambda a: pl.BlockSpec(a.shape, lambda b: (0, 0))
    return pl.pallas_call(
        _ssd_body,
        grid=(bsz,),
        in_specs=[tok(SSD_WIDTH), tok(nconv), tok(LANES), full(conv_w), full(conv_b), full(dt_bias),
                  full(a_log), full(d_skip), full(norm_g)],
        out_specs=tok(SSD_WIDTH),
        out_shape=jax.ShapeDtypeStruct((bsz, s, SSD_WIDTH), BF16),
        scratch_shapes=[pltpu.VMEM((s + 16, nconv), F32),
                        pltpu.VMEM((s, nconv), F32),
                        pltpu.VMEM((s, LANES), F32),
                        pltpu.VMEM((s, LANES), F32),
                        pltpu.VMEM((s, LANES), F32),
                        pltpu.VMEM((s, LANES), F32),
                        pltpu.VMEM((s, LANES), F32),
                        pltpu.VMEM((s, LANES), F32),
                        pltpu.VMEM((s, SSD_BC), BF16),
                        pltpu.VMEM((s, SSD_WIDTH), BF16),
                        pltpu.VMEM((s, SSD_WIDTH), BF16),
                        pltpu.VMEM((s // SSD_CHUNK * 8, SSD_WIDTH), F32),
                        pltpu.VMEM((s // SSD_CHUNK * 8, SSD_WIDTH), F32),
                        pltpu.VMEM((s, SSD_WIDTH), F32),
                        pltpu.VMEM((SSD_BC, SSD_WIDTH), F32),
                        pltpu.VMEM((SSD_BC, SSD_WIDTH), F32)],
        compiler_params=_params("parallel"),
        name="ssd_mixer",
    )(z, xbc, dt_raw, conv_w, conv_b, dt_bias, a_log, d_skip, norm_g)


def _outproj_body(att_ref, ssd_ref, x_ref, g1_ref, sh_ref, sc_ref, ng_ref, w_ref, rw_ref,
                  x1_ref, h2_ref, aff_ref):
    na = att_ref.shape[2]
    mix = (jnp.dot(att_ref[0], w_ref[0:na, :], preferred_element_type=F32)
           + jnp.dot(ssd_ref[0], w_ref[na:, :], preferred_element_type=F32))
    x1 = x_ref[0] + g1_ref[0] * mix
    x1_ref[0] = x1
    h2 = _rms(x1) * ng_ref[...]
    h2 = h2 * (1.0 + sc_ref[0]) + sh_ref[0]
    h2_hi = h2.astype(BF16)
    h2_ref[0] = h2_hi
    h2_lo = (h2 - h2_hi.astype(F32)).astype(BF16)
    nt = lambda a, b: lax.dot_general(a, b, (((1,), (1,)), ((), ())), preferred_element_type=F32)
    ne = rw_ref.shape[1]
    both = nt(rw_ref[...].reshape(2 * ne, rw_ref.shape[2]), h2_hi)
    logits = both[0:ne] + (nt(rw_ref[0], h2_lo) + both[ne:2 * ne])
    e = jnp.exp(logits - jnp.max(logits, axis=0, keepdims=True))
    aff_ref[0] = e / jnp.sum(e, axis=0, keepdims=True)


def _out_proj(att, ssd, x, g1, sh2, sc2, ng, w_out, rw_t, ts):
    bsz, s, d = x.shape
    tok = lambda wd: pl.BlockSpec((1, ts, wd), lambda b, t: (b, t, 0))
    vec = pl.BlockSpec((1, 1, d), lambda b, t: (b, 0, 0))
    full = lambda a: pl.BlockSpec(a.shape, lambda b, t: (0, 0))
    ne = rw_t.shape[1]
    return pl.pallas_call(
        _outproj_body,
        grid=(bsz, s // ts),
        in_specs=[tok(att.shape[2]), tok(ssd.shape[2]), tok(d), vec, vec, vec, full(ng), full(w_out),
                  pl.BlockSpec(rw_t.shape, lambda b, t: (0, 0, 0))],
        out_specs=[tok(d), tok(d), pl.BlockSpec((1, ne, ts), lambda b, t: (b, 0, t))],
        out_shape=[jax.ShapeDtypeStruct((bsz, s, d), F32),
                   jax.ShapeDtypeStruct((bsz, s, d), BF16),
                   jax.ShapeDtypeStruct((bsz, ne, s), F32)],
        compiler_params=_params("parallel", "arbitrary"),
        name="out_proj_router",
    )(att, ssd, x, g1, sh2, sc2, ng, w_out, rw_t)


def _topk_body(aff_ref, pos_ref, post_ref, ut_ref, *, cap):
    nrow, s = aff_ref.shape
    r = lax.broadcasted_iota(I32, (s, s), 0)
    c = lax.broadcasted_iota(I32, (s, s), 1)
    ut_ref[...] = jnp.where(r < c, 1.0, 0.0).astype(BF16)

    bits = pltpu.bitcast(aff_ref[...], I32)

    def count(mask):
        return jnp.sum(jnp.where(mask, 1, 0), axis=-1, keepdims=True)

    thr = jnp.zeros((nrow, 1), I32)
    for bit in range(30, -1, -1):
        cand = thr | (1 << bit)
        thr = jnp.where(count(bits >= cand) >= cap, cand, thr)
    gt = bits > thr
    eq = bits == thr
    need = cap - count(gt)
    eq_rank = jnp.dot(jnp.where(eq, 1.0, 0.0).astype(BF16), ut_ref[...], preferred_element_type=F32)
    sel = gt | (eq & (eq_rank < need.astype(F32)))
    slot = jnp.dot(jnp.where(sel, 1.0, 0.0).astype(BF16), ut_ref[...], preferred_element_type=F32)
    pos = jnp.where(sel, slot, -1.0)
    pos_ref[...] = pos.astype(I32)
    post_ref[...] = pos.T.astype(I32)


def _topk(aff_t, cap):
    bsz, ne, s = aff_t.shape
    nrow = bsz * ne
    pos, pos_t = pl.pallas_call(
        functools.partial(_topk_body, cap=cap),
        grid=(1,),
        in_specs=[pl.BlockSpec((nrow, s), lambda i: (0, 0))],
        out_specs=[pl.BlockSpec((nrow, s), lambda i: (0, 0)),
                   pl.BlockSpec((s, nrow), lambda i: (0, 0))],
        out_shape=[jax.ShapeDtypeStruct((nrow, s), I32),
                   jax.ShapeDtypeStruct((s, nrow), I32)],
        scratch_shapes=[pltpu.VMEM((s, s), BF16)],
        compiler_params=_params("arbitrary"),
        name="expert_topk",
    )(aff_t.reshape(nrow, s))
    return pos.reshape(bsz, ne, s), pos_t


def _gather_body(h2_ref, pos_ref, aff_ref, xg_ref, gc_ref, *, cap):
    ne, s = pos_ref.shape[1], pos_ref.shape[2]
    slot = lax.broadcasted_iota(I32, (cap, s), 0)
    h2 = h2_ref[0]
    for e in range(ne):
        hit = pos_ref[0, e:e + 1, :] == slot
        xg_ref[e] = jnp.dot(jnp.where(hit, 1.0, 0.0).astype(BF16), h2,
                            preferred_element_type=F32).astype(BF16)
        gate = jnp.sum(jnp.where(hit, aff_ref[0, e:e + 1, :], 0.0), axis=-1, keepdims=True)
        gc_ref[e] = jnp.broadcast_to(gate, (cap, LANES))


def _gather(h2, pos, aff_t, cap):
    bsz, s, d = h2.shape
    ne = pos.shape[1]
    return pl.pallas_call(
        functools.partial(_gather_body, cap=cap),
        grid=(bsz,),
        in_specs=[pl.BlockSpec((1, s, d), lambda b: (b, 0, 0)),
                  pl.BlockSpec((1, ne, s), lambda b: (b, 0, 0)),
                  pl.BlockSpec((1, ne, s), lambda b: (b, 0, 0))],
        out_specs=[pl.BlockSpec((ne, cap, d), lambda b: (0, b, 0)),
                   pl.BlockSpec((ne, cap, LANES), lambda b: (0, b, 0))],
        out_shape=[jax.ShapeDtypeStruct((ne, bsz * cap, d), BF16),
                   jax.ShapeDtypeStruct((ne, bsz * cap, LANES), F32)],
        compiler_params=_params("parallel"),
        name="moe_gather",
    )(h2, pos, aff_t)


def _ffn_body(xg_ref, gc_ref, wg_ref, wu_ref, wd_ref, y_ref, acc_ref):
    f = pl.program_id(1)

    @pl.when((pl.program_id(0) == 0) & (f == 0))
    def _():
        acc_ref[...] = jnp.zeros_like(acc_ref)

    wg = wg_ref[0].astype(BF16)
    wu = wu_ref[0].astype(BF16)
    wd = wd_ref[0].astype(BF16)
    nq = 4
    rq = xg_ref.shape[1] // nq
    hidden = {}

    def gate_up(r):
        xg = xg_ref[0, r * rq:(r + 1) * rq, :]
        gate = jnp.dot(xg, wg, preferred_element_type=F32)
        up = jnp.dot(xg, wu, preferred_element_type=F32)
        hidden[r] = (_silu(gate) * up).astype(BF16)

    def down(r):
        rows = slice(r * rq, (r + 1) * rq)
        part = jnp.dot(hidden.pop(r), wd, preferred_element_type=F32)
        acc_ref[rows, :] = jnp.where(f == 0, part, acc_ref[rows, :] + part)

    gate_up(0)
    for r in range(nq):
        if r + 1 < nq:
            gate_up(r + 1)
        down(r)

    @pl.when(f == pl.num_programs(1) - 1)
    def _():
        m, d = acc_ref.shape
        gcol = gc_ref[0]
        y_ref[0] = (acc_ref[...] * jnp.concatenate([gcol] * (d // LANES), axis=-1)).astype(BF16)


def _ffn(xg, gc, w_gate, w_up, w_down, tf):
    ne, m, d = xg.shape
    ff = w_gate.shape[2]
    return pl.pallas_call(
        _ffn_body,
        grid=(ne, ff // tf),
        in_specs=[pl.BlockSpec((1, m, d), lambda e, f: (e, 0, 0)),
                  pl.BlockSpec((1, m, LANES), lambda e, f: (e, 0, 0)),
                  pl.BlockSpec((1, d, tf), lambda e, f: (e, 0, f)),
                  pl.BlockSpec((1, d, tf), lambda e, f: (e, 0, f)),
                  pl.BlockSpec((1, tf, d), lambda e, f: (e, f, 0))],
        out_specs=pl.BlockSpec((1, m, d), lambda e, f: (e, 0, 0)),
        out_shape=jax.ShapeDtypeStruct((ne, m, d), BF16),
        scratch_shapes=[pltpu.VMEM((m, d), F32)],
        compiler_params=_params("parallel", "arbitrary"),
        name="moe_ffn",
    )(xg, gc, w_gate, w_up, w_down)


def _scatter_body(post_ref, y_ref, x1_ref, g2_ref, ng_ref, o_ref, *, cap):
    ne = y_ref.shape[0]
    ts, nrow = post_ref.shape
    d = y_ref.shape[2]
    r = lax.broadcasted_iota(I32, (nrow, LANES), 0)
    c = lax.broadcasted_iota(I32, (nrow, LANES), 1)
    pick = jnp.where(r == pl.program_id(0) * ne + c, 1.0, 0.0).astype(BF16)
    post = jnp.dot(post_ref[...].astype(F32).astype(BF16), pick, preferred_element_type=F32)
    slot = lax.broadcasted_iota(I32, (ts, cap), 1).astype(F32)
    onehot = jnp.concatenate(
        [jnp.where(post[:, e:e + 1] == slot, 1.0, 0.0).astype(BF16) for e in range(ne)], axis=-1)
    moe = jnp.dot(onehot, y_ref[...].reshape(ne * cap, d), preferred_element_type=F32)
    x2 = x1_ref[0] + g2_ref[0] * moe
    o_ref[0] = _rms(x2) * ng_ref[...]


def _scatter(post, y, x1, g2, ng, cap, ts):
    bsz, s, d = x1.shape
    ne = y.shape[0]
    return pl.pallas_call(
        functools.partial(_scatter_body, cap=cap),
        grid=(bsz, s // ts),
        in_specs=[pl.BlockSpec((ts, post.shape[1]), lambda b, t: (t, 0)),
                  pl.BlockSpec((ne, cap, d), lambda b, t: (0, b, 0)),
                  pl.BlockSpec((1, ts, d), lambda b, t: (b, t, 0)),
                  pl.BlockSpec((1, 1, d), lambda b, t: (b, 0, 0)),
                  pl.BlockSpec((1, d), lambda b, t: (0, 0))],
        out_specs=pl.BlockSpec((1, ts, d), lambda b, t: (b, t, 0)),
        out_shape=jax.ShapeDtypeStruct((bsz, s, d), F32),
        compiler_params=_params("parallel", "arbitrary"),
        name="moe_scatter_final",
    )(post, y, x1, g2, ng)


def _pad_lanes(a, n=LANES):
    return jnp.pad(a, [(0, 0)] * (a.ndim - 1) + [(0, n - a.shape[-1])])


def kernel(x, c, ada_w, ada_b, norm_mix_g, norm_ffn_g, norm_final_g, w_in, lambda_q1, lambda_k1, lambda_q2, lambda_k2, attn_subln_g, rel_bias_table, conv_w, conv_b, dt_bias_f, dt_bias_b, A_log_f, A_log_b, D_skip, ssm_norm_g, w_out, router_w, w_gate, w_up, w_down):
    bsz, s, d = x.shape
    depth = ada_w.shape[0]
    ne = router_w.shape[2]
    cap = CAPACITY_FACTOR * s // ne
    att_w = ATT_HEADS * ATT_V_DIM
    nconv = conv_w.shape[3]
    splits = (att_w, att_w, att_w, SSD_WIDTH, nconv, 2 * SSD_HEADS)
    offs = [0]
    for wd in splits:
        offs.append(offs[-1] + wd)
    assert offs[-1] == w_in.shape[2]
    row = lambda a: a.reshape(1, -1)

    assert depth == 1, "the final RMSNorm is fused into the (single) layer's scatter kernel"
    for l in range(depth):
        lam_init = 0.8 - 0.6 * math.exp(-0.3 * l)
        mod = _ada_mod(c, ada_w[l], ada_b[l])
        sh1, sc1, g1, sh2, sc2, g2 = [m.reshape(bsz, 1, d) for m in jnp.split(mod, 6, axis=-1)]

        q, k, v_t, z, xbc, dt_raw = _in_proj(x, sh1, sc1, row(norm_mix_g[l]), w_in[l].astype(BF16), offs, ts=1024)

        att = _attention(q, k, v_t, rel_bias_table, row(lambda_q1[l]), row(lambda_k1[l]),
                         row(lambda_q2[l]), row(lambda_k2[l]), row(attn_subln_g[l]), lam_init, tq=2048)

        dt_bias = _pad_lanes(jnp.concatenate([dt_bias_f[l], dt_bias_b[l]]).reshape(1, -1))
        a_log = _pad_lanes(jnp.concatenate([A_log_f[l], A_log_b[l]]).reshape(1, -1))
        d_skip = jnp.repeat(D_skip[l], SSD_HEADDIM).reshape(1, -1)
        ssd = _ssd(z, xbc, dt_raw, conv_w[l].reshape(SSD_CONV, nconv), row(conv_b[l]), dt_bias, a_log,
                   d_skip, row(ssm_norm_g[l]))

        rw_t = router_w[l].T
        rw_hi = rw_t.astype(BF16)
        rw_split = jnp.stack([rw_hi, (rw_t - rw_hi.astype(F32)).astype(BF16)])
        x1, h2, aff_t = _out_proj(att, ssd, x, g1, sh2, sc2, row(norm_ffn_g[l]), w_out[l].astype(BF16),
                                  rw_split, ts=1024)
        pos, pos_t = _topk(aff_t, cap)
        xg, gc = _gather(h2, pos, aff_t, cap)
        y = _ffn(xg, gc, w_gate[l], w_up[l], w_down[l], tf=256)
        x = _scatter(pos_t, y, x1, g2, row(norm_final_g), cap, ts=1024)
    return x
```

```python
import functools
import math

import jax
import jax.numpy as jnp
from jax import lax
from jax.experimental import pallas as pl
from jax.experimental.pallas import tpu as pltpu

F32 = jnp.float32
BF16 = jnp.bfloat16
I32 = jnp.int32
HIGHEST = lax.Precision.HIGHEST

ATT_HEADS = 4
ATT_QK_DIM = 64
ATT_V_DIM = 128
N_BUCKETS = 32
MAX_DISTANCE = 128
SSD_HEADS = 8
SSD_HEADDIM = 64
SSD_GROUPS = 2
SSD_STATE = 64
SSD_CONV = 5
SSD_CHUNK = 128
SSD_WIDTH = SSD_HEADS * SSD_HEADDIM
SSD_BC = SSD_GROUPS * SSD_STATE
N_EXPERTS = 16
CAPACITY_FACTOR = 2
EPS = 1e-6
LANES = 128
NEG_BIG = -1e30
LOG2E = math.log2(math.e)

VMEM_LIMIT = 56 * 1024 * 1024


def _params(*semantics):
    return pltpu.CompilerParams(dimension_semantics=semantics, vmem_limit_bytes=VMEM_LIMIT)


def _silu(v):
    return v * jax.nn.sigmoid(v)


def _rms(v, eps=EPS):
    return v * lax.rsqrt(jnp.mean(v * v, axis=-1, keepdims=True) + eps)


def _split_bf16(v):
    hi = v.astype(BF16)
    return hi, (v - hi.astype(F32)).astype(BF16)


def _mod_body(c_ref, w_ref, b_ref, o_ref):
    a_hi, a_lo = _split_bf16(_silu(c_ref[...]))
    w_hi, w_lo = _split_bf16(w_ref[...])
    dot = functools.partial(jnp.dot, preferred_element_type=F32)
    o_ref[...] = dot(a_hi, w_hi) + (dot(a_lo, w_hi) + dot(a_hi, w_lo)) + b_ref[...]


def _ada_mod(c, w, b):
    bsz, d = c.shape
    n = w.shape[1]
    tn = n // 4
    return pl.pallas_call(
        _mod_body,
        grid=(n // tn,),
        in_specs=[pl.BlockSpec((bsz, d), lambda j: (0, 0)),
                  pl.BlockSpec((d, tn), lambda j: (0, j)),
                  pl.BlockSpec((1, tn), lambda j: (0, j))],
        out_specs=pl.BlockSpec((bsz, tn), lambda j: (0, j)),
        out_shape=jax.ShapeDtypeStruct((bsz, n), F32),
        compiler_params=_params("arbitrary"),
        name="ada_mod",
    )(c, w, b.reshape(1, n))


def _inproj_body(x_ref, sh_ref, sc_ref, g_ref, w_ref, q_ref, k_ref, vt_ref, z_ref, xbc_ref, dt_ref, wvt_ref,
                 *, offs, qscale):
    @pl.when((pl.program_id(0) == 0) & (pl.program_id(1) == 0))
    def _():
        wvt_ref[...] = w_ref[:, offs[2]:offs[3]].astype(F32).T.astype(BF16)

    h = _rms(x_ref[0]) * g_ref[...]
    h = (h * (1.0 + sc_ref[0]) + sh_ref[0]).astype(BF16)

    def proj(i):
        return jnp.dot(h, w_ref[:, offs[i]:offs[i + 1]], preferred_element_type=F32)

    q_ref[0] = (proj(0) * qscale).astype(BF16)
    k_ref[0] = proj(1).astype(BF16)
    vt_ref[0, 0] = lax.dot_general(wvt_ref[...], h, (((1,), (1,)), ((), ())),
                                   preferred_element_type=F32).astype(BF16)
    z_ref[0] = proj(3)
    xbc_ref[0] = proj(4)
    dt = proj(5)
    dt_ref[0] = jnp.concatenate([dt, jnp.zeros((dt.shape[0], LANES - dt.shape[1]), F32)], axis=1)


def _in_proj(x, sh, sc, g, w, offs, ts):
    bsz, s, d = x.shape
    wq, wk, wv, wz, wxbc = [offs[i + 1] - offs[i] for i in range(5)]
    tok = lambda wd: pl.BlockSpec((1, ts, wd), lambda b, t: (b, t, 0))
    vec = pl.BlockSpec((1, 1, d), lambda b, t: (b, 0, 0))
    tok_out = lambda wd, dt: (tok(wd), jax.ShapeDtypeStruct((bsz, s, wd), dt))
    outs = [tok_out(wq, BF16), tok_out(wk, BF16),
            (pl.BlockSpec((1, 1, wv, ts), lambda b, t: (b, t, 0, 0)), jax.ShapeDtypeStruct((bsz, s // ts, wv, ts), BF16)),
            tok_out(wz, F32), tok_out(wxbc, F32), tok_out(LANES, F32)]
    return pl.pallas_call(
        functools.partial(_inproj_body, offs=tuple(offs), qscale=ATT_QK_DIM ** -0.5 * LOG2E),
        grid=(bsz, s // ts),
        in_specs=[tok(d), vec, vec,
                  pl.BlockSpec((1, d), lambda b, t: (0, 0)),
                  pl.BlockSpec(w.shape, lambda b, t: (0, 0))],
        out_specs=[o[0] for o in outs],
        out_shape=[o[1] for o in outs],
        scratch_shapes=[pltpu.VMEM((wv, d), BF16)],
        compiler_params=_params("arbitrary", "arbitrary"),
        name="in_proj",
    )(x, sh, sc, g, w)


def _t5_bucket(rel):
    n = jnp.abs(rel)
    large = jnp.full(rel.shape, 8, I32)
    for t in (12, 16, 23, 32, 46, 64, 91):
        large = large + jnp.where(n >= t, 1, 0)
    return jnp.where(rel > 0, 16, 0) + jnp.where(n < 8, n, large)


def _attn_body(tbl_ref, lq1_ref, lk1_ref, lq2_ref, lk2_ref, g_ref, q_ref, k_ref, vt_ref, o_ref,
               bias_ref, acc_ref, *, lam_init, tk):
    h = pl.program_id(0)
    qi = pl.program_id(1)
    s, tq = bias_ref.shape
    qw = 2 * LANES
    assert tq == s and tk == qw, "near / far tile classification below is static for one full-width step"
    near_w = 3 * qw
    ring = pl.next_power_of_2(tk + near_w - 1)

    @pl.when(pl.program_id(2) == 0)
    def _():
        j = lax.broadcasted_iota(I32, (8, ring), 1)
        for kt in range(s // tk):
            w0 = min(max((kt - 1) * qw, 0), tq - near_w)
            bucket = _t5_bucket((kt * tk + tk - 1) - w0 - j)
            w = jnp.zeros((8, ring), F32)
            for bb in range(N_BUCKETS):
                w = jnp.where(bucket == bb, tbl_ref[bb, h] * LOG2E, w)
            big = jnp.broadcast_to(w[0:1], (tk, ring))
            rolled = pltpu.roll(big, ring - (tk - 1), axis=1, stride=1, stride_axis=0)
            bias_ref[kt * tk:(kt + 1) * tk, w0:w0 + near_w] = rolled[:, :near_w]

    lam = (jnp.exp(jnp.sum(lq1_ref[...] * lk1_ref[...], axis=-1, keepdims=True))
           - jnp.exp(jnp.sum(lq2_ref[...] * lk2_ref[...], axis=-1, keepdims=True)) + lam_init)
    q = q_ref[0]
    k_all = k_ref[0]
    lane = lax.broadcasted_iota(I32, q.shape, 1)
    qms = [jnp.where((lane >= m * ATT_QK_DIM) & (lane < (m + 1) * ATT_QK_DIM), q, jnp.zeros_like(q))
           for m in range(2)]
    nqc = tq // qw

    far = jnp.full((8, LANES), MAX_DISTANCE, I32)
    side_const = []
    for rel in (-far, far):
        bucket = _t5_bucket(rel)
        cst = jnp.zeros((8, LANES), F32)
        for bb in range(N_BUCKETS):
            cst = jnp.where(bucket == bb, tbl_ref[bb, h] * LOG2E, cst)
        c_hi = cst.astype(BF16)
        rem = cst - c_hi.astype(F32)
        c_mid = rem.astype(BF16)
        c_lo = (rem - c_mid.astype(F32)).astype(BF16)
        side_const.append([jnp.broadcast_to(p[0:1], k_all.shape) for p in (c_hi, c_mid, c_lo)])
    q_aug, k_side = [], []
    for m in range(2):
        spare = (1 - m) * ATT_QK_DIM
        hot = [jnp.where(lane == spare + i, 1.0, 0.0).astype(BF16) for i in range(3)]
        keep = jnp.where((lane >= spare) & (lane < spare + 3), 0.0, 1.0).astype(BF16)
        q_aug.append(qms[m] + (hot[0] + hot[1] + hot[2]))
        k_kept = k_all * keep
        k_side.append([k_kept + (hot[0] * parts[0] + hot[1] * parts[1] + hot[2] * parts[2]) for parts in side_const])
    nkb, kb = vt_ref.shape[1], vt_ref.shape[3]
    ones_rows = 16

    streams = [(m, c) for m in range(2) for c in range(nqc)]
    run_max = {st: jnp.full((1, qw), NEG_BIG, F32) for st in streams}
    nt = s // tk
    items = [(t, st) for t in range(nt) for st in streams]
    pending = {}
    depth = min(7, len(streams) - 1)

    def logits_part(item):
        t, (m, c) = item
        rows, cols = slice(t * tk, (t + 1) * tk), slice(c * qw, (c + 1) * qw)
        nt_dot = lambda a, b: lax.dot_general(a, b, (((1,), (1,)), ((), ())), preferred_element_type=F32)
        max_off = (t + 1) * tk - 1 - c * qw
        min_off = t * tk - ((c + 1) * qw - 1)
        if max_off <= -MAX_DISTANCE:
            logit = nt_dot(k_side[m][0][rows, :], q_aug[m][cols, :])
        elif min_off >= MAX_DISTANCE:
            logit = nt_dot(k_side[m][1][rows, :], q_aug[m][cols, :])
        else:
            logit = bias_ref[rows, cols] + nt_dot(k_all[rows, :], qms[m][cols, :])
        mx = logit[0:8, :]
        for r in range(1, tk // 8):
            mx = jnp.maximum(mx, logit[r * 8:(r + 1) * 8, :])
        pending[item] = (logit, jnp.max(mx, axis=0, keepdims=True))

    def pv_part(item):
        t, st = item
        m, c = st
        cols = slice(c * qw, (c + 1) * qw)
        logit, tile_max = pending.pop(item)
        m_old = run_max[st]
        m_new = jnp.maximum(m_old, tile_max)
        run_max[st] = m_new
        probs = jnp.exp2(logit - m_new).astype(BF16)
        blk, off = (t * tk) // kb, (t * tk) % kb
        vt_ext = jnp.concatenate([vt_ref[0, blk][:, off:off + tk], jnp.ones((ones_rows, tk), BF16)], axis=0)
        part = jnp.dot(vt_ext, probs, preferred_element_type=F32)
        if t == 0:
            acc_ref[m, :, cols] = part
        else:
            acc_ref[m, :, cols] = acc_ref[m, :, cols] * jnp.exp2(m_old - m_new) + part

    for i in range(len(items) + depth):
        if i < len(items):
            logits_part(items[i])
        if i >= depth:
            pv_part(items[i - depth])
    outs = [acc_ref[m, :ATT_V_DIM, :] / acc_ref[m, ATT_V_DIM:ATT_V_DIM + 1, :] for m in range(2)]
    att = outs[0] - lam * outs[1]
    att = att * lax.rsqrt(jnp.mean(att * att, axis=0, keepdims=True) + EPS)
    o_ref[0] = (att * (g_ref[...] * (1.0 - lam_init))).T.astype(BF16)


def _attention(q, k, v_t, tbl, lq1, lk1, lq2, lk2, subln_g, lam_init, tq):
    bsz, s, _ = q.shape
    nkb, kb = v_t.shape[1], v_t.shape[3]
    small = lambda n: pl.BlockSpec((1, n), lambda h, i, b: (0, 0))
    return pl.pallas_call(
        functools.partial(_attn_body, lam_init=lam_init, tk=2 * LANES),
        grid=(ATT_HEADS, s // tq, bsz),
        in_specs=[pl.BlockSpec(memory_space=pltpu.SMEM),
                  small(ATT_QK_DIM), small(ATT_QK_DIM), small(ATT_QK_DIM), small(ATT_QK_DIM),
                  pl.BlockSpec((ATT_V_DIM, 1), lambda h, i, b: (0, 0)),
                  pl.BlockSpec((1, tq, ATT_V_DIM), lambda h, i, b: (b, i, h)),
                  pl.BlockSpec((1, s, ATT_V_DIM), lambda h, i, b: (b, 0, h)),
                  pl.BlockSpec((1, nkb, ATT_V_DIM, kb), lambda h, i, b: (b, 0, h, 0))],
        out_specs=pl.BlockSpec((1, tq, ATT_V_DIM), lambda h, i, b: (b, i, h)),
        out_shape=jax.ShapeDtypeStruct((bsz, s, ATT_HEADS * ATT_V_DIM), BF16),
        scratch_shapes=[pltpu.VMEM((s, tq), F32),
                        pltpu.VMEM((2, ATT_V_DIM + 16, tq), F32)],
        compiler_params=_params("parallel", "parallel", "arbitrary"),
        name="diff_attention",
    )(tbl, lq1, lk1, lq2, lk2, subln_g.reshape(ATT_V_DIM, 1), q, k, v_t)


def _head_selector(off):
    r = lax.broadcasted_iota(I32, (LANES, SSD_WIDTH), 0)
    c = lax.broadcasted_iota(I32, (LANES, SSD_WIDTH), 1)
    return jnp.where(r - off == c // SSD_HEADDIM, 1.0, 0.0).astype(BF16)


def _expand_heads(cols, selector):
    hi = cols.astype(BF16)
    lo = (cols - hi.astype(F32)).astype(BF16)
    return (jnp.dot(hi, selector, preferred_element_type=F32)
            + jnp.dot(lo, selector, preferred_element_type=F32))


def _ssd_body(z_ref, xbc_ref, dtr_ref, cw_ref, cb_ref, dtb_ref, alog_ref, dskip_ref, ng_ref, o_ref,
              xp_ref, xc_ref, dt_ref, da_ref, cs_ref, cst_ref, ecst_ref, dtt_ref, bt_ref, xdf_ref, xdb_ref,
              dtotf_ref, dtotb_ref, y_ref, hf_ref, hb_ref):
    s = xbc_ref.shape[1]
    ch = SSD_CHUNK
    assert ch == LANES
    nc = s // ch
    pad = 8
    nconv = xbc_ref.shape[2]

    xp_ref[0:pad, :] = jnp.zeros((pad, nconv), F32)
    xp_ref[pad + s:pad + s + pad, :] = jnp.zeros((pad, nconv), F32)
    xp_ref[pad:pad + s, :] = xbc_ref[0]

    def conv_chunk(c, carry):
        r0 = pl.multiple_of(c * ch, ch)
        nwin = ch + 2 * pad
        win = xp_ref[pl.ds(r0, nwin), :]
        acc = jnp.broadcast_to(cb_ref[...], (ch, nconv))
        for j in range(SSD_CONV):
            shift = (SSD_CONV // 2 - j) % nwin
            tap = win if shift == 0 else pltpu.roll(win, shift, axis=0)
            acc = acc + cw_ref[j:j + 1, :] * tap[pad:pad + ch, :]
        xc_ref[pl.ds(r0, ch), :] = _silu(acc)
        return carry

    lax.fori_loop(0, nc, conv_chunk, 0)

    lane = lax.broadcasted_iota(I32, (s, LANES), 1)
    pre = dtr_ref[0] + dtb_ref[...]
    dt = jnp.maximum(pre, 0.0) + jnp.log1p(jnp.exp(-jnp.abs(pre)))
    dt = jnp.where(lane < 2 * SSD_HEADS, dt, 0.0)
    dt_ref[...] = dt
    da_ref[...] = dt * (-jnp.exp(alog_ref[...]))

    row = lax.broadcasted_iota(I32, (ch, ch), 0)
    col = lax.broadcasted_iota(I32, (ch, ch), 1)
    tri = jnp.where(col <= row, 1.0, 0.0).astype(F32)
    lower = col <= row
    upper = col >= row
    lane_c = lax.broadcasted_iota(I32, (ch, LANES), 1)
    st_rows = lax.broadcasted_iota(I32, (SSD_BC, SSD_WIDTH), 0) // SSD_STATE
    st_cols = lax.broadcasted_iota(I32, (SSD_BC, SSD_WIDTH), 1) // (SSD_WIDTH // SSD_GROUPS)
    same_group = st_rows == st_cols
    sel_f = _head_selector(0)
    sel_b = _head_selector(SSD_HEADS)

    hf_ref[...] = jnp.zeros_like(hf_ref)
    hb_ref[...] = jnp.zeros_like(hb_ref)

    def wide(x):
        return jnp.concatenate([x[c * ch:(c + 1) * ch, :] for c in range(nc)], axis=1)

    da_wide = wide(da_ref[...])
    cs_wide = jnp.dot(tri, da_wide, precision=HIGHEST, preferred_element_type=F32)
    for c in range(nc):
        cs_ref[c * ch:(c + 1) * ch, :] = cs_wide[:, c * LANES:(c + 1) * LANES]
    cst_ref[...] = cs_wide.T
    ecst_ref[...] = (cs_wide - da_wide).T
    dtt_ref[...] = wide(dt_ref[...]).T
    bt_ref[...] = wide(xc_ref[:, SSD_WIDTH:SSD_WIDTH + SSD_BC]).T.astype(BF16)

    def prep_chunk(c, carry):
        r0 = pl.multiple_of(c * ch, ch)
        r8 = pl.multiple_of(c * 8, 8)
        da = da_ref[pl.ds(r0, ch), :]
        dtc = dt_ref[pl.ds(r0, ch), :]
        cs = cs_ref[pl.ds(r0, ch), :]
        tot = cs[ch - 1:ch, :]
        xs = xc_ref[pl.ds(r0, ch), 0:SSD_WIDTH]
        tot8 = jnp.broadcast_to(jnp.exp(tot), (8, LANES))
        ff = _expand_heads(jnp.concatenate([dtc * jnp.exp(tot - cs), tot8], axis=0), sel_f)
        xdf_ref[pl.ds(r0, ch), :] = (xs * ff[0:ch]).astype(BF16)
        dtotf_ref[pl.ds(r8, 8), :] = ff[ch:ch + 8]
        fb = _expand_heads(jnp.concatenate([dtc * jnp.exp(cs - da), tot8], axis=0), sel_b)
        xdb_ref[pl.ds(r0, ch), :] = (xs * fb[0:ch]).astype(BF16)
        dtotb_ref[pl.ds(r8, 8), :] = fb[ch:ch + 8]
        return carry

    lax.fori_loop(0, nc, prep_chunk, 0, unroll=4)

    def fwd_chunk(c, carry):
        r0 = pl.multiple_of(c * ch, ch)
        da = da_ref[pl.ds(r0, ch), :]
        cs = cs_ref[pl.ds(r0, ch), :]
        ecs = cs - da
        nh2 = 2 * SSD_HEADS
        cs_t = cst_ref[pl.ds(r0, nh2), :]
        ecs_t = ecst_ref[pl.ds(r0, nh2), :]
        dt_t = dtt_ref[pl.ds(r0, nh2), :]
        xs = xc_ref[pl.ds(r0, ch), 0:SSD_WIDTH]
        cm = xc_ref[pl.ds(r0, ch), SSD_WIDTH + SSD_BC:SSD_WIDTH + 2 * SSD_BC]
        bt = bt_ref[pl.ds(r0, ch), :]
        cmb = cm.astype(BF16)
        xsb = xs.astype(BF16)

        hf = hf_ref[...]
        y_ref[pl.ds(r0, ch), :] = (jnp.dot(cmb, hf.astype(BF16), preferred_element_type=F32)
                                   * _expand_heads(jnp.exp(cs), sel_f))
        upd = jnp.dot(bt, xdf_ref[pl.ds(r0, ch), :], preferred_element_type=F32)
        hf_ref[...] = hf * dtotf_ref[pl.ds(pl.multiple_of(c * 8, 8), 1), :] + jnp.where(same_group, upd, 0.0)

        g_mats = []
        for g in range(SSD_GROUPS):
            cg = jnp.where(lane_c // SSD_STATE == g, cmb, jnp.zeros_like(cmb))
            g_mats.append(jnp.dot(cg, bt, preferred_element_type=F32))
        for pair in range(SSD_HEADS // 2):
            xpair = xsb[:, pair * LANES:(pair + 1) * LANES]
            ypair = jnp.zeros((ch, LANES), F32)
            for sub in range(2):
                hh = 2 * pair + sub
                g = hh // (SSD_HEADS // SSD_GROUPS)
                d_f = cs[:, hh:hh + 1] - cs_t[hh:hh + 1, :]
                d_b = ecs_t[SSD_HEADS + hh:SSD_HEADS + hh + 1, :] - ecs[:, SSD_HEADS + hh:SSD_HEADS + hh + 1]
                decay = jnp.exp(jnp.where(lower, d_f, d_b))
                wgt = (jnp.where(lower, dt_t[hh:hh + 1, :], 0.0)
                       + jnp.where(upper, dt_t[SSD_HEADS + hh:SSD_HEADS + hh + 1, :], 0.0))
                m = (g_mats[g] * decay * wgt).astype(BF16)
                xh = jnp.where(lane_c // SSD_HEADDIM == sub, xpair, jnp.zeros_like(xpair))
                ypair = ypair + jnp.dot(m, xh, preferred_element_type=F32)
            y_ref[pl.ds(r0, ch), pair * LANES:(pair + 1) * LANES] += ypair
        return carry

    lax.fori_loop(0, nc, fwd_chunk, 0, unroll=4)

    def bwd_chunk(i, carry):
        c = nc - 1 - i
        r0 = pl.multiple_of(c * ch, ch)
        da = da_ref[pl.ds(r0, ch), :]
        cs = cs_ref[pl.ds(r0, ch), :]
        tot = cs[ch - 1:ch, :]
        ecs = cs - da
        xs = xc_ref[pl.ds(r0, ch), 0:SSD_WIDTH]
        cm = xc_ref[pl.ds(r0, ch), SSD_WIDTH + SSD_BC:SSD_WIDTH + 2 * SSD_BC]
        bt = bt_ref[pl.ds(r0, ch), :]

        hb = hb_ref[...]
        y_off = (jnp.dot(cm.astype(BF16), hb.astype(BF16), preferred_element_type=F32)
                 * _expand_heads(jnp.exp(tot - ecs), sel_b))
        upd = jnp.dot(bt, xdb_ref[pl.ds(r0, ch), :], preferred_element_type=F32)
        hb_ref[...] = hb * dtotb_ref[pl.ds(pl.multiple_of(c * 8, 8), 1), :] + jnp.where(same_group, upd, 0.0)

        y = y_ref[pl.ds(r0, ch), :] + y_off + dskip_ref[...] * xs
        y = y * _silu(z_ref[0, pl.ds(r0, ch), :])
        gw = SSD_WIDTH // SSD_GROUPS
        yn = jnp.concatenate([_rms(y[:, g * gw:(g + 1) * gw]) for g in range(SSD_GROUPS)], axis=-1)
        o_ref[0, pl.ds(r0, ch), :] = (yn * ng_ref[...]).astype(BF16)
        return carry

    lax.fori_loop(0, nc, bwd_chunk, 0, unroll=4)


def _ssd(z, xbc, dt_raw, conv_w, conv_b, dt_bias, a_log, d_skip, norm_g):
    bsz, s, nconv = xbc.shape
    tok = lambda wd: pl.BlockSpec((1, s, wd), lambda b: (b, 0, 0))
    full = lambda a: pl.BlockSpec(a.shape, lambda b: (0, 0))
    return pl.pallas_call(
        _ssd_body,
        grid=(bsz,),
        in_specs=[tok(SSD_WIDTH), tok(nconv), tok(LANES), full(conv_w), full(conv_b), full(dt_bias),
                  full(a_log), full(d_skip), full(norm_g)],
        out_specs=tok(SSD_WIDTH),
        out_shape=jax.ShapeDtypeStruct((bsz, s, SSD_WIDTH), BF16),
        scratch_shapes=[pltpu.VMEM((s + 16, nconv), F32),
                        pltpu.VMEM((s, nconv), F32),
                        pltpu.VMEM((s, LANES), F32),
                        pltpu.VMEM((s, LANES), F32),
                        pltpu.VMEM((s, LANES), F32),
                        pltpu.VMEM((s, LANES), F32),
                        pltpu.VMEM((s, LANES), F32),
                        pltpu.VMEM((s, LANES), F32),
                        pltpu.VMEM((s, SSD_BC), BF16),
                        pltpu.VMEM((s, SSD_WIDTH), BF16),
                        pltpu.VMEM((s, SSD_WIDTH), BF16),
                        pltpu.VMEM((s // SSD_CHUNK * 8, SSD_WIDTH), F32),
                        pltpu.VMEM((s // SSD_CHUNK * 8, SSD_WIDTH), F32),
                        pltpu.VMEM((s, SSD_WIDTH), F32),
                        pltpu.VMEM((SSD_BC, SSD_WIDTH), F32),
                        pltpu.VMEM((SSD_BC, SSD_WIDTH), F32)],
        compiler_params=_params("parallel"),
        name="ssd_mixer",
    )(z, xbc, dt_raw, conv_w, conv_b, dt_bias, a_log, d_skip, norm_g)


def _outproj_body(att_ref, ssd_ref, x_ref, g1_ref, sh_ref, sc_ref, ng_ref, w_ref, rw_ref,
                  x1_ref, h2_ref, aff_ref):
    na = att_ref.shape[2]
    mix = (jnp.dot(att_ref[0], w_ref[0:na, :], preferred_element_type=F32)
           + jnp.dot(ssd_ref[0], w_ref[na:, :], preferred_element_type=F32))
    x1 = x_ref[0] + g1_ref[0] * mix
    x1_ref[0] = x1
    h2 = _rms(x1) * ng_ref[...]
    h2 = h2 * (1.0 + sc_ref[0]) + sh_ref[0]
    h2_hi = h2.astype(BF16)
    h2_ref[0] = h2_hi
    h2_lo = (h2 - h2_hi.astype(F32)).astype(BF16)
    nt = lambda a, b: lax.dot_general(a, b, (((1,), (1,)), ((), ())), preferred_element_type=F32)
    ne = rw_ref.shape[1]
    both = nt(rw_ref[...].reshape(2 * ne, rw_ref.shape[2]), h2_hi)
    logits = both[0:ne] + (nt(rw_ref[0], h2_lo) + both[ne:2 * ne])
    e = jnp.exp(logits - jnp.max(logits, axis=0, keepdims=True))
    aff_ref[0] = e / jnp.sum(e, axis=0, keepdims=True)


def _out_proj(att, ssd, x, g1, sh2, sc2, ng, w_out, rw_t, ts):
    bsz, s, d = x.shape
    tok = lambda wd: pl.BlockSpec((1, ts, wd), lambda b, t: (b, t, 0))
    vec = pl.BlockSpec((1, 1, d), lambda b, t: (b, 0, 0))
    full = lambda a: pl.BlockSpec(a.shape, lambda b, t: (0, 0))
    ne = rw_t.shape[1]
    return pl.pallas_call(
        _outproj_body,
        grid=(bsz, s // ts),
        in_specs=[tok(att.shape[2]), tok(ssd.shape[2]), tok(d), vec, vec, vec, full(ng), full(w_out),
                  pl.BlockSpec(rw_t.shape, lambda b, t: (0, 0, 0))],
        out_specs=[tok(d), tok(d), pl.BlockSpec((1, ne, ts), lambda b, t: (b, 0, t))],
        out_shape=[jax.ShapeDtypeStruct((bsz, s, d), F32),
                   jax.ShapeDtypeStruct((bsz, s, d), BF16),
                   jax.ShapeDtypeStruct((bsz, ne, s), F32)],
        compiler_params=_params("parallel", "arbitrary"),
        name="out_proj_router",
    )(att, ssd, x, g1, sh2, sc2, ng, w_out, rw_t)


def _topk_body(aff_ref, pos_ref, post_ref, ut_ref, *, cap):
    nrow, s = aff_ref.shape
    r = lax.broadcasted_iota(I32, (s, s), 0)
    c = lax.broadcasted_iota(I32, (s, s), 1)
    ut_ref[...] = jnp.where(r < c, 1.0, 0.0).astype(BF16)

    bits = pltpu.bitcast(aff_ref[...], I32)

    def count(mask):
        return jnp.sum(jnp.where(mask, 1, 0), axis=-1, keepdims=True)

    thr = jnp.zeros((nrow, 1), I32)
    for bit in range(30, -1, -1):
        cand = thr | (1 << bit)
        thr = jnp.where(count(bits >= cand) >= cap, cand, thr)
    gt = bits > thr
    eq = bits == thr
    need = cap - count(gt)
    eq_rank = jnp.dot(jnp.where(eq, 1.0, 0.0).astype(BF16), ut_ref[...], preferred_element_type=F32)
    sel = gt | (eq & (eq_rank < need.astype(F32)))
    slot = jnp.dot(jnp.where(sel, 1.0, 0.0).astype(BF16), ut_ref[...], preferred_element_type=F32)
    pos = jnp.where(sel, slot, -1.0)
    pos_ref[...] = pos.astype(I32)
    post_ref[...] = pos.T.astype(I32)


def _topk(aff_t, cap):
    bsz, ne, s = aff_t.shape
    nrow = bsz * ne
    pos, pos_t = pl.pallas_call(
        functools.partial(_topk_body, cap=cap),
        grid=(1,),
        in_specs=[pl.BlockSpec((nrow, s), lambda i: (0, 0))],
        out_specs=[pl.BlockSpec((nrow, s), lambda i: (0, 0)),
                   pl.BlockSpec((s, nrow), lambda i: (0, 0))],
        out_shape=[jax.ShapeDtypeStruct((nrow, s), I32),
                   jax.ShapeDtypeStruct((s, nrow), I32)],
        scratch_shapes=[pltpu.VMEM((s, s), BF16)],
        compiler_params=_params("arbitrary"),
        name="expert_topk",
    )(aff_t.reshape(nrow, s))
    return pos.reshape(bsz, ne, s), pos_t


def _gather_body(h2_ref, pos_ref, aff_ref, xg_ref, gc_ref, *, cap):
    ne, s = pos_ref.shape[1], pos_ref.shape[2]
    slot = lax.broadcasted_iota(I32, (cap, s), 0)
    h2 = h2_ref[0]
    for e in range(ne):
        hit = pos_ref[0, e:e + 1, :] == slot
        xg_ref[e] = jnp.dot(jnp.where(hit, 1.0, 0.0).astype(BF16), h2,
                            preferred_element_type=F32).astype(BF16)
        gate = jnp.sum(jnp.where(hit, aff_ref[0, e:e + 1, :], 0.0), axis=-1, keepdims=True)
        gc_ref[e] = jnp.broadcast_to(gate, (cap, LANES))


def _gather(h2, pos, aff_t, cap):
    bsz, s, d = h2.shape
    ne = pos.shape[1]
    return pl.pallas_call(
        functools.partial(_gather_body, cap=cap),
        grid=(bsz,),
        in_specs=[pl.BlockSpec((1, s, d), lambda b: (b, 0, 0)),
                  pl.BlockSpec((1, ne, s), lambda b: (b, 0, 0)),
                  pl.BlockSpec((1, ne, s), lambda b: (b, 0, 0))],
        out_specs=[pl.BlockSpec((ne, cap, d), lambda b: (0, b, 0)),
                   pl.BlockSpec((ne, cap, LANES), lambda b: (0, b, 0))],
        out_shape=[jax.ShapeDtypeStruct((ne, bsz * cap, d), BF16),
                   jax.ShapeDtypeStruct((ne, bsz * cap, LANES), F32)],
        compiler_params=_params("parallel"),
        name="moe_gather",
    )(h2, pos, aff_t)


def _ffn_body(xg_ref, gc_ref, wg_ref, wu_ref, wd_ref, y_ref, acc_ref):
    f = pl.program_id(1)

    @pl.when((pl.program_id(0) == 0) & (f == 0))
    def _():
        acc_ref[...] = jnp.zeros_like(acc_ref)

    wg = wg_ref[0].astype(BF16)
    wu = wu_ref[0].astype(BF16)
    wd = wd_ref[0].astype(BF16)
    nq = 4
    rq = xg_ref.shape[1] // nq
    hidden = {}

    def gate_up(r):
        xg = xg_ref[0, r * rq:(r + 1) * rq, :]
        gate = jnp.dot(xg, wg, preferred_element_type=F32)
        up = jnp.dot(xg, wu, preferred_element_type=F32)
        hidden[r] = (_silu(gate) * up).astype(BF16)

    def down(r):
        rows = slice(r * rq, (r + 1) * rq)
        part = jnp.dot(hidden.pop(r), wd, preferred_element_type=F32)
        acc_ref[rows, :] = jnp.where(f == 0, part, acc_ref[rows, :] + part)

    gate_up(0)
    for r in range(nq):
        if r + 1 < nq:
            gate_up(r + 1)
        down(r)

    @pl.when(f == pl.num_programs(1) - 1)
    def _():
        m, d = acc_ref.shape
        gcol = gc_ref[0]
        y_ref[0] = (acc_ref[...] * jnp.concatenate([gcol] * (d // LANES), axis=-1)).astype(BF16)


def _ffn(xg, gc, w_gate, w_up, w_down, tf):
    ne, m, d = xg.shape
    ff = w_gate.shape[2]
    return pl.pallas_call(
        _ffn_body,
        grid=(ne, ff // tf),
        in_specs=[pl.BlockSpec((1, m, d), lambda e, f: (e, 0, 0)),
                  pl.BlockSpec((1, m, LANES), lambda e, f: (e, 0, 0)),
                  pl.BlockSpec((1, d, tf), lambda e, f: (e, 0, f)),
                  pl.BlockSpec((1, d, tf), lambda e, f: (e, 0, f)),
                  pl.BlockSpec((1, tf, d), lambda e, f: (e, f, 0))],
        out_specs=pl.BlockSpec((1, m, d), lambda e, f: (e, 0, 0)),
        out_shape=jax.ShapeDtypeStruct((ne, m, d), BF16),
        scratch_shapes=[pltpu.VMEM((m, d), F32)],
        compiler_params=_params("parallel", "arbitrary"),
        name="moe_ffn",
    )(xg, gc, w_gate, w_up, w_down)


def _scatter_body(post_ref, y_ref, x1_ref, g2_ref, ng_ref, o_ref, *, cap):
    ne = y_ref.shape[0]
    ts, nrow = post_ref.shape
    d = y_ref.shape[2]
    r = lax.broadcasted_iota(I32, (nrow, LANES), 0)
    c = lax.broadcasted_iota(I32, (nrow, LANES), 1)
    pick = jnp.where(r == pl.program_id(0) * ne + c, 1.0, 0.0).astype(BF16)
    post = jnp.dot(post_ref[...].astype(F32).astype(BF16), pick, preferred_element_type=F32)
    slot = lax.broadcasted_iota(I32, (ts, cap), 1).astype(F32)
    onehot = jnp.concatenate(
        [jnp.where(post[:, e:e + 1] == slot, 1.0, 0.0).astype(BF16) for e in range(ne)], axis=-1)
    moe = jnp.dot(onehot, y_ref[...].reshape(ne * cap, d), preferred_element_type=F32)
    x2 = x1_ref[0] + g2_ref[0] * moe
    o_ref[0] = _rms(x2) * ng_ref[...]


def _scatter(post, y, x1, g2, ng, cap, ts):
    bsz, s, d = x1.shape
    ne = y.shape[0]
    return pl.pallas_call(
        functools.partial(_scatter_body, cap=cap),
        grid=(bsz, s // ts),
        in_specs=[pl.BlockSpec((ts, post.shape[1]), lambda b, t: (t, 0)),
                  pl.BlockSpec((ne, cap, d), lambda b, t: (0, b, 0)),
                  pl.BlockSpec((1, ts, d), lambda b, t: (b, t, 0)),
                  pl.BlockSpec((1, 1, d), lambda b, t: (b, 0, 0)),
                  pl.BlockSpec((1, d), lambda b, t: (0, 0))],
        out_specs=pl.BlockSpec((1, ts, d), lambda b, t: (b, t, 0)),
        out_shape=jax.ShapeDtypeStruct((bsz, s, d), F32),
        compiler_params=_params("parallel", "arbitrary"),
        name="moe_scatter_final",
    )(post, y, x1, g2, ng)


def _pad_lanes(a, n=LANES):
    return jnp.pad(a, [(0, 0)] * (a.ndim - 1) + [(0, n - a.shape[-1])])


def kernel(x, c, ada_w, ada_b, norm_mix_g, norm_ffn_g, norm_final_g, w_in, lambda_q1, lambda_k1, lambda_q2, lambda_k2, attn_subln_g, rel_bias_table, conv_w, conv_b, dt_bias_f, dt_bias_b, A_log_f, A_log_b, D_skip, ssm_norm_g, w_out, router_w, w_gate, w_up, w_down):
    bsz, s, d = x.shape
    depth = ada_w.shape[0]
    ne = router_w.shape[2]
    cap = CAPACITY_FACTOR * s // ne
    att_w = ATT_HEADS * ATT_V_DIM
    nconv = conv_w.shape[3]
    splits = (att_w, att_w, att_w, SSD_WIDTH, nconv, 2 * SSD_HEADS)
    offs = [0]
    for wd in splits:
        offs.append(offs[-1] + wd)
    assert offs[-1] == w_in.shape[2]
    row = lambda a: a.reshape(1, -1)

    assert depth == 1, "the final RMSNorm is fused into the (single) layer's scatter kernel"
    for l in range(depth):
        lam_init = 0.8 - 0.6 * math.exp(-0.3 * l)
        mod = _ada_mod(c, ada_w[l], ada_b[l])
        sh1, sc1, g1, sh2, sc2, g2 = [m.reshape(bsz, 1, d) for m in jnp.split(mod, 6, axis=-1)]

        q, k, v_t, z, xbc, dt_raw = _in_proj(x, sh1, sc1, row(norm_mix_g[l]), w_in[l].astype(BF16), offs, ts=1024)

        att = _attention(q, k, v_t, rel_bias_table, row(lambda_q1[l]), row(lambda_k1[l]),
                         row(lambda_q2[l]), row(lambda_k2[l]), row(attn_subln_g[l]), lam_init, tq=2048)

        dt_bias = _pad_lanes(jnp.concatenate([dt_bias_f[l], dt_bias_b[l]]).reshape(1, -1))
        a_log = _pad_lanes(jnp.concatenate([A_log_f[l], A_log_b[l]]).reshape(1, -1))
        d_skip = jnp.repeat(D_skip[l], SSD_HEADDIM).reshape(1, -1)
        ssd = _ssd(z, xbc, dt_raw, conv_w[l].reshape(SSD_CONV, nconv), row(conv_b[l]), dt_bias, a_log,
                   d_skip, row(ssm_norm_g[l]))

        rw_t = router_w[l].T
        rw_hi = rw_t.astype(BF16)
        rw_split = jnp.stack([rw_hi, (rw_t - rw_hi.astype(F32)).astype(BF16)])
        x1, h2, aff_t = _out_proj(att, ssd, x, g1, sh2, sc2, row(norm_ffn_g[l]), w_out[l].astype(BF16),
                                  rw_split, ts=1024)
        pos, pos_t = _topk(aff_t, cap)
        xg, gc = _gather(h2, pos, aff_t, cap)
        y = _ffn(xg, gc, w_gate[l], w_up[l], w_down[l], tf=256)
        x = _scatter(pos_t, y, x1, g2, row(norm_final_g), cap, ts=1024)
    return x
```
